```python
import jax
import jax.numpy as jnp
from jax import lax
import numpy as np


D_MODEL = 1024
BATCH = 8
SEQ = 4096
DEPTH = 2

CTX_LEN = 256
GRID_W = 64
EPS = 1e-6
ROPE_THETA = 10000.0
Q_BLOCK = 128

M_HEADS = 4
M_DH = 96
M_WIDTH = M_HEADS * M_DH
M_CHUNK = 64
M_CONV = 3

A_HEADS = 6
A_KV = 2
A_DH = 64
A_WIDTH = A_HEADS * A_DH

L_HEADS = 4
L_NOPE = 64
L_ROPE = 32
L_QK = L_NOPE + L_ROPE
L_DV = 64
L_QRANK = 256
L_KVRANK = 128
L_WIDTH = L_HEADS * L_DV

MIX_WIDTH = M_WIDTH + A_WIDTH + L_WIDTH
IN_SIZES = (M_WIDTH, M_WIDTH, M_WIDTH, M_WIDTH, 4 * M_HEADS,
            A_WIDTH, A_KV * A_DH, A_KV * A_DH,
            L_QRANK, L_KVRANK, L_ROPE)
IN_WIDTH = sum(IN_SIZES)

D_FF = 3584
N_EXPERTS = 8
TOP_K = 2

F32 = jnp.float32

kernel_name = 'hybrid_mlstm_gqa_mla_prefix_dit'


def rmsnorm(x, g):
    xf = x.astype(F32)
    y = xf * lax.rsqrt(jnp.mean(xf * xf, axis=-1, keepdims=True) + EPS)
    return (y * g.astype(F32)).astype(x.dtype)


def modulate(x, g, shift, scale):
    return rmsnorm(x, g) * (1 + scale) + shift


def to_heads(a, n_heads):
    b, t, _ = a.shape
    return a.reshape(b, t, n_heads, -1).transpose(0, 2, 1, 3)


def merge_heads(o):
    b, hk, g, t, dv = o.shape
    return o.transpose(0, 3, 1, 2, 4).reshape(b, t, hk * g * dv)


def axial_rope_angles(rows, cols, rot_dim):
    nf = rot_dim // 4
    inv = ROPE_THETA ** (-jnp.arange(nf, dtype=F32) / nf)
    ar = rows[:, None] * inv
    ac = cols[:, None] * inv
    ang = jnp.concatenate([ar, ar, ac, ac], axis=-1)
    return jnp.cos(ang), jnp.sin(ang)


def apply_rope(x, cos, sin):
    half = x.shape[-1] // 2
    quarter = half // 2

    def rotate_half(u):
        return jnp.concatenate([-u[..., quarter:], u[..., :quarter]], axis=-1)

    rot = jnp.concatenate([rotate_half(x[..., :half]), rotate_half(x[..., half:])], axis=-1)
    return (x.astype(F32) * cos + rot.astype(F32) * sin).astype(x.dtype)


def centred_dwconv(x, w):
    k, ch = w.shape
    pad = k // 2
    return lax.conv_general_dilated(x, w.astype(x.dtype)[:, None, :], window_strides=(1,),
                                    padding=[(pad, pad)], dimension_numbers=('NWC', 'WIO', 'NWC'),
                                    feature_group_count=ch)


def attend_blocked(q, k, v, scale):
    b, hk, g, tq, d = q.shape
    nb = tq // Q_BLOCK
    qb = jnp.moveaxis(q.reshape(b, hk, g, nb, Q_BLOCK, d), 3, 0)

    def one_block(qi):
        s = jnp.einsum('bhgqd,bhkd->bhgqk', qi, k, preferred_element_type=F32) * scale
        p = jax.nn.softmax(s, axis=-1)
        return jnp.einsum('bhgqk,bhkd->bhgqd', p.astype(v.dtype), v)

    o = lax.map(one_block, qb)
    return jnp.moveaxis(o, 0, 3).reshape(b, hk, g, tq, v.shape[-1])


def mlstm_scan(q, k, v, log_i, log_f, state):
    b, h, t, dh = q.shape
    nc = t // M_CHUNK

    def to_chunks(a):
        a = a.reshape(b, h, nc, M_CHUNK, *a.shape[3:])
        return jnp.moveaxis(a, 2, 0)

    tril = jnp.tril(jnp.ones((M_CHUNK, M_CHUNK), dtype=bool))

    def step(carry, xs):
        c_st, n_st, m_st = carry
        qc, kc, vc, ic, fc = xs
        bcum = jnp.cumsum(fc, axis=-1)
        dmat = jnp.where(tril, bcum[..., :, None] - bcum[..., None, :] + ic[..., None, :], -jnp.inf)
        m_inter = bcum + m_st[..., None]
        m_t = jnp.maximum(m_inter, jnp.max(dmat, axis=-1))
        w = jnp.exp(dmat - m_t[..., None]) * jnp.einsum('bhtd,bhsd->bhts', qc, kc)
        a_inter = jnp.exp(m_inter - m_t)
        num = a_inter[..., None] * jnp.einsum('bhtd,bhde->bhte', qc, c_st) + jnp.einsum('bhts,bhse->bhte', w, vc)
        den = a_inter * jnp.einsum('bhtd,bhd->bht', qc, n_st) + jnp.sum(w, axis=-1)
        h_out = num / jnp.maximum(jnp.abs(den), jnp.exp(-m_t))[..., None]
        g = bcum[..., -1:] - bcum + ic
        m_new = jnp.maximum(bcum[..., -1] + m_st, jnp.max(g, axis=-1))
        decay = jnp.exp(bcum[..., -1] + m_st - m_new)
        wk = jnp.exp(g - m_new[..., None])
        c_new = decay[..., None, None] * c_st + jnp.einsum('bhs,bhsd,bhse->bhde', wk, kc, vc)
        n_new = decay[..., None] * n_st + jnp.einsum('bhs,bhsd->bhd', wk, kc)
        return (c_new, n_new, m_new), h_out

    state, hs = lax.scan(step, state, tuple(to_chunks(a) for a in (q, k, v, log_i, log_f)))
    return jnp.moveaxis(hs, 0, 2).reshape(b, h, t, dh), state


def mlstm_prep(mq, mk, mv, mg, conv_w, gate_b):
    b, t, _ = mq.shape
    qk = jax.nn.silu(centred_dwconv(jnp.concatenate([mq, mk], axis=-1), conv_w))
    q = to_heads(qk[..., :M_WIDTH], M_HEADS).astype(F32)
    k = to_heads(qk[..., M_WIDTH:], M_HEADS).astype(F32) * (M_DH ** -0.5)
    v = to_heads(mv, M_HEADS).astype(F32)
    g = (mg + gate_b).astype(F32).transpose(0, 2, 1).reshape(b, 4, M_HEADS, t)
    log_i = g[:, 0::2]
    log_f = jax.nn.log_sigmoid(g[:, 1::2])
    return q, k, v, log_i, log_f


def mlstm_bidirectional(ctx_in, lat_in):
    qc, kc, vc, ic, fc = ctx_in
    ql, kl, vl, il, fl = lat_in
    b, h, _, dh = qc.shape
    zero = (jnp.zeros((b, h, dh, dh), F32), jnp.zeros((b, h, dh), F32), jnp.zeros((b, h), F32))

    def flip(a):
        return jnp.flip(a, axis=2)

    hc_f, st_f = mlstm_scan(qc, kc, vc, ic[:, 0], fc[:, 0], zero)
    hl_f, _ = mlstm_scan(ql, kl, vl, il[:, 0], fl[:, 0], st_f)
    hc_b, st_b = mlstm_scan(flip(qc), flip(kc), flip(vc), flip(ic[:, 1]), flip(fc[:, 1]), zero)
    hl_b, _ = mlstm_scan(flip(ql), flip(kl), flip(vl), flip(il[:, 1]), flip(fl[:, 1]), st_b)
    return hc_f + flip(hc_b), hl_f + flip(hl_b)


def mlstm_out(h, o, norm_g):
    b, nh, t, dh = h.shape
    hn = rmsnorm(h.transpose(0, 2, 1, 3), norm_g.reshape(nh, dh)).reshape(b, t, nh * dh)
    return (hn * jax.nn.sigmoid(o.astype(F32))).astype(o.dtype)


def gqa_prep(aq, ak, av, qn, kn, rope):
    q = rmsnorm(to_heads(aq, A_HEADS), qn)
    k = rmsnorm(to_heads(ak, A_KV), kn)
    v = to_heads(av, A_KV)
    if rope is not None:
        q = apply_rope(q, *rope)
        k = apply_rope(k, *rope)
    b, _, t, _ = q.shape
    return q.reshape(b, A_KV, A_HEADS // A_KV, t, A_DH), k, v


def mla_prep(cq, ckv, kr, cq_norm, ckv_norm, wuq, wukv, qn, kn, rope):
    q = to_heads(rmsnorm(cq, cq_norm) @ wuq, L_HEADS)
    kv = to_heads(rmsnorm(ckv, ckv_norm) @ wukv, L_HEADS)
    k_nope = rmsnorm(kv[..., :L_NOPE], kn[:L_NOPE])
    v = kv[..., L_NOPE:]
    q_nope = rmsnorm(q[..., :L_NOPE], qn[:L_NOPE])
    q_rope = rmsnorm(q[..., L_NOPE:], qn[L_NOPE:])
    k_rope = rmsnorm(kr, kn[L_NOPE:])
    if rope is not None:
        q_rope = apply_rope(q_rope, *rope)
        k_rope = apply_rope(k_rope, *rope)
    b, h, t, _ = q.shape
    q = jnp.concatenate([q_nope, q_rope], axis=-1)
    k = jnp.concatenate([k_nope, jnp.broadcast_to(k_rope[:, None], (b, h, t, L_ROPE))], axis=-1)
    return q[:, :, None], k, v


def hybrid_mixer(h_lat, h_ctx, need_ctx, w_in, m_conv, m_gate_b, m_norm, a_qnorm, a_knorm,
                 l_cq_norm, l_ckv_norm, l_wuq, l_wukv, l_qnorm, l_knorm, w_out, rope_a, rope_l):
    split_at = np.cumsum(IN_SIZES)[:-1].tolist()
    pl = jnp.split(h_lat @ w_in, split_at, axis=-1)
    pc = jnp.split(h_ctx @ w_in, split_at, axis=-1)
    hm_ctx, hm_lat = mlstm_bidirectional(mlstm_prep(pc[0], pc[1], pc[2], pc[4], m_conv, m_gate_b),
                                         mlstm_prep(pl[0], pl[1], pl[2], pl[4], m_conv, m_gate_b))
    qa_l, ka_l, va_l = gqa_prep(pl[5], pl[6], pl[7], a_qnorm, a_knorm, rope_a)
    qa_c, ka_c, va_c = gqa_prep(pc[5], pc[6], pc[7], a_qnorm, a_knorm, None)
    oa_lat = merge_heads(attend_blocked(qa_l, jnp.concatenate([ka_c, ka_l], axis=2),
                                        jnp.concatenate([va_c, va_l], axis=2), A_DH ** -0.5))
    ql_l, kl_l, vl_l = mla_prep(pl[8], pl[9], pl[10], l_cq_norm, l_ckv_norm, l_wuq, l_wukv, l_qnorm, l_knorm, rope_l)
    ql_c, kl_c, vl_c = mla_prep(pc[8], pc[9], pc[10], l_cq_norm, l_ckv_norm, l_wuq, l_wukv, l_qnorm, l_knorm, None)
    ol_lat = merge_heads(attend_blocked(ql_l, jnp.concatenate([kl_c, kl_l], axis=2),
                                        jnp.concatenate([vl_c, vl_l], axis=2), L_QK ** -0.5))
    o_lat = jnp.concatenate([mlstm_out(hm_lat, pl[3], m_norm), oa_lat, ol_lat], axis=-1) @ w_out
    if not need_ctx:
        return o_lat, None
    oa_ctx = merge_heads(attend_blocked(qa_c, ka_c, va_c, A_DH ** -0.5))
    ol_ctx = merge_heads(attend_blocked(ql_c, kl_c, vl_c, L_QK ** -0.5))
    o_ctx = jnp.concatenate([mlstm_out(hm_ctx, pc[3], m_norm), oa_ctx, ol_ctx], axis=-1) @ w_out
    return o_lat, o_ctx


def swiglu(x, wg, wu, wd):
    return (jax.nn.silu(x @ wg) * (x @ wu)) @ wd


def moe_swiglu(x, router, wg, wu, wd):
    b, t, d = x.shape
    xt = x.reshape(b * t, d)
    logits = (xt @ router).astype(F32)
    top_val, top_idx = lax.top_k(logits, TOP_K)
    top_w = jax.nn.softmax(top_val, axis=-1)
    combine = jnp.einsum('nk,nke->ne', top_w, jax.nn.one_hot(top_idx, N_EXPERTS, dtype=F32)).astype(x.dtype)
    out = jnp.zeros_like(xt)
    for e in range(N_EXPERTS):
        out = out + combine[:, e:e + 1] * swiglu(xt, wg[e], wu[e], wd[e])
    return out.reshape(b, t, d)


def setup_inputs(seed: int = 0) -> dict:
    key = jax.random.key(seed)
    ks = iter(jax.random.split(key, 32))

    def nrm(shape, scale):
        return jax.random.normal(next(ks), shape, F32) * scale

    def gain(shape):
        return 1.0 + nrm(shape, 0.1)

    nd = (DEPTH + 1) // 2
    nm = DEPTH // 2
    d = D_MODEL
    gate_offset = jnp.tile(jnp.repeat(jnp.array([0.0, 3.0], F32), M_HEADS), 2)
    return {
        'x': nrm((BATCH, SEQ, d), 1.0),
        'c': nrm((BATCH, d), 1.0),
        'ctx': nrm((BATCH, CTX_LEN, d), 1.0),
        'c_ctx': nrm((d,), 1.0),
        'mod_w': nrm((DEPTH, d, 6 * d), 0.5 * d ** -0.5),
        'mod_b': nrm((DEPTH, 6 * d), 0.01),
        'norm_mix': gain((DEPTH, d)),
        'norm_ffn': gain((DEPTH, d)),
        'w_in': nrm((DEPTH, d, IN_WIDTH), d ** -0.5),
        'm_conv': nrm((DEPTH, M_CONV, 2 * M_WIDTH), M_CONV ** -0.5),
        'm_gate_b': gate_offset + nrm((DEPTH, 4 * M_HEADS), 0.5),
        'm_norm': gain((DEPTH, M_WIDTH)),
        'a_qnorm': gain((DEPTH, A_DH)),
        'a_knorm': gain((DEPTH, A_DH)),
        'l_cq_norm': gain((DEPTH, L_QRANK)),
        'l_ckv_norm': gain((DEPTH, L_KVRANK)),
        'l_wuq': nrm((DEPTH, L_QRANK, L_HEADS * L_QK), L_QRANK ** -0.5),
        'l_wukv': nrm((DEPTH, L_KVRANK, L_HEADS * (L_NOPE + L_DV)), L_KVRANK ** -0.5),
        'l_qnorm': gain((DEPTH, L_QK)),
        'l_knorm': gain((DEPTH, L_QK)),
        'w_out': nrm((DEPTH, MIX_WIDTH, d), MIX_WIDTH ** -0.5),
        'ffn_wg': nrm((nd, d, D_FF), d ** -0.5),
        'ffn_wu': nrm((nd, d, D_FF), d ** -0.5),
        'ffn_wd': nrm((nd, D_FF, d), D_FF ** -0.5),
        'moe_router': nrm((nm, d, N_EXPERTS), d ** -0.5),
        'moe_wg': nrm((nm, N_EXPERTS, d, D_FF), d ** -0.5),
        'moe_wu': nrm((nm, N_EXPERTS, d, D_FF), d ** -0.5),
        'moe_wd': nrm((nm, N_EXPERTS, D_FF, d), D_FF ** -0.5),
    }


def reference(x, c, ctx, c_ctx, mod_w, mod_b, norm_mix, norm_ffn, w_in, m_conv, m_gate_b, m_norm,
              a_qnorm, a_knorm, l_cq_norm, l_ckv_norm, l_wuq, l_wukv, l_qnorm, l_knorm, w_out,
              ffn_wg, ffn_wu, ffn_wd, moe_router, moe_wg, moe_wu, moe_wd):
    seq = x.shape[1]
    ROWS = seq // GRID_W
    rows = jnp.repeat(jnp.arange(ROWS, dtype=F32), GRID_W)
    cols = jnp.tile(jnp.arange(GRID_W, dtype=F32), ROWS)
    rope_a = axial_rope_angles(rows, cols, A_DH)
    rope_l = axial_rope_angles(rows, cols, L_ROPE)
    silu_c = jax.nn.silu(c)
    silu_cc = jax.nn.silu(c_ctx)
    for i in range(DEPTH):
        need_ctx = i < DEPTH - 1
        mod_lat = (silu_c @ mod_w[i] + mod_b[i])[:, None, :]
        mod_ctx = silu_cc @ mod_w[i] + mod_b[i]
        sh_a, sc_a, g_a, sh_f, sc_f, g_f = jnp.split(mod_lat, 6, axis=-1)
        csh_a, csc_a, cg_a, csh_f, csc_f, cg_f = jnp.split(mod_ctx, 6, axis=-1)
        o_lat, o_ctx = hybrid_mixer(modulate(x, norm_mix[i], sh_a, sc_a),
                                    modulate(ctx, norm_mix[i], csh_a, csc_a), need_ctx,
                                    w_in[i], m_conv[i], m_gate_b[i], m_norm[i], a_qnorm[i], a_knorm[i],
                                    l_cq_norm[i], l_ckv_norm[i], l_wuq[i], l_wukv[i], l_qnorm[i], l_knorm[i],
                                    w_out[i], rope_a, rope_l)
        x = x + g_a * o_lat
        j = i // 2
        if i % 2 == 0:
            ffn = lambda u: swiglu(u, ffn_wg[j], ffn_wu[j], ffn_wd[j])
        else:
            ffn = lambda u: moe_swiglu(u, moe_router[j], moe_wg[j], moe_wu[j], moe_wd[j])
        x = x + g_f * ffn(modulate(x, norm_ffn[i], sh_f, sc_f))
        if need_ctx:
            ctx = ctx + cg_a * o_ctx
            ctx = ctx + cg_f * ffn(modulate(ctx, norm_ffn[i], csh_f, csc_f))
    return x
```

```python
import functools
import math

import numpy as np
import jax
import jax.numpy as jnp
from jax import lax
from jax.experimental import pallas as pl
from jax.experimental.pallas import tpu as pltpu

F32 = jnp.float32
BF16 = jnp.bfloat16

GRID_W = 64
EPS = 1e-6
ROPE_THETA = 10000.0
M_HEADS, M_DH = 4, 96
A_HEADS, A_KV, A_DH = 6, 2, 64
L_HEADS, L_NOPE, L_ROPE, L_DV = 4, 64, 32, 64
L_QK = L_NOPE + L_ROPE
L_QRANK, L_KVRANK = 256, 128
N_EXPERTS, TOP_K = 8, 2
IN_SIZES = (384, 384, 384, 384, 16, 384, 128, 128, 256, 128, 32)

LANE = 128
SLOT = LANE
N_QHEADS = A_HEADS + L_HEADS
N_KVHEADS = A_KV + L_HEADS
VMEM_LIMIT = 56 * 1024 * 1024

O_MQK, O_MV, O_MO, O_MISC = 0, 1024, 1536, 2048
O_QA, O_KA, O_VA, O_CQ, O_CKV, IN_PAD = 2176, 2944, 3200, 3456, 3712, 3840
MISC_KR = 64
DEN_LANE = M_DH

TB = 256
TQ = 256
TK = 512
TM_FFN = 1088
TF = 512
ML = 256


def _cparams(sem):
    return pltpu.CompilerParams(dimension_semantics=sem, vmem_limit_bytes=VMEM_LIMIT)


def _sigmoid(x):
    return 1.0 / (1.0 + jnp.exp(-x))


def _silu(x):
    return x * _sigmoid(x)


def _lane_iota(shape):
    return lax.broadcasted_iota(jnp.int32, shape, len(shape) - 1)


def _row_iota(shape):
    return lax.broadcasted_iota(jnp.int32, shape, 0)


def _modulated(x, gain, mod_ref, t0, ctx_len, k_shift, k_scale):
    tm = x.shape[0]
    is_ctx = (t0 + _row_iota((tm, 1))) < ctx_len
    shift = jnp.where(is_ctx, mod_ref[0, k_shift:k_shift + 1, :], mod_ref[1, k_shift:k_shift + 1, :])
    scale = jnp.where(is_ctx, mod_ref[0, k_scale:k_scale + 1, :], mod_ref[1, k_scale:k_scale + 1, :])
    y = x * lax.rsqrt(jnp.mean(x * x, axis=-1, keepdims=True) + EPS) * gain
    return y * (1.0 + scale) + shift


def _gate_rows(mod_ref, t0, tm, ctx_len, k_gate):
    is_ctx = (t0 + _row_iota((tm, 1))) < ctx_len
    return jnp.where(is_ctx, mod_ref[0, k_gate:k_gate + 1, :], mod_ref[1, k_gate:k_gate + 1, :])


def _mod_kernel(c_ref, w_ref, b_ref, o_ref):
    s = _silu(c_ref[...]).astype(BF16)
    o_ref[...] = jnp.dot(s, w_ref[...].astype(BF16), preferred_element_type=F32) + b_ref[...]


def _mod_table(cc, mod_w, mod_b):
    depth, d, n = mod_w.shape
    rows = cc.shape[0]
    return pl.pallas_call(
        _mod_kernel,
        grid=(depth, n // d),
        in_specs=[pl.BlockSpec((rows, d), lambda l, j: (0, 0)),
                  pl.BlockSpec((None, d, d), lambda l, j: (l, 0, j)),
                  pl.BlockSpec((None, 1, d), lambda l, j: (l, 0, j))],
        out_specs=pl.BlockSpec((None, rows, d), lambda l, j: (l, 0, j)),
        out_shape=jax.ShapeDtypeStruct((depth, rows, n), F32),
        compiler_params=_cparams(("arbitrary", "arbitrary")),
        name="mod_table",
    )(cc, mod_w, mod_b.reshape(depth, 1, n))


def _slot_rms(x, lo, hi):
    lane = _lane_iota(x.shape)
    sq = jnp.where((lane >= lo) & (lane < hi), x * x, 0.0)
    return lax.rsqrt(jnp.sum(sq, axis=-1, keepdims=True) * (1.0 / (hi - lo)) + EPS)


def _rope(x, cos, sin_m, sin_p, quarter):
    return (x * cos + pltpu.roll(x, LANE - quarter, 1) * sin_m + pltpu.roll(x, quarter, 1) * sin_p)


def _inproj_kernel(ctx_len, x_ref, mod_ref, g_ref, w_ref, rope_ref, an_ref, ln_ref, cqn_ref, ckvn_ref,
                   wuq_ref, wukv_ref, mqk_ref, mv_ref, mo_ref, misc_ref, q_ref, k_ref, v_ref):
    tm = x_ref.shape[0]
    t0 = pl.program_id(1) * tm
    h = _modulated(x_ref[...], g_ref[...], mod_ref, t0, ctx_len, 0, 1).astype(BF16)

    def proj(a, b):
        return jnp.dot(h, w_ref[:, a:b], preferred_element_type=F32)

    mqk_ref[...] = proj(O_MQK, O_MV).astype(BF16)
    mv_ref[...] = proj(O_MV, O_MO).astype(BF16)
    mo_ref[...] = proj(O_MO, O_MISC).astype(BF16)
    misc = proj(O_MISC, O_QA)
    misc_ref[...] = misc

    cos_a, sinm_a, sinp_a = rope_ref[0], rope_ref[1], rope_ref[2]
    cos_l, sinm_l, sinp_l = rope_ref[3], rope_ref[4], rope_ref[5]
    a_scale = A_DH ** -0.5
    l_scale = L_QK ** -0.5

    pa = proj(O_QA, O_CQ)
    gq, gk = an_ref[0:1, :], an_ref[1:2, :]
    for i in range(A_HEADS):
        x = pa[:, i * SLOT:(i + 1) * SLOT]
        x = x * _slot_rms(x, 0, A_DH) * gq
        q_ref[:, i * SLOT:(i + 1) * SLOT] = (_rope(x, cos_a, sinm_a, sinp_a, A_DH // 4) * a_scale).astype(BF16)
    for i in range(A_KV):
        x = pa[:, (A_HEADS + i) * SLOT:(A_HEADS + i + 1) * SLOT]
        x = x * _slot_rms(x, 0, A_DH) * gk
        k_ref[:, i * SLOT:(i + 1) * SLOT] = _rope(x, cos_a, sinm_a, sinp_a, A_DH // 4).astype(BF16)
    v_ref[:, 0:A_KV * SLOT] = pa[:, (A_HEADS + A_KV) * SLOT:(A_HEADS + 2 * A_KV) * SLOT].astype(BF16)

    pc = proj(O_CQ, IN_PAD)
    cq = pc[:, 0:L_QRANK]
    cq = (cq * lax.rsqrt(jnp.mean(cq * cq, axis=-1, keepdims=True) + EPS) * cqn_ref[...]).astype(BF16)
    ckv = pc[:, L_QRANK:L_QRANK + L_KVRANK]
    ckv = (ckv * lax.rsqrt(jnp.mean(ckv * ckv, axis=-1, keepdims=True) + EPS) * ckvn_ref[...]).astype(BF16)
    ql = jnp.dot(cq, wuq_ref[...], preferred_element_type=F32)
    kvl = jnp.dot(ckv, wukv_ref[...], preferred_element_type=F32)
    gq_l, gk_l = ln_ref[0:1, :], ln_ref[1:2, :]
    lane = _lane_iota((tm, SLOT))
    kr = jnp.where((lane >= MISC_KR) & (lane < MISC_KR + L_ROPE), misc, 0.0)
    kr = kr * _slot_rms(kr, MISC_KR, MISC_KR + L_ROPE) * gk_l
    kr = _rope(kr, cos_l, sinm_l, sinp_l, L_ROPE // 4)
    for i in range(L_HEADS):
        x = ql[:, i * SLOT:(i + 1) * SLOT]
        inv = jnp.where(lane < L_NOPE, _slot_rms(x, 0, L_NOPE), _slot_rms(x, L_NOPE, L_QK))
        x = x * inv * gq_l
        q_ref[:, (A_HEADS + i) * SLOT:(A_HEADS + i + 1) * SLOT] = (
            _rope(x, cos_l, sinm_l, sinp_l, L_ROPE // 4) * l_scale).astype(BF16)
        kn = kvl[:, i * SLOT:(i + 1) * SLOT]
        kn = kn * _slot_rms(kn, 0, L_NOPE) * gk_l
        k_ref[:, (A_KV + i) * SLOT:(A_KV + i + 1) * SLOT] = (kn + kr).astype(BF16)
    v_ref[:, A_KV * SLOT:] = kvl[:, L_HEADS * SLOT:].astype(BF16)


def _inproj(xs, modtab, gain, w_in_p, rope_tab, an, ln, cqn, ckvn, wuq_p, wukv_p, ctx_len):
    b, s, d = xs.shape
    grid = (b, s // TB)
    tok = lambda n: pl.BlockSpec((None, TB, n), lambda i, j: (i, j, 0))
    full = lambda a: pl.BlockSpec(a.shape, lambda i, j: (0,) * a.ndim)
    out_widths = (1024, 512, 512, SLOT, N_QHEADS * SLOT, N_KVHEADS * SLOT, N_KVHEADS * SLOT)
    out_dtypes = (BF16, BF16, BF16, F32, BF16, BF16, BF16)
    return pl.pallas_call(
        functools.partial(_inproj_kernel, ctx_len),
        grid=grid,
        in_specs=[tok(d),
                  pl.BlockSpec((None, 2, 6, d), lambda i, j: (i, 0, 0, 0)),
                  full(gain), full(w_in_p),
                  pl.BlockSpec((6, TB, SLOT), lambda i, j: (0, j, 0)),
                  full(an), full(ln), full(cqn), full(ckvn), full(wuq_p), full(wukv_p)],
        out_specs=[tok(n) for n in out_widths],
        out_shape=[jax.ShapeDtypeStruct((b, s, n), dt) for n, dt in zip(out_widths, out_dtypes)],
        compiler_params=_cparams(("parallel", "parallel")),
        name="inproj",
    )(xs, modtab, gain, w_in_p, rope_tab, an, ln, cqn, ckvn, wuq_p, wukv_p)


def _conv_kernel(ctx_len, s_len, x_ref, prev_ref, next_ref, w_ref, sc_ref, o_ref):
    tm = x_ref.shape[0]
    t0 = pl.program_id(1) * tm
    x = x_ref[...].astype(F32)
    row = _row_iota((tm, 1))
    has_prev = jnp.logical_and(t0 != 0, t0 != ctx_len)
    has_next = jnp.logical_and(t0 + tm != ctx_len, t0 + tm != s_len)
    hp = jnp.where(has_prev, prev_ref[15:16, :].astype(F32), 0.0)
    hn = jnp.where(has_next, next_ref[0:1, :].astype(F32), 0.0)
    xp = jnp.where(row == 0, hp, pltpu.roll(x, 1, 0))
    xn = jnp.where(row == tm - 1, hn, pltpu.roll(x, tm - 1, 0))
    y = xp * w_ref[0:1, :] + x * w_ref[1:2, :] + xn * w_ref[2:3, :]
    o_ref[...] = (_silu(y) * sc_ref[...]).astype(o_ref.dtype)


def _conv(mqk, conv_w, conv_scale, ctx_len):
    b, s, n = mqk.shape
    hb = TB // 16
    last = s // 16 - 1
    return pl.pallas_call(
        functools.partial(_conv_kernel, ctx_len, s),
        grid=(b, s // TB),
        in_specs=[pl.BlockSpec((None, TB, n), lambda i, j: (i, j, 0)),
                  pl.BlockSpec((None, 16, n), lambda i, j: (i, jnp.maximum(j * hb - 1, 0), 0)),
                  pl.BlockSpec((None, 16, n), lambda i, j: (i, jnp.minimum((j + 1) * hb, last), 0)),
                  pl.BlockSpec((3, n), lambda i, j: (0, 0)),
                  pl.BlockSpec((1, n), lambda i, j: (0, 0))],
        out_specs=pl.BlockSpec((None, TB, n), lambda i, j: (i, j, 0)),
        out_shape=jax.ShapeDtypeStruct((b, s, n), BF16),
        compiler_params=_cparams(("parallel", "parallel")),
        name="mlstm_conv",
    )(mqk, mqk, mqk, conv_w, conv_scale)


def _log_sigmoid(x):
    return jnp.minimum(x, 0.0) - jnp.log(1.0 + jnp.exp(-jnp.abs(x)))


def _mlstm_kernel(n_chunks, q_ref, k_ref, v_ref, misc_ref, gb_ref, hf_ref, hb_ref, c_ref, m_ref):
    hp = pl.program_id(1)
    c_ref[...] = jnp.zeros_like(c_ref)
    m_ref[...] = jnp.zeros_like(m_ref)
    r = _row_iota((ML, ML))
    cidx = _lane_iota((ML, ML))
    tri_f = (r >= cidx)
    tri_b = (r <= cidx)
    tri_f32 = tri_f.astype(F32)
    tri_b32 = tri_b.astype(F32)
    lane = _lane_iota((ML, SLOT))

    def chunk(t0, d, tri, tri32, out_ref):
        g = misc_ref[pl.ds(t0, ML), :] + gb_ref[...]
        logf = _log_sigmoid(g)
        bcum = jnp.dot(tri32, logf, preferred_element_type=F32, precision=lax.Precision.HIGHEST)
        bcum_t = bcum.T
        g_t = g.T
        for hh in range(2):
            head = hp * 2 + hh
            li = d * 2 * M_HEADS + head
            sel_i = (lane == li)
            sel_f = (lane == li + M_HEADS)
            i_col = jnp.sum(jnp.where(sel_i, g, 0.0), axis=-1, keepdims=True)
            b_col = jnp.sum(jnp.where(sel_f, bcum, 0.0), axis=-1, keepdims=True)
            rsel_i = (_row_iota((SLOT, ML)) == li)
            rsel_f = (_row_iota((SLOT, ML)) == li + M_HEADS)
            i_row = jnp.sum(jnp.where(rsel_i, g_t, 0.0), axis=0, keepdims=True)
            b_row = jnp.sum(jnp.where(rsel_f, bcum_t, 0.0), axis=0, keepdims=True)
            sidx = d * 2 + hh
            c_st = c_ref[sidx]
            m_st = m_ref[sidx]
            m_s = m_st[:, 0:1]
            q = q_ref[pl.ds(t0, ML), hh * SLOT:(hh + 1) * SLOT]
            k = k_ref[pl.ds(t0, ML), hh * SLOT:(hh + 1) * SLOT]
            v = v_ref[pl.ds(t0, ML), hh * SLOT:(hh + 1) * SLOT]
            v = jnp.where(lane == DEN_LANE, 1.0, v.astype(F32)).astype(BF16)
            dmat = jnp.where(tri, b_col - b_row + i_row, -jnp.inf)
            m_inter = b_col + m_s
            m_t = jnp.maximum(m_inter, jnp.max(dmat, axis=-1, keepdims=True))
            qk = lax.dot_general(q, k, (((1,), (1,)), ((), ())), preferred_element_type=F32)
            w = jnp.exp(dmat - m_t) * qk
            a_inter = jnp.exp(m_inter - m_t)
            num = a_inter * jnp.dot(q, c_st.astype(BF16), preferred_element_type=F32) + jnp.dot(
                w.astype(BF16), v, preferred_element_type=F32)
            den = jnp.sum(jnp.where(lane == DEN_LANE, num, 0.0), axis=-1, keepdims=True)
            h_out = num / jnp.maximum(jnp.abs(den), jnp.exp(-m_t))
            out_ref[pl.ds(t0, ML), hh * SLOT:(hh + 1) * SLOT] = h_out.astype(out_ref.dtype)
            total = b_col[ML - 1:ML, :] if d == 0 else b_col[0:1, :]
            gg = total - b_col + i_col
            m_new = jnp.maximum(total + m_s, jnp.max(gg, axis=0, keepdims=True))
            decay = jnp.exp(total + m_s - m_new)
            wk = jnp.exp(gg - m_new)
            kw = (k.astype(F32) * wk).astype(BF16)
            c_ref[sidx] = decay * c_st + lax.dot_general(
                kw, v, (((0,), (0,)), ((), ())), preferred_element_type=F32)
            m_ref[sidx] = jnp.broadcast_to(m_new, (1, SLOT))

    def step(n, carry):
        tf0 = pl.multiple_of(n * ML, ML)
        tb0 = pl.multiple_of(jnp.where(n == 0, 0, n_chunks - n) * ML, ML)
        chunk(tf0, 0, tri_f, tri_f32, hf_ref)
        chunk(tb0, 1, tri_b, tri_b32, hb_ref)
        return carry

    lax.fori_loop(0, n_chunks, step, 0)


def _mlstm(qk_c, mv, misc, gate_b):
    b, s, _ = mv.shape
    n_chunks = s // ML
    pair = 2 * SLOT
    kspec = pl.BlockSpec((None, s, pair), lambda i, j: (i, 0, 2 + j))
    spec = pl.BlockSpec((None, s, pair), lambda i, j: (i, 0, j))
    return pl.pallas_call(
        functools.partial(_mlstm_kernel, n_chunks),
        grid=(b, M_HEADS // 2),
        in_specs=[spec, kspec, spec,
                  pl.BlockSpec((None, s, SLOT), lambda i, j: (i, 0, 0)),
                  pl.BlockSpec((1, SLOT), lambda i, j: (0, 0))],
        out_specs=[spec, spec],
        out_shape=[jax.ShapeDtypeStruct((b, s, M_HEADS * SLOT), BF16)] * 2,
        scratch_shapes=[pltpu.VMEM((4, SLOT, SLOT), F32), pltpu.VMEM((4, 1, SLOT), F32)],
        compiler_params=_cparams(("parallel", "parallel")),
        name="mlstm_scan",
    )(qk_c, qk_c, mv, misc, gate_b)


def _attn_kernel(ctx_len, q_off, q_ref, k_ref, v_ref, o_ref):
    s_len = k_ref.shape[0]
    q = q_ref[...]

    def scores(k):
        return lax.dot_general(q, k, (((1,), (1,)), ((), ())), preferred_element_type=F32)

    s0 = scores(k_ref[0:ctx_len, :])
    m0 = jnp.max(s0, axis=-1, keepdims=True)
    p0 = jnp.exp(s0 - m0)
    l0 = jnp.sum(p0, axis=-1, keepdims=True)
    acc0 = jnp.dot(p0.astype(BF16), v_ref[0:ctx_len, :], preferred_element_type=F32)
    is_ctx = (pl.program_id(2) + q_off) * TQ < ctx_len

    @pl.when(is_ctx)
    def _():
        o_ref[...] = (acc0 / l0).astype(o_ref.dtype)

    @pl.when(jnp.logical_not(is_ctx))
    def _():
        def body(c, carry):
            m, l, acc = carry
            t0 = pl.multiple_of(ctx_len + c * TK, TK // 2)
            sc = scores(k_ref[pl.ds(t0, TK), :])
            m_new = jnp.maximum(m, jnp.max(sc, axis=-1, keepdims=True))
            alpha = jnp.exp(m - m_new)
            p = jnp.exp(sc - m_new)
            l_new = alpha * l + jnp.sum(p, axis=-1, keepdims=True)
            acc_new = alpha * acc + jnp.dot(p.astype(BF16), v_ref[pl.ds(t0, TK), :],
                                            preferred_element_type=F32)
            return m_new, l_new, acc_new

        m, l, acc = lax.fori_loop(0, (s_len - ctx_len) // TK, body, (m0, l0, acc0))
        o_ref[...] = (acc / l).astype(o_ref.dtype)


def _kv_slot(h):
    return jnp.where(h < A_HEADS, h // (A_HEADS // A_KV), h - A_HEADS + A_KV)


def _attention(q_all, k_all, v_all, ctx_len, need_ctx):
    b, s, _ = q_all.shape
    q_off = 0 if need_ctx else ctx_len // TQ
    nq = s // TQ - q_off
    return pl.pallas_call(
        functools.partial(_attn_kernel, ctx_len, q_off),
        grid=(b, N_QHEADS, nq),
        in_specs=[pl.BlockSpec((None, TQ, SLOT), lambda i, h, j: (i, j + q_off, h)),
                  pl.BlockSpec((None, s, SLOT), lambda i, h, j: (i, 0, _kv_slot(h))),
                  pl.BlockSpec((None, s, SLOT), lambda i, h, j: (i, 0, _kv_slot(h)))],
        out_specs=pl.BlockSpec((None, TQ, SLOT), lambda i, h, j: (i, j + q_off, h)),
        out_shape=jax.ShapeDtypeStruct((b, s, N_QHEADS * SLOT), BF16),
        compiler_params=_cparams(("parallel", "parallel", "parallel")),
        name="attention",
    )(q_all, k_all, v_all)


def _mixout_kernel(ctx_len, blk_off, x_ref, mod_ref, hf_ref, hb_ref, mo_ref, oa_ref, mn_ref, w_ref, o_ref):
    tm = x_ref.shape[0]
    t0 = (pl.program_id(1) + blk_off) * tm
    hm = hf_ref[...].astype(F32) + hb_ref[...].astype(F32)
    gate = _sigmoid(mo_ref[...].astype(F32))
    parts = []
    for i in range(M_HEADS):
        x = hm[:, i * SLOT:(i + 1) * SLOT]
        inv = lax.rsqrt(jnp.sum(x * x, axis=-1, keepdims=True) * (1.0 / M_DH) + EPS)
        parts.append(x * inv)
    hn = jnp.concatenate(parts, axis=-1) * mn_ref[...] * gate
    nm = M_HEADS * SLOT
    o = jnp.dot(hn.astype(BF16), w_ref[0:nm, :], preferred_element_type=F32)
    o = o + jnp.dot(oa_ref[...], w_ref[nm:, :], preferred_element_type=F32)
    o_ref[...] = x_ref[...] + _gate_rows(mod_ref, t0, tm, ctx_len, 2) * o


def _mixout(xs, modtab, hf, hb, mo, o_attn, m_norm_p, w_out_p, ctx_len, blk_off):
    b, s, d = xs.shape
    tok = lambda n: pl.BlockSpec((None, TB, n), lambda i, j: (i, j + blk_off, 0))
    full = lambda a: pl.BlockSpec(a.shape, lambda i, j: (0,) * a.ndim)
    return pl.pallas_call(
        functools.partial(_mixout_kernel, ctx_len, blk_off),
        grid=(b, s // TB - blk_off),
        in_specs=[tok(d),
                  pl.BlockSpec((None, 2, 6, d), lambda i, j: (i, 0, 0, 0)),
                  tok(M_HEADS * SLOT), tok(M_HEADS * SLOT), tok(M_HEADS * SLOT), tok(N_QHEADS * SLOT),
                  full(m_norm_p), full(w_out_p)],
        out_specs=tok(d),
        out_shape=jax.ShapeDtypeStruct((b, s, d), F32),
        input_output_aliases={0: 0},
        compiler_params=_cparams(("parallel", "parallel")),
        name="mix_out",
    )(xs, modtab, hf, hb, mo, o_attn, m_norm_p, w_out_p)


def _ffn_kernel(ctx_len, x_ref, mod_ref, g_ref, wg_ref, wu_ref, wd_ref, o_ref, h_sc, acc_sc):
    tm = x_ref.shape[0]
    t0 = pl.program_id(1) * tm
    f = pl.program_id(2)

    @pl.when(f == 0)
    def _():
        h_sc[...] = _modulated(x_ref[...], g_ref[...], mod_ref, t0, ctx_len, 3, 4).astype(BF16)
        acc_sc[...] = jnp.zeros_like(acc_sc)

    h = h_sc[...]
    a = jnp.dot(h, wg_ref[...], preferred_element_type=F32)
    u = jnp.dot(h, wu_ref[...], preferred_element_type=F32)
    acc_sc[...] += jnp.dot((_silu(a) * u).astype(BF16), wd_ref[...], preferred_element_type=F32)

    @pl.when(f == pl.num_programs(2) - 1)
    def _():
        o_ref[...] = x_ref[...] + _gate_rows(mod_ref, t0, tm, ctx_len, 5) * acc_sc[...]


def _ffn(xs, modtab, gain, wg, wu, wd, ctx_len):
    b, s, d = xs.shape
    tm = s // 4
    nf = wg.shape[1] // TF
    return pl.pallas_call(
        functools.partial(_ffn_kernel, ctx_len),
        grid=(b, s // tm, nf),
        in_specs=[pl.BlockSpec((None, tm, d), lambda i, j, f: (i, j, 0)),
                  pl.BlockSpec((None, 2, 6, d), lambda i, j, f: (i, 0, 0, 0)),
                  pl.BlockSpec((1, d), lambda i, j, f: (0, 0)),
                  pl.BlockSpec((d, TF), lambda i, j, f: (0, f)),
                  pl.BlockSpec((d, TF), lambda i, j, f: (0, f)),
                  pl.BlockSpec((TF, d), lambda i, j, f: (f, 0))],
        out_specs=pl.BlockSpec((None, tm, d), lambda i, j, f: (i, j, 0)),
        out_shape=jax.ShapeDtypeStruct((b, s, d), F32),
        scratch_shapes=[pltpu.VMEM((tm, d), BF16), pltpu.VMEM((tm, d), F32)],
        input_output_aliases={0: 0},
        compiler_params=_cparams(("parallel", "parallel", "arbitrary")),
        name="ffn_dense",
    )(xs, modtab, gain, wg, wu, wd)


def _top2_combine(logits):
    lane = _lane_iota(logits.shape)
    lane_f = lane.astype(F32)
    lg = jnp.where(lane < N_EXPERTS, logits, -jnp.inf)
    v1 = jnp.max(lg, axis=-1, keepdims=True)
    i1 = jnp.min(jnp.where(lg == v1, lane_f, float(LANE)), axis=-1, keepdims=True)
    rest = jnp.where(lane_f == i1, -jnp.inf, lg)
    v2 = jnp.max(rest, axis=-1, keepdims=True)
    i2 = jnp.min(jnp.where(rest == v2, lane_f, float(LANE)), axis=-1, keepdims=True)
    e2 = jnp.exp(v2 - v1)
    w1 = 1.0 / (1.0 + e2)
    w2 = e2 / (1.0 + e2)
    return jnp.where(lane_f == i1, w1, 0.0) + jnp.where(lane_f == i2, w2, 0.0)


def _moe_kernel(ctx_len, x_ref, mod_ref, g_ref, r_ref, wg_ref, wu_ref, wd_ref, o_ref,
                h_sc, comb_sc, cw_sc, acc_sc):
    tm = x_ref.shape[0]
    t0 = pl.program_id(1) * tm
    e = pl.program_id(2)
    f = pl.program_id(3)

    @pl.when(jnp.logical_and(e == 0, f == 0))
    def _():
        h = _modulated(x_ref[...], g_ref[...], mod_ref, t0, ctx_len, 3, 4)
        logits = jnp.dot(h, r_ref[...], preferred_element_type=F32, precision=lax.Precision.HIGHEST)
        comb_sc[...] = _top2_combine(logits)
        h_sc[...] = h.astype(BF16)
        acc_sc[...] = jnp.zeros_like(acc_sc)

    @pl.when(f == 0)
    def _():
        comb = comb_sc[...]
        cw_sc[...] = jnp.sum(jnp.where(_lane_iota(comb.shape) == e, comb, 0.0), axis=-1, keepdims=True)

    h = h_sc[...]
    a = jnp.dot(h, wg_ref[...], preferred_element_type=F32)
    u = jnp.dot(h, wu_ref[...], preferred_element_type=F32)
    hid = (_silu(a) * u * cw_sc[...]).astype(BF16)
    acc_sc[...] += jnp.dot(hid, wd_ref[...], preferred_element_type=F32)

    @pl.when(jnp.logical_and(e == pl.num_programs(2) - 1, f == pl.num_programs(3) - 1))
    def _():
        o_ref[...] = x_ref[...] + _gate_rows(mod_ref, t0, tm, ctx_len, 5) * acc_sc[...]


def _moe(xs, modtab, gain, router_p, wg, wu, wd, ctx_len):
    b, s, d = xs.shape
    tm = s // 4
    ne, _, dff = wg.shape
    return pl.pallas_call(
        functools.partial(_moe_kernel, ctx_len),
        grid=(b, s // tm, ne, dff // TF),
        in_specs=[pl.BlockSpec((None, tm, d), lambda i, j, e, f: (i, j, 0)),
                  pl.BlockSpec((None, 2, 6, d), lambda i, j, e, f: (i, 0, 0, 0)),
                  pl.BlockSpec((1, d), lambda i, j, e, f: (0, 0)),
                  pl.BlockSpec((d, LANE), lambda i, j, e, f: (0, 0)),
                  pl.BlockSpec((None, d, TF), lambda i, j, e, f: (e, 0, f)),
                  pl.BlockSpec((None, d, TF), lambda i, j, e, f: (e, 0, f)),
                  pl.BlockSpec((None, TF, d), lambda i, j, e, f: (e, f, 0))],
        out_specs=pl.BlockSpec((None, tm, d), lambda i, j, e, f: (i, j, 0)),
        out_shape=jax.ShapeDtypeStruct((b, s, d), F32),
        scratch_shapes=[pltpu.VMEM((tm, d), BF16), pltpu.VMEM((tm, LANE), F32),
                        pltpu.VMEM((tm, 1), F32), pltpu.VMEM((tm, d), F32)],
        input_output_aliases={0: 0},
        compiler_params=_cparams(("parallel", "parallel", "arbitrary", "arbitrary")),
        name="moe_dense",
    )(xs, modtab, gain, router_p, wg, wu, wd)


def _pad_heads(w, n_heads, dh, axis=-1):
    axis = axis % w.ndim
    shp = w.shape[:axis] + (n_heads, dh) + w.shape[axis + 1:]
    pad = [(0, 0)] * (w.ndim + 1)
    pad[axis + 1] = (0, SLOT - dh)
    out = jnp.pad(w.reshape(shp), pad)
    return out.reshape(w.shape[:axis] + (n_heads * SLOT,) + w.shape[axis + 1:])


def _rope_tables(seq, ctx_len):
    t = jnp.arange(seq)
    rows = (t // GRID_W).astype(F32)
    cols = (t % GRID_W).astype(F32)

    def angles(rot_dim):
        nf = rot_dim // 4
        inv = ROPE_THETA ** (-jnp.arange(nf, dtype=F32) / nf)
        ar = rows[:, None] * inv
        ac = cols[:, None] * inv
        return jnp.concatenate([ar, ar, ac, ac], axis=-1)

    def slot_tables(rot_dim, lane0):
        ang = angles(rot_dim)
        quarter = rot_dim // 4
        first = (jnp.arange(rot_dim) % (2 * quarter)) < quarter
        cos = jnp.ones((seq, SLOT), F32).at[:, lane0:lane0 + rot_dim].set(jnp.cos(ang))
        sin = jnp.sin(ang)
        sin_m = jnp.zeros((seq, SLOT), F32).at[:, lane0:lane0 + rot_dim].set(jnp.where(first, -sin, 0.0))
        sin_p = jnp.zeros((seq, SLOT), F32).at[:, lane0:lane0 + rot_dim].set(jnp.where(first, 0.0, sin))
        ident = jnp.stack([jnp.ones((ctx_len, SLOT), F32), jnp.zeros((ctx_len, SLOT), F32),
                           jnp.zeros((ctx_len, SLOT), F32)])
        return jnp.concatenate([ident, jnp.stack([cos, sin_m, sin_p])], axis=1)

    return jnp.concatenate([slot_tables(A_DH, 0), slot_tables(L_ROPE, L_NOPE)], axis=0)


def _layer_params(w_in, m_conv, m_gate_b, m_norm, a_qnorm, a_knorm, l_cq_norm, l_ckv_norm, l_wuq, l_wukv,
                  l_qnorm, l_knorm, w_out):
    d = w_in.shape[0]
    offs = np.cumsum((0,) + IN_SIZES)
    seg = [w_in[:, offs[i]:offs[i + 1]] for i in range(len(IN_SIZES))]
    misc = jnp.zeros((d, SLOT), F32).at[:, 0:4 * M_HEADS].set(seg[4]).at[:, MISC_KR:MISC_KR + L_ROPE].set(seg[10])
    w_in_p = jnp.concatenate(
        [_pad_heads(seg[0], M_HEADS, M_DH), _pad_heads(seg[1], M_HEADS, M_DH),
         _pad_heads(seg[2], M_HEADS, M_DH), _pad_heads(seg[3], M_HEADS, M_DH), misc,
         _pad_heads(seg[5], A_HEADS, A_DH), _pad_heads(seg[6], A_KV, A_DH), _pad_heads(seg[7], A_KV, A_DH),
         seg[8], seg[9]], axis=1).astype(BF16)
    conv_w = jnp.concatenate([_pad_heads(m_conv[:, :M_HEADS * M_DH], M_HEADS, M_DH),
                              _pad_heads(m_conv[:, M_HEADS * M_DH:], M_HEADS, M_DH)], axis=1)
    conv_scale = jnp.concatenate([jnp.ones((1, M_HEADS * SLOT), F32),
                                  jnp.full((1, M_HEADS * SLOT), M_DH ** -0.5, F32)], axis=1)
    gate_b = jnp.zeros((1, SLOT), F32).at[0, 0:4 * M_HEADS].set(m_gate_b)
    pad1 = lambda g: jnp.pad(g, (0, SLOT - g.shape[0]))
    an = jnp.stack([pad1(a_qnorm), pad1(a_knorm)])
    ln = jnp.stack([pad1(l_qnorm), pad1(l_knorm)])
    wuq_p = _pad_heads(l_wuq, L_HEADS, L_QK).astype(BF16)
    kv = l_wukv.reshape(L_KVRANK, L_HEADS, L_NOPE + L_DV)
    wukv_p = jnp.concatenate(
        [_pad_heads(kv[:, :, :L_NOPE].reshape(L_KVRANK, -1), L_HEADS, L_NOPE),
         _pad_heads(kv[:, :, L_NOPE:].reshape(L_KVRANK, -1), L_HEADS, L_DV)], axis=1).astype(BF16)
    nm, na = M_HEADS * M_DH, A_HEADS * A_DH
    w_out_p = jnp.concatenate(
        [_pad_heads(w_out[:nm], M_HEADS, M_DH, axis=0), _pad_heads(w_out[nm:nm + na], A_HEADS, A_DH, axis=0),
         _pad_heads(w_out[nm + na:], L_HEADS, L_DV, axis=0)], axis=0).astype(BF16)
    m_norm_p = _pad_heads(m_norm[None, :], M_HEADS, M_DH)
    return dict(w_in_p=w_in_p, conv_w=conv_w, conv_scale=conv_scale, gate_b=gate_b, an=an, ln=ln,
                cqn=l_cq_norm[None, :], ckvn=l_ckv_norm[None, :], wuq_p=wuq_p, wukv_p=wukv_p,
                w_out_p=w_out_p, m_norm_p=m_norm_p)


def kernel(x, c, ctx, c_ctx, mod_w, mod_b, norm_mix, norm_ffn, w_in, m_conv, m_gate_b, m_norm, a_qnorm, a_knorm,
           l_cq_norm, l_ckv_norm, l_wuq, l_wukv, l_qnorm, l_knorm, w_out, ffn_wg, ffn_wu, ffn_wd, moe_router,
           moe_wg, moe_wu, moe_wd):
    b, seq, d = x.shape
    ctx_len = ctx.shape[1]
    depth = mod_w.shape[0]
    assert ctx_len % TB == 0 and ctx_len % TQ == 0 and ctx_len == ML and seq % TK == 0 and seq % GRID_W == 0
    xs = jnp.concatenate([ctx, x], axis=1)
    mod_rows = 16
    cc = jnp.zeros((mod_rows, d), F32).at[:b].set(c).at[b].set(c_ctx)
    mod_all = _mod_table(cc, mod_w, mod_b)
    rope_tab = _rope_tables(seq, ctx_len)
    for i in range(depth):
        need_ctx = i < depth - 1
        lat = mod_all[i, :b].reshape(b, 1, 6, d)
        cm = jnp.broadcast_to(mod_all[i, b].reshape(1, 1, 6, d), (b, 1, 6, d))
        modtab = jnp.concatenate([cm, lat], axis=1)
        p = _layer_params(w_in[i], m_conv[i], m_gate_b[i], m_norm[i], a_qnorm[i], a_knorm[i], l_cq_norm[i],
                          l_ckv_norm[i], l_wuq[i], l_wukv[i], l_qnorm[i], l_knorm[i], w_out[i])
        mqk, mv, mo, misc, q_all, k_all, v_all = _inproj(
            xs, modtab, norm_mix[i][None, :], p["w_in_p"], rope_tab, p["an"], p["ln"], p["cqn"], p["ckvn"],
            p["wuq_p"], p["wukv_p"], ctx_len)
        qk_c = _conv(mqk, p["conv_w"], p["conv_scale"], ctx_len)
        hf, hb = _mlstm(qk_c, mv, misc, p["gate_b"])
        o_attn = _attention(q_all, k_all, v_all, ctx_len, need_ctx)
        xs = _mixout(xs, modtab, hf, hb, mo, o_attn, p["m_norm_p"], p["w_out_p"], ctx_len,
                     0 if need_ctx else ctx_len // TB)
        j = i // 2
        if i % 2 == 0:
            xs = _ffn(xs, modtab, norm_ffn[i][None, :], ffn_wg[j].astype(BF16), ffn_wu[j].astype(BF16),
                      ffn_wd[j].astype(BF16), ctx_len)
        else:
            router_p = jnp.pad(moe_router[j], ((0, 0), (0, LANE - N_EXPERTS)))
            xs = _moe(xs, modtab, norm_ffn[i][None, :], router_p, moe_wg[j].astype(BF16),
                      moe_wu[j].astype(BF16), moe_wd[j].astype(BF16), ctx_len)
    return xs[:, ctx_len:]
```

```python
import functools
import math

import numpy as np
import jax
import jax.numpy as jnp
from jax import lax
from jax.experimental import pallas as pl
from jax.experimental.pallas import tpu as pltpu

F32 = jnp.float32
BF16 = jnp.bfloat16

GRID_W = 64
EPS = 1e-6
ROPE_THETA = 10000.0
M_HEADS, M_DH = 4, 96
A_HEADS, A_KV, A_DH = 6, 2, 64
L_HEADS, L_NOPE, L_ROPE, L_DV = 4, 64, 32, 64
L_QK = L_NOPE + L_ROPE
L_QRANK, L_KVRANK = 256, 128
N_EXPERTS, TOP_K = 8, 2
IN_SIZES = (384, 384, 384, 384, 16, 384, 128, 128, 256, 128, 32)

LANE = 128
SLOT = LANE
N_QHEADS = A_HEADS + L_HEADS
N_KVHEADS = A_KV + L_HEADS
VMEM_LIMIT = 56 * 1024 * 1024

O_MQK, O_MV, O_MO, O_MISC = 0, 1024, 1536, 2048
O_QA, O_KA, O_VA, O_CQ, O_CKV, IN_PAD = 2176, 2944, 3200, 3456, 3712, 3840
MISC_KR = 64
DEN_LANE = M_DH
SUM_LANE = A_DH
LOG2E = math.log2(math.e)
assert A_DH == L_DV and SUM_LANE < SLOT

TB = 256
TQ = 256
TK = 512
TM_FFN = 1088
TF = 512
ML = 256
RT = 128


def _cparams(sem):
    return pltpu.CompilerParams(dimension_semantics=sem, vmem_limit_bytes=VMEM_LIMIT)


def _sigmoid(x):
    return 1.0 / (1.0 + jnp.exp(-x))


def _silu(x):
    return x * _sigmoid(x)


def _lane_iota(shape):
    return lax.broadcasted_iota(jnp.int32, shape, len(shape) - 1)


def _row_iota(shape):
    return lax.broadcasted_iota(jnp.int32, shape, 0)


def _modulated(x, gain, mod_ref, t0, ctx_len, k_shift, k_scale):
    tm = x.shape[0]
    is_ctx = (t0 + _row_iota((tm, 1))) < ctx_len
    shift = jnp.where(is_ctx, mod_ref[0, k_shift:k_shift + 1, :], mod_ref[1, k_shift:k_shift + 1, :])
    scale = jnp.where(is_ctx, mod_ref[0, k_scale:k_scale + 1, :], mod_ref[1, k_scale:k_scale + 1, :])
    y = x * lax.rsqrt(jnp.mean(x * x, axis=-1, keepdims=True) + EPS) * gain
    return y * (1.0 + scale) + shift


def _gate_rows(mod_ref, t0, tm, ctx_len, k_gate):
    is_ctx = (t0 + _row_iota((tm, 1))) < ctx_len
    return jnp.where(is_ctx, mod_ref[0, k_gate:k_gate + 1, :], mod_ref[1, k_gate:k_gate + 1, :])


def _mod_kernel(c_ref, w_ref, b_ref, o_ref):
    s = _silu(c_ref[...]).astype(BF16)
    o_ref[...] = jnp.dot(s, w_ref[...].astype(BF16), preferred_element_type=F32) + b_ref[...]


def _mod_table(cc, mod_w, mod_b):
    depth, d, n = mod_w.shape
    rows = cc.shape[0]
    return pl.pallas_call(
        _mod_kernel,
        grid=(depth, n // d),
        in_specs=[pl.BlockSpec((rows, d), lambda l, j: (0, 0)),
                  pl.BlockSpec((None, d, d), lambda l, j: (l, 0, j)),
                  pl.BlockSpec((None, 1, d), lambda l, j: (l, 0, j))],
        out_specs=pl.BlockSpec((None, rows, d), lambda l, j: (l, 0, j)),
        out_shape=jax.ShapeDtypeStruct((depth, rows, n), F32),
        compiler_params=_cparams(("arbitrary", "arbitrary")),
        name="mod_table",
    )(cc, mod_w, mod_b.reshape(depth, 1, n))


def _slot_rms(x, lo, hi):
    lane = _lane_iota(x.shape)
    sq = jnp.where((lane >= lo) & (lane < hi), x * x, 0.0)
    return lax.rsqrt(jnp.sum(sq, axis=-1, keepdims=True) * (1.0 / (hi - lo)) + EPS)


def _rope(x, cos, sin_m, sin_p, quarter):
    return (x * cos + pltpu.roll(x, LANE - quarter, 1) * sin_m + pltpu.roll(x, quarter, 1) * sin_p)


def _inproj_kernel(ctx_len, x_ref, mod_ref, g_ref, w_ref, rope_ref, an_ref, ln_ref, cqn_ref, ckvn_ref,
                   wuq_ref, wukv_ref, mqk_ref, mv_ref, mo_ref, misc_ref, q_ref, k_ref, v_ref):
    tm = x_ref.shape[0]
    t0 = pl.program_id(1) * tm
    h = _modulated(x_ref[...], g_ref[...], mod_ref, t0, ctx_len, 0, 1).astype(BF16)

    def proj(a, b):
        return jnp.dot(h, w_ref[:, a:b], preferred_element_type=F32)

    mqk_ref[...] = proj(O_MQK, O_MV).astype(BF16)
    mv_ref[...] = proj(O_MV, O_MO).astype(BF16)
    mo_ref[...] = proj(O_MO, O_MISC).astype(BF16)
    misc = proj(O_MISC, O_QA)
    misc_ref[...] = misc

    cos_a, sinm_a, sinp_a = rope_ref[0], rope_ref[1], rope_ref[2]
    cos_l, sinm_l, sinp_l = rope_ref[3], rope_ref[4], rope_ref[5]
    a_scale = A_DH ** -0.5 * LOG2E
    l_scale = L_QK ** -0.5 * LOG2E

    def with_ones(v):
        return jnp.where(_lane_iota(v.shape) % SLOT == SUM_LANE, 1.0, v).astype(BF16)

    pa = proj(O_QA, O_CQ)
    gq, gk = an_ref[0:1, :], an_ref[1:2, :]
    for i in range(A_HEADS):
        x = pa[:, i * SLOT:(i + 1) * SLOT]
        x = x * _slot_rms(x, 0, A_DH) * gq
        q_ref[:, i * SLOT:(i + 1) * SLOT] = (_rope(x, cos_a, sinm_a, sinp_a, A_DH // 4) * a_scale).astype(BF16)
    for i in range(A_KV):
        x = pa[:, (A_HEADS + i) * SLOT:(A_HEADS + i + 1) * SLOT]
        x = x * _slot_rms(x, 0, A_DH) * gk
        k_ref[:, i * SLOT:(i + 1) * SLOT] = _rope(x, cos_a, sinm_a, sinp_a, A_DH // 4).astype(BF16)
    v_ref[:, 0:A_KV * SLOT] = with_ones(pa[:, (A_HEADS + A_KV) * SLOT:(A_HEADS + 2 * A_KV) * SLOT])

    pc = proj(O_CQ, IN_PAD)
    cq = pc[:, 0:L_QRANK]
    cq = (cq * lax.rsqrt(jnp.mean(cq * cq, axis=-1, keepdims=True) + EPS) * cqn_ref[...]).astype(BF16)
    ckv = pc[:, L_QRANK:L_QRANK + L_KVRANK]
    ckv = (ckv * lax.rsqrt(jnp.mean(ckv * ckv, axis=-1, keepdims=True) + EPS) * ckvn_ref[...]).astype(BF16)
    ql = jnp.dot(cq, wuq_ref[...], preferred_element_type=F32)
    kvl = jnp.dot(ckv, wukv_ref[...], preferred_element_type=F32)
    gq_l, gk_l = ln_ref[0:1, :], ln_ref[1:2, :]
    lane = _lane_iota((tm, SLOT))
    kr = jnp.where((lane >= MISC_KR) & (lane < MISC_KR + L_ROPE), misc, 0.0)
    kr = kr * _slot_rms(kr, MISC_KR, MISC_KR + L_ROPE) * gk_l
    kr = _rope(kr, cos_l, sinm_l, sinp_l, L_ROPE // 4)
    for i in range(L_HEADS):
        x = ql[:, i * SLOT:(i + 1) * SLOT]
        inv = jnp.where(lane < L_NOPE, _slot_rms(x, 0, L_NOPE), _slot_rms(x, L_NOPE, L_QK))
        x = x * inv * gq_l
        q_ref[:, (A_HEADS + i) * SLOT:(A_HEADS + i + 1) * SLOT] = (
            _rope(x, cos_l, sinm_l, sinp_l, L_ROPE // 4) * l_scale).astype(BF16)
        kn = kvl[:, i * SLOT:(i + 1) * SLOT]
        kn = kn * _slot_rms(kn, 0, L_NOPE) * gk_l
        k_ref[:, (A_KV + i) * SLOT:(A_KV + i + 1) * SLOT] = (kn + kr).astype(BF16)
    v_ref[:, A_KV * SLOT:] = with_ones(kvl[:, L_HEADS * SLOT:])


def _inproj(xs, modtab, gain, w_in_p, rope_tab, an, ln, cqn, ckvn, wuq_p, wukv_p, ctx_len):
    b, s, d = xs.shape
    grid = (b, s // TB)
    tok = lambda n: pl.BlockSpec((None, TB, n), lambda i, j: (i, j, 0))
    full = lambda a: pl.BlockSpec(a.shape, lambda i, j: (0,) * a.ndim)
    out_widths = (1024, 512, 512, SLOT, N_QHEADS * SLOT, N_KVHEADS * SLOT, N_KVHEADS * SLOT)
    out_dtypes = (BF16, BF16, BF16, F32, BF16, BF16, BF16)
    return pl.pallas_call(
        functools.partial(_inproj_kernel, ctx_len),
        grid=grid,
        in_specs=[tok(d),
                  pl.BlockSpec((None, 2, 6, d), lambda i, j: (i, 0, 0, 0)),
                  full(gain), full(w_in_p),
                  pl.BlockSpec((6, TB, SLOT), lambda i, j: (0, j, 0)),
                  full(an), full(ln), full(cqn), full(ckvn), full(wuq_p), full(wukv_p)],
        out_specs=[tok(n) for n in out_widths],
        out_shape=[jax.ShapeDtypeStruct((b, s, n), dt) for n, dt in zip(out_widths, out_dtypes)],
        compiler_params=_cparams(("parallel", "parallel")),
        name="inproj",
    )(xs, modtab, gain, w_in_p, rope_tab, an, ln, cqn, ckvn, wuq_p, wukv_p)


def _conv_kernel(ctx_len, s_len, x_ref, prev_ref, next_ref, w_ref, sc_ref, o_ref):
    tm = x_ref.shape[0]
    t0 = pl.program_id(1) * tm
    x = x_ref[...].astype(F32)
    row = _row_iota((tm, 1))
    has_prev = jnp.logical_and(t0 != 0, t0 != ctx_len)
    has_next = jnp.logical_and(t0 + tm != ctx_len, t0 + tm != s_len)
    hp = jnp.where(has_prev, prev_ref[15:16, :].astype(F32), 0.0)
    hn = jnp.where(has_next, next_ref[0:1, :].astype(F32), 0.0)
    xp = jnp.where(row == 0, hp, pltpu.roll(x, 1, 0))
    xn = jnp.where(row == tm - 1, hn, pltpu.roll(x, tm - 1, 0))
    y = xp * w_ref[0:1, :] + x * w_ref[1:2, :] + xn * w_ref[2:3, :]
    o_ref[...] = (_silu(y) * sc_ref[...]).astype(o_ref.dtype)


def _conv(mqk, conv_w, conv_scale, ctx_len):
    b, s, n = mqk.shape
    hb = TB // 16
    last = s // 16 - 1
    return pl.pallas_call(
        functools.partial(_conv_kernel, ctx_len, s),
        grid=(b, s // TB),
        in_specs=[pl.BlockSpec((None, TB, n), lambda i, j: (i, j, 0)),
                  pl.BlockSpec((None, 16, n), lambda i, j: (i, jnp.maximum(j * hb - 1, 0), 0)),
                  pl.BlockSpec((None, 16, n), lambda i, j: (i, jnp.minimum((j + 1) * hb, last), 0)),
                  pl.BlockSpec((3, n), lambda i, j: (0, 0)),
                  pl.BlockSpec((1, n), lambda i, j: (0, 0))],
        out_specs=pl.BlockSpec((None, TB, n), lambda i, j: (i, j, 0)),
        out_shape=jax.ShapeDtypeStruct((b, s, n), BF16),
        compiler_params=_cparams(("parallel", "parallel")),
        name="mlstm_conv",
    )(mqk, mqk, mqk, conv_w, conv_scale)


def _log_sigmoid(x):
    return jnp.minimum(x, 0.0) - jnp.log(1.0 + jnp.exp(-jnp.abs(x)))


def _mlstm_kernel(n_chunks, q_ref, k_ref, v_ref, misc_ref, gb_ref, hf_ref, hb_ref, c_ref, m_ref):
    hp = pl.program_id(1)
    c_ref[...] = jnp.zeros_like(c_ref)
    m_ref[...] = jnp.zeros_like(m_ref)
    r = _row_iota((ML, ML))
    cidx = _lane_iota((ML, ML))
    tri_f = (r >= cidx)
    tri_b = (r <= cidx)
    tri_f32 = tri_f.astype(F32)
    tri_b32 = tri_b.astype(F32)
    lane = _lane_iota((ML, SLOT))

    def chunk(t0, d, tri, tri32, out_ref):
        g = misc_ref[pl.ds(t0, ML), :] + gb_ref[...]
        logf = _log_sigmoid(g)
        bcum = jnp.dot(tri32, logf, preferred_element_type=F32, precision=lax.Precision.HIGHEST)
        bcum_t = bcum.T
        g_t = g.T
        for hh in range(2):
            head = hp * 2 + hh
            li = d * 2 * M_HEADS + head
            sel_i = (lane == li)
            sel_f = (lane == li + M_HEADS)
            i_col = jnp.sum(jnp.where(sel_i, g, 0.0), axis=-1, keepdims=True)
            b_col = jnp.sum(jnp.where(sel_f, bcum, 0.0), axis=-1, keepdims=True)
            rsel_i = (_row_iota((SLOT, ML)) == li)
            rsel_f = (_row_iota((SLOT, ML)) == li + M_HEADS)
            i_row = jnp.sum(jnp.where(rsel_i, g_t, 0.0), axis=0, keepdims=True)
            b_row = jnp.sum(jnp.where(rsel_f, bcum_t, 0.0), axis=0, keepdims=True)
            sidx = d * 2 + hh
            c_st = c_ref[sidx]
            m_st = m_ref[sidx]
            m_s = m_st[:, 0:1]
            q = q_ref[pl.ds(t0, ML), hh * SLOT:(hh + 1) * SLOT]
            k = k_ref[pl.ds(t0, ML), hh * SLOT:(hh + 1) * SLOT]
            v = v_ref[pl.ds(t0, ML), hh * SLOT:(hh + 1) * SLOT]
            v = jnp.where(lane == DEN_LANE, 1.0, v.astype(F32)).astype(BF16)
            dmat = jnp.where(tri, b_col - b_row + i_row, -jnp.inf)
            m_inter = b_col + m_s
            m_t = jnp.maximum(m_inter, jnp.max(dmat, axis=-1, keepdims=True))
            qk = lax.dot_general(q, k, (((1,), (1,)), ((), ())), preferred_element_type=F32)
            w = jnp.exp(dmat - m_t) * qk
            a_inter = jnp.exp(m_inter - m_t)
            num = a_inter * jnp.dot(q, c_st.astype(BF16), preferred_element_type=F32) + jnp.dot(
                w.astype(BF16), v, preferred_element_type=F32)
            den = jnp.sum(jnp.where(lane == DEN_LANE, num, 0.0), axis=-1, keepdims=True)
            h_out = num / jnp.maximum(jnp.abs(den), jnp.exp(-m_t))
            out_ref[pl.ds(t0, ML), hh * SLOT:(hh + 1) * SLOT] = h_out.astype(out_ref.dtype)
            total = b_col[ML - 1:ML, :] if d == 0 else b_col[0:1, :]
            gg = total - b_col + i_col
            m_new = jnp.maximum(total + m_s, jnp.max(gg, axis=0, keepdims=True))
            decay = jnp.exp(total + m_s - m_new)
            wk = jnp.exp(gg - m_new)
            kw = (k.astype(F32) * wk).astype(BF16)
            c_ref[sidx] = decay * c_st + lax.dot_general(
                kw, v, (((0,), (0,)), ((), ())), preferred_element_type=F32)
            m_ref[sidx] = jnp.broadcast_to(m_new, (1, SLOT))

    def step(n, carry):
        tf0 = pl.multiple_of(n * ML, ML)
        tb0 = pl.multiple_of(jnp.where(n == 0, 0, n_chunks - n) * ML, ML)
        chunk(tf0, 0, tri_f, tri_f32, hf_ref)
        chunk(tb0, 1, tri_b, tri_b32, hb_ref)
        return carry

    lax.fori_loop(0, n_chunks, step, 0)


def _mlstm(qk_c, mv, misc, gate_b):
    b, s, _ = mv.shape
    n_chunks = s // ML
    pair = 2 * SLOT
    kspec = pl.BlockSpec((None, s, pair), lambda i, j: (i, 0, 2 + j))
    spec = pl.BlockSpec((None, s, pair), lambda i, j: (i, 0, j))
    return pl.pallas_call(
        functools.partial(_mlstm_kernel, n_chunks),
        grid=(b, M_HEADS // 2),
        in_specs=[spec, kspec, spec,
                  pl.BlockSpec((None, s, SLOT), lambda i, j: (i, 0, 0)),
                  pl.BlockSpec((1, SLOT), lambda i, j: (0, 0))],
        out_specs=[spec, spec],
        out_shape=[jax.ShapeDtypeStruct((b, s, M_HEADS * SLOT), BF16)] * 2,
        scratch_shapes=[pltpu.VMEM((4, SLOT, SLOT), F32), pltpu.VMEM((4, 1, SLOT), F32)],
        compiler_params=_cparams(("parallel", "parallel")),
        name="mlstm_scan",
    )(qk_c, qk_c, mv, misc, gate_b)


def _attn_kernel(ctx_len, q_off, n_heads, shared_kv, q_ref, k_ref, v_ref, o_ref):
    s_len = k_ref.shape[0]
    qs = [q_ref[:, g * SLOT:(g + 1) * SLOT] for g in range(n_heads)]

    def kv_cols(g):
        return slice(0, SLOT) if shared_kv else slice(g * SLOT, (g + 1) * SLOT)

    def chunk(carry, rows):
        out = []
        for g in range(n_heads):
            m, acc = carry[g]
            k = k_ref[rows, kv_cols(g)]
            v = v_ref[rows, kv_cols(g)]
            sc = lax.dot_general(qs[g], k, (((1,), (1,)), ((), ())), preferred_element_type=F32)
            m_new = jnp.maximum(m, jnp.max(sc, axis=-1, keepdims=True))
            alpha = jnp.exp2(m - m_new)
            p = jnp.exp2(sc - m_new).astype(BF16)
            out.append((m_new, alpha * acc + jnp.dot(p, v, preferred_element_type=F32)))
        return tuple(out)

    def finish(carry):
        for g in range(n_heads):
            acc = carry[g][1]
            o_ref[:, g * SLOT:(g + 1) * SLOT] = (acc / acc[:, SUM_LANE:SUM_LANE + 1]).astype(o_ref.dtype)

    init = tuple((jnp.full((TQ, 1), -jnp.inf, F32), jnp.zeros((TQ, SLOT), F32)) for _ in range(n_heads))
    c0 = chunk(init, slice(0, ctx_len))
    is_ctx = (pl.program_id(2) + q_off) * TQ < ctx_len

    @pl.when(is_ctx)
    def _():
        finish(c0)

    @pl.when(jnp.logical_not(is_ctx))
    def _():
        def body(c, carry):
            t0 = pl.multiple_of(ctx_len + c * TK, math.gcd(ctx_len, TK))
            return chunk(carry, pl.ds(t0, TK))

        finish(lax.fori_loop(0, (s_len - ctx_len) // TK, body, c0))


def _attention(q_all, k_all, v_all, ctx_len, need_ctx, q_slot0, kv_slot0, n_groups, n_heads, shared_kv):
    b, s, _ = q_all.shape
    q_off = 0 if need_ctx else ctx_len // TQ
    nq = s // TQ - q_off
    qw = n_heads * SLOT
    kw = SLOT if shared_kv else qw
    assert (q_slot0 * SLOT) % qw == 0 and (kv_slot0 * SLOT) % kw == 0
    qb0, kb0 = q_slot0 * SLOT // qw, kv_slot0 * SLOT // kw
    return pl.pallas_call(
        functools.partial(_attn_kernel, ctx_len, q_off, n_heads, shared_kv),
        grid=(b, n_groups, nq),
        in_specs=[pl.BlockSpec((None, TQ, qw), lambda i, g, j: (i, j + q_off, qb0 + g)),
                  pl.BlockSpec((None, s, kw), lambda i, g, j: (i, 0, kb0 + g)),
                  pl.BlockSpec((None, s, kw), lambda i, g, j: (i, 0, kb0 + g))],
        out_specs=pl.BlockSpec((None, TQ, qw), lambda i, g, j: (i, j + q_off, g)),
        out_shape=jax.ShapeDtypeStruct((b, s, n_groups * qw), BF16),
        compiler_params=_cparams(("parallel", "parallel", "parallel")),
        name="attention",
    )(q_all, k_all, v_all)


def _mixout_kernel(ctx_len, blk_off, x_ref, mod_ref, hf_ref, hb_ref, mo_ref, oa_ref, ol_ref, mn_ref, w_ref, o_ref):
    tm = x_ref.shape[0]
    t0 = (pl.program_id(1) + blk_off) * tm
    hm = hf_ref[...].astype(F32) + hb_ref[...].astype(F32)
    gate = _sigmoid(mo_ref[...].astype(F32))
    parts = []
    for i in range(M_HEADS):
        x = hm[:, i * SLOT:(i + 1) * SLOT]
        inv = lax.rsqrt(jnp.sum(x * x, axis=-1, keepdims=True) * (1.0 / M_DH) + EPS)
        parts.append(x * inv)
    hn = jnp.concatenate(parts, axis=-1) * mn_ref[...] * gate
    nm = M_HEADS * SLOT
    o = jnp.dot(hn.astype(BF16), w_ref[0:nm, :], preferred_element_type=F32)
    na = nm + A_HEADS * SLOT
    o = o + jnp.dot(oa_ref[...], w_ref[nm:na, :], preferred_element_type=F32)
    o = o + jnp.dot(ol_ref[...], w_ref[na:, :], preferred_element_type=F32)
    o_ref[...] = x_ref[...] + _gate_rows(mod_ref, t0, tm, ctx_len, 2) * o


def _mixout(xs, modtab, hf, hb, mo, o_gqa, o_mla, m_norm_p, w_out_p, ctx_len, blk_off):
    b, s, d = xs.shape
    tok = lambda n: pl.BlockSpec((None, TB, n), lambda i, j: (i, j + blk_off, 0))
    full = lambda a: pl.BlockSpec(a.shape, lambda i, j: (0,) * a.ndim)
    return pl.pallas_call(
        functools.partial(_mixout_kernel, ctx_len, blk_off),
        grid=(b, s // TB - blk_off),
        in_specs=[tok(d),
                  pl.BlockSpec((None, 2, 6, d), lambda i, j: (i, 0, 0, 0)),
                  tok(M_HEADS * SLOT), tok(M_HEADS * SLOT), tok(M_HEADS * SLOT), tok(A_HEADS * SLOT),
                  tok(L_HEADS * SLOT), full(m_norm_p), full(w_out_p)],
        out_specs=tok(d),
        out_shape=jax.ShapeDtypeStruct((b, s, d), F32),
        input_output_aliases={0: 0},
        compiler_params=_cparams(("parallel", "parallel")),
        name="mix_out",
    )(xs, modtab, hf, hb, mo, o_gqa, o_mla, m_norm_p, w_out_p)


def _ffn_kernel(ctx_len, x_ref, mod_ref, g_ref, wg_ref, wu_ref, wd_ref, o_ref, h_sc, acc_sc):
    tm = x_ref.shape[0]
    t0 = pl.program_id(1) * tm
    f = pl.program_id(2)

    @pl.when(f == 0)
    def _():
        h_sc[...] = _modulated(x_ref[...], g_ref[...], mod_ref, t0, ctx_len, 3, 4).astype(BF16)
        acc_sc[...] = jnp.zeros_like(acc_sc)

    h = h_sc[...]
    a = jnp.dot(h, wg_ref[...], preferred_element_type=F32)
    u = jnp.dot(h, wu_ref[...], preferred_element_type=F32)
    acc_sc[...] += jnp.dot((_silu(a) * u).astype(BF16), wd_ref[...], preferred_element_type=F32)

    @pl.when(f == pl.num_programs(2) - 1)
    def _():
        o_ref[...] = x_ref[...] + _gate_rows(mod_ref, t0, tm, ctx_len, 5) * acc_sc[...]


def _ffn(xs, modtab, gain, wg, wu, wd, ctx_len):
    b, s, d = xs.shape
    tm = s // 4
    nf = wg.shape[1] // TF
    return pl.pallas_call(
        functools.partial(_ffn_kernel, ctx_len),
        grid=(b, s // tm, nf),
        in_specs=[pl.BlockSpec((None, tm, d), lambda i, j, f: (i, j, 0)),
                  pl.BlockSpec((None, 2, 6, d), lambda i, j, f: (i, 0, 0, 0)),
                  pl.BlockSpec((1, d), lambda i, j, f: (0, 0)),
                  pl.BlockSpec((d, TF), lambda i, j, f: (0, f)),
                  pl.BlockSpec((d, TF), lambda i, j, f: (0, f)),
                  pl.BlockSpec((TF, d), lambda i, j, f: (f, 0))],
        out_specs=pl.BlockSpec((None, tm, d), lambda i, j, f: (i, j, 0)),
        out_shape=jax.ShapeDtypeStruct((b, s, d), F32),
        scratch_shapes=[pltpu.VMEM((tm, d), BF16), pltpu.VMEM((tm, d), F32)],
        input_output_aliases={0: 0},
        compiler_params=_cparams(("parallel", "parallel", "arbitrary")),
        name="ffn_dense",
    )(xs, modtab, gain, wg, wu, wd)


def _top2_combine(logits):
    lane = _lane_iota(logits.shape)
    lane_f = lane.astype(F32)
    lg = jnp.where(lane < N_EXPERTS, logits, -jnp.inf)
    v1 = jnp.max(lg, axis=-1, keepdims=True)
    i1 = jnp.min(jnp.where(lg == v1, lane_f, float(LANE)), axis=-1, keepdims=True)
    rest = jnp.where(lane_f == i1, -jnp.inf, lg)
    v2 = jnp.max(rest, axis=-1, keepdims=True)
    i2 = jnp.min(jnp.where(rest == v2, lane_f, float(LANE)), axis=-1, keepdims=True)
    e2 = jnp.exp(v2 - v1)
    w1 = 1.0 / (1.0 + e2)
    w2 = e2 / (1.0 + e2)
    sel = jnp.where((lane_f == i1) | (lane_f == i2), 1.0, 0.0)
    return jnp.where(lane_f == i1, w1, 0.0) + jnp.where(lane_f == i2, w2, 0.0), sel


def _moe_kernel(ctx_len, route_ctx, x_ref, mod_ref, g_ref, r_ref, tri_ref, wg_ref, wu_ref, wd_ref, o_ref,
                h_sc, comb_sc, rank_sc, rankt_sc, rankc_sc, wc_sc, xg_sc, acc_sc, y_sc, nt_sc):
    tm = x_ref.shape[0]
    t0 = pl.program_id(1) * tm
    e = pl.program_id(2)
    f = pl.program_id(3)

    @pl.when(jnp.logical_and(e == 0, f == 0))
    def _():
        h = _modulated(x_ref[...], g_ref[...], mod_ref, t0, ctx_len, 3, 4)
        logits = jnp.dot(h, r_ref[...], preferred_element_type=F32, precision=lax.Precision.HIGHEST)
        comb, sel = _top2_combine(logits)
        if not route_ctx:
            live = (t0 + _row_iota((tm, 1))) >= ctx_len
            comb = jnp.where(live, comb, 0.0)
            sel = jnp.where(live, sel, 0.0)
        rank = jnp.dot(tri_ref[...], sel.astype(BF16), preferred_element_type=F32)
        rank = jnp.where(sel > 0.0, rank, -1.0)
        comb_sc[...] = comb
        rank_sc[...] = rank
        pad = rankt_sc.shape[1] - tm
        rankt_sc[...] = jnp.concatenate([rank, jnp.full((pad, LANE), -1.0, F32)], axis=0).T
        h_sc[0:tm, :] = h.astype(BF16)
        h_sc[tm:, :] = jnp.zeros((pad, h_sc.shape[1]), BF16)
        y_sc[...] = jnp.zeros_like(y_sc)

    @pl.when(f == 0)
    def _():
        lane = _lane_iota((tm, LANE))
        rank_c = jnp.sum(jnp.where(lane == e, rank_sc[...], 0.0), axis=-1, keepdims=True)
        rankc_sc[...] = rank_c
        wc_sc[...] = jnp.sum(jnp.where(lane == e, comb_sc[...], 0.0), axis=-1, keepdims=True)
        n_rows = jnp.sum(jnp.where(rank_c >= 0.0, 1.0, 0.0)).astype(jnp.int32)
        n_tiles = lax.shift_right_logical(n_rows + (RT - 1), RT.bit_length() - 1)
        nt_sc[0] = n_tiles
        rank_r = rankt_sc[pl.ds(e, 1), :]

        def gather(i, c):
            r0 = pl.multiple_of(i * RT, RT)
            tgt = (r0 + _row_iota((RT, 1))).astype(F32)
            sel_t = jnp.where(rank_r == tgt, 1.0, 0.0).astype(BF16)
            xg_sc[pl.ds(r0, RT), :] = jnp.dot(sel_t, h_sc[...], preferred_element_type=F32).astype(BF16)
            acc_sc[pl.ds(r0, RT), :] = jnp.zeros((RT, acc_sc.shape[1]), F32)
            return c

        lax.fori_loop(0, n_tiles, gather, 0)

    def expert(i, c):
        r0 = pl.multiple_of(i * RT, RT)
        rows = xg_sc[pl.ds(r0, RT), :]
        a = jnp.dot(rows, wg_ref[...], preferred_element_type=F32)
        u = jnp.dot(rows, wu_ref[...], preferred_element_type=F32)
        acc_sc[pl.ds(r0, RT), :] += jnp.dot((_silu(a) * u).astype(BF16), wd_ref[...],
                                            preferred_element_type=F32)
        return c

    lax.fori_loop(0, nt_sc[0], expert, 0)

    @pl.when(f == pl.num_programs(3) - 1)
    def _():
        def scatter(i, c):
            r0 = pl.multiple_of(i * RT, RT)
            tgt = (r0 + _lane_iota((1, RT))).astype(F32)
            w_t = jnp.where(rankc_sc[...] == tgt, wc_sc[...], 0.0).astype(BF16)
            y_sc[...] += jnp.dot(w_t, acc_sc[pl.ds(r0, RT), :].astype(BF16), preferred_element_type=F32)
            return c

        lax.fori_loop(0, nt_sc[0], scatter, 0)

    @pl.when(jnp.logical_and(e == pl.num_programs(2) - 1, f == pl.num_programs(3) - 1))
    def _():
        o_ref[...] = x_ref[...] + _gate_rows(mod_ref, t0, tm, ctx_len, 5) * y_sc[...]


def _moe(xs, modtab, gain, router_p, wg, wu, wd, ctx_len, route_ctx):
    b, s, d = xs.shape
    tm = s // 4
    tp = -(-tm // RT) * RT
    ne, _, dff = wg.shape
    tri = jnp.tril(jnp.ones((tm, tm), BF16), -1)
    return pl.pallas_call(
        functools.partial(_moe_kernel, ctx_len, route_ctx),
        grid=(b, s // tm, ne, dff // TF),
        in_specs=[pl.BlockSpec((None, tm, d), lambda i, j, e, f: (i, j, 0)),
                  pl.BlockSpec((None, 2, 6, d), lambda i, j, e, f: (i, 0, 0, 0)),
                  pl.BlockSpec((1, d), lambda i, j, e, f: (0, 0)),
                  pl.BlockSpec((d, LANE), lambda i, j, e, f: (0, 0)),
                  pl.BlockSpec((tm, tm), lambda i, j, e, f: (0, 0)),
                  pl.BlockSpec((None, d, TF), lambda i, j, e, f: (e, 0, f)),
                  pl.BlockSpec((None, d, TF), lambda i, j, e, f: (e, 0, f)),
                  pl.BlockSpec((None, TF, d), lambda i, j, e, f: (e, f, 0))],
        out_specs=pl.BlockSpec((None, tm, d), lambda i, j, e, f: (i, j, 0)),
        out_shape=jax.ShapeDtypeStruct((b, s, d), F32),
        scratch_shapes=[pltpu.VMEM((tp, d), BF16),
                        pltpu.VMEM((tm, LANE), F32),
                        pltpu.VMEM((tm, LANE), F32),
                        pltpu.VMEM((LANE, tp), F32),
                        pltpu.VMEM((tm, 1), F32),
                        pltpu.VMEM((tm, 1), F32),
                        pltpu.VMEM((tp, d), BF16),
                        pltpu.VMEM((tp, d), F32),
                        pltpu.VMEM((tm, d), F32),
                        pltpu.SMEM((1,), jnp.int32)],
        input_output_aliases={0: 0},
        compiler_params=_cparams(("parallel", "parallel", "arbitrary", "arbitrary")),
        name="moe_top2",
    )(xs, modtab, gain, router_p, tri, wg, wu, wd)


def _pad_heads(w, n_heads, dh, axis=-1):
    axis = axis % w.ndim
    shp = w.shape[:axis] + (n_heads, dh) + w.shape[axis + 1:]
    pad = [(0, 0)] * (w.ndim + 1)
    pad[axis + 1] = (0, SLOT - dh)
    out = jnp.pad(w.reshape(shp), pad)
    return out.reshape(w.shape[:axis] + (n_heads * SLOT,) + w.shape[axis + 1:])


def _rope_tables(seq, ctx_len):
    t = jnp.arange(seq)
    rows = (t // GRID_W).astype(F32)
    cols = (t % GRID_W).astype(F32)

    def angles(rot_dim):
        nf = rot_dim // 4
        inv = ROPE_THETA ** (-jnp.arange(nf, dtype=F32) / nf)
        ar = rows[:, None] * inv
        ac = cols[:, None] * inv
        return jnp.concatenate([ar, ar, ac, ac], axis=-1)

    def slot_tables(rot_dim, lane0):
        ang = angles(rot_dim)
        quarter = rot_dim // 4
        first = (jnp.arange(rot_dim) % (2 * quarter)) < quarter
        cos = jnp.ones((seq, SLOT), F32).at[:, lane0:lane0 + rot_dim].set(jnp.cos(ang))
        sin = jnp.sin(ang)
        sin_m = jnp.zeros((seq, SLOT), F32).at[:, lane0:lane0 + rot_dim].set(jnp.where(first, -sin, 0.0))
        sin_p = jnp.zeros((seq, SLOT), F32).at[:, lane0:lane0 + rot_dim].set(jnp.where(first, 0.0, sin))
        ident = jnp.stack([jnp.ones((ctx_len, SLOT), F32), jnp.zeros((ctx_len, SLOT), F32),
                           jnp.zeros((ctx_len, SLOT), F32)])
        return jnp.concatenate([ident, jnp.stack([cos, sin_m, sin_p])], axis=1)

    return jnp.concatenate([slot_tables(A_DH, 0), slot_tables(L_ROPE, L_NOPE)], axis=0)


def _layer_params(w_in, m_conv, m_gate_b, m_norm, a_qnorm, a_knorm, l_cq_norm, l_ckv_norm, l_wuq, l_wukv,
                  l_qnorm, l_knorm, w_out):
    d = w_in.shape[0]
    offs = np.cumsum((0,) + IN_SIZES)
    seg = [w_in[:, offs[i]:offs[i + 1]] for i in range(len(IN_SIZES))]
    misc = jnp.zeros((d, SLOT), F32).at[:, 0:4 * M_HEADS].set(seg[4]).at[:, MISC_KR:MISC_KR + L_ROPE].set(seg[10])
    w_in_p = jnp.concatenate(
        [_pad_heads(seg[0], M_HEADS, M_DH), _pad_heads(seg[1], M_HEADS, M_DH),
         _pad_heads(seg[2], M_HEADS, M_DH), _pad_heads(seg[3], M_HEADS, M_DH), misc,
         _pad_heads(seg[5], A_HEADS, A_DH), _pad_heads(seg[6], A_KV, A_DH), _pad_heads(seg[7], A_KV, A_DH),
         seg[8], seg[9]], axis=1).astype(BF16)
    conv_w = jnp.concatenate([_pad_heads(m_conv[:, :M_HEADS * M_DH], M_HEADS, M_DH),
                              _pad_heads(m_conv[:, M_HEADS * M_DH:], M_HEADS, M_DH)], axis=1)
    conv_scale = jnp.concatenate([jnp.ones((1, M_HEADS * SLOT), F32),
                                  jnp.full((1, M_HEADS * SLOT), M_DH ** -0.5, F32)], axis=1)
    gate_b = jnp.zeros((1, SLOT), F32).at[0, 0:4 * M_HEADS].set(m_gate_b)
    pad1 = lambda g: jnp.pad(g, (0, SLOT - g.shape[0]))
    an = jnp.stack([pad1(a_qnorm), pad1(a_knorm)])
    ln = jnp.stack([pad1(l_qnorm), pad1(l_knorm)])
    wuq_p = _pad_heads(l_wuq, L_HEADS, L_QK).astype(BF16)
    kv = l_wukv.reshape(L_KVRANK, L_HEADS, L_NOPE + L_DV)
    wukv_p = jnp.concatenate(
        [_pad_heads(kv[:, :, :L_NOPE].reshape(L_KVRANK, -1), L_HEADS, L_NOPE),
         _pad_heads(kv[:, :, L_NOPE:].reshape(L_KVRANK, -1), L_HEADS, L_DV)], axis=1).astype(BF16)
    nm, na = M_HEADS * M_DH, A_HEADS * A_DH
    w_out_p = jnp.concatenate(
        [_pad_heads(w_out[:nm], M_HEADS, M_DH, axis=0), _pad_heads(w_out[nm:nm + na], A_HEADS, A_DH, axis=0),
         _pad_heads(w_out[nm + na:], L_HEADS, L_DV, axis=0)], axis=0).astype(BF16)
    m_norm_p = _pad_heads(m_norm[None, :], M_HEADS, M_DH)
    return dict(w_in_p=w_in_p, conv_w=conv_w, conv_scale=conv_scale, gate_b=gate_b, an=an, ln=ln,
                cqn=l_cq_norm[None, :], ckvn=l_ckv_norm[None, :], wuq_p=wuq_p, wukv_p=wukv_p,
                w_out_p=w_out_p, m_norm_p=m_norm_p)


def kernel(x, c, ctx, c_ctx, mod_w, mod_b, norm_mix, norm_ffn, w_in, m_conv, m_gate_b, m_norm, a_qnorm, a_knorm,
           l_cq_norm, l_ckv_norm, l_wuq, l_wukv, l_qnorm, l_knorm, w_out, ffn_wg, ffn_wu, ffn_wd, moe_router,
           moe_wg, moe_wu, moe_wd):
    b, seq, d = x.shape
    ctx_len = ctx.shape[1]
    depth = mod_w.shape[0]
    assert ctx_len % TB == 0 and ctx_len % TQ == 0 and ctx_len == ML and seq % TK == 0 and seq % GRID_W == 0
    xs = jnp.concatenate([ctx, x], axis=1)
    mod_rows = 16
    cc = jnp.zeros((mod_rows, d), F32).at[:b].set(c).at[b].set(c_ctx)
    mod_all = _mod_table(cc, mod_w, mod_b)
    rope_tab = _rope_tables(seq, ctx_len)
    for i in range(depth):
        need_ctx = i < depth - 1
        lat = mod_all[i, :b].reshape(b, 1, 6, d)
        cm = jnp.broadcast_to(mod_all[i, b].reshape(1, 1, 6, d), (b, 1, 6, d))
        modtab = jnp.concatenate([cm, lat], axis=1)
        p = _layer_params(w_in[i], m_conv[i], m_gate_b[i], m_norm[i], a_qnorm[i], a_knorm[i], l_cq_norm[i],
                          l_ckv_norm[i], l_wuq[i], l_wukv[i], l_qnorm[i], l_knorm[i], w_out[i])
        mqk, mv, mo, misc, q_all, k_all, v_all = _inproj(
            xs, modtab, norm_mix[i][None, :], p["w_in_p"], rope_tab, p["an"], p["ln"], p["cqn"], p["ckvn"],
            p["wuq_p"], p["wukv_p"], ctx_len)
        qk_c = _conv(mqk, p["conv_w"], p["conv_scale"], ctx_len)
        hf, hb = _mlstm(qk_c, mv, misc, p["gate_b"])
        o_gqa = _attention(q_all, k_all, v_all, ctx_len, need_ctx, 0, 0, A_KV, A_HEADS // A_KV, True)
        o_mla = _attention(q_all, k_all, v_all, ctx_len, need_ctx, A_HEADS, A_KV, L_HEADS // 2, 2, False)
        xs = _mixout(xs, modtab, hf, hb, mo, o_gqa, o_mla, p["m_norm_p"], p["w_out_p"], ctx_len,
                     0 if need_ctx else ctx_len // TB)
        j = i // 2
        if i % 2 == 0:
            xs = _ffn(xs, modtab, norm_ffn[i][None, :], ffn_wg[j].astype(BF16), ffn_wu[j].astype(BF16),
                      ffn_wd[j].astype(BF16), ctx_len)
        else:
            router_p = jnp.pad(moe_router[j], ((0, 0), (0, LANE - N_EXPERTS)))
            xs = _moe(xs, modtab, norm_ffn[i][None, :], router_p, moe_wg[j].astype(BF16),
                      moe_wu[j].astype(BF16), moe_wd[j].astype(BF16), ctx_len, need_ctx)
    return xs[:, ctx_len:]
```

```python
import functools
import math

import numpy as np
import jax
import jax.numpy as jnp
from jax import lax
from jax.experimental import pallas as pl
from jax.experimental.pallas import tpu as pltpu

F32 = jnp.float32
BF16 = jnp.bfloat16

GRID_W = 64
EPS = 1e-6
ROPE_THETA = 10000.0
M_HEADS, M_DH = 4, 96
A_HEADS, A_KV, A_DH = 6, 2, 64
L_HEADS, L_NOPE, L_ROPE, L_DV = 4, 64, 32, 64
L_QK = L_NOPE + L_ROPE
L_QRANK, L_KVRANK = 256, 128
N_EXPERTS, TOP_K = 8, 2
IN_SIZES = (384, 384, 384, 384, 16, 384, 128, 128, 256, 128, 32)

LANE = 128
SLOT = LANE
N_QHEADS = A_HEADS + L_HEADS
N_KVHEADS = A_KV + L_HEADS
VMEM_LIMIT = 56 * 1024 * 1024

O_MQK, O_MV, O_MO, O_MISC = 0, 1024, 1536, 2048
O_QA, O_KA, O_VA, O_CQ, O_CKV, IN_PAD = 2176, 2944, 3200, 3456, 3712, 3840
MISC_KR = 64
DEN_LANE = M_DH
SUM_LANE = A_DH
LOG2E = math.log2(math.e)
assert A_DH == L_DV and SUM_LANE < SLOT

TB = 256
TQ = 256
TK = 512
TM_FFN = 1088
TF = 512
ML = 256
RT = 128


def _cparams(sem):
    return pltpu.CompilerParams(dimension_semantics=sem, vmem_limit_bytes=VMEM_LIMIT)


def _sigmoid(x):
    return 1.0 / (1.0 + jnp.exp(-x))


def _silu(x):
    return x * _sigmoid(x)


def _lane_iota(shape):
    return lax.broadcasted_iota(jnp.int32, shape, len(shape) - 1)


def _row_iota(shape):
    return lax.broadcasted_iota(jnp.int32, shape, 0)


def _modulated(x, gain, mod_ref, t0, ctx_len, k_shift, k_scale):
    tm = x.shape[0]
    is_ctx = (t0 + _row_iota((tm, 1))) < ctx_len
    shift = jnp.where(is_ctx, mod_ref[0, k_shift:k_shift + 1, :], mod_ref[1, k_shift:k_shift + 1, :])
    scale = jnp.where(is_ctx, mod_ref[0, k_scale:k_scale + 1, :], mod_ref[1, k_scale:k_scale + 1, :])
    y = x * lax.rsqrt(jnp.mean(x * x, axis=-1, keepdims=True) + EPS) * gain
    return y * (1.0 + scale) + shift


def _gate_rows(mod_ref, t0, tm, ctx_len, k_gate):
    is_ctx = (t0 + _row_iota((tm, 1))) < ctx_len
    return jnp.where(is_ctx, mod_ref[0, k_gate:k_gate + 1, :], mod_ref[1, k_gate:k_gate + 1, :])


def _mod_kernel(c_ref, w_ref, b_ref, o_ref):
    s = _silu(c_ref[...]).astype(BF16)
    o_ref[...] = jnp.dot(s, w_ref[...].astype(BF16), preferred_element_type=F32) + b_ref[...]


def _mod_table(cc, mod_w, mod_b):
    depth, d, n = mod_w.shape
    rows = cc.shape[0]
    return pl.pallas_call(
        _mod_kernel,
        grid=(depth, n // d),
        in_specs=[pl.BlockSpec((rows, d), lambda l, j: (0, 0)),
                  pl.BlockSpec((None, d, d), lambda l, j: (l, 0, j)),
                  pl.BlockSpec((None, 1, d), lambda l, j: (l, 0, j))],
        out_specs=pl.BlockSpec((None, rows, d), lambda l, j: (l, 0, j)),
        out_shape=jax.ShapeDtypeStruct((depth, rows, n), F32),
        compiler_params=_cparams(("arbitrary", "arbitrary")),
        name="mod_table",
    )(cc, mod_w, mod_b.reshape(depth, 1, n))


def _slot_rms(x, lo, hi):
    lane = _lane_iota(x.shape)
    sq = jnp.where((lane >= lo) & (lane < hi), x * x, 0.0)
    return lax.rsqrt(jnp.sum(sq, axis=-1, keepdims=True) * (1.0 / (hi - lo)) + EPS)


def _rope(x, cos, sin_m, sin_p, quarter):
    return (x * cos + pltpu.roll(x, LANE - quarter, 1) * sin_m + pltpu.roll(x, quarter, 1) * sin_p)


def _inproj_kernel(ctx_len, x_ref, mod_ref, g_ref, w_ref, rope_ref, an_ref, ln_ref, cqn_ref, ckvn_ref,
                   wuq_ref, wukv_ref, mqk_ref, mv_ref, mo_ref, misc_ref, q_ref, k_ref, vt_ref):
    tm = x_ref.shape[0]
    t0 = pl.program_id(1) * tm
    h = _modulated(x_ref[...], g_ref[...], mod_ref, t0, ctx_len, 0, 1).astype(BF16)

    def proj(a, b):
        return jnp.dot(h, w_ref[:, a:b], preferred_element_type=F32)

    mqk_ref[...] = proj(O_MQK, O_MV).astype(BF16)
    mv_ref[...] = proj(O_MV, O_MO).astype(BF16)
    mo_ref[...] = proj(O_MO, O_MISC).astype(BF16)
    misc = proj(O_MISC, O_QA)
    misc_ref[...] = misc

    cos_a, sinm_a, sinp_a = rope_ref[0], rope_ref[1], rope_ref[2]
    cos_l, sinm_l, sinp_l = rope_ref[3], rope_ref[4], rope_ref[5]
    a_scale = A_DH ** -0.5 * LOG2E
    l_scale = L_QK ** -0.5 * LOG2E

    def with_ones_t(v):
        return jnp.where(_lane_iota(v.shape) % SLOT == SUM_LANE, 1.0, v).T.astype(BF16)

    pa = proj(O_QA, O_CQ)
    gq, gk = an_ref[0:1, :], an_ref[1:2, :]
    for i in range(A_HEADS):
        x = pa[:, i * SLOT:(i + 1) * SLOT]
        x = x * _slot_rms(x, 0, A_DH) * gq
        q_ref[:, i * SLOT:(i + 1) * SLOT] = (_rope(x, cos_a, sinm_a, sinp_a, A_DH // 4) * a_scale).astype(BF16)
    for i in range(A_KV):
        x = pa[:, (A_HEADS + i) * SLOT:(A_HEADS + i + 1) * SLOT]
        x = x * _slot_rms(x, 0, A_DH) * gk
        k_ref[:, i * SLOT:(i + 1) * SLOT] = _rope(x, cos_a, sinm_a, sinp_a, A_DH // 4).astype(BF16)
    vt_ref[0:A_KV * SLOT, :] = with_ones_t(pa[:, (A_HEADS + A_KV) * SLOT:(A_HEADS + 2 * A_KV) * SLOT])

    pc = proj(O_CQ, IN_PAD)
    cq = pc[:, 0:L_QRANK]
    cq = (cq * lax.rsqrt(jnp.mean(cq * cq, axis=-1, keepdims=True) + EPS) * cqn_ref[...]).astype(BF16)
    ckv = pc[:, L_QRANK:L_QRANK + L_KVRANK]
    ckv = (ckv * lax.rsqrt(jnp.mean(ckv * ckv, axis=-1, keepdims=True) + EPS) * ckvn_ref[...]).astype(BF16)
    ql = jnp.dot(cq, wuq_ref[...], preferred_element_type=F32)
    kvl = jnp.dot(ckv, wukv_ref[...], preferred_element_type=F32)
    gq_l, gk_l = ln_ref[0:1, :], ln_ref[1:2, :]
    lane = _lane_iota((tm, SLOT))
    kr = jnp.where((lane >= MISC_KR) & (lane < MISC_KR + L_ROPE), misc, 0.0)
    kr = kr * _slot_rms(kr, MISC_KR, MISC_KR + L_ROPE) * gk_l
    kr = _rope(kr, cos_l, sinm_l, sinp_l, L_ROPE // 4)
    for i in range(L_HEADS):
        x = ql[:, i * SLOT:(i + 1) * SLOT]
        inv = jnp.where(lane < L_NOPE, _slot_rms(x, 0, L_NOPE), _slot_rms(x, L_NOPE, L_QK))
        x = x * inv * gq_l
        q_ref[:, (A_HEADS + i) * SLOT:(A_HEADS + i + 1) * SLOT] = (
            _rope(x, cos_l, sinm_l, sinp_l, L_ROPE // 4) * l_scale).astype(BF16)
        kn = kvl[:, i * SLOT:(i + 1) * SLOT]
        kn = kn * _slot_rms(kn, 0, L_NOPE) * gk_l
        k_ref[:, (A_KV + i) * SLOT:(A_KV + i + 1) * SLOT] = (kn + kr).astype(BF16)
    vt_ref[A_KV * SLOT:, :] = with_ones_t(kvl[:, L_HEADS * SLOT:])


def _inproj(xs, modtab, gain, w_in_p, rope_tab, an, ln, cqn, ckvn, wuq_p, wukv_p, ctx_len):
    b, s, d = xs.shape
    grid = (b, s // TB)
    tok = lambda n: pl.BlockSpec((None, TB, n), lambda i, j: (i, j, 0))
    full = lambda a: pl.BlockSpec(a.shape, lambda i, j: (0,) * a.ndim)
    out_widths = (1024, 512, 512, SLOT, N_QHEADS * SLOT, N_KVHEADS * SLOT)
    out_dtypes = (BF16, BF16, BF16, F32, BF16, BF16)
    kvw = N_KVHEADS * SLOT
    out_specs = [tok(n) for n in out_widths] + [pl.BlockSpec((None, kvw, TB), lambda i, j: (i, 0, j))]
    out_shape = [jax.ShapeDtypeStruct((b, s, n), dt) for n, dt in zip(out_widths, out_dtypes)]
    out_shape.append(jax.ShapeDtypeStruct((b, kvw, s), BF16))
    return pl.pallas_call(
        functools.partial(_inproj_kernel, ctx_len),
        grid=grid,
        in_specs=[tok(d),
                  pl.BlockSpec((None, 2, 6, d), lambda i, j: (i, 0, 0, 0)),
                  full(gain), full(w_in_p),
                  pl.BlockSpec((6, TB, SLOT), lambda i, j: (0, j, 0)),
                  full(an), full(ln), full(cqn), full(ckvn), full(wuq_p), full(wukv_p)],
        out_specs=out_specs,
        out_shape=out_shape,
        compiler_params=_cparams(("parallel", "parallel")),
        name="inproj",
    )(xs, modtab, gain, w_in_p, rope_tab, an, ln, cqn, ckvn, wuq_p, wukv_p)


def _conv_kernel(ctx_len, s_len, x_ref, prev_ref, next_ref, w_ref, sc_ref, o_ref):
    tm = x_ref.shape[0]
    t0 = pl.program_id(1) * tm
    x = x_ref[...].astype(F32)
    row = _row_iota((tm, 1))
    has_prev = jnp.logical_and(t0 != 0, t0 != ctx_len)
    has_next = jnp.logical_and(t0 + tm != ctx_len, t0 + tm != s_len)
    hp = jnp.where(has_prev, prev_ref[15:16, :].astype(F32), 0.0)
    hn = jnp.where(has_next, next_ref[0:1, :].astype(F32), 0.0)
    xp = jnp.where(row == 0, hp, pltpu.roll(x, 1, 0))
    xn = jnp.where(row == tm - 1, hn, pltpu.roll(x, tm - 1, 0))
    y = xp * w_ref[0:1, :] + x * w_ref[1:2, :] + xn * w_ref[2:3, :]
    o_ref[...] = (_silu(y) * sc_ref[...]).astype(o_ref.dtype)


def _conv(mqk, conv_w, conv_scale, ctx_len):
    b, s, n = mqk.shape
    hb = TB // 16
    last = s // 16 - 1
    return pl.pallas_call(
        functools.partial(_conv_kernel, ctx_len, s),
        grid=(b, s // TB),
        in_specs=[pl.BlockSpec((None, TB, n), lambda i, j: (i, j, 0)),
                  pl.BlockSpec((None, 16, n), lambda i, j: (i, jnp.maximum(j * hb - 1, 0), 0)),
                  pl.BlockSpec((None, 16, n), lambda i, j: (i, jnp.minimum((j + 1) * hb, last), 0)),
                  pl.BlockSpec((3, n), lambda i, j: (0, 0)),
                  pl.BlockSpec((1, n), lambda i, j: (0, 0))],
        out_specs=pl.BlockSpec((None, TB, n), lambda i, j: (i, j, 0)),
        out_shape=jax.ShapeDtypeStruct((b, s, n), BF16),
        compiler_params=_cparams(("parallel", "parallel")),
        name="mlstm_conv",
    )(mqk, mqk, mqk, conv_w, conv_scale)


def _log_sigmoid(x):
    return jnp.minimum(x, 0.0) - jnp.log(1.0 + jnp.exp(-jnp.abs(x)))


def _mlstm_kernel(n_chunks, q_ref, k_ref, v_ref, misc_ref, gb_ref, hf_ref, hb_ref, c_ref, m_ref):
    hp = pl.program_id(1)
    c_ref[...] = jnp.zeros_like(c_ref)
    m_ref[...] = jnp.zeros_like(m_ref)
    r = _row_iota((ML, ML))
    cidx = _lane_iota((ML, ML))
    tri_f = (r >= cidx)
    tri_b = (r <= cidx)
    tri_f32 = tri_f.astype(F32)
    tri_b32 = tri_b.astype(F32)
    lane = _lane_iota((ML, SLOT))

    def chunk(t0, d, tri, tri32, out_ref):
        g = misc_ref[pl.ds(t0, ML), :] + gb_ref[...]
        logf = _log_sigmoid(g)
        bcum = jnp.dot(tri32, logf, preferred_element_type=F32, precision=lax.Precision.HIGHEST)
        bcum_t = bcum.T
        g_t = g.T
        for hh in range(2):
            head = hp * 2 + hh
            li = d * 2 * M_HEADS + head
            sel_i = (lane == li)
            sel_f = (lane == li + M_HEADS)
            i_col = jnp.sum(jnp.where(sel_i, g, 0.0), axis=-1, keepdims=True)
            b_col = jnp.sum(jnp.where(sel_f, bcum, 0.0), axis=-1, keepdims=True)
            rsel_i = (_row_iota((SLOT, ML)) == li)
            rsel_f = (_row_iota((SLOT, ML)) == li + M_HEADS)
            i_row = jnp.sum(jnp.where(rsel_i, g_t, 0.0), axis=0, keepdims=True)
            b_row = jnp.sum(jnp.where(rsel_f, bcum_t, 0.0), axis=0, keepdims=True)
            sidx = d * 2 + hh
            c_st = c_ref[sidx]
            m_st = m_ref[sidx]
            m_s = m_st[:, 0:1]
            q = q_ref[pl.ds(t0, ML), hh * SLOT:(hh + 1) * SLOT]
            k = k_ref[pl.ds(t0, ML), hh * SLOT:(hh + 1) * SLOT]
            v = v_ref[pl.ds(t0, ML), hh * SLOT:(hh + 1) * SLOT]
            v = jnp.where(lane == DEN_LANE, 1.0, v.astype(F32)).astype(BF16)
            dmat = jnp.where(tri, b_col - b_row + i_row, -jnp.inf)
            m_inter = b_col + m_s
            m_t = jnp.maximum(m_inter, jnp.max(dmat, axis=-1, keepdims=True))
            qk = lax.dot_general(q, k, (((1,), (1,)), ((), ())), preferred_element_type=F32)
            w = jnp.exp(dmat - m_t) * qk
            a_inter = jnp.exp(m_inter - m_t)
            num = a_inter * jnp.dot(q, c_st.astype(BF16), preferred_element_type=F32) + jnp.dot(
                w.astype(BF16), v, preferred_element_type=F32)
            den = jnp.sum(jnp.where(lane == DEN_LANE, num, 0.0), axis=-1, keepdims=True)
            h_out = num / jnp.maximum(jnp.abs(den), jnp.exp(-m_t))
            out_ref[pl.ds(t0, ML), hh * SLOT:(hh + 1) * SLOT] = h_out.astype(out_ref.dtype)
            total = b_col[ML - 1:ML, :] if d == 0 else b_col[0:1, :]
            gg = total - b_col + i_col
            m_new = jnp.maximum(total + m_s, jnp.max(gg, axis=0, keepdims=True))
            decay = jnp.exp(total + m_s - m_new)
            wk = jnp.exp(gg - m_new)
            kw = (k.astype(F32) * wk).astype(BF16)
            c_ref[sidx] = decay * c_st + lax.dot_general(
                kw, v, (((0,), (0,)), ((), ())), preferred_element_type=F32)
            m_ref[sidx] = jnp.broadcast_to(m_new, (1, SLOT))

    def step(n, carry):
        tf0 = pl.multiple_of(n * ML, ML)
        tb0 = pl.multiple_of(jnp.where(n == 0, 0, n_chunks - n) * ML, ML)
        chunk(tf0, 0, tri_f, tri_f32, hf_ref)
        chunk(tb0, 1, tri_b, tri_b32, hb_ref)
        return carry

    lax.fori_loop(0, n_chunks, step, 0)


def _mlstm(qk_c, mv, misc, gate_b):
    b, s, _ = mv.shape
    n_chunks = s // ML
    pair = 2 * SLOT
    kspec = pl.BlockSpec((None, s, pair), lambda i, j: (i, 0, 2 + j))
    spec = pl.BlockSpec((None, s, pair), lambda i, j: (i, 0, j))
    return pl.pallas_call(
        functools.partial(_mlstm_kernel, n_chunks),
        grid=(b, M_HEADS // 2),
        in_specs=[spec, kspec, spec,
                  pl.BlockSpec((None, s, SLOT), lambda i, j: (i, 0, 0)),
                  pl.BlockSpec((1, SLOT), lambda i, j: (0, 0))],
        out_specs=[spec, spec],
        out_shape=[jax.ShapeDtypeStruct((b, s, M_HEADS * SLOT), BF16)] * 2,
        scratch_shapes=[pltpu.VMEM((4, SLOT, SLOT), F32), pltpu.VMEM((4, 1, SLOT), F32)],
        compiler_params=_cparams(("parallel", "parallel")),
        name="mlstm_scan",
    )(qk_c, qk_c, mv, misc, gate_b)


def _attn_kernel(ctx_len, q_off, n_heads, shared_kv, q_ref, k_ref, vt_ref, o_ref, st_sc, pt_sc, acc_sc):
    s_len = k_ref.shape[0]
    qs = [q_ref[:, g * SLOT:(g + 1) * SLOT] for g in range(n_heads)]

    def kv_cols(g):
        return slice(0, SLOT) if shared_kv else slice(g * SLOT, (g + 1) * SLOT)

    def chunk(carry, rows):
        out = []
        for g in range(n_heads):
            m, acc = carry[g]
            k = k_ref[rows, kv_cols(g)]
            vt = vt_ref[kv_cols(g), rows]
            st = lax.dot_general(k, qs[g], (((1,), (1,)), ((), ())), preferred_element_type=F32)
            m_new = jnp.maximum(m, jnp.max(st, axis=0, keepdims=True))
            alpha = jnp.exp2(m - m_new)
            pt = jnp.exp2(st - m_new).astype(BF16)
            out.append((m_new, alpha * acc + jnp.dot(vt, pt, preferred_element_type=F32)))
        return tuple(out)

    def finish(carry):
        for g in range(n_heads):
            acc = carry[g][1]
            o = acc / acc[SUM_LANE:SUM_LANE + 1, :]
            o_ref[:, g * SLOT:(g + 1) * SLOT] = o.T.astype(o_ref.dtype)

    init = tuple((jnp.full((1, TQ), -jnp.inf, F32), jnp.zeros((SLOT, TQ), F32)) for _ in range(n_heads))
    c0 = chunk(init, pl.ds(0, ctx_len))
    is_ctx = (pl.program_id(2) + q_off) * TQ < ctx_len

    @pl.when(is_ctx)
    def _():
        finish(c0)

    n_lat = (s_len - ctx_len) // TK
    assert n_lat >= 3

    def rows_of(c):
        t0 = ctx_len + c * TK
        return pl.ds(t0 if isinstance(c, int) else pl.multiple_of(t0, math.gcd(ctx_len, TK)), TK)

    def scores(c, par):
        for g in range(n_heads):
            st_sc[par * n_heads + g] = lax.dot_general(k_ref[rows_of(c), kv_cols(g)], qs[g],
                                                       (((1,), (1,)), ((), ())), preferred_element_type=F32)

    def softmax(par, m):
        ms, alphas = [], []
        for g in range(n_heads):
            st = st_sc[par * n_heads + g]
            m_new = jnp.maximum(m[g], jnp.max(st, axis=0, keepdims=True))
            pt_sc[par * n_heads + g] = jnp.exp2(st - m_new).astype(BF16)
            ms.append(m_new)
            alphas.append(jnp.exp2(m[g] - m_new))
        return tuple(ms), tuple(alphas)

    def values(c, par, alpha):
        for g in range(n_heads):
            acc_sc[g] = alpha[g] * acc_sc[g] + jnp.dot(vt_ref[kv_cols(g), rows_of(c)],
                                                       pt_sc[par * n_heads + g], preferred_element_type=F32)

    def stage(c, par, m, alpha):
        scores(c + 1, 1 - par)
        m_new, alpha_new = softmax(par, m)
        values(c - 1, 1 - par, alpha)
        return m_new, alpha_new

    assert n_lat % 2 == 0

    @pl.when(jnp.logical_not(is_ctx))
    def _():
        for g in range(n_heads):
            acc_sc[g] = c0[g][1]
        scores(0, 0)
        m, alpha = softmax(0, tuple(c[0] for c in c0))
        scores(1, 1)

        def body(t, carry):
            c = 2 * t + 1
            return stage(c + 1, 0, *stage(c, 1, *carry))

        m, alpha = lax.fori_loop(0, (n_lat - 2) // 2, body, (m, alpha))
        values(n_lat - 2, 0, alpha)
        m, alpha = softmax(1, m)
        values(n_lat - 1, 1, alpha)
        finish(tuple((m[g], acc_sc[g]) for g in range(n_heads)))


def _attention(q_all, k_all, vt_all, ctx_len, need_ctx, q_slot0, kv_slot0, n_groups, n_heads, shared_kv):
    b, s, _ = q_all.shape
    q_off = 0 if need_ctx else ctx_len // TQ
    nq = s // TQ - q_off
    qw = n_heads * SLOT
    kw = SLOT if shared_kv else qw
    assert (q_slot0 * SLOT) % qw == 0 and (kv_slot0 * SLOT) % kw == 0
    qb0, kb0 = q_slot0 * SLOT // qw, kv_slot0 * SLOT // kw
    return pl.pallas_call(
        functools.partial(_attn_kernel, ctx_len, q_off, n_heads, shared_kv),
        grid=(b, n_groups, nq),
        in_specs=[pl.BlockSpec((None, TQ, qw), lambda i, g, j: (i, j + q_off, qb0 + g)),
                  pl.BlockSpec((None, s, kw), lambda i, g, j: (i, 0, kb0 + g)),
                  pl.BlockSpec((None, kw, s), lambda i, g, j: (i, kb0 + g, 0))],
        out_specs=pl.BlockSpec((None, TQ, qw), lambda i, g, j: (i, j + q_off, g)),
        out_shape=jax.ShapeDtypeStruct((b, s, n_groups * qw), BF16),
        scratch_shapes=[pltpu.VMEM((2 * n_heads, TK, TQ), F32), pltpu.VMEM((2 * n_heads, TK, TQ), BF16),
                        pltpu.VMEM((n_heads, SLOT, TQ), F32)],
        compiler_params=_cparams(("parallel", "parallel", "parallel")),
        name="attention",
    )(q_all, k_all, vt_all)


def _mixout_kernel(ctx_len, blk_off, x_ref, mod_ref, hf_ref, hb_ref, mo_ref, oa_ref, ol_ref, mn_ref, w_ref, o_ref):
    tm = x_ref.shape[0]
    t0 = (pl.program_id(1) + blk_off) * tm
    hm = hf_ref[...].astype(F32) + hb_ref[...].astype(F32)
    gate = _sigmoid(mo_ref[...].astype(F32))
    parts = []
    for i in range(M_HEADS):
        x = hm[:, i * SLOT:(i + 1) * SLOT]
        inv = lax.rsqrt(jnp.sum(x * x, axis=-1, keepdims=True) * (1.0 / M_DH) + EPS)
        parts.append(x * inv)
    hn = jnp.concatenate(parts, axis=-1) * mn_ref[...] * gate
    nm = M_HEADS * SLOT
    o = jnp.dot(hn.astype(BF16), w_ref[0:nm, :], preferred_element_type=F32)
    na = nm + A_HEADS * SLOT
    o = o + jnp.dot(oa_ref[...], w_ref[nm:na, :], preferred_element_type=F32)
    o = o + jnp.dot(ol_ref[...], w_ref[na:, :], preferred_element_type=F32)
    o_ref[...] = x_ref[...] + _gate_rows(mod_ref, t0, tm, ctx_len, 2) * o


def _mixout(xs, modtab, hf, hb, mo, o_gqa, o_mla, m_norm_p, w_out_p, ctx_len, blk_off):
    b, s, d = xs.shape
    tok = lambda n: pl.BlockSpec((None, TB, n), lambda i, j: (i, j + blk_off, 0))
    full = lambda a: pl.BlockSpec(a.shape, lambda i, j: (0,) * a.ndim)
    return pl.pallas_call(
        functools.partial(_mixout_kernel, ctx_len, blk_off),
        grid=(b, s // TB - blk_off),
        in_specs=[tok(d),
                  pl.BlockSpec((None, 2, 6, d), lambda i, j: (i, 0, 0, 0)),
                  tok(M_HEADS * SLOT), tok(M_HEADS * SLOT), tok(M_HEADS * SLOT), tok(A_HEADS * SLOT),
                  tok(L_HEADS * SLOT), full(m_norm_p), full(w_out_p)],
        out_specs=tok(d),
        out_shape=jax.ShapeDtypeStruct((b, s, d), F32),
        input_output_aliases={0: 0},
        compiler_params=_cparams(("parallel", "parallel")),
        name="mix_out",
    )(xs, modtab, hf, hb, mo, o_gqa, o_mla, m_norm_p, w_out_p)


def _ffn_kernel(ctx_len, x_ref, mod_ref, g_ref, wg_ref, wu_ref, wd_ref, o_ref, h_sc, acc_sc):
    tm = x_ref.shape[0]
    t0 = pl.program_id(1) * tm
    f = pl.program_id(2)

    @pl.when(f == 0)
    def _():
        h_sc[...] = _modulated(x_ref[...], g_ref[...], mod_ref, t0, ctx_len, 3, 4).astype(BF16)
        acc_sc[...] = jnp.zeros_like(acc_sc)

    h = h_sc[...]
    a = jnp.dot(h, wg_ref[...], preferred_element_type=F32)
    u = jnp.dot(h, wu_ref[...], preferred_element_type=F32)
    acc_sc[...] += jnp.dot((_silu(a) * u).astype(BF16), wd_ref[...], preferred_element_type=F32)

    @pl.when(f == pl.num_programs(2) - 1)
    def _():
        o_ref[...] = x_ref[...] + _gate_rows(mod_ref, t0, tm, ctx_len, 5) * acc_sc[...]


def _ffn(xs, modtab, gain, wg, wu, wd, ctx_len):
    b, s, d = xs.shape
    tm = s // 4
    nf = wg.shape[1] // TF
    return pl.pallas_call(
        functools.partial(_ffn_kernel, ctx_len),
        grid=(b, s // tm, nf),
        in_specs=[pl.BlockSpec((None, tm, d), lambda i, j, f: (i, j, 0)),
                  pl.BlockSpec((None, 2, 6, d), lambda i, j, f: (i, 0, 0, 0)),
                  pl.BlockSpec((1, d), lambda i, j, f: (0, 0)),
                  pl.BlockSpec((d, TF), lambda i, j, f: (0, f)),
                  pl.BlockSpec((d, TF), lambda i, j, f: (0, f)),
                  pl.BlockSpec((TF, d), lambda i, j, f: (f, 0))],
        out_specs=pl.BlockSpec((None, tm, d), lambda i, j, f: (i, j, 0)),
        out_shape=jax.ShapeDtypeStruct((b, s, d), F32),
        scratch_shapes=[pltpu.VMEM((tm, d), BF16), pltpu.VMEM((tm, d), F32)],
        input_output_aliases={0: 0},
        compiler_params=_cparams(("parallel", "parallel", "arbitrary")),
        name="ffn_dense",
    )(xs, modtab, gain, wg, wu, wd)


def _top2_combine(logits):
    lane = _lane_iota(logits.shape)
    lane_f = lane.astype(F32)
    lg = jnp.where(lane < N_EXPERTS, logits, -jnp.inf)
    v1 = jnp.max(lg, axis=-1, keepdims=True)
    i1 = jnp.min(jnp.where(lg == v1, lane_f, float(LANE)), axis=-1, keepdims=True)
    rest = jnp.where(lane_f == i1, -jnp.inf, lg)
    v2 = jnp.max(rest, axis=-1, keepdims=True)
    i2 = jnp.min(jnp.where(rest == v2, lane_f, float(LANE)), axis=-1, keepdims=True)
    e2 = jnp.exp(v2 - v1)
    w1 = 1.0 / (1.0 + e2)
    w2 = e2 / (1.0 + e2)
    sel = jnp.where((lane_f == i1) | (lane_f == i2), 1.0, 0.0)
    return jnp.where(lane_f == i1, w1, 0.0) + jnp.where(lane_f == i2, w2, 0.0), sel


def _moe_kernel(ctx_len, route_ctx, x_ref, mod_ref, g_ref, r_ref, tri_ref, wg_ref, wu_ref, wd_ref, o_ref,
                h_sc, comb_sc, rank_sc, rankt_sc, rankc_sc, wc_sc, xg_sc, acc_sc, y_sc, nt_sc):
    tm = x_ref.shape[0]
    t0 = pl.program_id(1) * tm
    e = pl.program_id(2)
    f = pl.program_id(3)

    @pl.when(jnp.logical_and(e == 0, f == 0))
    def _():
        h = _modulated(x_ref[...], g_ref[...], mod_ref, t0, ctx_len, 3, 4)
        logits = jnp.dot(h, r_ref[...], preferred_element_type=F32, precision=lax.Precision.HIGHEST)
        comb, sel = _top2_combine(logits)
        if not route_ctx:
            live = (t0 + _row_iota((tm, 1))) >= ctx_len
            comb = jnp.where(live, comb, 0.0)
            sel = jnp.where(live, sel, 0.0)
        rank = jnp.dot(tri_ref[...], sel.astype(BF16), preferred_element_type=F32)
        rank = jnp.where(sel > 0.0, rank, -1.0)
        comb_sc[...] = comb
        rank_sc[...] = rank
        pad = rankt_sc.shape[1] - tm
        rankt_sc[...] = jnp.concatenate([rank, jnp.full((pad, LANE), -1.0, F32)], axis=0).T
        h_sc[0:tm, :] = h.astype(BF16)
        h_sc[tm:, :] = jnp.zeros((pad, h_sc.shape[1]), BF16)
        y_sc[...] = jnp.zeros_like(y_sc)

    @pl.when(f == 0)
    def _():
        lane = _lane_iota((tm, LANE))
        rank_c = jnp.sum(jnp.where(lane == e, rank_sc[...], 0.0), axis=-1, keepdims=True)
        rankc_sc[...] = rank_c
        wc_sc[...] = jnp.sum(jnp.where(lane == e, comb_sc[...], 0.0), axis=-1, keepdims=True)
        n_rows = jnp.sum(jnp.where(rank_c >= 0.0, 1.0, 0.0)).astype(jnp.int32)
        n_tiles = lax.shift_right_logical(n_rows + (RT - 1), RT.bit_length() - 1)
        nt_sc[0] = n_tiles
        rank_r = rankt_sc[pl.ds(e, 1), :]

        def gather(i, c):
            r0 = pl.multiple_of(i * RT, RT)
            tgt = (r0 + _row_iota((RT, 1))).astype(F32)
            sel_t = jnp.where(rank_r == tgt, 1.0, 0.0).astype(BF16)
            xg_sc[pl.ds(r0, RT), :] = jnp.dot(sel_t, h_sc[...], preferred_element_type=F32).astype(BF16)
            acc_sc[pl.ds(r0, RT), :] = jnp.zeros((RT, acc_sc.shape[1]), F32)
            return c

        lax.fori_loop(0, n_tiles, gather, 0)

    def expert(i, c):
        r0 = pl.multiple_of(i * RT, RT)
        rows = xg_sc[pl.ds(r0, RT), :]
        a = jnp.dot(rows, wg_ref[...], preferred_element_type=F32)
        u = jnp.dot(rows, wu_ref[...], preferred_element_type=F32)
        acc_sc[pl.ds(r0, RT), :] += jnp.dot((_silu(a) * u).astype(BF16), wd_ref[...],
                                            preferred_element_type=F32)
        return c

    lax.fori_loop(0, nt_sc[0], expert, 0)

    @pl.when(f == pl.num_programs(3) - 1)
    def _():
        def scatter(i, c):
            r0 = pl.multiple_of(i * RT, RT)
            tgt = (r0 + _lane_iota((1, RT))).astype(F32)
            w_t = jnp.where(rankc_sc[...] == tgt, wc_sc[...], 0.0).astype(BF16)
            y_sc[...] += jnp.dot(w_t, acc_sc[pl.ds(r0, RT), :].astype(BF16), preferred_element_type=F32)
            return c

        lax.fori_loop(0, nt_sc[0], scatter, 0)

    @pl.when(jnp.logical_and(e == pl.num_programs(2) - 1, f == pl.num_programs(3) - 1))
    def _():
        o_ref[...] = x_ref[...] + _gate_rows(mod_ref, t0, tm, ctx_len, 5) * y_sc[...]


def _moe(xs, modtab, gain, router_p, wg, wu, wd, ctx_len, route_ctx):
    b, s, d = xs.shape
    tm = s // 4
    tp = -(-tm // RT) * RT
    ne, _, dff = wg.shape
    tri = jnp.tril(jnp.ones((tm, tm), BF16), -1)
    return pl.pallas_call(
        functools.partial(_moe_kernel, ctx_len, route_ctx),
        grid=(b, s // tm, ne, dff // TF),
        in_specs=[pl.BlockSpec((None, tm, d), lambda i, j, e, f: (i, j, 0)),
                  pl.BlockSpec((None, 2, 6, d), lambda i, j, e, f: (i, 0, 0, 0)),
                  pl.BlockSpec((1, d), lambda i, j, e, f: (0, 0)),
                  pl.BlockSpec((d, LANE), lambda i, j, e, f: (0, 0)),
                  pl.BlockSpec((tm, tm), lambda i, j, e, f: (0, 0)),
                  pl.BlockSpec((None, d, TF), lambda i, j, e, f: (e, 0, f)),
                  pl.BlockSpec((None, d, TF), lambda i, j, e, f: (e, 0, f)),
                  pl.BlockSpec((None, TF, d), lambda i, j, e, f: (e, f, 0))],
        out_specs=pl.BlockSpec((None, tm, d), lambda i, j, e, f: (i, j, 0)),
        out_shape=jax.ShapeDtypeStruct((b, s, d), F32),
        scratch_shapes=[pltpu.VMEM((tp, d), BF16),
                        pltpu.VMEM((tm, LANE), F32),
                        pltpu.VMEM((tm, LANE), F32),
                        pltpu.VMEM((LANE, tp), F32),
                        pltpu.VMEM((tm, 1), F32),
                        pltpu.VMEM((tm, 1), F32),
                        pltpu.VMEM((tp, d), BF16),
                        pltpu.VMEM((tp, d), F32),
                        pltpu.VMEM((tm, d), F32),
                        pltpu.SMEM((1,), jnp.int32)],
        input_output_aliases={0: 0},
        compiler_params=_cparams(("parallel", "parallel", "arbitrary", "arbitrary")),
        name="moe_top2",
    )(xs, modtab, gain, router_p, tri, wg, wu, wd)


def _pad_heads(w, n_heads, dh, axis=-1):
    axis = axis % w.ndim
    shp = w.shape[:axis] + (n_heads, dh) + w.shape[axis + 1:]
    pad = [(0, 0)] * (w.ndim + 1)
    pad[axis + 1] = (0, SLOT - dh)
    out = jnp.pad(w.reshape(shp), pad)
    return out.reshape(w.shape[:axis] + (n_heads * SLOT,) + w.shape[axis + 1:])


def _rope_tables(seq, ctx_len):
    t = jnp.arange(seq)
    rows = (t // GRID_W).astype(F32)
    cols = (t % GRID_W).astype(F32)

    def angles(rot_dim):
        nf = rot_dim // 4
        inv = ROPE_THETA ** (-jnp.arange(nf, dtype=F32) / nf)
        ar = rows[:, None] * inv
        ac = cols[:, None] * inv
        return jnp.concatenate([ar, ar, ac, ac], axis=-1)

    def slot_tables(rot_dim, lane0):
        ang = angles(rot_dim)
        quarter = rot_dim // 4
        first = (jnp.arange(rot_dim) % (2 * quarter)) < quarter
        cos = jnp.ones((seq, SLOT), F32).at[:, lane0:lane0 + rot_dim].set(jnp.cos(ang))
        sin = jnp.sin(ang)
        sin_m = jnp.zeros((seq, SLOT), F32).at[:, lane0:lane0 + rot_dim].set(jnp.where(first, -sin, 0.0))
        sin_p = jnp.zeros((seq, SLOT), F32).at[:, lane0:lane0 + rot_dim].set(jnp.where(first, 0.0, sin))
        ident = jnp.stack([jnp.ones((ctx_len, SLOT), F32), jnp.zeros((ctx_len, SLOT), F32),
                           jnp.zeros((ctx_len, SLOT), F32)])
        return jnp.concatenate([ident, jnp.stack([cos, sin_m, sin_p])], axis=1)

    return jnp.concatenate([slot_tables(A_DH, 0), slot_tables(L_ROPE, L_NOPE)], axis=0)


def _layer_params(w_in, m_conv, m_gate_b, m_norm, a_qnorm, a_knorm, l_cq_norm, l_ckv_norm, l_wuq, l_wukv,
                  l_qnorm, l_knorm, w_out):
    d = w_in.shape[0]
    offs = np.cumsum((0,) + IN_SIZES)
    seg = [w_in[:, offs[i]:offs[i + 1]] for i in range(len(IN_SIZES))]
    misc = jnp.zeros((d, SLOT), F32).at[:, 0:4 * M_HEADS].set(seg[4]).at[:, MISC_KR:MISC_KR + L_ROPE].set(seg[10])
    w_in_p = jnp.concatenate(
        [_pad_heads(seg[0], M_HEADS, M_DH), _pad_heads(seg[1], M_HEADS, M_DH),
         _pad_heads(seg[2], M_HEADS, M_DH), _pad_heads(seg[3], M_HEADS, M_DH), misc,
         _pad_heads(seg[5], A_HEADS, A_DH), _pad_heads(seg[6], A_KV, A_DH), _pad_heads(seg[7], A_KV, A_DH),
         seg[8], seg[9]], axis=1).astype(BF16)
    conv_w = jnp.concatenate([_pad_heads(m_conv[:, :M_HEADS * M_DH], M_HEADS, M_DH),
                              _pad_heads(m_conv[:, M_HEADS * M_DH:], M_HEADS, M_DH)], axis=1)
    conv_scale = jnp.concatenate([jnp.ones((1, M_HEADS * SLOT), F32),
                                  jnp.full((1, M_HEADS * SLOT), M_DH ** -0.5, F32)], axis=1)
    gate_b = jnp.zeros((1, SLOT), F32).at[0, 0:4 * M_HEADS].set(m_gate_b)
    pad1 = lambda g: jnp.pad(g, (0, SLOT - g.shape[0]))
    an = jnp.stack([pad1(a_qnorm), pad1(a_knorm)])
    ln = jnp.stack([pad1(l_qnorm), pad1(l_knorm)])
    wuq_p = _pad_heads(l_wuq, L_HEADS, L_QK).astype(BF16)
    kv = l_wukv.reshape(L_KVRANK, L_HEADS, L_NOPE + L_DV)
    wukv_p = jnp.concatenate(
        [_pad_heads(kv[:, :, :L_NOPE].reshape(L_KVRANK, -1), L_HEADS, L_NOPE),
         _pad_heads(kv[:, :, L_NOPE:].reshape(L_KVRANK, -1), L_HEADS, L_DV)], axis=1).astype(BF16)
    nm, na = M_HEADS * M_DH, A_HEADS * A_DH
    w_out_p = jnp.concatenate(
        [_pad_heads(w_out[:nm], M_HEADS, M_DH, axis=0), _pad_heads(w_out[nm:nm + na], A_HEADS, A_DH, axis=0),
         _pad_heads(w_out[nm + na:], L_HEADS, L_DV, axis=0)], axis=0).astype(BF16)
    m_norm_p = _pad_heads(m_norm[None, :], M_HEADS, M_DH)
    return dict(w_in_p=w_in_p, conv_w=conv_w, conv_scale=conv_scale, gate_b=gate_b, an=an, ln=ln,
                cqn=l_cq_norm[None, :], ckvn=l_ckv_norm[None, :], wuq_p=wuq_p, wukv_p=wukv_p,
                w_out_p=w_out_p, m_norm_p=m_norm_p)


def kernel(x, c, ctx, c_ctx, mod_w, mod_b, norm_mix, norm_ffn, w_in, m_conv, m_gate_b, m_norm, a_qnorm, a_knorm,
           l_cq_norm, l_ckv_norm, l_wuq, l_wukv, l_qnorm, l_knorm, w_out, ffn_wg, ffn_wu, ffn_wd, moe_router,
           moe_wg, moe_wu, moe_wd):
    b, seq, d = x.shape
    ctx_len = ctx.shape[1]
    depth = mod_w.shape[0]
    assert ctx_len % TB == 0 and ctx_len % TQ == 0 and ctx_len == ML and seq % TK == 0 and seq % GRID_W == 0
    xs = jnp.concatenate([ctx, x], axis=1)
    mod_rows = 16
    cc = jnp.zeros((mod_rows, d), F32).at[:b].set(c).at[b].set(c_ctx)
    mod_all = _mod_table(cc, mod_w, mod_b)
    rope_tab = _rope_tables(seq, ctx_len)
    for i in range(depth):
        need_ctx = i < depth - 1
        lat = mod_all[i, :b].reshape(b, 1, 6, d)
        cm = jnp.broadcast_to(mod_all[i, b].reshape(1, 1, 6, d), (b, 1, 6, d))
        modtab = jnp.concatenate([cm, lat], axis=1)
        p = _layer_params(w_in[i], m_conv[i], m_gate_b[i], m_norm[i], a_qnorm[i], a_knorm[i], l_cq_norm[i],
                          l_ckv_norm[i], l_wuq[i], l_wukv[i], l_qnorm[i], l_knorm[i], w_out[i])
        mqk, mv, mo, misc, q_all, k_all, v_all = _inproj(
            xs, modtab, norm_mix[i][None, :], p["w_in_p"], rope_tab, p["an"], p["ln"], p["cqn"], p["ckvn"],
            p["wuq_p"], p["wukv_p"], ctx_len)
        qk_c = _conv(mqk, p["conv_w"], p["conv_scale"], ctx_len)
        hf, hb = _mlstm(qk_c, mv, misc, p["gate_b"])
        o_gqa = _attention(q_all, k_all, v_all, ctx_len, need_ctx, 0, 0, A_KV, A_HEADS // A_KV, True)
        o_mla = _attention(q_all, k_all, v_all, ctx_len, need_ctx, A_HEADS, A_KV, L_HEADS // 2, 2, False)
        xs = _mixout(xs, modtab, hf, hb, mo, o_gqa, o_mla, p["m_norm_p"], p["w_out_p"], ctx_len,
                     0 if need_ctx else ctx_len // TB)
        j = i // 2
        if i % 2 == 0:
            xs = _ffn(xs, modtab, norm_ffn[i][None, :], ffn_wg[j].astype(BF16), ffn_wu[j].astype(BF16),
                      ffn_wd[j].astype(BF16), ctx_len)
        else:
            router_p = jnp.pad(moe_router[j], ((0, 0), (0, LANE - N_EXPERTS)))
            xs = _moe(xs, modtab, norm_ffn[i][None, :], router_p, moe_wg[j].astype(BF16),
                      moe_wu[j].astype(BF16), moe_wd[j].astype(BF16), ctx_len, need_ctx)
    return xs[:, ctx_len:]
```

```python
import functools
import math

import numpy as np
import jax
import jax.numpy as jnp
from jax import lax
from jax.experimental import pallas as pl
from jax.experimental.pallas import tpu as pltpu

F32 = jnp.float32
BF16 = jnp.bfloat16

GRID_W = 64
EPS = 1e-6
ROPE_THETA = 10000.0
M_HEADS, M_DH = 4, 96
A_HEADS, A_KV, A_DH = 6, 2, 64
L_HEADS, L_NOPE, L_ROPE, L_DV = 4, 64, 32, 64
L_QK = L_NOPE + L_ROPE
L_QRANK, L_KVRANK = 256, 128
N_EXPERTS, TOP_K = 8, 2
IN_SIZES = (384, 384, 384, 384, 16, 384, 128, 128, 256, 128, 32)

LANE = 128
SLOT = LANE
N_QHEADS = A_HEADS + L_HEADS
N_KVHEADS = A_KV + L_HEADS
VMEM_LIMIT = 56 * 1024 * 1024

O_MQK, O_MV, O_MO, O_MISC = 0, 1024, 1536, 2048
O_QA, O_KA, O_VA, O_CQ, O_CKV, IN_PAD = 2176, 2944, 3200, 3456, 3712, 3840
MISC_KR = 64
DEN_LANE = M_DH
SUM_LANE = A_DH
LOG2E = math.log2(math.e)
assert A_DH == L_DV and SUM_LANE < SLOT

TB = 256
TQ = 256
TK = 512
TM_FFN = 1088
TF = 512
ML = 256
RT = 160
GT = 256
TF_MOE = 896


def _cparams(sem):
    return pltpu.CompilerParams(dimension_semantics=sem, vmem_limit_bytes=VMEM_LIMIT)


def _sigmoid(x):
    return 1.0 / (1.0 + jnp.exp(-x))


def _silu(x):
    return x * _sigmoid(x)


def _lane_iota(shape):
    return lax.broadcasted_iota(jnp.int32, shape, len(shape) - 1)


def _row_iota(shape):
    return lax.broadcasted_iota(jnp.int32, shape, 0)


def _modulated(x, gain, mod_ref, t0, ctx_len, k_shift, k_scale):
    tm = x.shape[0]
    is_ctx = (t0 + _row_iota((tm, 1))) < ctx_len
    shift = jnp.where(is_ctx, mod_ref[0, k_shift:k_shift + 1, :], mod_ref[1, k_shift:k_shift + 1, :])
    scale = jnp.where(is_ctx, mod_ref[0, k_scale:k_scale + 1, :], mod_ref[1, k_scale:k_scale + 1, :])
    y = x * lax.rsqrt(jnp.mean(x * x, axis=-1, keepdims=True) + EPS) * gain
    return y * (1.0 + scale) + shift


def _gate_rows(mod_ref, t0, tm, ctx_len, k_gate):
    is_ctx = (t0 + _row_iota((tm, 1))) < ctx_len
    return jnp.where(is_ctx, mod_ref[0, k_gate:k_gate + 1, :], mod_ref[1, k_gate:k_gate + 1, :])


def _mod_kernel(c_ref, w_ref, b_ref, o_ref):
    s = _silu(c_ref[...]).astype(BF16)
    o_ref[...] = jnp.dot(s, w_ref[...].astype(BF16), preferred_element_type=F32) + b_ref[...]


def _mod_table(cc, mod_w, mod_b):
    depth, d, n = mod_w.shape
    rows = cc.shape[0]
    return pl.pallas_call(
        _mod_kernel,
        grid=(depth, n // d),
        in_specs=[pl.BlockSpec((rows, d), lambda l, j: (0, 0)),
                  pl.BlockSpec((None, d, d), lambda l, j: (l, 0, j)),
                  pl.BlockSpec((None, 1, d), lambda l, j: (l, 0, j))],
        out_specs=pl.BlockSpec((None, rows, d), lambda l, j: (l, 0, j)),
        out_shape=jax.ShapeDtypeStruct((depth, rows, n), F32),
        compiler_params=_cparams(("arbitrary", "arbitrary")),
        name="mod_table",
    )(cc, mod_w, mod_b.reshape(depth, 1, n))


def _slot_rms(x, lo, hi):
    lane = _lane_iota(x.shape)
    sq = jnp.where((lane >= lo) & (lane < hi), x * x, 0.0)
    return lax.rsqrt(jnp.sum(sq, axis=-1, keepdims=True) * (1.0 / (hi - lo)) + EPS)


def _rope(x, cos, sin_m, sin_p, quarter):
    return (x * cos + pltpu.roll(x, LANE - quarter, 1) * sin_m + pltpu.roll(x, quarter, 1) * sin_p)


def _inproj_kernel(ctx_len, x_ref, mod_ref, g_ref, w_ref, rope_ref, an_ref, ln_ref, cqn_ref, ckvn_ref,
                   wuq_ref, wukv_ref, mqk_ref, mv_ref, mo_ref, misc_ref, q_ref, k_ref, vt_ref):
    tm = x_ref.shape[0]
    t0 = pl.program_id(1) * tm
    h = _modulated(x_ref[...], g_ref[...], mod_ref, t0, ctx_len, 0, 1).astype(BF16)

    def proj(a, b):
        return jnp.dot(h, w_ref[:, a:b], preferred_element_type=F32)

    mqk_ref[...] = proj(O_MQK, O_MV).astype(BF16)
    mv_ref[...] = proj(O_MV, O_MO).astype(BF16)
    mo_ref[...] = proj(O_MO, O_MISC).astype(BF16)
    misc = proj(O_MISC, O_QA)
    misc_ref[...] = misc

    cos_a, sinm_a, sinp_a = rope_ref[0], rope_ref[1], rope_ref[2]
    cos_l, sinm_l, sinp_l = rope_ref[3], rope_ref[4], rope_ref[5]
    a_scale = A_DH ** -0.5 * LOG2E
    l_scale = L_QK ** -0.5 * LOG2E

    def with_ones_t(v):
        return jnp.where(_lane_iota(v.shape) % SLOT == SUM_LANE, 1.0, v).T.astype(BF16)

    pa = proj(O_QA, O_CQ)
    gq, gk = an_ref[0:1, :], an_ref[1:2, :]
    for i in range(A_HEADS):
        x = pa[:, i * SLOT:(i + 1) * SLOT]
        x = x * _slot_rms(x, 0, A_DH) * gq
        q_ref[:, i * SLOT:(i + 1) * SLOT] = (_rope(x, cos_a, sinm_a, sinp_a, A_DH // 4) * a_scale).astype(BF16)
    for i in range(A_KV):
        x = pa[:, (A_HEADS + i) * SLOT:(A_HEADS + i + 1) * SLOT]
        x = x * _slot_rms(x, 0, A_DH) * gk
        k_ref[:, i * SLOT:(i + 1) * SLOT] = _rope(x, cos_a, sinm_a, sinp_a, A_DH // 4).astype(BF16)
    vt_ref[0:A_KV * SLOT, :] = with_ones_t(pa[:, (A_HEADS + A_KV) * SLOT:(A_HEADS + 2 * A_KV) * SLOT])

    pc = proj(O_CQ, IN_PAD)
    cq = pc[:, 0:L_QRANK]
    cq = (cq * lax.rsqrt(jnp.mean(cq * cq, axis=-1, keepdims=True) + EPS) * cqn_ref[...]).astype(BF16)
    ckv = pc[:, L_QRANK:L_QRANK + L_KVRANK]
    ckv = (ckv * lax.rsqrt(jnp.mean(ckv * ckv, axis=-1, keepdims=True) + EPS) * ckvn_ref[...]).astype(BF16)
    ql = jnp.dot(cq, wuq_ref[...], preferred_element_type=F32)
    kvl = jnp.dot(ckv, wukv_ref[...], preferred_element_type=F32)
    gq_l, gk_l = ln_ref[0:1, :], ln_ref[1:2, :]
    lane = _lane_iota((tm, SLOT))
    kr = jnp.where((lane >= MISC_KR) & (lane < MISC_KR + L_ROPE), misc, 0.0)
    kr = kr * _slot_rms(kr, MISC_KR, MISC_KR + L_ROPE) * gk_l
    kr = _rope(kr, cos_l, sinm_l, sinp_l, L_ROPE // 4)
    for i in range(L_HEADS):
        x = ql[:, i * SLOT:(i + 1) * SLOT]
        inv = jnp.where(lane < L_NOPE, _slot_rms(x, 0, L_NOPE), _slot_rms(x, L_NOPE, L_QK))
        x = x * inv * gq_l
        q_ref[:, (A_HEADS + i) * SLOT:(A_HEADS + i + 1) * SLOT] = (
            _rope(x, cos_l, sinm_l, sinp_l, L_ROPE // 4) * l_scale).astype(BF16)
        kn = kvl[:, i * SLOT:(i + 1) * SLOT]
        kn = kn * _slot_rms(kn, 0, L_NOPE) * gk_l
        k_ref[:, (A_KV + i) * SLOT:(A_KV + i + 1) * SLOT] = (kn + kr).astype(BF16)
    vt_ref[A_KV * SLOT:, :] = with_ones_t(kvl[:, L_HEADS * SLOT:])


def _inproj(xs, modtab, gain, w_in_p, rope_tab, an, ln, cqn, ckvn, wuq_p, wukv_p, ctx_len):
    b, s, d = xs.shape
    grid = (b, s // TB)
    tok = lambda n: pl.BlockSpec((None, TB, n), lambda i, j: (i, j, 0))
    full = lambda a: pl.BlockSpec(a.shape, lambda i, j: (0,) * a.ndim)
    out_widths = (1024, 512, 512, SLOT, N_QHEADS * SLOT, N_KVHEADS * SLOT)
    out_dtypes = (BF16, BF16, BF16, F32, BF16, BF16)
    kvw = N_KVHEADS * SLOT
    out_specs = [tok(n) for n in out_widths] + [pl.BlockSpec((None, kvw, TB), lambda i, j: (i, 0, j))]
    out_shape = [jax.ShapeDtypeStruct((b, s, n), dt) for n, dt in zip(out_widths, out_dtypes)]
    out_shape.append(jax.ShapeDtypeStruct((b, kvw, s), BF16))
    return pl.pallas_call(
        functools.partial(_inproj_kernel, ctx_len),
        grid=grid,
        in_specs=[tok(d),
                  pl.BlockSpec((None, 2, 6, d), lambda i, j: (i, 0, 0, 0)),
                  full(gain), full(w_in_p),
                  pl.BlockSpec((6, TB, SLOT), lambda i, j: (0, j, 0)),
                  full(an), full(ln), full(cqn), full(ckvn), full(wuq_p), full(wukv_p)],
        out_specs=out_specs,
        out_shape=out_shape,
        compiler_params=_cparams(("parallel", "parallel")),
        name="inproj",
    )(xs, modtab, gain, w_in_p, rope_tab, an, ln, cqn, ckvn, wuq_p, wukv_p)


def _conv_kernel(ctx_len, s_len, x_ref, prev_ref, next_ref, w_ref, sc_ref, o_ref):
    tm = x_ref.shape[0]
    t0 = pl.program_id(1) * tm
    x = x_ref[...].astype(F32)
    row = _row_iota((tm, 1))
    has_prev = jnp.logical_and(t0 != 0, t0 != ctx_len)
    has_next = jnp.logical_and(t0 + tm != ctx_len, t0 + tm != s_len)
    hp = jnp.where(has_prev, prev_ref[15:16, :].astype(F32), 0.0)
    hn = jnp.where(has_next, next_ref[0:1, :].astype(F32), 0.0)
    xp = jnp.where(row == 0, hp, pltpu.roll(x, 1, 0))
    xn = jnp.where(row == tm - 1, hn, pltpu.roll(x, tm - 1, 0))
    y = xp * w_ref[0:1, :] + x * w_ref[1:2, :] + xn * w_ref[2:3, :]
    o_ref[...] = (_silu(y) * sc_ref[...]).astype(o_ref.dtype)


def _conv(mqk, conv_w, conv_scale, ctx_len):
    b, s, n = mqk.shape
    hb = TB // 16
    last = s // 16 - 1
    return pl.pallas_call(
        functools.partial(_conv_kernel, ctx_len, s),
        grid=(b, s // TB),
        in_specs=[pl.BlockSpec((None, TB, n), lambda i, j: (i, j, 0)),
                  pl.BlockSpec((None, 16, n), lambda i, j: (i, jnp.maximum(j * hb - 1, 0), 0)),
                  pl.BlockSpec((None, 16, n), lambda i, j: (i, jnp.minimum((j + 1) * hb, last), 0)),
                  pl.BlockSpec((3, n), lambda i, j: (0, 0)),
                  pl.BlockSpec((1, n), lambda i, j: (0, 0))],
        out_specs=pl.BlockSpec((None, TB, n), lambda i, j: (i, j, 0)),
        out_shape=jax.ShapeDtypeStruct((b, s, n), BF16),
        compiler_params=_cparams(("parallel", "parallel")),
        name="mlstm_conv",
    )(mqk, mqk, mqk, conv_w, conv_scale)


def _log_sigmoid(x):
    return jnp.minimum(x, 0.0) - jnp.log(1.0 + jnp.exp(-jnp.abs(x)))


def _mlstm_kernel(n_chunks, q_ref, k_ref, v_ref, misc_ref, gb_ref, hf_ref, hb_ref, c_ref, m_ref):
    hp = pl.program_id(1)
    c_ref[...] = jnp.zeros_like(c_ref)
    m_ref[...] = jnp.zeros_like(m_ref)
    r = _row_iota((ML, ML))
    cidx = _lane_iota((ML, ML))
    tri_f = (r >= cidx)
    tri_b = (r <= cidx)
    tri_f32 = tri_f.astype(F32)
    tri_b32 = tri_b.astype(F32)
    lane = _lane_iota((ML, SLOT))

    def chunk(t0, d, tri, tri32, out_ref):
        g = misc_ref[pl.ds(t0, ML), :] + gb_ref[...]
        logf = _log_sigmoid(g)
        bcum = jnp.dot(tri32, logf, preferred_element_type=F32, precision=lax.Precision.HIGHEST)
        bcum_t = bcum.T
        g_t = g.T
        for hh in range(2):
            head = hp * 2 + hh
            li = d * 2 * M_HEADS + head
            sel_i = (lane == li)
            sel_f = (lane == li + M_HEADS)
            i_col = jnp.sum(jnp.where(sel_i, g, 0.0), axis=-1, keepdims=True)
            b_col = jnp.sum(jnp.where(sel_f, bcum, 0.0), axis=-1, keepdims=True)
            rsel_i = (_row_iota((SLOT, ML)) == li)
            rsel_f = (_row_iota((SLOT, ML)) == li + M_HEADS)
            i_row = jnp.sum(jnp.where(rsel_i, g_t, 0.0), axis=0, keepdims=True)
            b_row = jnp.sum(jnp.where(rsel_f, bcum_t, 0.0), axis=0, keepdims=True)
            sidx = d * 2 + hh
            c_st = c_ref[sidx]
            m_st = m_ref[sidx]
            m_s = m_st[:, 0:1]
            q = q_ref[pl.ds(t0, ML), hh * SLOT:(hh + 1) * SLOT]
            k = k_ref[pl.ds(t0, ML), hh * SLOT:(hh + 1) * SLOT]
            v = v_ref[pl.ds(t0, ML), hh * SLOT:(hh + 1) * SLOT]
            v = jnp.where(lane == DEN_LANE, 1.0, v.astype(F32)).astype(BF16)
            dmat = jnp.where(tri, b_col - b_row + i_row, -jnp.inf)
            m_inter = b_col + m_s
            m_t = jnp.maximum(m_inter, jnp.max(dmat, axis=-1, keepdims=True))
            qk = lax.dot_general(q, k, (((1,), (1,)), ((), ())), preferred_element_type=F32)
            w = jnp.exp(dmat - m_t) * qk
            a_inter = jnp.exp(m_inter - m_t)
            num = a_inter * jnp.dot(q, c_st.astype(BF16), preferred_element_type=F32) + jnp.dot(
                w.astype(BF16), v, preferred_element_type=F32)
            den = jnp.sum(jnp.where(lane == DEN_LANE, num, 0.0), axis=-1, keepdims=True)
            h_out = num / jnp.maximum(jnp.abs(den), jnp.exp(-m_t))
            out_ref[pl.ds(t0, ML), hh * SLOT:(hh + 1) * SLOT] = h_out.astype(out_ref.dtype)
            total = b_col[ML - 1:ML, :] if d == 0 else b_col[0:1, :]
            gg = total - b_col + i_col
            m_new = jnp.maximum(total + m_s, jnp.max(gg, axis=0, keepdims=True))
            decay = jnp.exp(total + m_s - m_new)
            wk = jnp.exp(gg - m_new)
            kw = (k.astype(F32) * wk).astype(BF16)
            c_ref[sidx] = decay * c_st + lax.dot_general(
                kw, v, (((0,), (0,)), ((), ())), preferred_element_type=F32)
            m_ref[sidx] = jnp.broadcast_to(m_new, (1, SLOT))

    def step(n, carry):
        tf0 = pl.multiple_of(n * ML, ML)
        tb0 = pl.multiple_of(jnp.where(n == 0, 0, n_chunks - n) * ML, ML)
        chunk(tf0, 0, tri_f, tri_f32, hf_ref)
        chunk(tb0, 1, tri_b, tri_b32, hb_ref)
        return carry

    lax.fori_loop(0, n_chunks, step, 0)


def _mlstm(qk_c, mv, misc, gate_b):
    b, s, _ = mv.shape
    n_chunks = s // ML
    pair = 2 * SLOT
    kspec = pl.BlockSpec((None, s, pair), lambda i, j: (i, 0, 2 + j))
    spec = pl.BlockSpec((None, s, pair), lambda i, j: (i, 0, j))
    return pl.pallas_call(
        functools.partial(_mlstm_kernel, n_chunks),
        grid=(b, M_HEADS // 2),
        in_specs=[spec, kspec, spec,
                  pl.BlockSpec((None, s, SLOT), lambda i, j: (i, 0, 0)),
                  pl.BlockSpec((1, SLOT), lambda i, j: (0, 0))],
        out_specs=[spec, spec],
        out_shape=[jax.ShapeDtypeStruct((b, s, M_HEADS * SLOT), BF16)] * 2,
        scratch_shapes=[pltpu.VMEM((4, SLOT, SLOT), F32), pltpu.VMEM((4, 1, SLOT), F32)],
        compiler_params=_cparams(("parallel", "parallel")),
        name="mlstm_scan",
    )(qk_c, qk_c, mv, misc, gate_b)


def _attn_kernel(ctx_len, q_off, n_heads, shared_kv, q_ref, k_ref, vt_ref, o_ref, st_sc, pt_sc, acc_sc):
    s_len = k_ref.shape[0]
    qs = [q_ref[:, g * SLOT:(g + 1) * SLOT] for g in range(n_heads)]

    def kv_cols(g):
        return slice(0, SLOT) if shared_kv else slice(g * SLOT, (g + 1) * SLOT)

    def chunk(carry, rows):
        out = []
        for g in range(n_heads):
            m, acc = carry[g]
            k = k_ref[rows, kv_cols(g)]
            vt = vt_ref[kv_cols(g), rows]
            st = lax.dot_general(k, qs[g], (((1,), (1,)), ((), ())), preferred_element_type=F32)
            m_new = jnp.maximum(m, jnp.max(st, axis=0, keepdims=True))
            alpha = jnp.exp2(m - m_new)
            pt = jnp.exp2(st - m_new).astype(BF16)
            out.append((m_new, alpha * acc + jnp.dot(vt, pt, preferred_element_type=F32)))
        return tuple(out)

    def finish(carry):
        for g in range(n_heads):
            acc = carry[g][1]
            o = acc / acc[SUM_LANE:SUM_LANE + 1, :]
            o_ref[:, g * SLOT:(g + 1) * SLOT] = o.T.astype(o_ref.dtype)

    init = tuple((jnp.full((1, TQ), -jnp.inf, F32), jnp.zeros((SLOT, TQ), F32)) for _ in range(n_heads))
    c0 = chunk(init, pl.ds(0, ctx_len))
    is_ctx = (pl.program_id(2) + q_off) * TQ < ctx_len

    @pl.when(is_ctx)
    def _():
        finish(c0)

    n_lat = (s_len - ctx_len) // TK
    assert n_lat >= 3

    def rows_of(c):
        t0 = ctx_len + c * TK
        return pl.ds(t0 if isinstance(c, int) else pl.multiple_of(t0, math.gcd(ctx_len, TK)), TK)

    def scores(c, par):
        for g in range(n_heads):
            st_sc[par * n_heads + g] = lax.dot_general(k_ref[rows_of(c), kv_cols(g)], qs[g],
                                                       (((1,), (1,)), ((), ())), preferred_element_type=F32)

    def softmax(par, m):
        ms, alphas = [], []
        for g in range(n_heads):
            st = st_sc[par * n_heads + g]
            m_new = jnp.maximum(m[g], jnp.max(st, axis=0, keepdims=True))
            pt_sc[par * n_heads + g] = jnp.exp2(st - m_new).astype(BF16)
            ms.append(m_new)
            alphas.append(jnp.exp2(m[g] - m_new))
        return tuple(ms), tuple(alphas)

    def values(c, par, alpha):
        for g in range(n_heads):
            acc_sc[g] = alpha[g] * acc_sc[g] + jnp.dot(vt_ref[kv_cols(g), rows_of(c)],
                                                       pt_sc[par * n_heads + g], preferred_element_type=F32)

    def stage(c, par, m, alpha):
        scores(c + 1, 1 - par)
        m_new, alpha_new = softmax(par, m)
        values(c - 1, 1 - par, alpha)
        return m_new, alpha_new

    assert n_lat % 2 == 0

    @pl.when(jnp.logical_not(is_ctx))
    def _():
        for g in range(n_heads):
            acc_sc[g] = c0[g][1]
        scores(0, 0)
        m, alpha = softmax(0, tuple(c[0] for c in c0))
        scores(1, 1)

        def body(t, carry):
            c = 2 * t + 1
            return stage(c + 1, 0, *stage(c, 1, *carry))

        m, alpha = lax.fori_loop(0, (n_lat - 2) // 2, body, (m, alpha))
        values(n_lat - 2, 0, alpha)
        m, alpha = softmax(1, m)
        values(n_lat - 1, 1, alpha)
        finish(tuple((m[g], acc_sc[g]) for g in range(n_heads)))


def _attention(q_all, k_all, vt_all, ctx_len, need_ctx, q_slot0, kv_slot0, n_groups, n_heads, shared_kv):
    b, s, _ = q_all.shape
    q_off = 0 if need_ctx else ctx_len // TQ
    nq = s // TQ - q_off
    qw = n_heads * SLOT
    kw = SLOT if shared_kv else qw
    assert (q_slot0 * SLOT) % qw == 0 and (kv_slot0 * SLOT) % kw == 0
    qb0, kb0 = q_slot0 * SLOT // qw, kv_slot0 * SLOT // kw
    return pl.pallas_call(
        functools.partial(_attn_kernel, ctx_len, q_off, n_heads, shared_kv),
        grid=(b, n_groups, nq),
        in_specs=[pl.BlockSpec((None, TQ, qw), lambda i, g, j: (i, j + q_off, qb0 + g)),
                  pl.BlockSpec((None, s, kw), lambda i, g, j: (i, 0, kb0 + g)),
                  pl.BlockSpec((None, kw, s), lambda i, g, j: (i, kb0 + g, 0))],
        out_specs=pl.BlockSpec((None, TQ, qw), lambda i, g, j: (i, j + q_off, g)),
        out_shape=jax.ShapeDtypeStruct((b, s, n_groups * qw), BF16),
        scratch_shapes=[pltpu.VMEM((2 * n_heads, TK, TQ), F32), pltpu.VMEM((2 * n_heads, TK, TQ), BF16),
                        pltpu.VMEM((n_heads, SLOT, TQ), F32)],
        compiler_params=_cparams(("parallel", "parallel", "parallel")),
        name="attention",
    )(q_all, k_all, vt_all)


def _mixout_kernel(ctx_len, blk_off, x_ref, mod_ref, hf_ref, hb_ref, mo_ref, oa_ref, ol_ref, mn_ref, w_ref, o_ref):
    tm = x_ref.shape[0]
    t0 = (pl.program_id(1) + blk_off) * tm
    hm = hf_ref[...].astype(F32) + hb_ref[...].astype(F32)
    gate = _sigmoid(mo_ref[...].astype(F32))
    parts = []
    for i in range(M_HEADS):
        x = hm[:, i * SLOT:(i + 1) * SLOT]
        inv = lax.rsqrt(jnp.sum(x * x, axis=-1, keepdims=True) * (1.0 / M_DH) + EPS)
        parts.append(x * inv)
    hn = jnp.concatenate(parts, axis=-1) * mn_ref[...] * gate
    nm = M_HEADS * SLOT
    o = jnp.dot(hn.astype(BF16), w_ref[0:nm, :], preferred_element_type=F32)
    na = nm + A_HEADS * SLOT
    o = o + jnp.dot(oa_ref[...], w_ref[nm:na, :], preferred_element_type=F32)
    o = o + jnp.dot(ol_ref[...], w_ref[na:, :], preferred_element_type=F32)
    o_ref[...] = x_ref[...] + _gate_rows(mod_ref, t0, tm, ctx_len, 2) * o


def _mixout(xs, modtab, hf, hb, mo, o_gqa, o_mla, m_norm_p, w_out_p, ctx_len, lat_only):
    b, s, d = xs.shape
    blk_off = ctx_len // TB if lat_only else 0
    tok = lambda n: pl.BlockSpec((None, TB, n), lambda i, j: (i, j + blk_off, 0))
    full = lambda a: pl.BlockSpec(a.shape, lambda i, j: (0,) * a.ndim)
    return pl.pallas_call(
        functools.partial(_mixout_kernel, ctx_len, blk_off),
        grid=(b, s // TB - blk_off),
        in_specs=[tok(d),
                  pl.BlockSpec((None, 2, 6, d), lambda i, j: (i, 0, 0, 0)),
                  tok(M_HEADS * SLOT), tok(M_HEADS * SLOT), tok(M_HEADS * SLOT), tok(A_HEADS * SLOT),
                  tok(L_HEADS * SLOT), full(m_norm_p), full(w_out_p)],
        out_specs=pl.BlockSpec((None, TB, d), lambda i, j: (i, j, 0)),
        out_shape=jax.ShapeDtypeStruct((b, s - blk_off * TB, d), F32),
        input_output_aliases={} if lat_only else {0: 0},
        compiler_params=_cparams(("parallel", "parallel")),
        name="mix_out",
    )(xs, modtab, hf, hb, mo, o_gqa, o_mla, m_norm_p, w_out_p)


def _ffn_kernel(ctx_len, x_ref, mod_ref, g_ref, wg_ref, wu_ref, wd_ref, o_ref, h_sc, acc_sc):
    tm = x_ref.shape[0]
    t0 = pl.program_id(1) * tm
    f = pl.program_id(2)

    @pl.when(f == 0)
    def _():
        h_sc[...] = _modulated(x_ref[...], g_ref[...], mod_ref, t0, ctx_len, 3, 4).astype(BF16)
        acc_sc[...] = jnp.zeros_like(acc_sc)

    h = h_sc[...]
    a = jnp.dot(h, wg_ref[...], preferred_element_type=F32)
    u = jnp.dot(h, wu_ref[...], preferred_element_type=F32)
    acc_sc[...] += jnp.dot((_silu(a) * u).astype(BF16), wd_ref[...], preferred_element_type=F32)

    @pl.when(f == pl.num_programs(2) - 1)
    def _():
        o_ref[...] = x_ref[...] + _gate_rows(mod_ref, t0, tm, ctx_len, 5) * acc_sc[...]


def _ffn(xs, modtab, gain, wg, wu, wd, ctx_len):
    b, s, d = xs.shape
    tm = s // 4
    nf = wg.shape[1] // TF
    return pl.pallas_call(
        functools.partial(_ffn_kernel, ctx_len),
        grid=(b, s // tm, nf),
        in_specs=[pl.BlockSpec((None, tm, d), lambda i, j, f: (i, j, 0)),
                  pl.BlockSpec((None, 2, 6, d), lambda i, j, f: (i, 0, 0, 0)),
                  pl.BlockSpec((1, d), lambda i, j, f: (0, 0)),
                  pl.BlockSpec((d, TF), lambda i, j, f: (0, f)),
                  pl.BlockSpec((d, TF), lambda i, j, f: (0, f)),
                  pl.BlockSpec((TF, d), lambda i, j, f: (f, 0))],
        out_specs=pl.BlockSpec((None, tm, d), lambda i, j, f: (i, j, 0)),
        out_shape=jax.ShapeDtypeStruct((b, s, d), F32),
        scratch_shapes=[pltpu.VMEM((tm, d), BF16), pltpu.VMEM((tm, d), F32)],
        input_output_aliases={0: 0},
        compiler_params=_cparams(("parallel", "parallel", "arbitrary")),
        name="ffn_dense",
    )(xs, modtab, gain, wg, wu, wd)


def _top2_combine(logits):
    lane = _lane_iota(logits.shape)
    lane_f = lane.astype(F32)
    lg = jnp.where(lane < N_EXPERTS, logits, -jnp.inf)
    v1 = jnp.max(lg, axis=-1, keepdims=True)
    i1 = jnp.min(jnp.where(lg == v1, lane_f, float(LANE)), axis=-1, keepdims=True)
    rest = jnp.where(lane_f == i1, -jnp.inf, lg)
    v2 = jnp.max(rest, axis=-1, keepdims=True)
    i2 = jnp.min(jnp.where(rest == v2, lane_f, float(LANE)), axis=-1, keepdims=True)
    e2 = jnp.exp(v2 - v1)
    w1 = 1.0 / (1.0 + e2)
    w2 = e2 / (1.0 + e2)
    sel = jnp.where((lane_f == i1) | (lane_f == i2), 1.0, 0.0)
    return jnp.where(lane_f == i1, w1, 0.0) + jnp.where(lane_f == i2, w2, 0.0), sel


def _moe_kernel(ctx_len, x_ref, mod_ref, g_ref, r_ref, tri_ref, wg_ref, wu_ref, wd_ref, o_ref,
                h_sc, comb_sc, rank_sc, rankt_sc, rankc_sc, wc_sc, xg_sc, acc_sc, y_sc, nt_sc):
    tm = x_ref.shape[0]
    t0 = pl.program_id(1) * tm
    e = pl.program_id(2)
    f = pl.program_id(3)

    @pl.when(jnp.logical_and(e == 0, f == 0))
    def _():
        h = _modulated(x_ref[...], g_ref[...], mod_ref, t0, ctx_len, 3, 4)
        logits = jnp.dot(h, r_ref[...], preferred_element_type=F32, precision=lax.Precision.HIGHEST)
        comb, sel = _top2_combine(logits)
        rank =jnp.dot(tri_ref[...], sel.astype(BF16), preferred_element_type=F32)
        rank = jnp.where(sel > 0.0, rank, -1.0)
        comb_sc[...] = comb
        rank_sc[...] = rank
        pad = rankt_sc.shape[1] - tm
        h_sc[0:tm, :] = h.astype(BF16)
        if pad:
            rankt_sc[...] = jnp.concatenate([rank, jnp.full((pad, LANE), -1.0, F32)], axis=0).T
            h_sc[tm:, :] = jnp.zeros((pad, h_sc.shape[1]), BF16)
        else:
            rankt_sc[...] = rank.T
        y_sc[...] = jnp.zeros_like(y_sc)

    @pl.when(f == 0)
    def _():
        lane = _lane_iota((tm, LANE))
        rank_c = jnp.sum(jnp.where(lane == e, rank_sc[...], 0.0), axis=-1, keepdims=True)
        rankc_sc[...] = rank_c
        wc_sc[...] = jnp.sum(jnp.where(lane == e, comb_sc[...], 0.0), axis=-1, keepdims=True)
        n_rows = jnp.sum(jnp.where(rank_c >= 0.0, 1.0, 0.0)).astype(jnp.int32)
        n_rt = lax.div(n_rows + (RT - 1), RT)
        n_gt = lax.div(n_rt * RT + (GT - 1), GT)
        nt_sc[0] = n_rt
        nt_sc[1] = n_gt
        rank_r = rankt_sc[pl.ds(e, 1), :]

        def gather(i, c):
            r0 = pl.multiple_of(i * GT, GT)
            tgt = (r0 + _row_iota((GT, 1))).astype(F32)
            sel_t = jnp.where(rank_r == tgt, 1.0, 0.0).astype(BF16)
            xg_sc[pl.ds(r0, GT), :] = jnp.dot(sel_t, h_sc[...], preferred_element_type=F32).astype(BF16)
            acc_sc[pl.ds(r0, GT), :] = jnp.zeros((GT, acc_sc.shape[1]), F32)
            return c

        lax.fori_loop(0, n_gt, gather, 0)

    def expert(i, c):
        r0 = pl.multiple_of(i * RT, 16)
        rows = xg_sc[pl.ds(r0, RT), :]
        a = jnp.dot(rows, wg_ref[...], preferred_element_type=F32)
        u = jnp.dot(rows, wu_ref[...], preferred_element_type=F32)
        acc_sc[pl.ds(r0, RT), :] += jnp.dot((_silu(a) * u).astype(BF16), wd_ref[...],
                                            preferred_element_type=F32)
        return c

    lax.fori_loop(0, nt_sc[0], expert, 0)

    @pl.when(f == pl.num_programs(3) - 1)
    def _():
        def scatter(i, c):
            r0 = pl.multiple_of(i * GT, GT)
            tgt = (r0 + _lane_iota((1, GT))).astype(F32)
            w_t = jnp.where(rankc_sc[...] == tgt, wc_sc[...], 0.0).astype(BF16)
            y_sc[...] += jnp.dot(w_t, acc_sc[pl.ds(r0, GT), :].astype(BF16), preferred_element_type=F32)
            return c

        lax.fori_loop(0, nt_sc[1], scatter, 0)

    @pl.when(jnp.logical_and(e == pl.num_programs(2) - 1, f == pl.num_programs(3) - 1))
    def _():
        o_ref[...] = x_ref[...] + _gate_rows(mod_ref, t0, tm, ctx_len, 5) * y_sc[...]


def _moe(xs, modtab, gain, router_p, wg, wu, wd, ctx_len):
    b, s, d = xs.shape
    tm = s // 4
    tp = -(-tm // LANE) * LANE
    tr = -(-(-(-tm // RT) * RT) // GT) * GT
    ne, _, dff = wg.shape
    tri = jnp.tril(jnp.ones((tm, tm), BF16), -1)
    TF = TF_MOE
    assert dff % TF == 0
    return pl.pallas_call(
        functools.partial(_moe_kernel, ctx_len),
        grid=(b, s // tm, ne, dff // TF),
        in_specs=[pl.BlockSpec((None, tm, d), lambda i, j, e, f: (i, j, 0)),
                  pl.BlockSpec((None, 2, 6, d), lambda i, j, e, f: (i, 0, 0, 0)),
                  pl.BlockSpec((1, d), lambda i, j, e, f: (0, 0)),
                  pl.BlockSpec((d, LANE), lambda i, j, e, f: (0, 0)),
                  pl.BlockSpec((tm, tm), lambda i, j, e, f: (0, 0)),
                  pl.BlockSpec((None, d, TF), lambda i, j, e, f: (e, 0, f)),
                  pl.BlockSpec((None, d, TF), lambda i, j, e, f: (e, 0, f)),
                  pl.BlockSpec((None, TF, d), lambda i, j, e, f: (e, f, 0))],
        out_specs=pl.BlockSpec((None, tm, d), lambda i, j, e, f: (i, j, 0)),
        out_shape=jax.ShapeDtypeStruct((b, s, d), F32),
        scratch_shapes=[pltpu.VMEM((tp, d), BF16),
                        pltpu.VMEM((tm, LANE), F32),
                        pltpu.VMEM((tm, LANE), F32),
                        pltpu.VMEM((LANE, tp), F32),
                        pltpu.VMEM((tm, 1), F32),
                        pltpu.VMEM((tm, 1), F32),
                        pltpu.VMEM((tr, d), BF16),
                        pltpu.VMEM((tr, d), F32),
                        pltpu.VMEM((tm, d), F32),
                        pltpu.SMEM((2,), jnp.int32)],
        input_output_aliases={0: 0},
        compiler_params=_cparams(("parallel", "parallel", "arbitrary", "arbitrary")),
        name="moe_top2",
    )(xs, modtab, gain, router_p, tri, wg, wu, wd)


def _pad_heads(w, n_heads, dh, axis=-1):
    axis = axis % w.ndim
    shp = w.shape[:axis] + (n_heads, dh) + w.shape[axis + 1:]
    pad = [(0, 0)] * (w.ndim + 1)
    pad[axis + 1] = (0, SLOT - dh)
    out = jnp.pad(w.reshape(shp), pad)
    return out.reshape(w.shape[:axis] + (n_heads * SLOT,) + w.shape[axis + 1:])


def _rope_tables(seq, ctx_len):
    t = jnp.arange(seq)
    rows = (t // GRID_W).astype(F32)
    cols = (t % GRID_W).astype(F32)

    def angles(rot_dim):
        nf = rot_dim // 4
        inv = ROPE_THETA ** (-jnp.arange(nf, dtype=F32) / nf)
        ar = rows[:, None] * inv
        ac = cols[:, None] * inv
        return jnp.concatenate([ar, ar, ac, ac], axis=-1)

    def slot_tables(rot_dim, lane0):
        ang = angles(rot_dim)
        quarter = rot_dim // 4
        first = (jnp.arange(rot_dim) % (2 * quarter)) < quarter
        cos = jnp.ones((seq, SLOT), F32).at[:, lane0:lane0 + rot_dim].set(jnp.cos(ang))
        sin = jnp.sin(ang)
        sin_m = jnp.zeros((seq, SLOT), F32).at[:, lane0:lane0 + rot_dim].set(jnp.where(first, -sin, 0.0))
        sin_p = jnp.zeros((seq, SLOT), F32).at[:, lane0:lane0 + rot_dim].set(jnp.where(first, 0.0, sin))
        ident = jnp.stack([jnp.ones((ctx_len, SLOT), F32), jnp.zeros((ctx_len, SLOT), F32),
                           jnp.zeros((ctx_len, SLOT), F32)])
        return jnp.concatenate([ident, jnp.stack([cos, sin_m, sin_p])], axis=1)

    return jnp.concatenate([slot_tables(A_DH, 0), slot_tables(L_ROPE, L_NOPE)], axis=0)


def _layer_params(w_in, m_conv, m_gate_b, m_norm, a_qnorm, a_knorm, l_cq_norm, l_ckv_norm, l_wuq, l_wukv,
                  l_qnorm, l_knorm, w_out):
    d = w_in.shape[0]
    offs = np.cumsum((0,) + IN_SIZES)
    seg = [w_in[:, offs[i]:offs[i + 1]] for i in range(len(IN_SIZES))]
    misc = jnp.zeros((d, SLOT), F32).at[:, 0:4 * M_HEADS].set(seg[4]).at[:, MISC_KR:MISC_KR + L_ROPE].set(seg[10])
    w_in_p = jnp.concatenate(
        [_pad_heads(seg[0], M_HEADS, M_DH), _pad_heads(seg[1], M_HEADS, M_DH),
         _pad_heads(seg[2], M_HEADS, M_DH), _pad_heads(seg[3], M_HEADS, M_DH), misc,
         _pad_heads(seg[5], A_HEADS, A_DH), _pad_heads(seg[6], A_KV, A_DH), _pad_heads(seg[7], A_KV, A_DH),
         seg[8], seg[9]], axis=1).astype(BF16)
    conv_w = jnp.concatenate([_pad_heads(m_conv[:, :M_HEADS * M_DH], M_HEADS, M_DH),
                              _pad_heads(m_conv[:, M_HEADS * M_DH:], M_HEADS, M_DH)], axis=1)
    conv_scale = jnp.concatenate([jnp.ones((1, M_HEADS * SLOT), F32),
                                  jnp.full((1, M_HEADS * SLOT), M_DH ** -0.5, F32)], axis=1)
    gate_b = jnp.zeros((1, SLOT), F32).at[0, 0:4 * M_HEADS].set(m_gate_b)
    pad1 = lambda g: jnp.pad(g, (0, SLOT - g.shape[0]))
    an = jnp.stack([pad1(a_qnorm), pad1(a_knorm)])
    ln = jnp.stack([pad1(l_qnorm), pad1(l_knorm)])
    wuq_p = _pad_heads(l_wuq, L_HEADS, L_QK).astype(BF16)
    kv = l_wukv.reshape(L_KVRANK, L_HEADS, L_NOPE + L_DV)
    wukv_p = jnp.concatenate(
        [_pad_heads(kv[:, :, :L_NOPE].reshape(L_KVRANK, -1), L_HEADS, L_NOPE),
         _pad_heads(kv[:, :, L_NOPE:].reshape(L_KVRANK, -1), L_HEADS, L_DV)], axis=1).astype(BF16)
    nm, na = M_HEADS * M_DH, A_HEADS * A_DH
    w_out_p = jnp.concatenate(
        [_pad_heads(w_out[:nm], M_HEADS, M_DH, axis=0), _pad_heads(w_out[nm:nm + na], A_HEADS, A_DH, axis=0),
         _pad_heads(w_out[nm + na:], L_HEADS, L_DV, axis=0)], axis=0).astype(BF16)
    m_norm_p = _pad_heads(m_norm[None, :], M_HEADS, M_DH)
    return dict(w_in_p=w_in_p, conv_w=conv_w, conv_scale=conv_scale, gate_b=gate_b, an=an, ln=ln,
                cqn=l_cq_norm[None, :], ckvn=l_ckv_norm[None, :], wuq_p=wuq_p, wukv_p=wukv_p,
                w_out_p=w_out_p, m_norm_p=m_norm_p)


def kernel(x, c, ctx, c_ctx, mod_w, mod_b, norm_mix, norm_ffn, w_in, m_conv, m_gate_b, m_norm, a_qnorm, a_knorm,
           l_cq_norm, l_ckv_norm, l_wuq, l_wukv, l_qnorm, l_knorm, w_out, ffn_wg, ffn_wu, ffn_wd, moe_router,
           moe_wg, moe_wu, moe_wd):
    b, seq, d = x.shape
    ctx_len = ctx.shape[1]
    depth = mod_w.shape[0]
    assert ctx_len % TB == 0 and ctx_len % TQ == 0 and ctx_len == ML and seq % TK == 0 and seq % GRID_W == 0
    xs = jnp.concatenate([ctx, x], axis=1)
    mod_rows = 16
    cc = jnp.zeros((mod_rows, d), F32).at[:b].set(c).at[b].set(c_ctx)
    mod_all = _mod_table(cc, mod_w, mod_b)
    rope_tab = _rope_tables(seq, ctx_len)
    for i in range(depth):
        need_ctx = i < depth - 1
        lat = mod_all[i, :b].reshape(b, 1, 6, d)
        cm = jnp.broadcast_to(mod_all[i, b].reshape(1, 1, 6, d), (b, 1, 6, d))
        modtab = jnp.concatenate([cm, lat], axis=1)
        p = _layer_params(w_in[i], m_conv[i], m_gate_b[i], m_norm[i], a_qnorm[i], a_knorm[i], l_cq_norm[i],
                          l_ckv_norm[i], l_wuq[i], l_wukv[i], l_qnorm[i], l_knorm[i], w_out[i])
        mqk, mv, mo, misc, q_all, k_all, v_all = _inproj(
            xs, modtab, norm_mix[i][None, :], p["w_in_p"], rope_tab, p["an"], p["ln"], p["cqn"], p["ckvn"],
            p["wuq_p"], p["wukv_p"], ctx_len)
        qk_c = _conv(mqk, p["conv_w"], p["conv_scale"], ctx_len)
        hf, hb = _mlstm(qk_c, mv, misc, p["gate_b"])
        o_gqa = _attention(q_all, k_all, v_all, ctx_len, need_ctx, 0, 0, A_KV, A_HEADS // A_KV, True)
        o_mla = _attention(q_all, k_all, v_all, ctx_len, need_ctx, A_HEADS, A_KV, L_HEADS // 2, 2, False)
        xs = _mixout(xs, modtab, hf, hb, mo, o_gqa, o_mla, p["m_norm_p"], p["w_out_p"], ctx_len, not need_ctx)
        ffn_ctx = ctx_len if need_ctx else 0
        j = i // 2
        if i % 2 == 0:
            xs = _ffn(xs, modtab, norm_ffn[i][None, :], ffn_wg[j].astype(BF16), ffn_wu[j].astype(BF16),
                      ffn_wd[j].astype(BF16), ffn_ctx)
        else:
            router_p = jnp.pad(moe_router[j], ((0, 0), (0, LANE - N_EXPERTS)))
            xs = _moe(xs, modtab, norm_ffn[i][None, :], router_p, moe_wg[j].astype(BF16),
                      moe_wu[j].astype(BF16), moe_wd[j].astype(BF16), ffn_ctx)
    return xs
```

```python
import functools
import math

import numpy as np
import jax
import jax.numpy as jnp
from jax import lax
from jax.experimental import pallas as pl
from jax.experimental.pallas import tpu as pltpu

F32 = jnp.float32
BF16 = jnp.bfloat16

GRID_W = 64
EPS = 1e-6
ROPE_THETA = 10000.0
M_HEADS, M_DH = 4, 96
A_HEADS, A_KV, A_DH = 6, 2, 64
L_HEADS, L_NOPE, L_ROPE, L_DV = 4, 64, 32, 64
L_QK = L_NOPE + L_ROPE
L_QRANK, L_KVRANK = 256, 128
N_EXPERTS, TOP_K = 8, 2
IN_SIZES = (384, 384, 384, 384, 16, 384, 128, 128, 256, 128, 32)

LANE = 128
SLOT = LANE
N_KVHEADS = A_KV + L_HEADS
Q_MLA = 8
Q_SLOTS = Q_MLA + L_HEADS
KV_GQA = L_HEADS
VMEM_LIMIT = 56 * 1024 * 1024

O_MQK, O_MV, O_MO, O_MISC = 0, 1024, 1536, 2048
O_QA, O_KA, O_VA, O_CQ, O_CKV, IN_PAD = 2176, 2944, 3200, 3456, 3712, 3840
MISC_KR = 64
DEN_LANE = M_DH
SUM_LANE = A_DH
LOG2E = math.log2(math.e)
assert A_DH == L_DV and SUM_LANE < SLOT

TB = 256
TQ = 256
TK = 512
TM_FFN = 1088
TF = 512
ML = 256
RT = 160
GT = 256
TF_MOE = 896


def _cparams(sem):
    return pltpu.CompilerParams(dimension_semantics=sem, vmem_limit_bytes=VMEM_LIMIT)


def _sigmoid(x):
    return 1.0 / (1.0 + jnp.exp(-x))


def _silu(x):
    return x * _sigmoid(x)


def _lane_iota(shape):
    return lax.broadcasted_iota(jnp.int32, shape, len(shape) - 1)


def _row_iota(shape):
    return lax.broadcasted_iota(jnp.int32, shape, 0)


def _modulated(x, gain, mod_ref, t0, ctx_len, k_shift, k_scale):
    tm = x.shape[0]
    is_ctx = (t0 + _row_iota((tm, 1))) < ctx_len
    shift = jnp.where(is_ctx, mod_ref[0, k_shift:k_shift + 1, :], mod_ref[1, k_shift:k_shift + 1, :])
    scale = jnp.where(is_ctx, mod_ref[0, k_scale:k_scale + 1, :], mod_ref[1, k_scale:k_scale + 1, :])
    y = x * lax.rsqrt(jnp.mean(x * x, axis=-1, keepdims=True) + EPS) * gain
    return y * (1.0 + scale) + shift


def _gate_rows(mod_ref, t0, tm, ctx_len, k_gate):
    is_ctx = (t0 + _row_iota((tm, 1))) < ctx_len
    return jnp.where(is_ctx, mod_ref[0, k_gate:k_gate + 1, :], mod_ref[1, k_gate:k_gate + 1, :])


def _mod_kernel(c_ref, w_ref, b_ref, o_ref):
    s = _silu(c_ref[...]).astype(BF16)
    o_ref[...] = jnp.dot(s, w_ref[...].astype(BF16), preferred_element_type=F32) + b_ref[...]


def _mod_table(cc, mod_w, mod_b):
    depth, d, n = mod_w.shape
    rows = cc.shape[0]
    return pl.pallas_call(
        _mod_kernel,
        grid=(depth, n // d),
        in_specs=[pl.BlockSpec((rows, d), lambda l, j: (0, 0)),
                  pl.BlockSpec((None, d, d), lambda l, j: (l, 0, j)),
                  pl.BlockSpec((None, 1, d), lambda l, j: (l, 0, j))],
        out_specs=pl.BlockSpec((None, rows, d), lambda l, j: (l, 0, j)),
        out_shape=jax.ShapeDtypeStruct((depth, rows, n), F32),
        compiler_params=_cparams(("arbitrary", "arbitrary")),
        name="mod_table",
    )(cc, mod_w, mod_b.reshape(depth, 1, n))


def _slot_rms(x, lo, hi):
    lane = _lane_iota(x.shape)
    sq = jnp.where((lane >= lo) & (lane < hi), x * x, 0.0)
    return lax.rsqrt(jnp.sum(sq, axis=-1, keepdims=True) * (1.0 / (hi - lo)) + EPS)


def _rope(x, cos, sin_m, sin_p, quarter):
    return (x * cos + pltpu.roll(x, LANE - quarter, 1) * sin_m + pltpu.roll(x, quarter, 1) * sin_p)


def _inproj_kernel(ctx_len, x_ref, mod_ref, g_ref, w_ref, rope_ref, an_ref, ln_ref, cqn_ref, ckvn_ref,
                   wuq_ref, wukv_ref, mqk_ref, mv_ref, mo_ref, misc_ref, q_ref, k_ref, vt_ref):
    tm = x_ref.shape[0]
    t0 = pl.program_id(1) * tm
    h = _modulated(x_ref[...], g_ref[...], mod_ref, t0, ctx_len, 0, 1).astype(BF16)

    def proj(a, b):
        return jnp.dot(h, w_ref[:, a:b], preferred_element_type=F32)

    mqk_ref[...] = proj(O_MQK, O_MV).astype(BF16)
    mv_ref[...] = proj(O_MV, O_MO).astype(BF16)
    mo_ref[...] = proj(O_MO, O_MISC).astype(BF16)
    misc = proj(O_MISC, O_QA)
    misc_ref[...] = misc

    cos_a, sinm_a, sinp_a = rope_ref[0], rope_ref[1], rope_ref[2]
    cos_l, sinm_l, sinp_l = rope_ref[3], rope_ref[4], rope_ref[5]
    a_scale = A_DH ** -0.5 * LOG2E
    l_scale = L_QK ** -0.5 * LOG2E

    def with_ones_t(v):
        return jnp.where(_lane_iota(v.shape) % SLOT == SUM_LANE, 1.0, v).T.astype(BF16)

    pa = proj(O_QA, O_CQ)
    gq, gk = an_ref[0:1, :], an_ref[1:2, :]
    for i in range(A_HEADS):
        x = pa[:, i * SLOT:(i + 1) * SLOT]
        x = x * _slot_rms(x, 0, A_DH) * gq
        q_ref[:, i * SLOT:(i + 1) * SLOT] = (_rope(x, cos_a, sinm_a, sinp_a, A_DH // 4) * a_scale).astype(BF16)
    for i in range(A_KV):
        x = pa[:, (A_HEADS + i) * SLOT:(A_HEADS + i + 1) * SLOT]
        x = x * _slot_rms(x, 0, A_DH) * gk
        k_ref[:, (KV_GQA + i) * SLOT:(KV_GQA + i + 1) * SLOT] = (
            _rope(x, cos_a, sinm_a, sinp_a, A_DH // 4).astype(BF16))
    vt_ref[KV_GQA * SLOT:, :] = with_ones_t(pa[:, (A_HEADS + A_KV) * SLOT:(A_HEADS + 2 * A_KV) * SLOT])
    q_ref[:, A_HEADS * SLOT:Q_MLA * SLOT] = jnp.zeros((tm, (Q_MLA - A_HEADS) * SLOT), BF16)

    pc = proj(O_CQ, IN_PAD)
    cq = pc[:, 0:L_QRANK]
    cq = (cq * lax.rsqrt(jnp.mean(cq * cq, axis=-1, keepdims=True) + EPS) * cqn_ref[...]).astype(BF16)
    ckv = pc[:, L_QRANK:L_QRANK + L_KVRANK]
    ckv = (ckv * lax.rsqrt(jnp.mean(ckv * ckv, axis=-1, keepdims=True) + EPS) * ckvn_ref[...]).astype(BF16)
    ql = jnp.dot(cq, wuq_ref[...], preferred_element_type=F32)
    kvl = jnp.dot(ckv, wukv_ref[...], preferred_element_type=F32)
    gq_l, gk_l = ln_ref[0:1, :], ln_ref[1:2, :]
    lane = _lane_iota((tm, SLOT))
    kr = jnp.where((lane >= MISC_KR) & (lane < MISC_KR + L_ROPE), misc, 0.0)
    kr = kr * _slot_rms(kr, MISC_KR, MISC_KR + L_ROPE) * gk_l
    kr = _rope(kr, cos_l, sinm_l, sinp_l, L_ROPE // 4)
    for i in range(L_HEADS):
        x = ql[:, i * SLOT:(i + 1) * SLOT]
        inv = jnp.where(lane < L_NOPE, _slot_rms(x, 0, L_NOPE), _slot_rms(x, L_NOPE, L_QK))
        x = x * inv * gq_l
        q_ref[:, (Q_MLA + i) * SLOT:(Q_MLA + i + 1) * SLOT] = (
            _rope(x, cos_l, sinm_l, sinp_l, L_ROPE // 4) * l_scale).astype(BF16)
        kn = kvl[:, i * SLOT:(i + 1) * SLOT]
        kn = kn * _slot_rms(kn, 0, L_NOPE) * gk_l
        k_ref[:, i * SLOT:(i + 1) * SLOT] = (kn + kr).astype(BF16)
    vt_ref[0:KV_GQA * SLOT, :] = with_ones_t(kvl[:, L_HEADS * SLOT:])


def _inproj(xs, modtab, gain, w_in_p, rope_tab, an, ln, cqn, ckvn, wuq_p, wukv_p, ctx_len):
    b, s, d = xs.shape
    grid = (b, s // TB)
    tok = lambda n: pl.BlockSpec((None, TB, n), lambda i, j: (i, j, 0))
    full = lambda a: pl.BlockSpec(a.shape, lambda i, j: (0,) * a.ndim)
    out_widths = (1024, 512, 512, SLOT, Q_SLOTS * SLOT, N_KVHEADS * SLOT)
    out_dtypes = (BF16, BF16, BF16, F32, BF16, BF16)
    kvw = N_KVHEADS * SLOT
    out_specs = [tok(n) for n in out_widths] + [pl.BlockSpec((None, kvw, TB), lambda i, j: (i, 0, j))]
    out_shape = [jax.ShapeDtypeStruct((b, s, n), dt) for n, dt in zip(out_widths, out_dtypes)]
    out_shape.append(jax.ShapeDtypeStruct((b, kvw, s), BF16))
    return pl.pallas_call(
        functools.partial(_inproj_kernel, ctx_len),
        grid=grid,
        in_specs=[tok(d),
                  pl.BlockSpec((None, 2, 6, d), lambda i, j: (i, 0, 0, 0)),
                  full(gain), full(w_in_p),
                  pl.BlockSpec((6, TB, SLOT), lambda i, j: (0, j, 0)),
                  full(an), full(ln), full(cqn), full(ckvn), full(wuq_p), full(wukv_p)],
        out_specs=out_specs,
        out_shape=out_shape,
        compiler_params=_cparams(("parallel", "parallel")),
        name="inproj",
    )(xs, modtab, gain, w_in_p, rope_tab, an, ln, cqn, ckvn, wuq_p, wukv_p)


def _conv_kernel(ctx_len, s_len, x_ref, prev_ref, next_ref, w_ref, sc_ref, o_ref):
    tm = x_ref.shape[0]
    t0 = pl.program_id(1) * tm
    x = x_ref[...].astype(F32)
    row = _row_iota((tm, 1))
    has_prev = jnp.logical_and(t0 != 0, t0 != ctx_len)
    has_next = jnp.logical_and(t0 + tm != ctx_len, t0 + tm != s_len)
    hp = jnp.where(has_prev, prev_ref[15:16, :].astype(F32), 0.0)
    hn = jnp.where(has_next, next_ref[0:1, :].astype(F32), 0.0)
    xp = jnp.where(row == 0, hp, pltpu.roll(x, 1, 0))
    xn = jnp.where(row == tm - 1, hn, pltpu.roll(x, tm - 1, 0))
    y = xp * w_ref[0:1, :] + x * w_ref[1:2, :] + xn * w_ref[2:3, :]
    o_ref[...] = (_silu(y) * sc_ref[...]).astype(o_ref.dtype)


def _conv(mqk, conv_w, conv_scale, ctx_len):
    b, s, n = mqk.shape
    hb = TB // 16
    last = s // 16 - 1
    return pl.pallas_call(
        functools.partial(_conv_kernel, ctx_len, s),
        grid=(b, s // TB),
        in_specs=[pl.BlockSpec((None, TB, n), lambda i, j: (i, j, 0)),
                  pl.BlockSpec((None, 16, n), lambda i, j: (i, jnp.maximum(j * hb - 1, 0), 0)),
                  pl.BlockSpec((None, 16, n), lambda i, j: (i, jnp.minimum((j + 1) * hb, last), 0)),
                  pl.BlockSpec((3, n), lambda i, j: (0, 0)),
                  pl.BlockSpec((1, n), lambda i, j: (0, 0))],
        out_specs=pl.BlockSpec((None, TB, n), lambda i, j: (i, j, 0)),
        out_shape=jax.ShapeDtypeStruct((b, s, n), BF16),
        compiler_params=_cparams(("parallel", "parallel")),
        name="mlstm_conv",
    )(mqk, mqk, mqk, conv_w, conv_scale)


def _log_sigmoid(x):
    return jnp.minimum(x, 0.0) - jnp.log(1.0 + jnp.exp(-jnp.abs(x)))


def _mlstm_kernel(n_chunks, q_ref, k_ref, v_ref, misc_ref, gb_ref, hf_ref, hb_ref, c_ref, m_ref):
    hp = pl.program_id(1)
    c_ref[...] = jnp.zeros_like(c_ref)
    m_ref[...] = jnp.zeros_like(m_ref)
    r = _row_iota((ML, ML))
    cidx = _lane_iota((ML, ML))
    tri_f = (r >= cidx)
    tri_b = (r <= cidx)
    tri_f32 = tri_f.astype(F32)
    tri_b32 = tri_b.astype(F32)
    lane = _lane_iota((ML, SLOT))

    def chunk(t0, d, tri, tri32, out_ref):
        g = misc_ref[pl.ds(t0, ML), :] + gb_ref[...]
        logf = _log_sigmoid(g)
        bcum = jnp.dot(tri32, logf, preferred_element_type=F32, precision=lax.Precision.HIGHEST)
        bcum_t = bcum.T
        g_t = g.T
        for hh in range(2):
            head = hp * 2 + hh
            li = d * 2 * M_HEADS + head
            sel_i = (lane == li)
            sel_f = (lane == li + M_HEADS)
            i_col = jnp.sum(jnp.where(sel_i, g, 0.0), axis=-1, keepdims=True)
            b_col = jnp.sum(jnp.where(sel_f, bcum, 0.0), axis=-1, keepdims=True)
            rsel_i = (_row_iota((SLOT, ML)) == li)
            rsel_f = (_row_iota((SLOT, ML)) == li + M_HEADS)
            i_row = jnp.sum(jnp.where(rsel_i, g_t, 0.0), axis=0, keepdims=True)
            b_row = jnp.sum(jnp.where(rsel_f, bcum_t, 0.0), axis=0, keepdims=True)
            sidx = d * 2 + hh
            c_st = c_ref[sidx]
            m_st = m_ref[sidx]
            m_s = m_st[:, 0:1]
            q = q_ref[pl.ds(t0, ML), hh * SLOT:(hh + 1) * SLOT]
            k = k_ref[pl.ds(t0, ML), hh * SLOT:(hh + 1) * SLOT]
            v = v_ref[pl.ds(t0, ML), hh * SLOT:(hh + 1) * SLOT]
            v = jnp.where(lane == DEN_LANE, 1.0, v.astype(F32)).astype(BF16)
            dmat = jnp.where(tri, b_col - b_row + i_row, -jnp.inf)
            m_inter = b_col + m_s
            m_t = jnp.maximum(m_inter, jnp.max(dmat, axis=-1, keepdims=True))
            qk = lax.dot_general(q, k, (((1,), (1,)), ((), ())), preferred_element_type=F32)
            w = jnp.exp(dmat - m_t) * qk
            a_inter = jnp.exp(m_inter - m_t)
            num = a_inter * jnp.dot(q, c_st.astype(BF16), preferred_element_type=F32) + jnp.dot(
                w.astype(BF16), v, preferred_element_type=F32)
            den = jnp.sum(jnp.where(lane == DEN_LANE, num, 0.0), axis=-1, keepdims=True)
            h_out = num / jnp.maximum(jnp.abs(den), jnp.exp(-m_t))
            out_ref[pl.ds(t0, ML), hh * SLOT:(hh + 1) * SLOT] = h_out.astype(out_ref.dtype)
            total = b_col[ML - 1:ML, :] if d == 0 else b_col[0:1, :]
            gg = total - b_col + i_col
            m_new = jnp.maximum(total + m_s, jnp.max(gg, axis=0, keepdims=True))
            decay = jnp.exp(total + m_s - m_new)
            wk = jnp.exp(gg - m_new)
            kw = (k.astype(F32) * wk).astype(BF16)
            c_ref[sidx] = decay * c_st + lax.dot_general(
                kw, v, (((0,), (0,)), ((), ())), preferred_element_type=F32)
            m_ref[sidx] = jnp.broadcast_to(m_new, (1, SLOT))

    def step(n, carry):
        tf0 = pl.multiple_of(n * ML, ML)
        tb0 = pl.multiple_of(jnp.where(n == 0, 0, n_chunks - n) * ML, ML)
        chunk(tf0, 0, tri_f, tri_f32, hf_ref)
        chunk(tb0, 1, tri_b, tri_b32, hb_ref)
        return carry

    lax.fori_loop(0, n_chunks, step, 0)


def _mlstm(qk_c, mv, misc, gate_b):
    b, s, _ = mv.shape
    n_chunks = s // ML
    pair = 2 * SLOT
    kspec = pl.BlockSpec((None, s, pair), lambda i, j: (i, 0, 2 + j))
    spec = pl.BlockSpec((None, s, pair), lambda i, j: (i, 0, j))
    return pl.pallas_call(
        functools.partial(_mlstm_kernel, n_chunks),
        grid=(b, M_HEADS // 2),
        in_specs=[spec, kspec, spec,
                  pl.BlockSpec((None, s, SLOT), lambda i, j: (i, 0, 0)),
                  pl.BlockSpec((1, SLOT), lambda i, j: (0, 0))],
        out_specs=[spec, spec],
        out_shape=[jax.ShapeDtypeStruct((b, s, M_HEADS * SLOT), BF16)] * 2,
        scratch_shapes=[pltpu.VMEM((4, SLOT, SLOT), F32), pltpu.VMEM((4, 1, SLOT), F32)],
        compiler_params=_cparams(("parallel", "parallel")),
        name="mlstm_scan",
    )(qk_c, qk_c, mv, misc, gate_b)


def _attn_kernel(ctx_len, q_off, kv_of, q_ref, k_ref, vt_ref, o_ref, st_sc, pt_sc, acc_sc):
    s_len = k_ref.shape[0]
    n_heads = len(kv_of)
    qs = [q_ref[:, g * SLOT:(g + 1) * SLOT] for g in range(n_heads)]

    def kv_cols(g):
        return slice(kv_of[g] * SLOT, (kv_of[g] + 1) * SLOT)

    def chunk(carry, rows):
        out = []
        for g in range(n_heads):
            m, acc = carry[g]
            k = k_ref[rows, kv_cols(g)]
            vt = vt_ref[kv_cols(g), rows]
            st = lax.dot_general(k, qs[g], (((1,), (1,)), ((), ())), preferred_element_type=F32)
            m_new = jnp.maximum(m, jnp.max(st, axis=0, keepdims=True))
            alpha = jnp.exp2(m - m_new)
            pt = jnp.exp2(st - m_new).astype(BF16)
            out.append((m_new, alpha * acc + jnp.dot(vt, pt, preferred_element_type=F32)))
        return tuple(out)

    def finish(carry):
        for g in range(n_heads):
            acc = carry[g][1]
            o = acc / acc[SUM_LANE:SUM_LANE + 1, :]
            o_ref[:, g * SLOT:(g + 1) * SLOT] = o.T.astype(o_ref.dtype)

    init = tuple((jnp.full((1, TQ), -jnp.inf, F32), jnp.zeros((SLOT, TQ), F32)) for _ in range(n_heads))
    is_ctx = (pl.program_id(2) + q_off) * TQ < ctx_len

    @pl.when(is_ctx)
    def _():
        finish(chunk(init, pl.ds(0, ctx_len)))

    n_lat = (s_len - ctx_len) // TK
    n0 = ctx_len + TK
    assert n_lat >= 4 and n_lat % 2 == 0 and st_sc.shape[1] == n0

    def rows_of(c):
        if isinstance(c, int):
            return (pl.ds(0, n0), n0) if c == 0 else (pl.ds(ctx_len + c * TK, TK), TK)
        return pl.ds(pl.multiple_of(ctx_len + c * TK, math.gcd(ctx_len, TK)), TK), TK

    def scores(c, par):
        rows, n = rows_of(c)
        for g in range(n_heads):
            st_sc[par * n_heads + g, 0:n, :] = lax.dot_general(
                k_ref[rows, kv_cols(g)], qs[g], (((1,), (1,)), ((), ())), preferred_element_type=F32)

    def softmax(par, m, n=TK):
        ms, alphas = [], []
        for g in range(n_heads):
            st = st_sc[par * n_heads + g, 0:n, :]
            m_new = jnp.maximum(m[g], jnp.max(st, axis=0, keepdims=True))
            pt_sc[par * n_heads + g, 0:n, :] = jnp.exp2(st - m_new).astype(BF16)
            ms.append(m_new)
            alphas.append(jnp.exp2(m[g] - m_new))
        return tuple(ms), tuple(alphas)

    def values(c, par, alpha):
        rows, n = rows_of(c)
        for g in range(n_heads):
            acc_sc[g] = alpha[g] * acc_sc[g] + jnp.dot(vt_ref[kv_cols(g), rows], pt_sc[par * n_heads + g, 0:n, :],
                                                       preferred_element_type=F32)

    def stage(c, par, m, alpha):
        scores(c + 1, 1 - par)
        m_new, alpha_new = softmax(par, m)
        values(c - 1, 1 - par, alpha)
        return m_new, alpha_new

    @pl.when(jnp.logical_not(is_ctx))
    def _():
        for g in range(n_heads):
            acc_sc[g] = init[g][1]
        scores(0, 0)
        m, alpha = softmax(0, tuple(c[0] for c in init), n0)
        scores(1, 1)
        m, alpha = stage(1, 1, m, alpha)

        def body(t, carry):
            c = 2 * t + 2
            return stage(c + 1, 1, *stage(c, 0, *carry))

        m, alpha = lax.fori_loop(0, (n_lat - 4) // 2, body, (m, alpha))
        m, alpha = stage(n_lat - 2, 0, m, alpha)
        values(n_lat - 2, 0, alpha)
        m, alpha = softmax(1, m)
        values(n_lat - 1, 1, alpha)
        finish(tuple((m[g], acc_sc[g]) for g in range(n_heads)))


def _attention(q_all, k_all, vt_all, ctx_len, need_ctx, q_slot0, kv_slot0, n_kv, kv_of):
    b, s, _ = q_all.shape
    q_off = 0 if need_ctx else ctx_len // TQ
    nq = s // TQ - q_off
    n_heads = len(kv_of)
    qw = n_heads * SLOT
    kw = n_kv * SLOT
    assert (q_slot0 * SLOT) % qw == 0 and (kv_slot0 * SLOT) % kw == 0
    qb0, kb0 = q_slot0 * SLOT // qw, kv_slot0 * SLOT // kw
    return pl.pallas_call(
        functools.partial(_attn_kernel, ctx_len, q_off, kv_of),
        grid=(b, 1, nq),
        in_specs=[pl.BlockSpec((None, TQ, qw), lambda i, g, j: (i, j + q_off, qb0)),
                  pl.BlockSpec((None, s, kw), lambda i, g, j: (i, 0, kb0)),
                  pl.BlockSpec((None, kw, s), lambda i, g, j: (i, kb0, 0))],
        out_specs=pl.BlockSpec((None, TQ, qw), lambda i, g, j: (i, j + q_off, 0)),
        out_shape=jax.ShapeDtypeStruct((b, s, qw), BF16),
        scratch_shapes=[pltpu.VMEM((2 * n_heads, ctx_len + TK, TQ), F32),
                        pltpu.VMEM((2 * n_heads, ctx_len + TK, TQ), BF16),
                        pltpu.VMEM((n_heads, SLOT, TQ), F32)],
        compiler_params=_cparams(("parallel", "parallel", "parallel")),
        name="attention",
    )(q_all, k_all, vt_all)


def _mixout_kernel(ctx_len, blk_off, x_ref, mod_ref, hf_ref, hb_ref, mo_ref, oa_ref, ol_ref, mn_ref, w_ref, o_ref):
    tm = x_ref.shape[0]
    t0 = (pl.program_id(1) + blk_off) * tm
    hm = hf_ref[...].astype(F32) + hb_ref[...].astype(F32)
    gate = _sigmoid(mo_ref[...].astype(F32))
    parts = []
    for i in range(M_HEADS):
        x = hm[:, i * SLOT:(i + 1) * SLOT]
        inv = lax.rsqrt(jnp.sum(x * x, axis=-1, keepdims=True) * (1.0 / M_DH) + EPS)
        parts.append(x * inv)
    hn = jnp.concatenate(parts, axis=-1) * mn_ref[...] * gate
    nm = M_HEADS * SLOT
    o = jnp.dot(hn.astype(BF16), w_ref[0:nm, :], preferred_element_type=F32)
    na = nm + A_HEADS * SLOT
    o = o + jnp.dot(oa_ref[...], w_ref[nm:na, :], preferred_element_type=F32)
    o = o + jnp.dot(ol_ref[...], w_ref[na:, :], preferred_element_type=F32)
    o_ref[...] = x_ref[...] + _gate_rows(mod_ref, t0, tm, ctx_len, 2) * o


def _mixout(xs, modtab, hf, hb, mo, o_gqa, o_mla, m_norm_p, w_out_p, ctx_len, lat_only):
    b, s, d = xs.shape
    blk_off = ctx_len // TB if lat_only else 0
    tok = lambda n: pl.BlockSpec((None, TB, n), lambda i, j: (i, j + blk_off, 0))
    full = lambda a: pl.BlockSpec(a.shape, lambda i, j: (0,) * a.ndim)
    return pl.pallas_call(
        functools.partial(_mixout_kernel, ctx_len, blk_off),
        grid=(b, s // TB - blk_off),
        in_specs=[tok(d),
                  pl.BlockSpec((None, 2, 6, d), lambda i, j: (i, 0, 0, 0)),
                  tok(M_HEADS * SLOT), tok(M_HEADS * SLOT), tok(M_HEADS * SLOT), tok(A_HEADS * SLOT),
                  tok(L_HEADS * SLOT), full(m_norm_p), full(w_out_p)],
        out_specs=pl.BlockSpec((None, TB, d), lambda i, j: (i, j, 0)),
        out_shape=jax.ShapeDtypeStruct((b, s - blk_off * TB, d), F32),
        input_output_aliases={} if lat_only else {0: 0},
        compiler_params=_cparams(("parallel", "parallel")),
        name="mix_out",
    )(xs, modtab, hf, hb, mo, o_gqa, o_mla, m_norm_p, w_out_p)


def _ffn_kernel(ctx_len, x_ref, mod_ref, g_ref, wg_ref, wu_ref, wd_ref, o_ref, h_sc, acc_sc):
    tm = x_ref.shape[0]
    t0 = pl.program_id(1) * tm
    f = pl.program_id(2)

    @pl.when(f == 0)
    def _():
        h_sc[...] = _modulated(x_ref[...], g_ref[...], mod_ref, t0, ctx_len, 3, 4).astype(BF16)
        acc_sc[...] = jnp.zeros_like(acc_sc)

    h = h_sc[...]
    a = jnp.dot(h, wg_ref[...], preferred_element_type=F32)
    u = jnp.dot(h, wu_ref[...], preferred_element_type=F32)
    acc_sc[...] += jnp.dot((_silu(a) * u).astype(BF16), wd_ref[...], preferred_element_type=F32)

    @pl.when(f == pl.num_programs(2) - 1)
    def _():
        o_ref[...] = x_ref[...] + _gate_rows(mod_ref, t0, tm, ctx_len, 5) * acc_sc[...]


def _ffn(xs, modtab, gain, wg, wu, wd, ctx_len):
    b, s, d = xs.shape
    tm = s // 4
    nf = wg.shape[1] // TF
    return pl.pallas_call(
        functools.partial(_ffn_kernel, ctx_len),
        grid=(b, s // tm, nf),
        in_specs=[pl.BlockSpec((None, tm, d), lambda i, j, f: (i, j, 0)),
                  pl.BlockSpec((None, 2, 6, d), lambda i, j, f: (i, 0, 0, 0)),
                  pl.BlockSpec((1, d), lambda i, j, f: (0, 0)),
                  pl.BlockSpec((d, TF), lambda i, j, f: (0, f)),
                  pl.BlockSpec((d, TF), lambda i, j, f: (0, f)),
                  pl.BlockSpec((TF, d), lambda i, j, f: (f, 0))],
        out_specs=pl.BlockSpec((None, tm, d), lambda i, j, f: (i, j, 0)),
        out_shape=jax.ShapeDtypeStruct((b, s, d), F32),
        scratch_shapes=[pltpu.VMEM((tm, d), BF16), pltpu.VMEM((tm, d), F32)],
        input_output_aliases={0: 0},
        compiler_params=_cparams(("parallel", "parallel", "arbitrary")),
        name="ffn_dense",
    )(xs, modtab, gain, wg, wu, wd)


def _top2_combine(logits):
    lane = _lane_iota(logits.shape)
    lane_f = lane.astype(F32)
    lg = jnp.where(lane < N_EXPERTS, logits, -jnp.inf)
    v1 = jnp.max(lg, axis=-1, keepdims=True)
    i1 = jnp.min(jnp.where(lg == v1, lane_f, float(LANE)), axis=-1, keepdims=True)
    rest = jnp.where(lane_f == i1, -jnp.inf, lg)
    v2 = jnp.max(rest, axis=-1, keepdims=True)
    i2 = jnp.min(jnp.where(rest == v2, lane_f, float(LANE)), axis=-1, keepdims=True)
    e2 = jnp.exp(v2 - v1)
    w1 = 1.0 / (1.0 + e2)
    w2 = e2 / (1.0 + e2)
    sel = jnp.where((lane_f == i1) | (lane_f == i2), 1.0, 0.0)
    return jnp.where(lane_f == i1, w1, 0.0) + jnp.where(lane_f == i2, w2, 0.0), sel


def _moe_kernel(ctx_len, x_ref, mod_ref, g_ref, r_ref, tri_ref, wg_ref, wu_ref, wd_ref, o_ref,
                h_sc, comb_sc, rank_sc, rankt_sc, rankc_sc, wc_sc, xg_sc, acc_sc, y_sc, nt_sc):
    tm = x_ref.shape[0]
    t0 = pl.program_id(1) * tm
    e = pl.program_id(2)
    f = pl.program_id(3)

    @pl.when(jnp.logical_and(e == 0, f == 0))
    def _():
        h = _modulated(x_ref[...], g_ref[...], mod_ref, t0, ctx_len, 3, 4)
        logits = jnp.dot(h, r_ref[...], preferred_element_type=F32, precision=lax.Precision.HIGHEST)
        comb, sel = _top2_combine(logits)
        rank =jnp.dot(tri_ref[...], sel.astype(BF16), preferred_element_type=F32)
        rank = jnp.where(sel > 0.0, rank, -1.0)
        comb_sc[...] = comb
        rank_sc[...] = rank
        pad = rankt_sc.shape[1] - tm
        h_sc[0:tm, :] = h.astype(BF16)
        if pad:
            rankt_sc[...] = jnp.concatenate([rank, jnp.full((pad, LANE), -1.0, F32)], axis=0).T
            h_sc[tm:, :] = jnp.zeros((pad, h_sc.shape[1]), BF16)
        else:
            rankt_sc[...] = rank.T
        y_sc[...] = jnp.zeros_like(y_sc)

    @pl.when(f == 0)
    def _():
        lane = _lane_iota((tm, LANE))
        rank_c = jnp.sum(jnp.where(lane == e, rank_sc[...], 0.0), axis=-1, keepdims=True)
        rankc_sc[...] = rank_c
        wc_sc[...] = jnp.sum(jnp.where(lane == e, comb_sc[...], 0.0), axis=-1, keepdims=True)
        n_rows = jnp.sum(jnp.where(rank_c >= 0.0, 1.0, 0.0)).astype(jnp.int32)
        n_rt = lax.div(n_rows + (RT - 1), RT)
        n_gt = lax.div(n_rt * RT + (GT - 1), GT)
        nt_sc[0] = n_rt
        nt_sc[1] = n_gt
        rank_r = rankt_sc[pl.ds(e, 1), :]

        def gather(i, c):
            r0 = pl.multiple_of(i * GT, GT)
            tgt = (r0 + _row_iota((GT, 1))).astype(F32)
            sel_t = jnp.where(rank_r == tgt, 1.0, 0.0).astype(BF16)
            xg_sc[pl.ds(r0, GT), :] = jnp.dot(sel_t, h_sc[...], preferred_element_type=F32).astype(BF16)
            acc_sc[pl.ds(r0, GT), :] = jnp.zeros((GT, acc_sc.shape[1]), F32)
            return c

        lax.fori_loop(0, n_gt, gather, 0)

    def expert(i, c):
        r0 = pl.multiple_of(i * RT, 16)
        rows = xg_sc[pl.ds(r0, RT), :]
        a = jnp.dot(rows, wg_ref[...], preferred_element_type=F32)
        u = jnp.dot(rows, wu_ref[...], preferred_element_type=F32)
        acc_sc[pl.ds(r0, RT), :] += jnp.dot((_silu(a) * u).astype(BF16), wd_ref[...],
                                            preferred_element_type=F32)
        return c

    lax.fori_loop(0, nt_sc[0], expert, 0)

    @pl.when(f == pl.num_programs(3) - 1)
    def _():
        def scatter(i, c):
            r0 = pl.multiple_of(i * GT, GT)
            tgt = (r0 + _lane_iota((1, GT))).astype(F32)
            w_t = jnp.where(rankc_sc[...] == tgt, wc_sc[...], 0.0).astype(BF16)
            y_sc[...] += jnp.dot(w_t, acc_sc[pl.ds(r0, GT), :].astype(BF16), preferred_element_type=F32)
            return c

        lax.fori_loop(0, nt_sc[1], scatter, 0)

    @pl.when(jnp.logical_and(e == pl.num_programs(2) - 1, f == pl.num_programs(3) - 1))
    def _():
        o_ref[...] = x_ref[...] + _gate_rows(mod_ref, t0, tm, ctx_len, 5) * y_sc[...]


def _moe(xs, modtab, gain, router_p, wg, wu, wd, ctx_len):
    b, s, d = xs.shape
    tm = s // 4
    tp = -(-tm // LANE) * LANE
    tr = -(-(-(-tm // RT) * RT) // GT) * GT
    ne, _, dff = wg.shape
    tri = jnp.tril(jnp.ones((tm, tm), BF16), -1)
    TF = TF_MOE
    assert dff % TF == 0
    return pl.pallas_call(
        functools.partial(_moe_kernel, ctx_len),
        grid=(b, s // tm, ne, dff // TF),
        in_specs=[pl.BlockSpec((None, tm, d), lambda i, j, e, f: (i, j, 0)),
                  pl.BlockSpec((None, 2, 6, d), lambda i, j, e, f: (i, 0, 0, 0)),
                  pl.BlockSpec((1, d), lambda i, j, e, f: (0, 0)),
                  pl.BlockSpec((d, LANE), lambda i, j, e, f: (0, 0)),
                  pl.BlockSpec((tm, tm), lambda i, j, e, f: (0, 0)),
                  pl.BlockSpec((None, d, TF), lambda i, j, e, f: (e, 0, f)),
                  pl.BlockSpec((None, d, TF), lambda i, j, e, f: (e, 0, f)),
                  pl.BlockSpec((None, TF, d), lambda i, j, e, f: (e, f, 0))],
        out_specs=pl.BlockSpec((None, tm, d), lambda i, j, e, f: (i, j, 0)),
        out_shape=jax.ShapeDtypeStruct((b, s, d), F32),
        scratch_shapes=[pltpu.VMEM((tp, d), BF16),
                        pltpu.VMEM((tm, LANE), F32),
                        pltpu.VMEM((tm, LANE), F32),
                        pltpu.VMEM((LANE, tp), F32),
                        pltpu.VMEM((tm, 1), F32),
                        pltpu.VMEM((tm, 1), F32),
                        pltpu.VMEM((tr, d), BF16),
                        pltpu.VMEM((tr, d), F32),
                        pltpu.VMEM((tm, d), F32),
                        pltpu.SMEM((2,), jnp.int32)],
        input_output_aliases={0: 0},
        compiler_params=_cparams(("parallel", "parallel", "arbitrary", "arbitrary")),
        name="moe_top2",
    )(xs, modtab, gain, router_p, tri, wg, wu, wd)


def _pad_heads(w, n_heads, dh, axis=-1):
    axis = axis % w.ndim
    shp = w.shape[:axis] + (n_heads, dh) + w.shape[axis + 1:]
    pad = [(0, 0)] * (w.ndim + 1)
    pad[axis + 1] = (0, SLOT - dh)
    out = jnp.pad(w.reshape(shp), pad)
    return out.reshape(w.shape[:axis] + (n_heads * SLOT,) + w.shape[axis + 1:])


def _rope_tables(seq, ctx_len):
    t = jnp.arange(seq)
    rows = (t // GRID_W).astype(F32)
    cols = (t % GRID_W).astype(F32)

    def angles(rot_dim):
        nf = rot_dim // 4
        inv = ROPE_THETA ** (-jnp.arange(nf, dtype=F32) / nf)
        ar = rows[:, None] * inv
        ac = cols[:, None] * inv
        return jnp.concatenate([ar, ar, ac, ac], axis=-1)

    def slot_tables(rot_dim, lane0):
        ang = angles(rot_dim)
        quarter = rot_dim // 4
        first = (jnp.arange(rot_dim) % (2 * quarter)) < quarter
        cos = jnp.ones((seq, SLOT), F32).at[:, lane0:lane0 + rot_dim].set(jnp.cos(ang))
        sin = jnp.sin(ang)
        sin_m = jnp.zeros((seq, SLOT), F32).at[:, lane0:lane0 + rot_dim].set(jnp.where(first, -sin, 0.0))
        sin_p = jnp.zeros((seq, SLOT), F32).at[:, lane0:lane0 + rot_dim].set(jnp.where(first, 0.0, sin))
        ident = jnp.stack([jnp.ones((ctx_len, SLOT), F32), jnp.zeros((ctx_len, SLOT), F32),
                           jnp.zeros((ctx_len, SLOT), F32)])
        return jnp.concatenate([ident, jnp.stack([cos, sin_m, sin_p])], axis=1)

    return jnp.concatenate([slot_tables(A_DH, 0), slot_tables(L_ROPE, L_NOPE)], axis=0)


def _layer_params(w_in, m_conv, m_gate_b, m_norm, a_qnorm, a_knorm, l_cq_norm, l_ckv_norm, l_wuq, l_wukv,
                  l_qnorm, l_knorm, w_out):
    d = w_in.shape[0]
    offs = np.cumsum((0,) + IN_SIZES)
    seg = [w_in[:, offs[i]:offs[i + 1]] for i in range(len(IN_SIZES))]
    misc = jnp.zeros((d, SLOT), F32).at[:, 0:4 * M_HEADS].set(seg[4]).at[:, MISC_KR:MISC_KR + L_ROPE].set(seg[10])
    w_in_p = jnp.concatenate(
        [_pad_heads(seg[0], M_HEADS, M_DH), _pad_heads(seg[1], M_HEADS, M_DH),
         _pad_heads(seg[2], M_HEADS, M_DH), _pad_heads(seg[3], M_HEADS, M_DH), misc,
         _pad_heads(seg[5], A_HEADS, A_DH), _pad_heads(seg[6], A_KV, A_DH), _pad_heads(seg[7], A_KV, A_DH),
         seg[8], seg[9]], axis=1).astype(BF16)
    conv_w = jnp.concatenate([_pad_heads(m_conv[:, :M_HEADS * M_DH], M_HEADS, M_DH),
                              _pad_heads(m_conv[:, M_HEADS * M_DH:], M_HEADS, M_DH)], axis=1)
    conv_scale = jnp.concatenate([jnp.ones((1, M_HEADS * SLOT), F32),
                                  jnp.full((1, M_HEADS * SLOT), M_DH ** -0.5, F32)], axis=1)
    gate_b = jnp.zeros((1, SLOT), F32).at[0, 0:4 * M_HEADS].set(m_gate_b)
    pad1 = lambda g: jnp.pad(g, (0, SLOT - g.shape[0]))
    an = jnp.stack([pad1(a_qnorm), pad1(a_knorm)])
    ln = jnp.stack([pad1(l_qnorm), pad1(l_knorm)])
    wuq_p = _pad_heads(l_wuq, L_HEADS, L_QK).astype(BF16)
    kv = l_wukv.reshape(L_KVRANK, L_HEADS, L_NOPE + L_DV)
    wukv_p = jnp.concatenate(
        [_pad_heads(kv[:, :, :L_NOPE].reshape(L_KVRANK, -1), L_HEADS, L_NOPE),
         _pad_heads(kv[:, :, L_NOPE:].reshape(L_KVRANK, -1), L_HEADS, L_DV)], axis=1).astype(BF16)
    nm, na = M_HEADS * M_DH, A_HEADS * A_DH
    w_out_p = jnp.concatenate(
        [_pad_heads(w_out[:nm], M_HEADS, M_DH, axis=0), _pad_heads(w_out[nm:nm + na], A_HEADS, A_DH, axis=0),
         _pad_heads(w_out[nm + na:], L_HEADS, L_DV, axis=0)], axis=0).astype(BF16)
    m_norm_p = _pad_heads(m_norm[None, :], M_HEADS, M_DH)
    return dict(w_in_p=w_in_p, conv_w=conv_w, conv_scale=conv_scale, gate_b=gate_b, an=an, ln=ln,
                cqn=l_cq_norm[None, :], ckvn=l_ckv_norm[None, :], wuq_p=wuq_p, wukv_p=wukv_p,
                w_out_p=w_out_p, m_norm_p=m_norm_p)


def kernel(x, c, ctx, c_ctx, mod_w, mod_b, norm_mix, norm_ffn, w_in, m_conv, m_gate_b, m_norm, a_qnorm, a_knorm,
           l_cq_norm, l_ckv_norm, l_wuq, l_wukv, l_qnorm, l_knorm, w_out, ffn_wg, ffn_wu, ffn_wd, moe_router,
           moe_wg, moe_wu, moe_wd):
    b, seq, d = x.shape
    ctx_len = ctx.shape[1]
    depth = mod_w.shape[0]
    assert ctx_len % TB == 0 and ctx_len % TQ == 0 and ctx_len == ML and seq % TK == 0 and seq % GRID_W == 0
    xs = jnp.concatenate([ctx, x], axis=1)
    mod_rows = 16
    cc = jnp.zeros((mod_rows, d), F32).at[:b].set(c).at[b].set(c_ctx)
    mod_all = _mod_table(cc, mod_w, mod_b)
    rope_tab = _rope_tables(seq, ctx_len)
    for i in range(depth):
        need_ctx = i < depth - 1
        lat = mod_all[i, :b].reshape(b, 1, 6, d)
        cm = jnp.broadcast_to(mod_all[i, b].reshape(1, 1, 6, d), (b, 1, 6, d))
        modtab = jnp.concatenate([cm, lat], axis=1)
        p = _layer_params(w_in[i], m_conv[i], m_gate_b[i], m_norm[i], a_qnorm[i], a_knorm[i], l_cq_norm[i],
                          l_ckv_norm[i], l_wuq[i], l_wukv[i], l_qnorm[i], l_knorm[i], w_out[i])
        mqk, mv, mo, misc, q_all, k_all, v_all = _inproj(
            xs, modtab, norm_mix[i][None, :], p["w_in_p"], rope_tab, p["an"], p["ln"], p["cqn"], p["ckvn"],
            p["wuq_p"], p["wukv_p"], ctx_len)
        qk_c = _conv(mqk, p["conv_w"], p["conv_scale"], ctx_len)
        hf, hb = _mlstm(qk_c, mv, misc, p["gate_b"])
        gqa_kv = tuple(h // (A_HEADS // A_KV) for h in range(A_HEADS))
        o_gqa = _attention(q_all, k_all, v_all, ctx_len, need_ctx, 0, KV_GQA, A_KV, gqa_kv)
        o_mla = _attention(q_all, k_all, v_all, ctx_len, need_ctx, Q_MLA, 0, L_HEADS, tuple(range(L_HEADS)))
        xs = _mixout(xs, modtab, hf, hb, mo, o_gqa, o_mla, p["m_norm_p"], p["w_out_p"], ctx_len, not need_ctx)
        ffn_ctx = ctx_len if need_ctx else 0
        j = i // 2
        if i % 2 == 0:
            xs = _ffn(xs, modtab, norm_ffn[i][None, :], ffn_wg[j].astype(BF16), ffn_wu[j].astype(BF16),
                      ffn_wd[j].astype(BF16), ffn_ctx)
        else:
            router_p = jnp.pad(moe_router[j], ((0, 0), (0, LANE - N_EXPERTS)))
            xs = _moe(xs, modtab, norm_ffn[i][None, :], router_p, moe_wg[j].astype(BF16),
                      moe_wu[j].astype(BF16), moe_wd[j].astype(BF16), ffn_ctx)
    return xs
```

```python
import functools
import math

import numpy as np
import jax
import jax.numpy as jnp
from jax import lax
from jax.experimental import pallas as pl
from jax.experimental.pallas import tpu as pltpu

F32 = jnp.float32
BF16 = jnp.bfloat16

GRID_W = 64
EPS = 1e-6
ROPE_THETA = 10000.0
M_HEADS, M_DH = 4, 96
A_HEADS, A_KV, A_DH = 6, 2, 64
L_HEADS, L_NOPE, L_ROPE, L_DV = 4, 64, 32, 64
L_QK = L_NOPE + L_ROPE
L_QRANK, L_KVRANK = 256, 128
N_EXPERTS, TOP_K = 8, 2
IN_SIZES = (384, 384, 384, 384, 16, 384, 128, 128, 256, 128, 32)

LANE = 128
SLOT = LANE
N_KVHEADS = A_KV + L_HEADS
Q_MLA = 8
Q_SLOTS = Q_MLA + L_HEADS
KV_GQA = L_HEADS
VMEM_LIMIT = 56 * 1024 * 1024

O_MQK, O_MV, O_MO, O_MISC = 0, 1024, 1536, 2048
O_QA, O_KA, O_VA, O_CQ, O_CKV, IN_PAD = 2176, 2944, 3200, 3456, 3712, 3840
MISC_KR = 64
DEN_LANE = M_DH
SUM_LANE = A_DH
LOG2E = math.log2(math.e)
assert A_DH == L_DV and SUM_LANE < SLOT

TB = 256
TQ = 256
TK = 512
TM_FFN = 1088
TF = 512
ML = 256
RT = 160
GT = 256
TF_MOE = 896


def _cparams(sem):
    return pltpu.CompilerParams(dimension_semantics=sem, vmem_limit_bytes=VMEM_LIMIT)


def _sigmoid(x):
    return 1.0 / (1.0 + jnp.exp(-x))


def _silu(x):
    return x * _sigmoid(x)


def _lane_iota(shape):
    return lax.broadcasted_iota(jnp.int32, shape, len(shape) - 1)


def _row_iota(shape):
    return lax.broadcasted_iota(jnp.int32, shape, 0)


def _modulated(x, gain, mod_ref, t0, ctx_len, k_shift, k_scale):
    tm = x.shape[0]
    is_ctx = (t0 + _row_iota((tm, 1))) < ctx_len
    shift = jnp.where(is_ctx, mod_ref[0, k_shift:k_shift + 1, :], mod_ref[1, k_shift:k_shift + 1, :])
    scale = jnp.where(is_ctx, mod_ref[0, k_scale:k_scale + 1, :], mod_ref[1, k_scale:k_scale + 1, :])
    y = x * lax.rsqrt(jnp.mean(x * x, axis=-1, keepdims=True) + EPS) * gain
    return y * (1.0 + scale) + shift


def _gate_rows(mod_ref, t0, tm, ctx_len, k_gate):
    is_ctx = (t0 + _row_iota((tm, 1))) < ctx_len
    return jnp.where(is_ctx, mod_ref[0, k_gate:k_gate + 1, :], mod_ref[1, k_gate:k_gate + 1, :])


def _mod_kernel(c_ref, w_ref, b_ref, o_ref):
    s = _silu(c_ref[...]).astype(BF16)
    o_ref[...] = jnp.dot(s, w_ref[...].astype(BF16), preferred_element_type=F32) + b_ref[...]


def _mod_table(cc, mod_w, mod_b):
    depth, d, n = mod_w.shape
    rows = cc.shape[0]
    return pl.pallas_call(
        _mod_kernel,
        grid=(depth, n // d),
        in_specs=[pl.BlockSpec((rows, d), lambda l, j: (0, 0)),
                  pl.BlockSpec((None, d, d), lambda l, j: (l, 0, j)),
                  pl.BlockSpec((None, 1, d), lambda l, j: (l, 0, j))],
        out_specs=pl.BlockSpec((None, rows, d), lambda l, j: (l, 0, j)),
        out_shape=jax.ShapeDtypeStruct((depth, rows, n), F32),
        compiler_params=_cparams(("arbitrary", "arbitrary")),
        name="mod_table",
    )(cc, mod_w, mod_b.reshape(depth, 1, n))


def _slot_rms(x, lo, hi):
    lane = _lane_iota(x.shape)
    sq = jnp.where((lane >= lo) & (lane < hi), x * x, 0.0)
    return lax.rsqrt(jnp.sum(sq, axis=-1, keepdims=True) * (1.0 / (hi - lo)) + EPS)


def _rope(x, cos, sin_m, sin_p, quarter):
    return (x * cos + pltpu.roll(x, LANE - quarter, 1) * sin_m + pltpu.roll(x, quarter, 1) * sin_p)


def _inproj_kernel(ctx_len, x_ref, mod_ref, g_ref, w_ref, rope_ref, an_ref, ln_ref, cqn_ref, ckvn_ref,
                   wuq_ref, wukv_ref, mqk_ref, mo_ref, misc_ref, q_ref, k_ref, mvt_ref, vt_ref):
    tm = x_ref.shape[0]
    t0 = pl.program_id(1) * tm
    h = _modulated(x_ref[...], g_ref[...], mod_ref, t0, ctx_len, 0, 1).astype(BF16)

    def proj(a, b):
        return jnp.dot(h, w_ref[:, a:b], preferred_element_type=F32)

    mqk_ref[...] = proj(O_MQK, O_MV).astype(BF16)
    mv = proj(O_MV, O_MO)
    mvt_ref[...] = jnp.where(_lane_iota(mv.shape) % SLOT == DEN_LANE, 1.0, mv).T.astype(BF16)
    mo_ref[...] = proj(O_MO, O_MISC).astype(BF16)
    misc = proj(O_MISC, O_QA)
    misc_ref[...] = misc

    cos_a, sinm_a, sinp_a = rope_ref[0], rope_ref[1], rope_ref[2]
    cos_l, sinm_l, sinp_l = rope_ref[3], rope_ref[4], rope_ref[5]
    a_scale = A_DH ** -0.5 * LOG2E
    l_scale = L_QK ** -0.5 * LOG2E

    def with_ones_t(v):
        return jnp.where(_lane_iota(v.shape) % SLOT == SUM_LANE, 1.0, v).T.astype(BF16)

    pa = proj(O_QA, O_CQ)
    gq, gk = an_ref[0:1, :], an_ref[1:2, :]
    for i in range(A_HEADS):
        x = pa[:, i * SLOT:(i + 1) * SLOT]
        x = x * _slot_rms(x, 0, A_DH) * gq
        q_ref[:, i * SLOT:(i + 1) * SLOT] = (_rope(x, cos_a, sinm_a, sinp_a, A_DH // 4) * a_scale).astype(BF16)
    for i in range(A_KV):
        x = pa[:, (A_HEADS + i) * SLOT:(A_HEADS + i + 1) * SLOT]
        x = x * _slot_rms(x, 0, A_DH) * gk
        k_ref[:, (KV_GQA + i) * SLOT:(KV_GQA + i + 1) * SLOT] = (
            _rope(x, cos_a, sinm_a, sinp_a, A_DH // 4).astype(BF16))
    vt_ref[KV_GQA * SLOT:, :] = with_ones_t(pa[:, (A_HEADS + A_KV) * SLOT:(A_HEADS + 2 * A_KV) * SLOT])
    q_ref[:, A_HEADS * SLOT:Q_MLA * SLOT] = jnp.zeros((tm, (Q_MLA - A_HEADS) * SLOT), BF16)

    pc = proj(O_CQ, IN_PAD)
    cq = pc[:, 0:L_QRANK]
    cq = (cq * lax.rsqrt(jnp.mean(cq * cq, axis=-1, keepdims=True) + EPS) * cqn_ref[...]).astype(BF16)
    ckv = pc[:, L_QRANK:L_QRANK + L_KVRANK]
    ckv = (ckv * lax.rsqrt(jnp.mean(ckv * ckv, axis=-1, keepdims=True) + EPS) * ckvn_ref[...]).astype(BF16)
    ql = jnp.dot(cq, wuq_ref[...], preferred_element_type=F32)
    kvl = jnp.dot(ckv, wukv_ref[...], preferred_element_type=F32)
    gq_l, gk_l = ln_ref[0:1, :], ln_ref[1:2, :]
    lane = _lane_iota((tm, SLOT))
    kr = jnp.where((lane >= MISC_KR) & (lane < MISC_KR + L_ROPE), misc, 0.0)
    kr = kr * _slot_rms(kr, MISC_KR, MISC_KR + L_ROPE) * gk_l
    kr = _rope(kr, cos_l, sinm_l, sinp_l, L_ROPE // 4)
    for i in range(L_HEADS):
        x = ql[:, i * SLOT:(i + 1) * SLOT]
        inv = jnp.where(lane < L_NOPE, _slot_rms(x, 0, L_NOPE), _slot_rms(x, L_NOPE, L_QK))
        x = x * inv * gq_l
        q_ref[:, (Q_MLA + i) * SLOT:(Q_MLA + i + 1) * SLOT] = (
            _rope(x, cos_l, sinm_l, sinp_l, L_ROPE // 4) * l_scale).astype(BF16)
        kn = kvl[:, i * SLOT:(i + 1) * SLOT]
        kn = kn * _slot_rms(kn, 0, L_NOPE) * gk_l
        k_ref[:, i * SLOT:(i + 1) * SLOT] = (kn + kr).astype(BF16)
    vt_ref[0:KV_GQA * SLOT, :] = with_ones_t(kvl[:, L_HEADS * SLOT:])


def _inproj(xs, modtab, gain, w_in_p, rope_tab, an, ln, cqn, ckvn, wuq_p, wukv_p, ctx_len):
    b, s, d = xs.shape
    grid = (b, s // TB)
    tok = lambda n: pl.BlockSpec((None, TB, n), lambda i, j: (i, j, 0))
    full = lambda a: pl.BlockSpec(a.shape, lambda i, j: (0,) * a.ndim)
    out_widths = (1024, 512, SLOT, Q_SLOTS * SLOT, N_KVHEADS * SLOT)
    out_dtypes = (BF16, BF16, F32, BF16, BF16)
    out_specs = [tok(n) for n in out_widths]
    out_shape = [jax.ShapeDtypeStruct((b, s, n), dt) for n, dt in zip(out_widths, out_dtypes)]
    for n in (M_HEADS * SLOT, N_KVHEADS * SLOT):
        out_specs.append(pl.BlockSpec((None, n, TB), lambda i, j: (i, 0, j)))
        out_shape.append(jax.ShapeDtypeStruct((b, n, s), BF16))
    return pl.pallas_call(
        functools.partial(_inproj_kernel, ctx_len),
        grid=grid,
        in_specs=[tok(d),
                  pl.BlockSpec((None, 2, 6, d), lambda i, j: (i, 0, 0, 0)),
                  full(gain), full(w_in_p),
                  pl.BlockSpec((6, TB, SLOT), lambda i, j: (0, j, 0)),
                  full(an), full(ln), full(cqn), full(ckvn), full(wuq_p), full(wukv_p)],
        out_specs=out_specs,
        out_shape=out_shape,
        compiler_params=_cparams(("parallel", "parallel")),
        name="inproj",
    )(xs, modtab, gain, w_in_p, rope_tab, an, ln, cqn, ckvn, wuq_p, wukv_p)


def _conv_kernel(ctx_len, s_len, x_ref, prev_ref, next_ref, w_ref, sc_ref, qt_ref, k_ref):
    tm = x_ref.shape[0]
    t0 = pl.program_id(1) * tm
    x = x_ref[...].astype(F32)
    row = _row_iota((tm, 1))
    has_prev = jnp.logical_and(t0 != 0, t0 != ctx_len)
    has_next = jnp.logical_and(t0 + tm != ctx_len, t0 + tm != s_len)
    hp = jnp.where(has_prev, prev_ref[15:16, :].astype(F32), 0.0)
    hn = jnp.where(has_next, next_ref[0:1, :].astype(F32), 0.0)
    xp = jnp.where(row == 0, hp, pltpu.roll(x, 1, 0))
    xn = jnp.where(row == tm - 1, hn, pltpu.roll(x, tm - 1, 0))
    y = xp * w_ref[0:1, :] + x * w_ref[1:2, :] + xn * w_ref[2:3, :]
    y = _silu(y) * sc_ref[...]
    half = y.shape[1] // 2
    qt_ref[...] = y[:, :half].T.astype(qt_ref.dtype)
    k_ref[...] = y[:, half:].astype(k_ref.dtype)


def _conv(mqk, conv_w, conv_scale, ctx_len):
    b, s, n = mqk.shape
    hb = TB // 16
    last = s // 16 - 1
    return pl.pallas_call(
        functools.partial(_conv_kernel, ctx_len, s),
        grid=(b, s // TB),
        in_specs=[pl.BlockSpec((None, TB, n), lambda i, j: (i, j, 0)),
                  pl.BlockSpec((None, 16, n), lambda i, j: (i, jnp.maximum(j * hb - 1, 0), 0)),
                  pl.BlockSpec((None, 16, n), lambda i, j: (i, jnp.minimum((j + 1) * hb, last), 0)),
                  pl.BlockSpec((3, n), lambda i, j: (0, 0)),
                  pl.BlockSpec((1, n), lambda i, j: (0, 0))],
        out_specs=[pl.BlockSpec((None, n // 2, TB), lambda i, j: (i, 0, j)),
                   pl.BlockSpec((None, TB, n // 2), lambda i, j: (i, j, 0))],
        out_shape=[jax.ShapeDtypeStruct((b, n // 2, s), BF16), jax.ShapeDtypeStruct((b, s, n // 2), BF16)],
        compiler_params=_cparams(("parallel", "parallel")),
        name="mlstm_conv",
    )(mqk, mqk, mqk, conv_w, conv_scale)


def _log_sigmoid(x):
    return jnp.minimum(x, 0.0) - jnp.log(1.0 + jnp.exp(-jnp.abs(x)))


def _mlstm_kernel(n_chunks, qt_ref, k_ref, vt_ref, misc_ref, gb_ref, hf_ref, hb_ref, c_ref, m_ref, gt_sc, bt_sc):
    hp = pl.program_id(1)
    c_ref[...] = jnp.zeros_like(c_ref)
    m_ref[...] = jnp.zeros_like(m_ref)
    r = _row_iota((ML, ML))
    cidx = _lane_iota((ML, ML))
    tri_f32 = (r >= cidx).astype(F32)
    tri_b32 = (r <= cidx).astype(F32)
    lane = _lane_iota((ML, SLOT))
    lane_t = _lane_iota((1, ML))

    def chunk(t0, d, tri32, out_ref):
        g = misc_ref[pl.ds(t0, ML), :] + gb_ref[...]
        logf = _log_sigmoid(g)
        bcum = jnp.dot(tri32, logf, preferred_element_type=F32, precision=lax.Precision.HIGHEST)
        gt_sc[d] = g.T
        bt_sc[d] = bcum.T
        ib = g - pltpu.roll(bcum, SLOT - M_HEADS, 1)
        allowed = (r <= cidx) if d == 0 else (r >= cidx)
        for hh in range(2):
            li = d * 2 * M_HEADS + hp * 2 + hh
            ib_col = jnp.sum(jnp.where(lane == li, ib, 0.0), axis=-1, keepdims=True)
            i_row = gt_sc[d, pl.ds(li, 1), :]
            b_row = bt_sc[d, pl.ds(li + M_HEADS, 1), :]
            sidx = d * 2 + hh
            ct_st = c_ref[sidx]
            m_s = m_ref[sidx]
            k = k_ref[pl.ds(t0, ML), hh * SLOT:(hh + 1) * SLOT]
            qt = qt_ref[hh * SLOT:(hh + 1) * SLOT, pl.ds(t0, ML)]
            vt = vt_ref[hh * SLOT:(hh + 1) * SLOT, pl.ds(t0, ML)]
            dmat = jnp.where(allowed, b_row + ib_col, -jnp.inf)
            m_inter = b_row + m_s
            m_t = jnp.maximum(m_inter, jnp.max(dmat, axis=0, keepdims=True))
            kq = jnp.dot(k, qt, preferred_element_type=F32)
            w = (jnp.exp(dmat - m_t) * kq).astype(BF16)
            a_inter = jnp.exp(m_inter - m_t)
            num = a_inter * jnp.dot(ct_st.astype(BF16), qt, preferred_element_type=F32) + jnp.dot(
                vt, w, preferred_element_type=F32)
            den = num[DEN_LANE:DEN_LANE + 1, :]
            h_out = num / jnp.maximum(jnp.abs(den), jnp.exp(-m_t))
            out_ref[pl.ds(t0, ML), hh * SLOT:(hh + 1) * SLOT] = h_out.T.astype(out_ref.dtype)
            last = ML - 1 if d == 0 else 0
            total = jnp.sum(jnp.where(lane_t == last, b_row, 0.0), axis=-1, keepdims=True)
            gg = total - b_row + i_row
            m_new = jnp.maximum(total + m_s, jnp.max(gg, axis=-1, keepdims=True))
            decay = jnp.exp(total + m_s - m_new)[:, 0:1]
            wk = jnp.exp(gg - m_new)
            vw = (vt.astype(F32) * wk).astype(BF16)
            c_ref[sidx] = decay * ct_st + jnp.dot(vw, k, preferred_element_type=F32)
            m_ref[sidx] = m_new

    def step(n, carry):
        tf0 = pl.multiple_of(n * ML, ML)
        tb0 = pl.multiple_of(jnp.where(n == 0, 0, n_chunks - n) * ML, ML)
        chunk(tf0, 0, tri_f32, hf_ref)
        chunk(tb0, 1, tri_b32, hb_ref)
        return carry

    lax.fori_loop(0, n_chunks, step, 0)


def _mlstm(q_t, k_c, mv_t, misc, gate_b):
    b, s, _ = k_c.shape
    n_chunks = s // ML
    pair = 2 * SLOT
    spec = pl.BlockSpec((None, s, pair), lambda i, j: (i, 0, j))
    spec_t = pl.BlockSpec((None, pair, s), lambda i, j: (i, j, 0))
    return pl.pallas_call(
        functools.partial(_mlstm_kernel, n_chunks),
        grid=(b, M_HEADS // 2),
        in_specs=[spec_t, spec, spec_t,
                  pl.BlockSpec((None, s, SLOT), lambda i, j: (i, 0, 0)),
                  pl.BlockSpec((1, SLOT), lambda i, j: (0, 0))],
        out_specs=[spec, spec],
        out_shape=[jax.ShapeDtypeStruct((b, s, M_HEADS * SLOT), BF16)] * 2,
        scratch_shapes=[pltpu.VMEM((4, SLOT, SLOT), F32), pltpu.VMEM((4, 1, ML), F32),
                        pltpu.VMEM((2, SLOT, ML), F32), pltpu.VMEM((2, SLOT, ML), F32)],
        compiler_params=_cparams(("parallel", "parallel")),
        name="mlstm_scan",
    )(q_t, k_c, mv_t, misc, gate_b)


def _attn_kernel(ctx_len, q_off, kv_of, q_ref, k_ref, vt_ref, o_ref, st_sc, pt_sc, acc_sc):
    s_len = k_ref.shape[0]
    n_heads = len(kv_of)
    qs = [q_ref[:, g * SLOT:(g + 1) * SLOT] for g in range(n_heads)]

    def kv_cols(g):
        return slice(kv_of[g] * SLOT, (kv_of[g] + 1) * SLOT)

    def chunk(carry, rows):
        out = []
        for g in range(n_heads):
            m, acc = carry[g]
            k = k_ref[rows, kv_cols(g)]
            vt = vt_ref[kv_cols(g), rows]
            st = lax.dot_general(k, qs[g], (((1,), (1,)), ((), ())), preferred_element_type=F32)
            m_new = jnp.maximum(m, jnp.max(st, axis=0, keepdims=True))
            alpha = jnp.exp2(m - m_new)
            pt = jnp.exp2(st - m_new).astype(BF16)
            out.append((m_new, alpha * acc + jnp.dot(vt, pt, preferred_element_type=F32)))
        return tuple(out)

    def finish(carry):
        for g in range(n_heads):
            acc = carry[g][1]
            o = acc / acc[SUM_LANE:SUM_LANE + 1, :]
            o_ref[:, g * SLOT:(g + 1) * SLOT] = o.T.astype(o_ref.dtype)

    init = tuple((jnp.full((1, TQ), -jnp.inf, F32), jnp.zeros((SLOT, TQ), F32)) for _ in range(n_heads))
    is_ctx = (pl.program_id(2) + q_off) * TQ < ctx_len

    @pl.when(is_ctx)
    def _():
        finish(chunk(init, pl.ds(0, ctx_len)))

    n_lat = (s_len - ctx_len) // TK
    n0 = ctx_len + TK
    assert n_lat >= 4 and n_lat % 2 == 0 and st_sc.shape[1] == n0

    def rows_of(c):
        if isinstance(c, int):
            return (pl.ds(0, n0), n0) if c == 0 else (pl.ds(ctx_len + c * TK, TK), TK)
        return pl.ds(pl.multiple_of(ctx_len + c * TK, math.gcd(ctx_len, TK)), TK), TK

    def scores(c, par):
        rows, n = rows_of(c)
        for g in range(n_heads):
            st_sc[par * n_heads + g, 0:n, :] = lax.dot_general(
                k_ref[rows, kv_cols(g)], qs[g], (((1,), (1,)), ((), ())), preferred_element_type=F32)

    def softmax(par, m, n=TK):
        ms, alphas = [], []
        for g in range(n_heads):
            st = st_sc[par * n_heads + g, 0:n, :]
            m_new = jnp.maximum(m[g], jnp.max(st, axis=0, keepdims=True))
            pt_sc[par * n_heads + g, 0:n, :] = jnp.exp2(st - m_new).astype(BF16)
            ms.append(m_new)
            alphas.append(jnp.exp2(m[g] - m_new))
        return tuple(ms), tuple(alphas)

    def values(c, par, alpha):
        rows, n = rows_of(c)
        for g in range(n_heads):
            acc_sc[g] = alpha[g] * acc_sc[g] + jnp.dot(vt_ref[kv_cols(g), rows], pt_sc[par * n_heads + g, 0:n, :],
                                                       preferred_element_type=F32)

    def stage(c, par, m, alpha):
        scores(c + 1, 1 - par)
        m_new, alpha_new = softmax(par, m)
        values(c - 1, 1 - par, alpha)
        return m_new, alpha_new

    @pl.when(jnp.logical_not(is_ctx))
    def _():
        for g in range(n_heads):
            acc_sc[g] = init[g][1]
        scores(0, 0)
        m, alpha = softmax(0, tuple(c[0] for c in init), n0)
        scores(1, 1)
        m, alpha = stage(1, 1, m, alpha)

        def body(t, carry):
            c = 2 * t + 2
            return stage(c + 1, 1, *stage(c, 0, *carry))

        m, alpha = lax.fori_loop(0, (n_lat - 4) // 2, body, (m, alpha))
        m, alpha = stage(n_lat - 2, 0, m, alpha)
        values(n_lat - 2, 0, alpha)
        m, alpha = softmax(1, m)
        values(n_lat - 1, 1, alpha)
        finish(tuple((m[g], acc_sc[g]) for g in range(n_heads)))


def _attention(q_all, k_all, vt_all, ctx_len, need_ctx, q_slot0, kv_slot0, n_kv, kv_of):
    b, s, _ = q_all.shape
    q_off = 0 if need_ctx else ctx_len // TQ
    nq = s // TQ - q_off
    n_heads = len(kv_of)
    qw = n_heads * SLOT
    kw = n_kv * SLOT
    assert (q_slot0 * SLOT) % qw == 0 and (kv_slot0 * SLOT) % kw == 0
    qb0, kb0 = q_slot0 * SLOT // qw, kv_slot0 * SLOT // kw
    return pl.pallas_call(
        functools.partial(_attn_kernel, ctx_len, q_off, kv_of),
        grid=(b, 1, nq),
        in_specs=[pl.BlockSpec((None, TQ, qw), lambda i, g, j: (i, j + q_off, qb0)),
                  pl.BlockSpec((None, s, kw), lambda i, g, j: (i, 0, kb0)),
                  pl.BlockSpec((None, kw, s), lambda i, g, j: (i, kb0, 0))],
        out_specs=pl.BlockSpec((None, TQ, qw), lambda i, g, j: (i, j + q_off, 0)),
        out_shape=jax.ShapeDtypeStruct((b, s, qw), BF16),
        scratch_shapes=[pltpu.VMEM((2 * n_heads, ctx_len + TK, TQ), F32),
                        pltpu.VMEM((2 * n_heads, ctx_len + TK, TQ), BF16),
                        pltpu.VMEM((n_heads, SLOT, TQ), F32)],
        compiler_params=_cparams(("parallel", "parallel", "parallel")),
        name="attention",
    )(q_all, k_all, vt_all)


def _mixout_kernel(ctx_len, blk_off, x_ref, mod_ref, hf_ref, hb_ref, mo_ref, oa_ref, ol_ref, mn_ref, w_ref, o_ref):
    tm = x_ref.shape[0]
    t0 = (pl.program_id(1) + blk_off) * tm
    hm = hf_ref[...].astype(F32) + hb_ref[...].astype(F32)
    gate = _sigmoid(mo_ref[...].astype(F32))
    parts = []
    for i in range(M_HEADS):
        x = hm[:, i * SLOT:(i + 1) * SLOT]
        inv = lax.rsqrt(jnp.sum(x * x, axis=-1, keepdims=True) * (1.0 / M_DH) + EPS)
        parts.append(x * inv)
    hn = jnp.concatenate(parts, axis=-1) * mn_ref[...] * gate
    nm = M_HEADS * SLOT
    o = jnp.dot(hn.astype(BF16), w_ref[0:nm, :], preferred_element_type=F32)
    na = nm + A_HEADS * SLOT
    o = o + jnp.dot(oa_ref[...], w_ref[nm:na, :], preferred_element_type=F32)
    o = o + jnp.dot(ol_ref[...], w_ref[na:, :], preferred_element_type=F32)
    o_ref[...] = x_ref[...] + _gate_rows(mod_ref, t0, tm, ctx_len, 2) * o


def _mixout(xs, modtab, hf, hb, mo, o_gqa, o_mla, m_norm_p, w_out_p, ctx_len, lat_only):
    b, s, d = xs.shape
    blk_off = ctx_len // TB if lat_only else 0
    tok = lambda n: pl.BlockSpec((None, TB, n), lambda i, j: (i, j + blk_off, 0))
    full = lambda a: pl.BlockSpec(a.shape, lambda i, j: (0,) * a.ndim)
    return pl.pallas_call(
        functools.partial(_mixout_kernel, ctx_len, blk_off),
        grid=(b, s // TB - blk_off),
        in_specs=[tok(d),
                  pl.BlockSpec((None, 2, 6, d), lambda i, j: (i, 0, 0, 0)),
                  tok(M_HEADS * SLOT), tok(M_HEADS * SLOT), tok(M_HEADS * SLOT), tok(A_HEADS * SLOT),
                  tok(L_HEADS * SLOT), full(m_norm_p), full(w_out_p)],
        out_specs=pl.BlockSpec((None, TB, d), lambda i, j: (i, j, 0)),
        out_shape=jax.ShapeDtypeStruct((b, s - blk_off * TB, d), F32),
        input_output_aliases={} if lat_only else {0: 0},
        compiler_params=_cparams(("parallel", "parallel")),
        name="mix_out",
    )(xs, modtab, hf, hb, mo, o_gqa, o_mla, m_norm_p, w_out_p)


def _ffn_kernel(ctx_len, x_ref, mod_ref, g_ref, wg_ref, wu_ref, wd_ref, o_ref, h_sc, acc_sc):
    tm = x_ref.shape[0]
    t0 = pl.program_id(1) * tm
    f = pl.program_id(2)

    @pl.when(f == 0)
    def _():
        h_sc[...] = _modulated(x_ref[...], g_ref[...], mod_ref, t0, ctx_len, 3, 4).astype(BF16)
        acc_sc[...] = jnp.zeros_like(acc_sc)

    h = h_sc[...]
    a = jnp.dot(h, wg_ref[...], preferred_element_type=F32)
    u = jnp.dot(h, wu_ref[...], preferred_element_type=F32)
    acc_sc[...] += jnp.dot((_silu(a) * u).astype(BF16), wd_ref[...], preferred_element_type=F32)

    @pl.when(f == pl.num_programs(2) - 1)
    def _():
        o_ref[...] = x_ref[...] + _gate_rows(mod_ref, t0, tm, ctx_len, 5) * acc_sc[...]


def _ffn(xs, modtab, gain, wg, wu, wd, ctx_len):
    b, s, d = xs.shape
    tm = s // 4
    nf = wg.shape[1] // TF
    wg, wu = (w.reshape(d, nf, TF).transpose(1, 0, 2) for w in (wg, wu))
    return pl.pallas_call(
        functools.partial(_ffn_kernel, ctx_len),
        grid=(b, s // tm, nf),
        in_specs=[pl.BlockSpec((None, tm, d), lambda i, j, f: (i, j, 0)),
                  pl.BlockSpec((None, 2, 6, d), lambda i, j, f: (i, 0, 0, 0)),
                  pl.BlockSpec((1, d), lambda i, j, f: (0, 0)),
                  pl.BlockSpec((None, d, TF), lambda i, j, f: (f, 0, 0)),
                  pl.BlockSpec((None, d, TF), lambda i, j, f: (f, 0, 0)),
                  pl.BlockSpec((TF, d), lambda i, j, f: (f, 0))],
        out_specs=pl.BlockSpec((None, tm, d), lambda i, j, f: (i, j, 0)),
        out_shape=jax.ShapeDtypeStruct((b, s, d), F32),
        scratch_shapes=[pltpu.VMEM((tm, d), BF16), pltpu.VMEM((tm, d), F32)],
        input_output_aliases={0: 0},
        compiler_params=_cparams(("parallel", "parallel", "arbitrary")),
        name="ffn_dense",
    )(xs, modtab, gain, wg, wu, wd)


def _top2_combine(logits):
    lane = _lane_iota(logits.shape)
    lane_f = lane.astype(F32)
    lg = jnp.where(lane < N_EXPERTS, logits, -jnp.inf)
    v1 = jnp.max(lg, axis=-1, keepdims=True)
    i1 = jnp.min(jnp.where(lg == v1, lane_f, float(LANE)), axis=-1, keepdims=True)
    rest = jnp.where(lane_f == i1, -jnp.inf, lg)
    v2 = jnp.max(rest, axis=-1, keepdims=True)
    i2 = jnp.min(jnp.where(rest == v2, lane_f, float(LANE)), axis=-1, keepdims=True)
    e2 = jnp.exp(v2 - v1)
    w1 = 1.0 / (1.0 + e2)
    w2 = e2 / (1.0 + e2)
    sel = jnp.where((lane_f == i1) | (lane_f == i2), 1.0, 0.0)
    return jnp.where(lane_f == i1, w1, 0.0) + jnp.where(lane_f == i2, w2, 0.0), sel


def _moe_kernel(ctx_len, x_ref, mod_ref, g_ref, r_ref, tri_ref, wg_ref, wu_ref, wd_ref, o_ref,
                h_sc, comb_sc, rank_sc, rankt_sc, rankc_sc, wc_sc, xg_sc, acc_sc, y_sc, nt_sc):
    tm = x_ref.shape[0]
    t0 = pl.program_id(1) * tm
    e = pl.program_id(2)
    f = pl.program_id(3)

    @pl.when(jnp.logical_and(e == 0, f == 0))
    def _():
        h = _modulated(x_ref[...], g_ref[...], mod_ref, t0, ctx_len, 3, 4)
        logits = jnp.dot(h, r_ref[...], preferred_element_type=F32, precision=lax.Precision.HIGHEST)
        comb, sel = _top2_combine(logits)
        rank =jnp.dot(tri_ref[...], sel.astype(BF16), preferred_element_type=F32)
        rank = jnp.where(sel > 0.0, rank, -1.0)
        comb_sc[...] = comb
        rank_sc[...] = rank
        pad = rankt_sc.shape[1] - tm
        h_sc[0:tm, :] = h.astype(BF16)
        if pad:
            rankt_sc[...] = jnp.concatenate([rank, jnp.full((pad, LANE), -1.0, F32)], axis=0).T
            h_sc[tm:, :] = jnp.zeros((pad, h_sc.shape[1]), BF16)
        else:
            rankt_sc[...] = rank.T
        y_sc[...] = jnp.zeros_like(y_sc)

    @pl.when(f == 0)
    def _():
        lane = _lane_iota((tm, LANE))
        rank_c = jnp.sum(jnp.where(lane == e, rank_sc[...], 0.0), axis=-1, keepdims=True)
        rankc_sc[...] = rank_c
        wc_sc[...] = jnp.sum(jnp.where(lane == e, comb_sc[...], 0.0), axis=-1, keepdims=True)
        n_rows = jnp.sum(jnp.where(rank_c >= 0.0, 1.0, 0.0)).astype(jnp.int32)
        n_rt = lax.div(n_rows + (RT - 1), RT)
        n_gt = lax.div(n_rt * RT + (GT - 1), GT)
        nt_sc[0] = n_rt
        nt_sc[1] = n_gt
        rank_r = rankt_sc[pl.ds(e, 1), :]

        def gather(i, c):
            r0 = pl.multiple_of(i * GT, GT)
            tgt = (r0 + _row_iota((GT, 1))).astype(F32)
            sel_t = jnp.where(rank_r == tgt, 1.0, 0.0).astype(BF16)
            xg_sc[pl.ds(r0, GT), :] = jnp.dot(sel_t, h_sc[...], preferred_element_type=F32).astype(BF16)
            acc_sc[pl.ds(r0, GT), :] = jnp.zeros((GT, acc_sc.shape[1]), F32)
            return c

        lax.fori_loop(0, n_gt, gather, 0)

    def expert(i, c):
        r0 = pl.multiple_of(i * RT, 16)
        rows = xg_sc[pl.ds(r0, RT), :]
        a = jnp.dot(rows, wg_ref[...], preferred_element_type=F32)
        u = jnp.dot(rows, wu_ref[...], preferred_element_type=F32)
        acc_sc[pl.ds(r0, RT), :] += jnp.dot((_silu(a) * u).astype(BF16), wd_ref[...],
                                            preferred_element_type=F32)
        return c

    lax.fori_loop(0, nt_sc[0], expert, 0)

    @pl.when(f == pl.num_programs(3) - 1)
    def _():
        def scatter(i, c):
            r0 = pl.multiple_of(i * GT, GT)
            tgt = (r0 + _lane_iota((1, GT))).astype(F32)
            w_t = jnp.where(rankc_sc[...] == tgt, wc_sc[...], 0.0).astype(BF16)
            y_sc[...] += jnp.dot(w_t, acc_sc[pl.ds(r0, GT), :].astype(BF16), preferred_element_type=F32)
            return c

        lax.fori_loop(0, nt_sc[1], scatter, 0)

    @pl.when(jnp.logical_and(e == pl.num_programs(2) - 1, f == pl.num_programs(3) - 1))
    def _():
        o_ref[...] = x_ref[...] + _gate_rows(mod_ref, t0, tm, ctx_len, 5) * y_sc[...]


def _moe(xs, modtab, gain, router_p, wg, wu, wd, ctx_len):
    b, s, d = xs.shape
    tm = s // 4
    tp = -(-tm // LANE) * LANE
    tr = -(-(-(-tm // RT) * RT) // GT) * GT
    ne, _, dff = wg.shape
    tri = jnp.tril(jnp.ones((tm, tm), BF16), -1)
    TF = TF_MOE
    assert dff % TF == 0

    def chunk_major(w):
        return w.reshape(ne, d, dff // TF, TF).transpose(0, 2, 1, 3)

    wg, wu = chunk_major(wg), chunk_major(wu)
    return pl.pallas_call(
        functools.partial(_moe_kernel, ctx_len),
        grid=(b, s // tm, ne, dff // TF),
        in_specs=[pl.BlockSpec((None, tm, d), lambda i, j, e, f: (i, j, 0)),
                  pl.BlockSpec((None, 2, 6, d), lambda i, j, e, f: (i, 0, 0, 0)),
                  pl.BlockSpec((1, d), lambda i, j, e, f: (0, 0)),
                  pl.BlockSpec((d, LANE), lambda i, j, e, f: (0, 0)),
                  pl.BlockSpec((tm, tm), lambda i, j, e, f: (0, 0), pipeline_mode=pl.Buffered(1)),
                  pl.BlockSpec((None, None, d, TF), lambda i, j, e, f: (e, f, 0, 0)),
                  pl.BlockSpec((None, None, d, TF), lambda i, j, e, f: (e, f, 0, 0)),
                  pl.BlockSpec((None, TF, d), lambda i, j, e, f: (e, f, 0))],
        out_specs=pl.BlockSpec((None, tm, d), lambda i, j, e, f: (i, j, 0)),
        out_shape=jax.ShapeDtypeStruct((b, s, d), F32),
        scratch_shapes=[pltpu.VMEM((tp, d), BF16),
                        pltpu.VMEM((tm, LANE), F32),
                        pltpu.VMEM((tm, LANE), F32),
                        pltpu.VMEM((LANE, tp), F32),
                        pltpu.VMEM((tm, 1), F32),
                        pltpu.VMEM((tm, 1), F32),
                        pltpu.VMEM((tr, d), BF16),
                        pltpu.VMEM((tr, d), F32),
                        pltpu.VMEM((tm, d), F32),
                        pltpu.SMEM((2,), jnp.int32)],
        input_output_aliases={0: 0},
        compiler_params=_cparams(("parallel", "parallel", "arbitrary", "arbitrary")),
        name="moe_top2",
    )(xs, modtab, gain, router_p, tri, wg, wu, wd)


def _pad_heads(w, n_heads, dh, axis=-1):
    axis = axis % w.ndim
    shp = w.shape[:axis] + (n_heads, dh) + w.shape[axis + 1:]
    pad = [(0, 0)] * (w.ndim + 1)
    pad[axis + 1] = (0, SLOT - dh)
    out = jnp.pad(w.reshape(shp), pad)
    return out.reshape(w.shape[:axis] + (n_heads * SLOT,) + w.shape[axis + 1:])


def _rope_tables(seq, ctx_len):
    t = jnp.arange(seq)
    rows = (t // GRID_W).astype(F32)
    cols = (t % GRID_W).astype(F32)

    def angles(rot_dim):
        nf = rot_dim // 4
        inv = ROPE_THETA ** (-jnp.arange(nf, dtype=F32) / nf)
        ar = rows[:, None] * inv
        ac = cols[:, None] * inv
        return jnp.concatenate([ar, ar, ac, ac], axis=-1)

    def slot_tables(rot_dim, lane0):
        ang = angles(rot_dim)
        quarter = rot_dim // 4
        first = (jnp.arange(rot_dim) % (2 * quarter)) < quarter
        cos = jnp.ones((seq, SLOT), F32).at[:, lane0:lane0 + rot_dim].set(jnp.cos(ang))
        sin = jnp.sin(ang)
        sin_m = jnp.zeros((seq, SLOT), F32).at[:, lane0:lane0 + rot_dim].set(jnp.where(first, -sin, 0.0))
        sin_p = jnp.zeros((seq, SLOT), F32).at[:, lane0:lane0 + rot_dim].set(jnp.where(first, 0.0, sin))
        ident = jnp.stack([jnp.ones((ctx_len, SLOT), F32), jnp.zeros((ctx_len, SLOT), F32),
                           jnp.zeros((ctx_len, SLOT), F32)])
        return jnp.concatenate([ident, jnp.stack([cos, sin_m, sin_p])], axis=1)

    return jnp.concatenate([slot_tables(A_DH, 0), slot_tables(L_ROPE, L_NOPE)], axis=0)


def _layer_params(w_in, m_conv, m_gate_b, m_norm, a_qnorm, a_knorm, l_cq_norm, l_ckv_norm, l_wuq, l_wukv,
                  l_qnorm, l_knorm, w_out):
    d = w_in.shape[0]
    offs = np.cumsum((0,) + IN_SIZES)
    seg = [w_in[:, offs[i]:offs[i + 1]] for i in range(len(IN_SIZES))]
    misc = jnp.zeros((d, SLOT), F32).at[:, 0:4 * M_HEADS].set(seg[4]).at[:, MISC_KR:MISC_KR + L_ROPE].set(seg[10])
    w_in_p = jnp.concatenate(
        [_pad_heads(seg[0], M_HEADS, M_DH), _pad_heads(seg[1], M_HEADS, M_DH),
         _pad_heads(seg[2], M_HEADS, M_DH), _pad_heads(seg[3], M_HEADS, M_DH), misc,
         _pad_heads(seg[5], A_HEADS, A_DH), _pad_heads(seg[6], A_KV, A_DH), _pad_heads(seg[7], A_KV, A_DH),
         seg[8], seg[9]], axis=1).astype(BF16)
    conv_w = jnp.concatenate([_pad_heads(m_conv[:, :M_HEADS * M_DH], M_HEADS, M_DH),
                              _pad_heads(m_conv[:, M_HEADS * M_DH:], M_HEADS, M_DH)], axis=1)
    conv_scale = jnp.concatenate([jnp.ones((1, M_HEADS * SLOT), F32),
                                  jnp.full((1, M_HEADS * SLOT), M_DH ** -0.5, F32)], axis=1)
    gate_b = jnp.zeros((1, SLOT), F32).at[0, 0:4 * M_HEADS].set(m_gate_b)
    pad1 = lambda g: jnp.pad(g, (0, SLOT - g.shape[0]))
    an = jnp.stack([pad1(a_qnorm), pad1(a_knorm)])
    ln = jnp.stack([pad1(l_qnorm), pad1(l_knorm)])
    wuq_p = _pad_heads(l_wuq, L_HEADS, L_QK).astype(BF16)
    kv = l_wukv.reshape(L_KVRANK, L_HEADS, L_NOPE + L_DV)
    wukv_p = jnp.concatenate(
        [_pad_heads(kv[:, :, :L_NOPE].reshape(L_KVRANK, -1), L_HEADS, L_NOPE),
         _pad_heads(kv[:, :, L_NOPE:].reshape(L_KVRANK, -1), L_HEADS, L_DV)], axis=1).astype(BF16)
    nm, na = M_HEADS * M_DH, A_HEADS * A_DH
    w_out_p = jnp.concatenate(
        [_pad_heads(w_out[:nm], M_HEADS, M_DH, axis=0), _pad_heads(w_out[nm:nm + na], A_HEADS, A_DH, axis=0),
         _pad_heads(w_out[nm + na:], L_HEADS, L_DV, axis=0)], axis=0).astype(BF16)
    m_norm_p = _pad_heads(m_norm[None, :], M_HEADS, M_DH)
    return dict(w_in_p=w_in_p, conv_w=conv_w, conv_scale=conv_scale, gate_b=gate_b, an=an, ln=ln,
                cqn=l_cq_norm[None, :], ckvn=l_ckv_norm[None, :], wuq_p=wuq_p, wukv_p=wukv_p,
                w_out_p=w_out_p, m_norm_p=m_norm_p)


def kernel(x, c, ctx, c_ctx, mod_w, mod_b, norm_mix, norm_ffn, w_in, m_conv, m_gate_b, m_norm, a_qnorm, a_knorm,
           l_cq_norm, l_ckv_norm, l_wuq, l_wukv, l_qnorm, l_knorm, w_out, ffn_wg, ffn_wu, ffn_wd, moe_router,
           moe_wg, moe_wu, moe_wd):
    b, seq, d = x.shape
    ctx_len = ctx.shape[1]
    depth = mod_w.shape[0]
    assert ctx_len % TB == 0 and ctx_len % TQ == 0 and ctx_len == ML and seq % TK == 0 and seq % GRID_W == 0
    xs = jnp.concatenate([ctx, x], axis=1)
    mod_rows = 16
    cc = jnp.zeros((mod_rows, d), F32).at[:b].set(c).at[b].set(c_ctx)
    mod_all = _mod_table(cc, mod_w, mod_b)
    rope_tab = _rope_tables(seq, ctx_len)
    for i in range(depth):
        need_ctx = i < depth - 1
        lat = mod_all[i, :b].reshape(b, 1, 6, d)
        cm = jnp.broadcast_to(mod_all[i, b].reshape(1, 1, 6, d), (b, 1, 6, d))
        modtab = jnp.concatenate([cm, lat], axis=1)
        p = _layer_params(w_in[i], m_conv[i], m_gate_b[i], m_norm[i], a_qnorm[i], a_knorm[i], l_cq_norm[i],
                          l_ckv_norm[i], l_wuq[i], l_wukv[i], l_qnorm[i], l_knorm[i], w_out[i])
        mqk, mo, misc, q_all, k_all, mv_t, v_all = _inproj(
            xs, modtab, norm_mix[i][None, :], p["w_in_p"], rope_tab, p["an"], p["ln"], p["cqn"], p["ckvn"],
            p["wuq_p"], p["wukv_p"], ctx_len)
        q_t, k_c = _conv(mqk, p["conv_w"], p["conv_scale"], ctx_len)
        hf, hb = _mlstm(q_t, k_c, mv_t, misc, p["gate_b"])
        gqa_kv = tuple(h // (A_HEADS // A_KV) for h in range(A_HEADS))
        o_gqa = _attention(q_all, k_all, v_all, ctx_len, need_ctx, 0, KV_GQA, A_KV, gqa_kv)
        o_mla = _attention(q_all, k_all, v_all, ctx_len, need_ctx, Q_MLA, 0, L_HEADS, tuple(range(L_HEADS)))
        xs = _mixout(xs, modtab, hf, hb, mo, o_gqa, o_mla, p["m_norm_p"], p["w_out_p"], ctx_len, not need_ctx)
        ffn_ctx = ctx_len if need_ctx else 0
        j = i // 2
        if i % 2 == 0:
            xs = _ffn(xs, modtab, norm_ffn[i][None, :], ffn_wg[j].astype(BF16), ffn_wu[j].astype(BF16),
                      ffn_wd[j].astype(BF16), ffn_ctx)
        else:
            router_p = jnp.pad(moe_router[j], ((0, 0), (0, LANE - N_EXPERTS)))
            xs = _moe(xs, modtab, norm_ffn[i][None, :], router_p, moe_wg[j].astype(BF16),
                      moe_wu[j].astype(BF16), moe_wd[j].astype(BF16), ffn_ctx)
    return xs
```

```python
import functools
import math

import numpy as np
import jax
import jax.numpy as jnp
from jax import lax
from jax.experimental import pallas as pl
from jax.experimental.pallas import tpu as pltpu

F32 = jnp.float32
BF16 = jnp.bfloat16

GRID_W = 64
EPS = 1e-6
ROPE_THETA = 10000.0
M_HEADS, M_DH = 4, 96
A_HEADS, A_KV, A_DH = 6, 2, 64
L_HEADS, L_NOPE, L_ROPE, L_DV = 4, 64, 32, 64
L_QK = L_NOPE + L_ROPE
L_QRANK, L_KVRANK = 256, 128
N_EXPERTS, TOP_K = 8, 2
IN_SIZES = (384, 384, 384, 384, 16, 384, 128, 128, 256, 128, 32)

LANE = 128
SLOT = LANE
N_KVHEADS = A_KV + L_HEADS
Q_MLA = 8
Q_SLOTS = Q_MLA + L_HEADS
KV_GQA = L_HEADS
VMEM_LIMIT = 56 * 1024 * 1024

O_MQK, O_MV, O_MO, O_MISC = 0, 1024, 1536, 2048
O_QA, O_KA, O_VA, O_CQ, O_CKV, IN_PAD = 2176, 2944, 3200, 3456, 3712, 3840
MISC_KR = 64
DEN_LANE = M_DH
SUM_LANE = A_DH
LOG2E = math.log2(math.e)
assert A_DH == L_DV and SUM_LANE < SLOT

TB = 256
TQ = 256
TK = 512
TM_FFN = 1088
TF = 512
ML = 256
RT = 144
GT = 2 * RT
ST = 3 * LANE
TF_MOE = 896


def _cparams(sem):
    return pltpu.CompilerParams(dimension_semantics=sem, vmem_limit_bytes=VMEM_LIMIT)


def _sigmoid(x):
    return 1.0 / (1.0 + jnp.exp(-x))


def _silu(x):
    return x * _sigmoid(x)


def _lane_iota(shape):
    return lax.broadcasted_iota(jnp.int32, shape, len(shape) - 1)


def _row_iota(shape):
    return lax.broadcasted_iota(jnp.int32, shape, 0)


def _modulated(x, gain, mod_ref, t0, ctx_len, k_shift, k_scale):
    tm = x.shape[0]
    is_ctx = (t0 + _row_iota((tm, 1))) < ctx_len
    shift = jnp.where(is_ctx, mod_ref[0, k_shift:k_shift + 1, :], mod_ref[1, k_shift:k_shift + 1, :])
    scale = jnp.where(is_ctx, mod_ref[0, k_scale:k_scale + 1, :], mod_ref[1, k_scale:k_scale + 1, :])
    y = x * lax.rsqrt(jnp.mean(x * x, axis=-1, keepdims=True) + EPS) * gain
    return y * (1.0 + scale) + shift


def _gate_rows(mod_ref, t0, tm, ctx_len, k_gate):
    is_ctx = (t0 + _row_iota((tm, 1))) < ctx_len
    return jnp.where(is_ctx, mod_ref[0, k_gate:k_gate + 1, :], mod_ref[1, k_gate:k_gate + 1, :])


def _mod_kernel(c_ref, w_ref, b_ref, o_ref):
    s = _silu(c_ref[...]).astype(BF16)
    o_ref[...] = jnp.dot(s, w_ref[...].astype(BF16), preferred_element_type=F32) + b_ref[...]


def _mod_table(cc, mod_w, mod_b):
    depth, d, n = mod_w.shape
    rows = cc.shape[0]
    return pl.pallas_call(
        _mod_kernel,
        grid=(depth, n // d),
        in_specs=[pl.BlockSpec((rows, d), lambda l, j: (0, 0)),
                  pl.BlockSpec((None, d, d), lambda l, j: (l, 0, j)),
                  pl.BlockSpec((None, 1, d), lambda l, j: (l, 0, j))],
        out_specs=pl.BlockSpec((None, rows, d), lambda l, j: (l, 0, j)),
        out_shape=jax.ShapeDtypeStruct((depth, rows, n), F32),
        compiler_params=_cparams(("arbitrary", "arbitrary")),
        name="mod_table",
    )(cc, mod_w, mod_b.reshape(depth, 1, n))


def _slot_rms(x, lo, hi):
    lane = _lane_iota(x.shape)
    sq = jnp.where((lane >= lo) & (lane < hi), x * x, 0.0)
    return lax.rsqrt(jnp.sum(sq, axis=-1, keepdims=True) * (1.0 / (hi - lo)) + EPS)


def _rope(x, cos, sin_m, sin_p, quarter):
    return (x * cos + pltpu.roll(x, LANE - quarter, 1) * sin_m + pltpu.roll(x, quarter, 1) * sin_p)


def _inproj_kernel(ctx_len, x_ref, mod_ref, g_ref, w_ref, rope_ref, an_ref, ln_ref, cqn_ref, ckvn_ref,
                   wuq_ref, wukv_ref, mqk_ref, mo_ref, misc_ref, q_ref, k_ref, mvt_ref, vt_ref):
    tm = x_ref.shape[0]
    t0 = pl.program_id(1) * tm
    h = _modulated(x_ref[...], g_ref[...], mod_ref, t0, ctx_len, 0, 1).astype(BF16)

    def proj(a, b):
        return jnp.dot(h, w_ref[:, a:b], preferred_element_type=F32)

    mqk_ref[...] = proj(O_MQK, O_MV).astype(BF16)
    mv = proj(O_MV, O_MO)
    mvt_ref[...] = jnp.where(_lane_iota(mv.shape) % SLOT == DEN_LANE, 1.0, mv).T.astype(BF16)
    mo_ref[...] = proj(O_MO, O_MISC).astype(BF16)
    misc = proj(O_MISC, O_QA)
    misc_ref[...] = misc

    cos_a, sinm_a, sinp_a = rope_ref[0], rope_ref[1], rope_ref[2]
    cos_l, sinm_l, sinp_l = rope_ref[3], rope_ref[4], rope_ref[5]
    a_scale = A_DH ** -0.5 * LOG2E
    l_scale = L_QK ** -0.5 * LOG2E

    def with_ones_t(v):
        return jnp.where(_lane_iota(v.shape) % SLOT == SUM_LANE, 1.0, v).T.astype(BF16)

    pa = proj(O_QA, O_CQ)
    gq, gk = an_ref[0:1, :], an_ref[1:2, :]
    for i in range(A_HEADS):
        x = pa[:, i * SLOT:(i + 1) * SLOT]
        x = x * _slot_rms(x, 0, A_DH) * gq
        q_ref[:, i * SLOT:(i + 1) * SLOT] = (_rope(x, cos_a, sinm_a, sinp_a, A_DH // 4) * a_scale).astype(BF16)
    for i in range(A_KV):
        x = pa[:, (A_HEADS + i) * SLOT:(A_HEADS + i + 1) * SLOT]
        x = x * _slot_rms(x, 0, A_DH) * gk
        k_ref[:, (KV_GQA + i) * SLOT:(KV_GQA + i + 1) * SLOT] = (
            _rope(x, cos_a, sinm_a, sinp_a, A_DH // 4).astype(BF16))
    vt_ref[KV_GQA * SLOT:, :] = with_ones_t(pa[:, (A_HEADS + A_KV) * SLOT:(A_HEADS + 2 * A_KV) * SLOT])
    q_ref[:, A_HEADS * SLOT:Q_MLA * SLOT] = jnp.zeros((tm, (Q_MLA - A_HEADS) * SLOT), BF16)

    pc = proj(O_CQ, IN_PAD)
    cq = pc[:, 0:L_QRANK]
    cq = (cq * lax.rsqrt(jnp.mean(cq * cq, axis=-1, keepdims=True) + EPS) * cqn_ref[...]).astype(BF16)
    ckv = pc[:, L_QRANK:L_QRANK + L_KVRANK]
    ckv = (ckv * lax.rsqrt(jnp.mean(ckv * ckv, axis=-1, keepdims=True) + EPS) * ckvn_ref[...]).astype(BF16)
    ql = jnp.dot(cq, wuq_ref[...], preferred_element_type=F32)
    kvl = jnp.dot(ckv, wukv_ref[...], preferred_element_type=F32)
    gq_l, gk_l = ln_ref[0:1, :], ln_ref[1:2, :]
    lane = _lane_iota((tm, SLOT))
    kr = jnp.where((lane >= MISC_KR) & (lane < MISC_KR + L_ROPE), misc, 0.0)
    kr = kr * _slot_rms(kr, MISC_KR, MISC_KR + L_ROPE) * gk_l
    kr = _rope(kr, cos_l, sinm_l, sinp_l, L_ROPE // 4)
    for i in range(L_HEADS):
        x = ql[:, i * SLOT:(i + 1) * SLOT]
        inv = jnp.where(lane < L_NOPE, _slot_rms(x, 0, L_NOPE), _slot_rms(x, L_NOPE, L_QK))
        x = x * inv * gq_l
        q_ref[:, (Q_MLA + i) * SLOT:(Q_MLA + i + 1) * SLOT] = (
            _rope(x, cos_l, sinm_l, sinp_l, L_ROPE // 4) * l_scale).astype(BF16)
        kn = kvl[:, i * SLOT:(i + 1) * SLOT]
        kn = kn * _slot_rms(kn, 0, L_NOPE) * gk_l
        k_ref[:, i * SLOT:(i + 1) * SLOT] = (kn + kr).astype(BF16)
    vt_ref[0:KV_GQA * SLOT, :] = with_ones_t(kvl[:, L_HEADS * SLOT:])


def _inproj(xs, modtab, gain, w_in_p, rope_tab, an, ln, cqn, ckvn, wuq_p, wukv_p, ctx_len):
    b, s, d = xs.shape
    grid = (b, s // TB)
    tok = lambda n: pl.BlockSpec((None, TB, n), lambda i, j: (i, j, 0))
    full = lambda a: pl.BlockSpec(a.shape, lambda i, j: (0,) * a.ndim)
    out_widths = (1024, 512, SLOT, Q_SLOTS * SLOT, N_KVHEADS * SLOT)
    out_dtypes = (BF16, BF16, F32, BF16, BF16)
    out_specs = [tok(n) for n in out_widths]
    out_shape = [jax.ShapeDtypeStruct((b, s, n), dt) for n, dt in zip(out_widths, out_dtypes)]
    for n in (M_HEADS * SLOT, N_KVHEADS * SLOT):
        out_specs.append(pl.BlockSpec((None, n, TB), lambda i, j: (i, 0, j)))
        out_shape.append(jax.ShapeDtypeStruct((b, n, s), BF16))
    return pl.pallas_call(
        functools.partial(_inproj_kernel, ctx_len),
        grid=grid,
        in_specs=[tok(d),
                  pl.BlockSpec((None, 2, 6, d), lambda i, j: (i, 0, 0, 0)),
                  full(gain), full(w_in_p),
                  pl.BlockSpec((6, TB, SLOT), lambda i, j: (0, j, 0)),
                  full(an), full(ln), full(cqn), full(ckvn), full(wuq_p), full(wukv_p)],
        out_specs=out_specs,
        out_shape=out_shape,
        compiler_params=_cparams(("parallel", "parallel")),
        name="inproj",
    )(xs, modtab, gain, w_in_p, rope_tab, an, ln, cqn, ckvn, wuq_p, wukv_p)


def _conv_kernel(ctx_len, s_len, x_ref, prev_ref, next_ref, w_ref, sc_ref, qt_ref, k_ref):
    tm = x_ref.shape[0]
    t0 = pl.program_id(1) * tm
    x = x_ref[...].astype(F32)
    row = _row_iota((tm, 1))
    has_prev = jnp.logical_and(t0 != 0, t0 != ctx_len)
    has_next = jnp.logical_and(t0 + tm != ctx_len, t0 + tm != s_len)
    hp = jnp.where(has_prev, prev_ref[15:16, :].astype(F32), 0.0)
    hn = jnp.where(has_next, next_ref[0:1, :].astype(F32), 0.0)
    xp = jnp.where(row == 0, hp, pltpu.roll(x, 1, 0))
    xn = jnp.where(row == tm - 1, hn, pltpu.roll(x, tm - 1, 0))
    y = xp * w_ref[0:1, :] + x * w_ref[1:2, :] + xn * w_ref[2:3, :]
    y = _silu(y) * sc_ref[...]
    half = y.shape[1] // 2
    qt_ref[...] = y[:, :half].T.astype(qt_ref.dtype)
    k_ref[...] = y[:, half:].astype(k_ref.dtype)


def _conv(mqk, conv_w, conv_scale, ctx_len):
    b, s, n = mqk.shape
    hb = TB // 16
    last = s // 16 - 1
    return pl.pallas_call(
        functools.partial(_conv_kernel, ctx_len, s),
        grid=(b, s // TB),
        in_specs=[pl.BlockSpec((None, TB, n), lambda i, j: (i, j, 0)),
                  pl.BlockSpec((None, 16, n), lambda i, j: (i, jnp.maximum(j * hb - 1, 0), 0)),
                  pl.BlockSpec((None, 16, n), lambda i, j: (i, jnp.minimum((j + 1) * hb, last), 0)),
                  pl.BlockSpec((3, n), lambda i, j: (0, 0)),
                  pl.BlockSpec((1, n), lambda i, j: (0, 0))],
        out_specs=[pl.BlockSpec((None, n // 2, TB), lambda i, j: (i, 0, j)),
                   pl.BlockSpec((None, TB, n // 2), lambda i, j: (i, j, 0))],
        out_shape=[jax.ShapeDtypeStruct((b, n // 2, s), BF16), jax.ShapeDtypeStruct((b, s, n // 2), BF16)],
        compiler_params=_cparams(("parallel", "parallel")),
        name="mlstm_conv",
    )(mqk, mqk, mqk, conv_w, conv_scale)


def _log_sigmoid(x):
    return jnp.minimum(x, 0.0) - jnp.log(1.0 + jnp.exp(-jnp.abs(x)))


def _mlstm_kernel(n_chunks, qt_ref, k_ref, vt_ref, misc_ref, gb_ref, hf_ref, hb_ref, c_ref, m_ref, gt_sc, bt_sc):
    hp = pl.program_id(1)
    c_ref[...] = jnp.zeros_like(c_ref)
    m_ref[...] = jnp.zeros_like(m_ref)
    r = _row_iota((ML, ML))
    cidx = _lane_iota((ML, ML))
    tri_f32 = (r >= cidx).astype(F32)
    tri_b32 = (r <= cidx).astype(F32)
    lane = _lane_iota((ML, SLOT))
    lane_t = _lane_iota((1, ML))

    def chunk(t0, d, tri32, out_ref):
        g = misc_ref[pl.ds(t0, ML), :] + gb_ref[...]
        logf = _log_sigmoid(g)
        bcum = jnp.dot(tri32, logf, preferred_element_type=F32, precision=lax.Precision.HIGHEST)
        gt_sc[d] = g.T
        bt_sc[d] = bcum.T
        ib = g - pltpu.roll(bcum, SLOT - M_HEADS, 1)
        allowed = (r <= cidx) if d == 0 else (r >= cidx)
        for hh in range(2):
            li = d * 2 * M_HEADS + hp * 2 + hh
            ib_col = jnp.sum(jnp.where(lane == li, ib, 0.0), axis=-1, keepdims=True)
            i_row = gt_sc[d, pl.ds(li, 1), :]
            b_row = bt_sc[d, pl.ds(li + M_HEADS, 1), :]
            sidx = d * 2 + hh
            ct_st = c_ref[sidx]
            m_s = m_ref[sidx]
            k = k_ref[pl.ds(t0, ML), hh * SLOT:(hh + 1) * SLOT]
            qt = qt_ref[hh * SLOT:(hh + 1) * SLOT, pl.ds(t0, ML)]
            vt = vt_ref[hh * SLOT:(hh + 1) * SLOT, pl.ds(t0, ML)]
            dmat = jnp.where(allowed, b_row + ib_col, -jnp.inf)
            m_inter = b_row + m_s
            m_t = jnp.maximum(m_inter, jnp.max(dmat, axis=0, keepdims=True))
            kq = jnp.dot(k, qt, preferred_element_type=F32)
            w = (jnp.exp(dmat - m_t) * kq).astype(BF16)
            a_inter = jnp.exp(m_inter - m_t)
            num = a_inter * jnp.dot(ct_st.astype(BF16), qt, preferred_element_type=F32) + jnp.dot(
                vt, w, preferred_element_type=F32)
            den = num[DEN_LANE:DEN_LANE + 1, :]
            h_out = num / jnp.maximum(jnp.abs(den), jnp.exp(-m_t))
            out_ref[pl.ds(t0, ML), hh * SLOT:(hh + 1) * SLOT] = h_out.T.astype(out_ref.dtype)
            last = ML - 1 if d == 0 else 0
            total = jnp.sum(jnp.where(lane_t == last, b_row, 0.0), axis=-1, keepdims=True)
            gg = total - b_row + i_row
            m_new = jnp.maximum(total + m_s, jnp.max(gg, axis=-1, keepdims=True))
            decay = jnp.exp(total + m_s - m_new)[:, 0:1]
            wk = jnp.exp(gg - m_new)
            vw = (vt.astype(F32) * wk).astype(BF16)
            c_ref[sidx] = decay * ct_st + jnp.dot(vw, k, preferred_element_type=F32)
            m_ref[sidx] = m_new

    def step(n, carry):
        tf0 = pl.multiple_of(n * ML, ML)
        tb0 = pl.multiple_of(jnp.where(n == 0, 0, n_chunks - n) * ML, ML)
        chunk(tf0, 0, tri_f32, hf_ref)
        chunk(tb0, 1, tri_b32, hb_ref)
        return carry

    lax.fori_loop(0, n_chunks, step, 0)


def _mlstm(q_t, k_c, mv_t, misc, gate_b):
    b, s, _ = k_c.shape
    n_chunks = s // ML
    pair = 2 * SLOT
    spec = pl.BlockSpec((None, s, pair), lambda i, j: (i, 0, j))
    spec_t = pl.BlockSpec((None, pair, s), lambda i, j: (i, j, 0))
    return pl.pallas_call(
        functools.partial(_mlstm_kernel, n_chunks),
        grid=(b, M_HEADS // 2),
        in_specs=[spec_t, spec, spec_t,
                  pl.BlockSpec((None, s, SLOT), lambda i, j: (i, 0, 0)),
                  pl.BlockSpec((1, SLOT), lambda i, j: (0, 0))],
        out_specs=[spec, spec],
        out_shape=[jax.ShapeDtypeStruct((b, s, M_HEADS * SLOT), BF16)] * 2,
        scratch_shapes=[pltpu.VMEM((4, SLOT, SLOT), F32), pltpu.VMEM((4, 1, ML), F32),
                        pltpu.VMEM((2, SLOT, ML), F32), pltpu.VMEM((2, SLOT, ML), F32)],
        compiler_params=_cparams(("parallel", "parallel")),
        name="mlstm_scan",
    )(q_t, k_c, mv_t, misc, gate_b)


def _attn_kernel(ctx_len, q_off, kv_of, q_ref, k_ref, vt_ref, o_ref, st_sc, pt_sc, acc_sc):
    s_len = k_ref.shape[0]
    n_heads = len(kv_of)
    qs = [q_ref[:, g * SLOT:(g + 1) * SLOT] for g in range(n_heads)]

    def kv_cols(g):
        return slice(kv_of[g] * SLOT, (kv_of[g] + 1) * SLOT)

    def chunk(carry, rows):
        out = []
        for g in range(n_heads):
            m, acc = carry[g]
            k = k_ref[rows, kv_cols(g)]
            vt = vt_ref[kv_cols(g), rows]
            st = lax.dot_general(k, qs[g], (((1,), (1,)), ((), ())), preferred_element_type=F32)
            m_new = jnp.maximum(m, jnp.max(st, axis=0, keepdims=True))
            alpha = jnp.exp2(m - m_new)
            pt = jnp.exp2(st - m_new).astype(BF16)
            out.append((m_new, alpha * acc + jnp.dot(vt, pt, preferred_element_type=F32)))
        return tuple(out)

    def finish(carry):
        for g in range(n_heads):
            acc = carry[g][1]
            o = acc / acc[SUM_LANE:SUM_LANE + 1, :]
            o_ref[:, g * SLOT:(g + 1) * SLOT] = o.T.astype(o_ref.dtype)

    init = tuple((jnp.full((1, TQ), -jnp.inf, F32), jnp.zeros((SLOT, TQ), F32)) for _ in range(n_heads))
    is_ctx = (pl.program_id(2) + q_off) * TQ < ctx_len

    @pl.when(is_ctx)
    def _():
        finish(chunk(init, pl.ds(0, ctx_len)))

    n_lat = (s_len - ctx_len) // TK
    n0 = ctx_len + TK
    assert n_lat >= 4 and n_lat % 2 == 0 and st_sc.shape[1] == n0

    def rows_of(c):
        if isinstance(c, int):
            return (pl.ds(0, n0), n0) if c == 0 else (pl.ds(ctx_len + c * TK, TK), TK)
        return pl.ds(pl.multiple_of(ctx_len + c * TK, math.gcd(ctx_len, TK)), TK), TK

    def scores(c, par):
        rows, n = rows_of(c)
        for g in range(n_heads):
            st_sc[par * n_heads + g, 0:n, :] = lax.dot_general(
                k_ref[rows, kv_cols(g)], qs[g], (((1,), (1,)), ((), ())), preferred_element_type=F32)

    def softmax(par, m, n=TK):
        ms, alphas = [], []
        for g in range(n_heads):
            st = st_sc[par * n_heads + g, 0:n, :]
            m_new = jnp.maximum(m[g], jnp.max(st, axis=0, keepdims=True))
            pt_sc[par * n_heads + g, 0:n, :] = jnp.exp2(st - m_new).astype(BF16)
            ms.append(m_new)
            alphas.append(jnp.exp2(m[g] - m_new))
        return tuple(ms), tuple(alphas)

    def values(c, par, alpha):
        rows, n = rows_of(c)
        for g in range(n_heads):
            acc_sc[g] = alpha[g] * acc_sc[g] + jnp.dot(vt_ref[kv_cols(g), rows], pt_sc[par * n_heads + g, 0:n, :],
                                                       preferred_element_type=F32)

    def stage(c, par, m, alpha):
        scores(c + 1, 1 - par)
        m_new, alpha_new = softmax(par, m)
        values(c - 1, 1 - par, alpha)
        return m_new, alpha_new

    @pl.when(jnp.logical_not(is_ctx))
    def _():
        for g in range(n_heads):
            acc_sc[g] = init[g][1]
        scores(0, 0)
        m, alpha = softmax(0, tuple(c[0] for c in init), n0)
        scores(1, 1)
        m, alpha = stage(1, 1, m, alpha)

        def body(t, carry):
            c = 2 * t + 2
            return stage(c + 1, 1, *stage(c, 0, *carry))

        m, alpha = lax.fori_loop(0, (n_lat - 4) // 2, body, (m, alpha))
        m, alpha = stage(n_lat - 2, 0, m, alpha)
        values(n_lat - 2, 0, alpha)
        m, alpha = softmax(1, m)
        values(n_lat - 1, 1, alpha)
        finish(tuple((m[g], acc_sc[g]) for g in range(n_heads)))


def _attention(q_all, k_all, vt_all, ctx_len, need_ctx, q_slot0, kv_slot0, n_kv, kv_of):
    b, s, _ = q_all.shape
    q_off = 0 if need_ctx else ctx_len // TQ
    nq = s // TQ - q_off
    n_heads = len(kv_of)
    qw = n_heads * SLOT
    kw = n_kv * SLOT
    assert (q_slot0 * SLOT) % qw == 0 and (kv_slot0 * SLOT) % kw == 0
    qb0, kb0 = q_slot0 * SLOT // qw, kv_slot0 * SLOT // kw
    return pl.pallas_call(
        functools.partial(_attn_kernel, ctx_len, q_off, kv_of),
        grid=(b, 1, nq),
        in_specs=[pl.BlockSpec((None, TQ, qw), lambda i, g, j: (i, j + q_off, qb0)),
                  pl.BlockSpec((None, s, kw), lambda i, g, j: (i, 0, kb0)),
                  pl.BlockSpec((None, kw, s), lambda i, g, j: (i, kb0, 0))],
        out_specs=pl.BlockSpec((None, TQ, qw), lambda i, g, j: (i, j + q_off, 0)),
        out_shape=jax.ShapeDtypeStruct((b, s, qw), BF16),
        scratch_shapes=[pltpu.VMEM((2 * n_heads, ctx_len + TK, TQ), F32),
                        pltpu.VMEM((2 * n_heads, ctx_len + TK, TQ), BF16),
                        pltpu.VMEM((n_heads, SLOT, TQ), F32)],
        compiler_params=_cparams(("parallel", "parallel", "parallel")),
        name="attention",
    )(q_all, k_all, vt_all)


def _mixout_kernel(ctx_len, blk_off, x_ref, mod_ref, hf_ref, hb_ref, mo_ref, oa_ref, ol_ref, mn_ref, w_ref, o_ref):
    tm = x_ref.shape[0]
    t0 = (pl.program_id(1) + blk_off) * tm
    hm = hf_ref[...].astype(F32) + hb_ref[...].astype(F32)
    gate = _sigmoid(mo_ref[...].astype(F32))
    parts = []
    for i in range(M_HEADS):
        x = hm[:, i * SLOT:(i + 1) * SLOT]
        inv = lax.rsqrt(jnp.sum(x * x, axis=-1, keepdims=True) * (1.0 / M_DH) + EPS)
        parts.append(x * inv)
    hn = jnp.concatenate(parts, axis=-1) * mn_ref[...] * gate
    nm = M_HEADS * SLOT
    o = jnp.dot(hn.astype(BF16), w_ref[0:nm, :], preferred_element_type=F32)
    na = nm + A_HEADS * SLOT
    o = o + jnp.dot(oa_ref[...], w_ref[nm:na, :], preferred_element_type=F32)
    o = o + jnp.dot(ol_ref[...], w_ref[na:, :], preferred_element_type=F32)
    o_ref[...] = x_ref[...] + _gate_rows(mod_ref, t0, tm, ctx_len, 2) * o


def _mixout(xs, modtab, hf, hb, mo, o_gqa, o_mla, m_norm_p, w_out_p, ctx_len, lat_only):
    b, s, d = xs.shape
    blk_off = ctx_len // TB if lat_only else 0
    tok = lambda n: pl.BlockSpec((None, TB, n), lambda i, j: (i, j + blk_off, 0))
    full = lambda a: pl.BlockSpec(a.shape, lambda i, j: (0,) * a.ndim)
    return pl.pallas_call(
        functools.partial(_mixout_kernel, ctx_len, blk_off),
        grid=(b, s // TB - blk_off),
        in_specs=[tok(d),
                  pl.BlockSpec((None, 2, 6, d), lambda i, j: (i, 0, 0, 0)),
                  tok(M_HEADS * SLOT), tok(M_HEADS * SLOT), tok(M_HEADS * SLOT), tok(A_HEADS * SLOT),
                  tok(L_HEADS * SLOT), full(m_norm_p), full(w_out_p)],
        out_specs=pl.BlockSpec((None, TB, d), lambda i, j: (i, j, 0)),
        out_shape=jax.ShapeDtypeStruct((b, s - blk_off * TB, d), F32),
        input_output_aliases={} if lat_only else {0: 0},
        compiler_params=_cparams(("parallel", "parallel")),
        name="mix_out",
    )(xs, modtab, hf, hb, mo, o_gqa, o_mla, m_norm_p, w_out_p)


def _ffn_kernel(ctx_len, x_ref, mod_ref, g_ref, wg_ref, wu_ref, wd_ref, o_ref, h_sc, acc_sc):
    tm = x_ref.shape[0]
    t0 = pl.program_id(1) * tm
    f = pl.program_id(2)

    @pl.when(f == 0)
    def _():
        h_sc[...] = _modulated(x_ref[...], g_ref[...], mod_ref, t0, ctx_len, 3, 4).astype(BF16)
        acc_sc[...] = jnp.zeros_like(acc_sc)

    h = h_sc[...]
    a = jnp.dot(h, wg_ref[...], preferred_element_type=F32)
    u = jnp.dot(h, wu_ref[...], preferred_element_type=F32)
    acc_sc[...] += jnp.dot((_silu(a) * u).astype(BF16), wd_ref[...], preferred_element_type=F32)

    @pl.when(f == pl.num_programs(2) - 1)
    def _():
        o_ref[...] = x_ref[...] + _gate_rows(mod_ref, t0, tm, ctx_len, 5) * acc_sc[...]


def _ffn(xs, modtab, gain, wg, wu, wd, ctx_len):
    b, s, d = xs.shape
    tm = s // 4
    nf = wg.shape[1] // TF
    return pl.pallas_call(
        functools.partial(_ffn_kernel, ctx_len),
        grid=(b, s // tm, nf),
        in_specs=[pl.BlockSpec((None, tm, d), lambda i, j, f: (i, j, 0)),
                  pl.BlockSpec((None, 2, 6, d), lambda i, j, f: (i, 0, 0, 0)),
                  pl.BlockSpec((1, d), lambda i, j, f: (0, 0)),
                  pl.BlockSpec((d, TF), lambda i, j, f: (0, f)),
                  pl.BlockSpec((d, TF), lambda i, j, f: (0, f)),
                  pl.BlockSpec((TF, d), lambda i, j, f: (f, 0))],
        out_specs=pl.BlockSpec((None, tm, d), lambda i, j, f: (i, j, 0)),
        out_shape=jax.ShapeDtypeStruct((b, s, d), F32),
        scratch_shapes=[pltpu.VMEM((tm, d), BF16), pltpu.VMEM((tm, d), F32)],
        input_output_aliases={0: 0},
        compiler_params=_cparams(("parallel", "parallel", "arbitrary")),
        name="ffn_dense",
    )(xs, modtab, gain, wg, wu, wd)


def _top2_combine(logits):
    lane = _lane_iota(logits.shape)
    lane_f = lane.astype(F32)
    lg = jnp.where(lane < N_EXPERTS, logits, -jnp.inf)
    v1 = jnp.max(lg, axis=-1, keepdims=True)
    i1 = jnp.min(jnp.where(lg == v1, lane_f, float(LANE)), axis=-1, keepdims=True)
    rest = jnp.where(lane_f == i1, -jnp.inf, lg)
    v2 = jnp.max(rest, axis=-1, keepdims=True)
    i2 = jnp.min(jnp.where(rest == v2, lane_f, float(LANE)), axis=-1, keepdims=True)
    e2 = jnp.exp(v2 - v1)
    w1 = 1.0 / (1.0 + e2)
    w2 = e2 / (1.0 + e2)
    sel = jnp.where((lane_f == i1) | (lane_f == i2), 1.0, 0.0)
    return jnp.where(lane_f == i1, w1, 0.0) + jnp.where(lane_f == i2, w2, 0.0), sel


def _moe_kernel(ctx_len, x_ref, mod_ref, g_ref, r_ref, tri_ref, wg_ref, wu_ref, wd_ref, o_ref,
                h_sc, comb_sc, rank_sc, rankt_sc, rankc_sc, wc_sc, xg_sc, acc_sc, y_sc, nt_sc):
    tm = x_ref.shape[0]
    t0 = pl.program_id(1) * tm
    e = pl.program_id(2)
    f = pl.program_id(3)

    @pl.when(jnp.logical_and(e == 0, f == 0))
    def _():
        h = _modulated(x_ref[...], g_ref[...], mod_ref, t0, ctx_len, 3, 4)
        logits = jnp.dot(h, r_ref[...], preferred_element_type=F32, precision=lax.Precision.HIGHEST)
        comb, sel = _top2_combine(logits)
        rank =jnp.dot(tri_ref[...], sel.astype(BF16), preferred_element_type=F32)
        rank = jnp.where(sel > 0.0, rank, -1.0)
        comb_sc[...] = comb
        rank_sc[...] = rank
        pad = rankt_sc.shape[1] - tm
        h_sc[0:tm, :] = h.astype(BF16)
        if pad:
            rankt_sc[...] = jnp.concatenate([rank, jnp.full((pad, LANE), -1.0, F32)], axis=0).T
            h_sc[tm:, :] = jnp.zeros((pad, h_sc.shape[1]), BF16)
        else:
            rankt_sc[...] = rank.T
        y_sc[...] = jnp.zeros_like(y_sc)

    @pl.when(f == 0)
    def _():
        lane = _lane_iota((tm, LANE))
        rank_c = jnp.sum(jnp.where(lane == e, rank_sc[...], 0.0), axis=-1, keepdims=True)
        rankc_sc[...] = rank_c
        wc_sc[...] = jnp.sum(jnp.where(lane == e, comb_sc[...], 0.0), axis=-1, keepdims=True)
        n_rows = jnp.sum(jnp.where(rank_c >= 0.0, 1.0, 0.0)).astype(jnp.int32)
        n_rt = lax.div(n_rows + (RT - 1), RT)
        n_gt = lax.div(n_rt * RT + (GT - 1), GT)
        n_st = lax.div(n_rt * RT + (ST - 1), ST)
        nt_sc[0] = n_rt
        nt_sc[1] = n_st
        rank_r = rankt_sc[pl.ds(e, 1), :]

        def gather(i, c):
            r0 = pl.multiple_of(i * GT, 16)
            tgt = (r0 + _row_iota((GT, 1))).astype(F32)
            sel_t = jnp.where(rank_r == tgt, 1.0, 0.0).astype(BF16)
            xg_sc[pl.ds(r0, GT), :] = jnp.dot(sel_t, h_sc[...], preferred_element_type=F32).astype(BF16)
            return c

        def clear(i, c):
            acc_sc[pl.ds(pl.multiple_of(i * ST, ST), ST), :] = jnp.zeros((ST, acc_sc.shape[1]), F32)
            return c

        lax.fori_loop(0, n_gt, gather, 0)
        lax.fori_loop(0, n_st, clear, 0)

    def expert(i, c):
        r0 = pl.multiple_of(i * RT, 16)
        rows = xg_sc[pl.ds(r0, RT), :]
        a = jnp.dot(rows, wg_ref[...], preferred_element_type=F32)
        u = jnp.dot(rows, wu_ref[...], preferred_element_type=F32)
        acc_sc[pl.ds(r0, RT), :] += jnp.dot((_silu(a) * u).astype(BF16), wd_ref[...],
                                            preferred_element_type=F32)
        return c

    lax.fori_loop(0, nt_sc[0], expert, 0)

    @pl.when(f == pl.num_programs(3) - 1)
    def _():
        def scatter(i, c):
            r0 = pl.multiple_of(i * ST, ST)
            tgt = (r0 + _lane_iota((1, ST))).astype(F32)
            w_t = jnp.where(rankc_sc[...] == tgt, wc_sc[...], 0.0).astype(BF16)
            y_sc[...] += jnp.dot(w_t, acc_sc[pl.ds(r0, ST), :].astype(BF16), preferred_element_type=F32)
            return c

        lax.fori_loop(0, nt_sc[1], scatter, 0)

    @pl.when(jnp.logical_and(e == pl.num_programs(2) - 1, f == pl.num_programs(3) - 1))
    def _():
        o_ref[...] = x_ref[...] + _gate_rows(mod_ref, t0, tm, ctx_len, 5) * y_sc[...]


def _moe(xs, modtab, gain, router_p, wg, wu, wd, ctx_len):
    b, s, d = xs.shape
    tm = s // 4
    tp = -(-tm // LANE) * LANE
    r_max = -(-tm // RT) * RT
    tr = max(-(-r_max // GT) * GT, -(-r_max // ST) * ST)
    ne, _, dff = wg.shape
    tri = jnp.tril(jnp.ones((tm, tm), BF16), -1)
    TF = TF_MOE
    assert dff % TF == 0
    return pl.pallas_call(
        functools.partial(_moe_kernel, ctx_len),
        grid=(b, s // tm, ne, dff // TF),
        in_specs=[pl.BlockSpec((None, tm, d), lambda i, j, e, f: (i, j, 0)),
                  pl.BlockSpec((None, 2, 6, d), lambda i, j, e, f: (i, 0, 0, 0)),
                  pl.BlockSpec((1, d), lambda i, j, e, f: (0, 0)),
                  pl.BlockSpec((d, LANE), lambda i, j, e, f: (0, 0)),
                  pl.BlockSpec((tm, tm), lambda i, j, e, f: (0, 0), pipeline_mode=pl.Buffered(1)),
                  pl.BlockSpec((None, d, TF), lambda i, j, e, f: (e, 0, f)),
                  pl.BlockSpec((None, d, TF), lambda i, j, e, f: (e, 0, f)),
                  pl.BlockSpec((None, TF, d), lambda i, j, e, f: (e, f, 0))],
        out_specs=pl.BlockSpec((None, tm, d), lambda i, j, e, f: (i, j, 0)),
        out_shape=jax.ShapeDtypeStruct((b, s, d), F32),
        scratch_shapes=[pltpu.VMEM((tp, d), BF16),
                        pltpu.VMEM((tm, LANE), F32),
                        pltpu.VMEM((tm, LANE), F32),
                        pltpu.VMEM((LANE, tp), F32),
                        pltpu.VMEM((tm, 1), F32),
                        pltpu.VMEM((tm, 1), F32),
                        pltpu.VMEM((tr, d), BF16),
                        pltpu.VMEM((tr, d), F32),
                        pltpu.VMEM((tm, d), F32),
                        pltpu.SMEM((2,), jnp.int32)],
        input_output_aliases={0: 0},
        compiler_params=_cparams(("parallel", "parallel", "arbitrary", "arbitrary")),
        name="moe_top2",
    )(xs, modtab, gain, router_p, tri, wg, wu, wd)


def _pad_heads(w, n_heads, dh, axis=-1):
    axis = axis % w.ndim
    shp = w.shape[:axis] + (n_heads, dh) + w.shape[axis + 1:]
    pad = [(0, 0)] * (w.ndim + 1)
    pad[axis + 1] = (0, SLOT - dh)
    out = jnp.pad(w.reshape(shp), pad)
    return out.reshape(w.shape[:axis] + (n_heads * SLOT,) + w.shape[axis + 1:])


def _rope_tables(seq, ctx_len):
    t = jnp.arange(seq)
    rows = (t // GRID_W).astype(F32)
    cols = (t % GRID_W).astype(F32)

    def angles(rot_dim):
        nf = rot_dim // 4
        inv = ROPE_THETA ** (-jnp.arange(nf, dtype=F32) / nf)
        ar = rows[:, None] * inv
        ac = cols[:, None] * inv
        return jnp.concatenate([ar, ar, ac, ac], axis=-1)

    def slot_tables(rot_dim, lane0):
        ang = angles(rot_dim)
        quarter = rot_dim // 4
        first = (jnp.arange(rot_dim) % (2 * quarter)) < quarter
        cos = jnp.ones((seq, SLOT), F32).at[:, lane0:lane0 + rot_dim].set(jnp.cos(ang))
        sin = jnp.sin(ang)
        sin_m = jnp.zeros((seq, SLOT), F32).at[:, lane0:lane0 + rot_dim].set(jnp.where(first, -sin, 0.0))
        sin_p = jnp.zeros((seq, SLOT), F32).at[:, lane0:lane0 + rot_dim].set(jnp.where(first, 0.0, sin))
        ident = jnp.stack([jnp.ones((ctx_len, SLOT), F32), jnp.zeros((ctx_len, SLOT), F32),
                           jnp.zeros((ctx_len, SLOT), F32)])
        return jnp.concatenate([ident, jnp.stack([cos, sin_m, sin_p])], axis=1)

    return jnp.concatenate([slot_tables(A_DH, 0), slot_tables(L_ROPE, L_NOPE)], axis=0)


def _layer_params(w_in, m_conv, m_gate_b, m_norm, a_qnorm, a_knorm, l_cq_norm, l_ckv_norm, l_wuq, l_wukv,
                  l_qnorm, l_knorm, w_out):
    d = w_in.shape[0]
    offs = np.cumsum((0,) + IN_SIZES)
    seg = [w_in[:, offs[i]:offs[i + 1]] for i in range(len(IN_SIZES))]
    misc = jnp.zeros((d, SLOT), F32).at[:, 0:4 * M_HEADS].set(seg[4]).at[:, MISC_KR:MISC_KR + L_ROPE].set(seg[10])
    w_in_p = jnp.concatenate(
        [_pad_heads(seg[0], M_HEADS, M_DH), _pad_heads(seg[1], M_HEADS, M_DH),
         _pad_heads(seg[2], M_HEADS, M_DH), _pad_heads(seg[3], M_HEADS, M_DH), misc,
         _pad_heads(seg[5], A_HEADS, A_DH), _pad_heads(seg[6], A_KV, A_DH), _pad_heads(seg[7], A_KV, A_DH),
         seg[8], seg[9]], axis=1).astype(BF16)
    conv_w = jnp.concatenate([_pad_heads(m_conv[:, :M_HEADS * M_DH], M_HEADS, M_DH),
                              _pad_heads(m_conv[:, M_HEADS * M_DH:], M_HEADS, M_DH)], axis=1)
    conv_scale = jnp.concatenate([jnp.ones((1, M_HEADS * SLOT), F32),
                                  jnp.full((1, M_HEADS * SLOT), M_DH ** -0.5, F32)], axis=1)
    gate_b = jnp.zeros((1, SLOT), F32).at[0, 0:4 * M_HEADS].set(m_gate_b)
    pad1 = lambda g: jnp.pad(g, (0, SLOT - g.shape[0]))
    an = jnp.stack([pad1(a_qnorm), pad1(a_knorm)])
    ln = jnp.stack([pad1(l_qnorm), pad1(l_knorm)])
    wuq_p = _pad_heads(l_wuq, L_HEADS, L_QK).astype(BF16)
    kv = l_wukv.reshape(L_KVRANK, L_HEADS, L_NOPE + L_DV)
    wukv_p = jnp.concatenate(
        [_pad_heads(kv[:, :, :L_NOPE].reshape(L_KVRANK, -1), L_HEADS, L_NOPE),
         _pad_heads(kv[:, :, L_NOPE:].reshape(L_KVRANK, -1), L_HEADS, L_DV)], axis=1).astype(BF16)
    nm, na = M_HEADS * M_DH, A_HEADS * A_DH
    w_out_p = jnp.concatenate(
        [_pad_heads(w_out[:nm], M_HEADS, M_DH, axis=0), _pad_heads(w_out[nm:nm + na], A_HEADS, A_DH, axis=0),
         _pad_heads(w_out[nm + na:], L_HEADS, L_DV, axis=0)], axis=0).astype(BF16)
    m_norm_p = _pad_heads(m_norm[None, :], M_HEADS, M_DH)
    return dict(w_in_p=w_in_p, conv_w=conv_w, conv_scale=conv_scale, gate_b=gate_b, an=an, ln=ln,
                cqn=l_cq_norm[None, :], ckvn=l_ckv_norm[None, :], wuq_p=wuq_p, wukv_p=wukv_p,
                w_out_p=w_out_p, m_norm_p=m_norm_p)


def kernel(x, c, ctx, c_ctx, mod_w, mod_b, norm_mix, norm_ffn, w_in, m_conv, m_gate_b, m_norm, a_qnorm, a_knorm,
           l_cq_norm, l_ckv_norm, l_wuq, l_wukv, l_qnorm, l_knorm, w_out, ffn_wg, ffn_wu, ffn_wd, moe_router,
           moe_wg, moe_wu, moe_wd):
    b, seq, d = x.shape
    ctx_len = ctx.shape[1]
    depth = mod_w.shape[0]
    assert ctx_len % TB == 0 and ctx_len % TQ == 0 and ctx_len == ML and seq % TK == 0 and seq % GRID_W == 0
    xs = jnp.concatenate([ctx, x], axis=1)
    mod_rows = 16
    cc = jnp.zeros((mod_rows, d), F32).at[:b].set(c).at[b].set(c_ctx)
    mod_all = _mod_table(cc, mod_w, mod_b)
    rope_tab = _rope_tables(seq, ctx_len)
    for i in range(depth):
        need_ctx = i < depth - 1
        lat = mod_all[i, :b].reshape(b, 1, 6, d)
        cm = jnp.broadcast_to(mod_all[i, b].reshape(1, 1, 6, d), (b, 1, 6, d))
        modtab = jnp.concatenate([cm, lat], axis=1)
        p = _layer_params(w_in[i], m_conv[i], m_gate_b[i], m_norm[i], a_qnorm[i], a_knorm[i], l_cq_norm[i],
                          l_ckv_norm[i], l_wuq[i], l_wukv[i], l_qnorm[i], l_knorm[i], w_out[i])
        mqk, mo, misc, q_all, k_all, mv_t, v_all = _inproj(
            xs, modtab, norm_mix[i][None, :], p["w_in_p"], rope_tab, p["an"], p["ln"], p["cqn"], p["ckvn"],
            p["wuq_p"], p["wukv_p"], ctx_len)
        q_t, k_c = _conv(mqk, p["conv_w"], p["conv_scale"], ctx_len)
        hf, hb = _mlstm(q_t, k_c, mv_t, misc, p["gate_b"])
        gqa_kv = tuple(h // (A_HEADS // A_KV) for h in range(A_HEADS))
        o_gqa = _attention(q_all, k_all, v_all, ctx_len, need_ctx, 0, KV_GQA, A_KV, gqa_kv)
        o_mla = _attention(q_all, k_all, v_all, ctx_len, need_ctx, Q_MLA, 0, L_HEADS, tuple(range(L_HEADS)))
        xs = _mixout(xs, modtab, hf, hb, mo, o_gqa, o_mla, p["m_norm_p"], p["w_out_p"], ctx_len, not need_ctx)
        ffn_ctx = ctx_len if need_ctx else 0
        j = i // 2
        if i % 2 == 0:
            xs = _ffn(xs, modtab, norm_ffn[i][None, :], ffn_wg[j].astype(BF16), ffn_wu[j].astype(BF16),
                      ffn_wd[j].astype(BF16), ffn_ctx)
        else:
            router_p = jnp.pad(moe_router[j], ((0, 0), (0, LANE - N_EXPERTS)))
            xs = _moe(xs, modtab, norm_ffn[i][None, :], router_p, moe_wg[j].astype(BF16),
                      moe_wu[j].astype(BF16), moe_wd[j].astype(BF16), ffn_ctx)
    return xs
```

```python
import functools
import math

import numpy as np
import jax
import jax.numpy as jnp
from jax import lax
from jax.experimental import pallas as pl
from jax.experimental.pallas import tpu as pltpu

F32 = jnp.float32
BF16 = jnp.bfloat16

GRID_W = 64
EPS = 1e-6
ROPE_THETA = 10000.0
M_HEADS, M_DH = 4, 96
A_HEADS, A_KV, A_DH = 6, 2, 64
L_HEADS, L_NOPE, L_ROPE, L_DV = 4, 64, 32, 64
L_QK = L_NOPE + L_ROPE
L_QRANK, L_KVRANK = 256, 128
N_EXPERTS, TOP_K = 8, 2
IN_SIZES = (384, 384, 384, 384, 16, 384, 128, 128, 256, 128, 32)

LANE = 128
SLOT = LANE
N_KVHEADS = A_KV + L_HEADS
Q_MLA = 8
Q_SLOTS = Q_MLA + L_HEADS
KV_GQA = L_HEADS
VMEM_LIMIT = 56 * 1024 * 1024

O_MQK, O_MV, O_MO, O_MISC = 0, 1024, 1536, 2048
O_QA, O_KA, O_VA, O_CQ, O_CKV, IN_PAD = 2176, 2944, 3200, 3456, 3712, 3840
MISC_KR = 64
DEN_LANE = M_DH
SUM_LANE = A_DH
LOG2E = math.log2(math.e)
assert A_DH == L_DV and SUM_LANE < SLOT

TB = 256
TQ = 256
TK = 512
TM_FFN = 1088
TF = 512
ML = 256
RT = 288
GT = RT
ST = 3 * LANE
TF_MOE = 896


def _cparams(sem):
    return pltpu.CompilerParams(dimension_semantics=sem, vmem_limit_bytes=VMEM_LIMIT)


def _sigmoid(x):
    return 1.0 / (1.0 + jnp.exp(-x))


def _silu(x):
    return x * _sigmoid(x)


def _lane_iota(shape):
    return lax.broadcasted_iota(jnp.int32, shape, len(shape) - 1)


def _row_iota(shape):
    return lax.broadcasted_iota(jnp.int32, shape, 0)


def _modulated(x, gain, mod_ref, t0, ctx_len, k_shift, k_scale):
    tm = x.shape[0]
    is_ctx = (t0 + _row_iota((tm, 1))) < ctx_len
    shift = jnp.where(is_ctx, mod_ref[0, k_shift:k_shift + 1, :], mod_ref[1, k_shift:k_shift + 1, :])
    scale = jnp.where(is_ctx, mod_ref[0, k_scale:k_scale + 1, :], mod_ref[1, k_scale:k_scale + 1, :])
    y = x * lax.rsqrt(jnp.mean(x * x, axis=-1, keepdims=True) + EPS) * gain
    return y * (1.0 + scale) + shift


def _gate_rows(mod_ref, t0, tm, ctx_len, k_gate):
    is_ctx = (t0 + _row_iota((tm, 1))) < ctx_len
    return jnp.where(is_ctx, mod_ref[0, k_gate:k_gate + 1, :], mod_ref[1, k_gate:k_gate + 1, :])


def _mod_kernel(c_ref, w_ref, b_ref, o_ref):
    s = _silu(c_ref[...]).astype(BF16)
    o_ref[...] = jnp.dot(s, w_ref[...].astype(BF16), preferred_element_type=F32) + b_ref[...]


def _mod_table(cc, mod_w, mod_b):
    depth, d, n = mod_w.shape
    rows = cc.shape[0]
    return pl.pallas_call(
        _mod_kernel,
        grid=(depth, n // d),
        in_specs=[pl.BlockSpec((rows, d), lambda l, j: (0, 0)),
                  pl.BlockSpec((None, d, d), lambda l, j: (l, 0, j)),
                  pl.BlockSpec((None, 1, d), lambda l, j: (l, 0, j))],
        out_specs=pl.BlockSpec((None, rows, d), lambda l, j: (l, 0, j)),
        out_shape=jax.ShapeDtypeStruct((depth, rows, n), F32),
        compiler_params=_cparams(("arbitrary", "arbitrary")),
        name="mod_table",
    )(cc, mod_w, mod_b.reshape(depth, 1, n))


def _slot_rms(x, lo, hi):
    lane = _lane_iota(x.shape)
    sq = jnp.where((lane >= lo) & (lane < hi), x * x, 0.0)
    return lax.rsqrt(jnp.sum(sq, axis=-1, keepdims=True) * (1.0 / (hi - lo)) + EPS)


def _rope(x, cos, sin_m, sin_p, quarter):
    return (x * cos + pltpu.roll(x, LANE - quarter, 1) * sin_m + pltpu.roll(x, quarter, 1) * sin_p)


def _inproj_kernel(ctx_len, x_ref, mod_ref, g_ref, w_ref, rope_ref, an_ref, ln_ref, cqn_ref, ckvn_ref,
                   wuq_ref, wukv_ref, mqk_ref, mo_ref, misc_ref, q_ref, k_ref, mvt_ref, vt_ref):
    tm = x_ref.shape[0]
    t0 = pl.program_id(1) * tm
    h = _modulated(x_ref[...], g_ref[...], mod_ref, t0, ctx_len, 0, 1).astype(BF16)

    def proj(a, b):
        return jnp.dot(h, w_ref[:, a:b], preferred_element_type=F32)

    mqk_ref[...] = proj(O_MQK, O_MV).astype(BF16)
    mv = proj(O_MV, O_MO)
    mvt_ref[...] = jnp.where(_lane_iota(mv.shape) % SLOT == DEN_LANE, 1.0, mv).T.astype(BF16)
    mo_ref[...] = proj(O_MO, O_MISC).astype(BF16)
    misc = proj(O_MISC, O_QA)
    misc_ref[...] = misc

    cos_a, sinm_a, sinp_a = rope_ref[0], rope_ref[1], rope_ref[2]
    cos_l, sinm_l, sinp_l = rope_ref[3], rope_ref[4], rope_ref[5]
    a_scale = A_DH ** -0.5 * LOG2E
    l_scale = L_QK ** -0.5 * LOG2E

    def with_ones_t(v):
        return jnp.where(_lane_iota(v.shape) % SLOT == SUM_LANE, 1.0, v).T.astype(BF16)

    pa = proj(O_QA, O_CQ)
    gq, gk = an_ref[0:1, :], an_ref[1:2, :]
    for i in range(A_HEADS):
        x = pa[:, i * SLOT:(i + 1) * SLOT]
        x = x * _slot_rms(x, 0, A_DH) * gq
        q_ref[:, i * SLOT:(i + 1) * SLOT] = (_rope(x, cos_a, sinm_a, sinp_a, A_DH // 4) * a_scale).astype(BF16)
    for i in range(A_KV):
        x = pa[:, (A_HEADS + i) * SLOT:(A_HEADS + i + 1) * SLOT]
        x = x * _slot_rms(x, 0, A_DH) * gk
        k_ref[:, (KV_GQA + i) * SLOT:(KV_GQA + i + 1) * SLOT] = (
            _rope(x, cos_a, sinm_a, sinp_a, A_DH // 4).astype(BF16))
    vt_ref[KV_GQA * SLOT:, :] = with_ones_t(pa[:, (A_HEADS + A_KV) * SLOT:(A_HEADS + 2 * A_KV) * SLOT])
    q_ref[:, A_HEADS * SLOT:Q_MLA * SLOT] = jnp.zeros((tm, (Q_MLA - A_HEADS) * SLOT), BF16)

    pc = proj(O_CQ, IN_PAD)
    cq = pc[:, 0:L_QRANK]
    cq = (cq * lax.rsqrt(jnp.mean(cq * cq, axis=-1, keepdims=True) + EPS) * cqn_ref[...]).astype(BF16)
    ckv = pc[:, L_QRANK:L_QRANK + L_KVRANK]
    ckv = (ckv * lax.rsqrt(jnp.mean(ckv * ckv, axis=-1, keepdims=True) + EPS) * ckvn_ref[...]).astype(BF16)
    ql = jnp.dot(cq, wuq_ref[...], preferred_element_type=F32)
    kvl = jnp.dot(ckv, wukv_ref[...], preferred_element_type=F32)
    gq_l, gk_l = ln_ref[0:1, :], ln_ref[1:2, :]
    lane = _lane_iota((tm, SLOT))
    kr = jnp.where((lane >= MISC_KR) & (lane < MISC_KR + L_ROPE), misc, 0.0)
    kr = kr * _slot_rms(kr, MISC_KR, MISC_KR + L_ROPE) * gk_l
    kr = _rope(kr, cos_l, sinm_l, sinp_l, L_ROPE // 4)
    for i in range(L_HEADS):
        x = ql[:, i * SLOT:(i + 1) * SLOT]
        inv = jnp.where(lane < L_NOPE, _slot_rms(x, 0, L_NOPE), _slot_rms(x, L_NOPE, L_QK))
        x = x * inv * gq_l
        q_ref[:, (Q_MLA + i) * SLOT:(Q_MLA + i + 1) * SLOT] = (
            _rope(x, cos_l, sinm_l, sinp_l, L_ROPE // 4) * l_scale).astype(BF16)
        kn = kvl[:, i * SLOT:(i + 1) * SLOT]
        kn = kn * _slot_rms(kn, 0, L_NOPE) * gk_l
        k_ref[:, i * SLOT:(i + 1) * SLOT] = (kn + kr).astype(BF16)
    vt_ref[0:KV_GQA * SLOT, :] = with_ones_t(kvl[:, L_HEADS * SLOT:])


def _inproj(xs, modtab, gain, w_in_p, rope_tab, an, ln, cqn, ckvn, wuq_p, wukv_p, ctx_len):
    b, s, d = xs.shape
    grid = (b, s // TB)
    tok = lambda n: pl.BlockSpec((None, TB, n), lambda i, j: (i, j, 0))
    full = lambda a: pl.BlockSpec(a.shape, lambda i, j: (0,) * a.ndim)
    out_widths = (1024, 512, SLOT, Q_SLOTS * SLOT, N_KVHEADS * SLOT)
    out_dtypes = (BF16, BF16, F32, BF16, BF16)
    out_specs = [tok(n) for n in out_widths]
    out_shape = [jax.ShapeDtypeStruct((b, s, n), dt) for n, dt in zip(out_widths, out_dtypes)]
    for n in (M_HEADS * SLOT, N_KVHEADS * SLOT):
        out_specs.append(pl.BlockSpec((None, n, TB), lambda i, j: (i, 0, j)))
        out_shape.append(jax.ShapeDtypeStruct((b, n, s), BF16))
    return pl.pallas_call(
        functools.partial(_inproj_kernel, ctx_len),
        grid=grid,
        in_specs=[tok(d),
                  pl.BlockSpec((None, 2, 6, d), lambda i, j: (i, 0, 0, 0)),
                  full(gain), full(w_in_p),
                  pl.BlockSpec((6, TB, SLOT), lambda i, j: (0, j, 0)),
                  full(an), full(ln), full(cqn), full(ckvn), full(wuq_p), full(wukv_p)],
        out_specs=out_specs,
        out_shape=out_shape,
        compiler_params=_cparams(("parallel", "parallel")),
        name="inproj",
    )(xs, modtab, gain, w_in_p, rope_tab, an, ln, cqn, ckvn, wuq_p, wukv_p)


def _conv_kernel(ctx_len, s_len, x_ref, prev_ref, next_ref, w_ref, sc_ref, qt_ref, k_ref):
    tm = x_ref.shape[0]
    t0 = pl.program_id(1) * tm
    x = x_ref[...].astype(F32)
    row = _row_iota((tm, 1))
    has_prev = jnp.logical_and(t0 != 0, t0 != ctx_len)
    has_next = jnp.logical_and(t0 + tm != ctx_len, t0 + tm != s_len)
    hp = jnp.where(has_prev, prev_ref[15:16, :].astype(F32), 0.0)
    hn = jnp.where(has_next, next_ref[0:1, :].astype(F32), 0.0)
    xp = jnp.where(row == 0, hp, pltpu.roll(x, 1, 0))
    xn = jnp.where(row == tm - 1, hn, pltpu.roll(x, tm - 1, 0))
    y = xp * w_ref[0:1, :] + x * w_ref[1:2, :] + xn * w_ref[2:3, :]
    y = _silu(y) * sc_ref[...]
    half = y.shape[1] // 2
    qt_ref[...] = y[:, :half].T.astype(qt_ref.dtype)
    k_ref[...] = y[:, half:].astype(k_ref.dtype)


def _conv(mqk, conv_w, conv_scale, ctx_len):
    b, s, n = mqk.shape
    hb = TB // 16
    last = s // 16 - 1
    return pl.pallas_call(
        functools.partial(_conv_kernel, ctx_len, s),
        grid=(b, s // TB),
        in_specs=[pl.BlockSpec((None, TB, n), lambda i, j: (i, j, 0)),
                  pl.BlockSpec((None, 16, n), lambda i, j: (i, jnp.maximum(j * hb - 1, 0), 0)),
                  pl.BlockSpec((None, 16, n), lambda i, j: (i, jnp.minimum((j + 1) * hb, last), 0)),
                  pl.BlockSpec((3, n), lambda i, j: (0, 0)),
                  pl.BlockSpec((1, n), lambda i, j: (0, 0))],
        out_specs=[pl.BlockSpec((None, n // 2, TB), lambda i, j: (i, 0, j)),
                   pl.BlockSpec((None, TB, n // 2), lambda i, j: (i, j, 0))],
        out_shape=[jax.ShapeDtypeStruct((b, n // 2, s), BF16), jax.ShapeDtypeStruct((b, s, n // 2), BF16)],
        compiler_params=_cparams(("parallel", "parallel")),
        name="mlstm_conv",
    )(mqk, mqk, mqk, conv_w, conv_scale)


def _log_sigmoid(x):
    return jnp.minimum(x, 0.0) - jnp.log(1.0 + jnp.exp(-jnp.abs(x)))


def _mlstm_kernel(n_chunks, qt_ref, k_ref, vt_ref, misc_ref, gb_ref, hf_ref, hb_ref, c_ref, m_ref, gt_sc, bt_sc):
    hp = pl.program_id(1)
    c_ref[...] = jnp.zeros_like(c_ref)
    m_ref[...] = jnp.zeros_like(m_ref)
    r = _row_iota((ML, ML))
    cidx = _lane_iota((ML, ML))
    tri_f32 = (r >= cidx).astype(F32)
    tri_b32 = (r <= cidx).astype(F32)
    lane = _lane_iota((ML, SLOT))
    lane_t = _lane_iota((1, ML))

    def chunk(t0, d, tri32, out_ref):
        g = misc_ref[pl.ds(t0, ML), :] + gb_ref[...]
        logf = _log_sigmoid(g)
        bcum = jnp.dot(tri32, logf, preferred_element_type=F32, precision=lax.Precision.HIGHEST)
        gt_sc[d] = g.T
        bt_sc[d] = bcum.T
        ib = g - pltpu.roll(bcum, SLOT - M_HEADS, 1)
        allowed = (r <= cidx) if d == 0 else (r >= cidx)
        for hh in range(2):
            li = d * 2 * M_HEADS + hp * 2 + hh
            ib_col = jnp.sum(jnp.where(lane == li, ib, 0.0), axis=-1, keepdims=True)
            i_row = gt_sc[d, pl.ds(li, 1), :]
            b_row = bt_sc[d, pl.ds(li + M_HEADS, 1), :]
            sidx = d * 2 + hh
            ct_st = c_ref[sidx]
            m_s = m_ref[sidx]
            k = k_ref[pl.ds(t0, ML), hh * SLOT:(hh + 1) * SLOT]
            qt = qt_ref[hh * SLOT:(hh + 1) * SLOT, pl.ds(t0, ML)]
            vt = vt_ref[hh * SLOT:(hh + 1) * SLOT, pl.ds(t0, ML)]
            dmat = jnp.where(allowed, b_row + ib_col, -jnp.inf)
            m_inter = b_row + m_s
            m_t = jnp.maximum(m_inter, jnp.max(dmat, axis=0, keepdims=True))
            kq = jnp.dot(k, qt, preferred_element_type=F32)
            w = (jnp.exp(dmat - m_t) * kq).astype(BF16)
            a_inter = jnp.exp(m_inter - m_t)
            num = a_inter * jnp.dot(ct_st.astype(BF16), qt, preferred_element_type=F32) + jnp.dot(
                vt, w, preferred_element_type=F32)
            den = num[DEN_LANE:DEN_LANE + 1, :]
            h_out = num / jnp.maximum(jnp.abs(den), jnp.exp(-m_t))
            out_ref[pl.ds(t0, ML), hh * SLOT:(hh + 1) * SLOT] = h_out.T.astype(out_ref.dtype)
            last = ML - 1 if d == 0 else 0
            total = jnp.sum(jnp.where(lane_t == last, b_row, 0.0), axis=-1, keepdims=True)
            gg = total - b_row + i_row
            m_new = jnp.maximum(total + m_s, jnp.max(gg, axis=-1, keepdims=True))
            decay = jnp.exp(total + m_s - m_new)[:, 0:1]
            wk = jnp.exp(gg - m_new)
            vw = (vt.astype(F32) * wk).astype(BF16)
            c_ref[sidx] = decay * ct_st + jnp.dot(vw, k, preferred_element_type=F32)
            m_ref[sidx] = m_new

    def step(n, carry):
        tf0 = pl.multiple_of(n * ML, ML)
        tb0 = pl.multiple_of(jnp.where(n == 0, 0, n_chunks - n) * ML, ML)
        chunk(tf0, 0, tri_f32, hf_ref)
        chunk(tb0, 1, tri_b32, hb_ref)
        return carry

    lax.fori_loop(0, n_chunks, step, 0)


def _mlstm(q_t, k_c, mv_t, misc, gate_b):
    b, s, _ = k_c.shape
    n_chunks = s // ML
    pair = 2 * SLOT
    spec = pl.BlockSpec((None, s, pair), lambda i, j: (i, 0, j))
    spec_t = pl.BlockSpec((None, pair, s), lambda i, j: (i, j, 0))
    return pl.pallas_call(
        functools.partial(_mlstm_kernel, n_chunks),
        grid=(b, M_HEADS // 2),
        in_specs=[spec_t, spec, spec_t,
                  pl.BlockSpec((None, s, SLOT), lambda i, j: (i, 0, 0)),
                  pl.BlockSpec((1, SLOT), lambda i, j: (0, 0))],
        out_specs=[spec, spec],
        out_shape=[jax.ShapeDtypeStruct((b, s, M_HEADS * SLOT), BF16)] * 2,
        scratch_shapes=[pltpu.VMEM((4, SLOT, SLOT), F32), pltpu.VMEM((4, 1, ML), F32),
                        pltpu.VMEM((2, SLOT, ML), F32), pltpu.VMEM((2, SLOT, ML), F32)],
        compiler_params=_cparams(("parallel", "parallel")),
        name="mlstm_scan",
    )(q_t, k_c, mv_t, misc, gate_b)


def _attn_kernel(ctx_len, q_off, kv_of, q_ref, k_ref, vt_ref, o_ref, st_sc, pt_sc, acc_sc):
    s_len = k_ref.shape[0]
    n_heads = len(kv_of)
    qs = [q_ref[:, g * SLOT:(g + 1) * SLOT] for g in range(n_heads)]

    def kv_cols(g):
        return slice(kv_of[g] * SLOT, (kv_of[g] + 1) * SLOT)

    def chunk(carry, rows):
        out = []
        for g in range(n_heads):
            m, acc = carry[g]
            k = k_ref[rows, kv_cols(g)]
            vt = vt_ref[kv_cols(g), rows]
            st = lax.dot_general(k, qs[g], (((1,), (1,)), ((), ())), preferred_element_type=F32)
            m_new = jnp.maximum(m, jnp.max(st, axis=0, keepdims=True))
            alpha = jnp.exp2(m - m_new)
            pt = jnp.exp2(st - m_new).astype(BF16)
            out.append((m_new, alpha * acc + jnp.dot(vt, pt, preferred_element_type=F32)))
        return tuple(out)

    def finish(carry):
        for g in range(n_heads):
            acc = carry[g][1]
            o = acc / acc[SUM_LANE:SUM_LANE + 1, :]
            o_ref[:, g * SLOT:(g + 1) * SLOT] = o.T.astype(o_ref.dtype)

    init = tuple((jnp.full((1, TQ), -jnp.inf, F32), jnp.zeros((SLOT, TQ), F32)) for _ in range(n_heads))
    is_ctx = (pl.program_id(2) + q_off) * TQ < ctx_len

    @pl.when(is_ctx)
    def _():
        finish(chunk(init, pl.ds(0, ctx_len)))

    n_lat = (s_len - ctx_len) // TK
    n0 = ctx_len + TK
    assert n_lat >= 4 and n_lat % 2 == 0 and st_sc.shape[1] == n0

    def rows_of(c):
        if isinstance(c, int):
            return (pl.ds(0, n0), n0) if c == 0 else (pl.ds(ctx_len + c * TK, TK), TK)
        return pl.ds(pl.multiple_of(ctx_len + c * TK, math.gcd(ctx_len, TK)), TK), TK

    def scores(c, par):
        rows, n = rows_of(c)
        for g in range(n_heads):
            st_sc[par * n_heads + g, 0:n, :] = lax.dot_general(
                k_ref[rows, kv_cols(g)], qs[g], (((1,), (1,)), ((), ())), preferred_element_type=F32)

    def softmax(par, m, n=TK):
        ms, alphas = [], []
        for g in range(n_heads):
            st = st_sc[par * n_heads + g, 0:n, :]
            m_new = jnp.maximum(m[g], jnp.max(st, axis=0, keepdims=True))
            pt_sc[par * n_heads + g, 0:n, :] = jnp.exp2(st - m_new).astype(BF16)
            ms.append(m_new)
            alphas.append(jnp.exp2(m[g] - m_new))
        return tuple(ms), tuple(alphas)

    def values(c, par, alpha):
        rows, n = rows_of(c)
        for g in range(n_heads):
            acc_sc[g] = alpha[g] * acc_sc[g] + jnp.dot(vt_ref[kv_cols(g), rows], pt_sc[par * n_heads + g, 0:n, :],
                                                       preferred_element_type=F32)

    def stage(c, par, m, alpha):
        scores(c + 1, 1 - par)
        m_new, alpha_new = softmax(par, m)
        values(c - 1, 1 - par, alpha)
        return m_new, alpha_new

    @pl.when(jnp.logical_not(is_ctx))
    def _():
        for g in range(n_heads):
            acc_sc[g] = init[g][1]
        scores(0, 0)
        m, alpha = softmax(0, tuple(c[0] for c in init), n0)
        scores(1, 1)
        m, alpha = stage(1, 1, m, alpha)

        def body(t, carry):
            c = 2 * t + 2
            return stage(c + 1, 1, *stage(c, 0, *carry))

        m, alpha = lax.fori_loop(0, (n_lat - 4) // 2, body, (m, alpha))
        m, alpha = stage(n_lat - 2, 0, m, alpha)
        values(n_lat - 2, 0, alpha)
        m, alpha = softmax(1, m)
        values(n_lat - 1, 1, alpha)
        finish(tuple((m[g], acc_sc[g]) for g in range(n_heads)))


def _attention(q_all, k_all, vt_all, ctx_len, need_ctx, q_slot0, kv_slot0, n_kv, kv_of):
    b, s, _ = q_all.shape
    q_off = 0 if need_ctx else ctx_len // TQ
    nq = s // TQ - q_off
    n_heads = len(kv_of)
    qw = n_heads * SLOT
    kw = n_kv * SLOT
    assert (q_slot0 * SLOT) % qw == 0 and (kv_slot0 * SLOT) % kw == 0
    qb0, kb0 = q_slot0 * SLOT // qw, kv_slot0 * SLOT // kw
    return pl.pallas_call(
        functools.partial(_attn_kernel, ctx_len, q_off, kv_of),
        grid=(b, 1, nq),
        in_specs=[pl.BlockSpec((None, TQ, qw), lambda i, g, j: (i, j + q_off, qb0)),
                  pl.BlockSpec((None, s, kw), lambda i, g, j: (i, 0, kb0)),
                  pl.BlockSpec((None, kw, s), lambda i, g, j: (i, kb0, 0))],
        out_specs=pl.BlockSpec((None, TQ, qw), lambda i, g, j: (i, j + q_off, 0)),
        out_shape=jax.ShapeDtypeStruct((b, s, qw), BF16),
        scratch_shapes=[pltpu.VMEM((2 * n_heads, ctx_len + TK, TQ), F32),
                        pltpu.VMEM((2 * n_heads, ctx_len + TK, TQ), BF16),
                        pltpu.VMEM((n_heads, SLOT, TQ), F32)],
        compiler_params=_cparams(("parallel", "parallel", "parallel")),
        name="attention",
    )(q_all, k_all, vt_all)


def _mixout_kernel(ctx_len, blk_off, x_ref, mod_ref, hf_ref, hb_ref, mo_ref, oa_ref, ol_ref, mn_ref, w_ref, o_ref):
    tm = x_ref.shape[0]
    t0 = (pl.program_id(1) + blk_off) * tm
    hm = hf_ref[...].astype(F32) + hb_ref[...].astype(F32)
    gate = _sigmoid(mo_ref[...].astype(F32))
    parts = []
    for i in range(M_HEADS):
        x = hm[:, i * SLOT:(i + 1) * SLOT]
        inv = lax.rsqrt(jnp.sum(x * x, axis=-1, keepdims=True) * (1.0 / M_DH) + EPS)
        parts.append(x * inv)
    hn = jnp.concatenate(parts, axis=-1) * mn_ref[...] * gate
    nm = M_HEADS * SLOT
    o = jnp.dot(hn.astype(BF16), w_ref[0:nm, :], preferred_element_type=F32)
    na = nm + A_HEADS * SLOT
    o = o + jnp.dot(oa_ref[...], w_ref[nm:na, :], preferred_element_type=F32)
    o = o + jnp.dot(ol_ref[...], w_ref[na:, :], preferred_element_type=F32)
    o_ref[...] = x_ref[...] + _gate_rows(mod_ref, t0, tm, ctx_len, 2) * o


def _mixout(xs, modtab, hf, hb, mo, o_gqa, o_mla, m_norm_p, w_out_p, ctx_len, lat_only):
    b, s, d = xs.shape
    blk_off = ctx_len // TB if lat_only else 0
    tok = lambda n: pl.BlockSpec((None, TB, n), lambda i, j: (i, j + blk_off, 0))
    full = lambda a: pl.BlockSpec(a.shape, lambda i, j: (0,) * a.ndim)
    return pl.pallas_call(
        functools.partial(_mixout_kernel, ctx_len, blk_off),
        grid=(b, s // TB - blk_off),
        in_specs=[tok(d),
                  pl.BlockSpec((None, 2, 6, d), lambda i, j: (i, 0, 0, 0)),
                  tok(M_HEADS * SLOT), tok(M_HEADS * SLOT), tok(M_HEADS * SLOT), tok(A_HEADS * SLOT),
                  tok(L_HEADS * SLOT), full(m_norm_p), full(w_out_p)],
        out_specs=pl.BlockSpec((None, TB, d), lambda i, j: (i, j, 0)),
        out_shape=jax.ShapeDtypeStruct((b, s - blk_off * TB, d), F32),
        input_output_aliases={} if lat_only else {0: 0},
        compiler_params=_cparams(("parallel", "parallel")),
        name="mix_out",
    )(xs, modtab, hf, hb, mo, o_gqa, o_mla, m_norm_p, w_out_p)


def _ffn_kernel(ctx_len, x_ref, mod_ref, g_ref, wg_ref, wu_ref, wd_ref, o_ref, h_sc, acc_sc):
    tm = x_ref.shape[0]
    t0 = pl.program_id(1) * tm
    f = pl.program_id(2)

    @pl.when(f == 0)
    def _():
        h_sc[...] = _modulated(x_ref[...], g_ref[...], mod_ref, t0, ctx_len, 3, 4).astype(BF16)
        acc_sc[...] = jnp.zeros_like(acc_sc)

    h = h_sc[...]
    a = jnp.dot(h, wg_ref[...], preferred_element_type=F32)
    u = jnp.dot(h, wu_ref[...], preferred_element_type=F32)
    acc_sc[...] += jnp.dot((_silu(a) * u).astype(BF16), wd_ref[...], preferred_element_type=F32)

    @pl.when(f == pl.num_programs(2) - 1)
    def _():
        o_ref[...] = x_ref[...] + _gate_rows(mod_ref, t0, tm, ctx_len, 5) * acc_sc[...]


def _ffn(xs, modtab, gain, wg, wu, wd, ctx_len):
    b, s, d = xs.shape
    tm = s // 4
    nf = wg.shape[1] // TF
    return pl.pallas_call(
        functools.partial(_ffn_kernel, ctx_len),
        grid=(b, s // tm, nf),
        in_specs=[pl.BlockSpec((None, tm, d), lambda i, j, f: (i, j, 0)),
                  pl.BlockSpec((None, 2, 6, d), lambda i, j, f: (i, 0, 0, 0)),
                  pl.BlockSpec((1, d), lambda i, j, f: (0, 0)),
                  pl.BlockSpec((d, TF), lambda i, j, f: (0, f)),
                  pl.BlockSpec((d, TF), lambda i, j, f: (0, f)),
                  pl.BlockSpec((TF, d), lambda i, j, f: (f, 0))],
        out_specs=pl.BlockSpec((None, tm, d), lambda i, j, f: (i, j, 0)),
        out_shape=jax.ShapeDtypeStruct((b, s, d), F32),
        scratch_shapes=[pltpu.VMEM((tm, d), BF16), pltpu.VMEM((tm, d), F32)],
        input_output_aliases={0: 0},
        compiler_params=_cparams(("parallel", "parallel", "arbitrary")),
        name="ffn_dense",
    )(xs, modtab, gain, wg, wu, wd)


def _top2_combine(logits):
    lane = _lane_iota(logits.shape)
    lane_f = lane.astype(F32)
    lg = jnp.where(lane < N_EXPERTS, logits, -jnp.inf)
    v1 = jnp.max(lg, axis=-1, keepdims=True)
    i1 = jnp.min(jnp.where(lg == v1, lane_f, float(LANE)), axis=-1, keepdims=True)
    rest = jnp.where(lane_f == i1, -jnp.inf, lg)
    v2 = jnp.max(rest, axis=-1, keepdims=True)
    i2 = jnp.min(jnp.where(rest == v2, lane_f, float(LANE)), axis=-1, keepdims=True)
    e2 = jnp.exp(v2 - v1)
    w1 = 1.0 / (1.0 + e2)
    w2 = e2 / (1.0 + e2)
    sel = jnp.where((lane_f == i1) | (lane_f == i2), 1.0, 0.0)
    return jnp.where(lane_f == i1, w1, 0.0) + jnp.where(lane_f == i2, w2, 0.0), sel


def _moe_kernel(ctx_len, x_ref, mod_ref, g_ref, r_ref, tri_ref, wg_ref, wu_ref, wd_ref, o_ref,
                h_sc, comb_sc, rank_sc, rankt_sc, rankc_sc, wc_sc, xg_sc, acc_sc, y_sc, nt_sc):
    tm = x_ref.shape[0]
    t0 = pl.program_id(1) * tm
    e = pl.program_id(2)
    f = pl.program_id(3)

    @pl.when(jnp.logical_and(e == 0, f == 0))
    def _():
        h = _modulated(x_ref[...], g_ref[...], mod_ref, t0, ctx_len, 3, 4)
        logits = jnp.dot(h, r_ref[...], preferred_element_type=F32, precision=lax.Precision.HIGHEST)
        comb, sel = _top2_combine(logits)
        rank =jnp.dot(tri_ref[...], sel.astype(BF16), preferred_element_type=F32)
        rank = jnp.where(sel > 0.0, rank, -1.0)
        comb_sc[...] = comb
        rank_sc[...] = rank
        pad = rankt_sc.shape[1] - tm
        h_sc[0:tm, :] = h.astype(BF16)
        if pad:
            rankt_sc[...] = jnp.concatenate([rank, jnp.full((pad, LANE), -1.0, F32)], axis=0).T
            h_sc[tm:, :] = jnp.zeros((pad, h_sc.shape[1]), BF16)
        else:
            rankt_sc[...] = rank.T
        y_sc[...] = jnp.zeros_like(y_sc)

    @pl.when(f == 0)
    def _():
        lane = _lane_iota((tm, LANE))
        rank_c = jnp.sum(jnp.where(lane == e, rank_sc[...], 0.0), axis=-1, keepdims=True)
        rankc_sc[...] = rank_c
        wc_sc[...] = jnp.sum(jnp.where(lane == e, comb_sc[...], 0.0), axis=-1, keepdims=True)
        n_rows = jnp.sum(jnp.where(rank_c >= 0.0, 1.0, 0.0)).astype(jnp.int32)
        n_rt = lax.div(n_rows + (RT - 1), RT)
        n_gt = lax.div(n_rt * RT + (GT - 1), GT)
        n_st = lax.div(n_rt * RT + (ST - 1), ST)
        nt_sc[0] = n_rt
        nt_sc[1] = n_st
        rank_r = rankt_sc[pl.ds(e, 1), :]

        def gather(i, c):
            r0 = pl.multiple_of(i * GT, 16)
            tgt = (r0 + _row_iota((GT, 1))).astype(F32)
            sel_t = jnp.where(rank_r == tgt, 1.0, 0.0).astype(BF16)
            xg_sc[pl.ds(r0, GT), :] = jnp.dot(sel_t, h_sc[...], preferred_element_type=F32).astype(BF16)
            return c

        def clear(i, c):
            acc_sc[pl.ds(pl.multiple_of(i * ST, ST), ST), :] = jnp.zeros((ST, acc_sc.shape[1]), F32)
            return c

        lax.fori_loop(0, n_gt, gather, 0)
        lax.fori_loop(0, n_st, clear, 0)

    def expert(i, c):
        r0 = pl.multiple_of(i * RT, 16)
        rows = xg_sc[pl.ds(r0, RT), :]
        a = jnp.dot(rows, wg_ref[...], preferred_element_type=F32)
        u = jnp.dot(rows, wu_ref[...], preferred_element_type=F32)
        acc_sc[pl.ds(r0, RT), :] += jnp.dot((_silu(a) * u).astype(BF16), wd_ref[...],
                                            preferred_element_type=F32)
        return c

    lax.fori_loop(0, nt_sc[0], expert, 0)

    @pl.when(f == pl.num_programs(3) - 1)
    def _():
        def scatter(i, c):
            r0 = pl.multiple_of(i * ST, ST)
            tgt = (r0 + _lane_iota((1, ST))).astype(F32)
            w_t = jnp.where(rankc_sc[...] == tgt, wc_sc[...], 0.0).astype(BF16)
            y_sc[...] += jnp.dot(w_t, acc_sc[pl.ds(r0, ST), :].astype(BF16), preferred_element_type=F32)
            return c

        lax.fori_loop(0, nt_sc[1], scatter, 0)

    @pl.when(jnp.logical_and(e == pl.num_programs(2) - 1, f == pl.num_programs(3) - 1))
    def _():
        o_ref[...] = x_ref[...] + _gate_rows(mod_ref, t0, tm, ctx_len, 5) * y_sc[...]


def _moe(xs, modtab, gain, router_p, wg, wu, wd, ctx_len):
    b, s, d = xs.shape
    tm = s // 4
    tp = -(-tm // LANE) * LANE
    r_max = -(-tm // RT) * RT
    tr = max(-(-r_max // GT) * GT, -(-r_max // ST) * ST)
    ne, _, dff = wg.shape
    tri = jnp.tril(jnp.ones((tm, tm), BF16), -1)
    TF = TF_MOE
    assert dff % TF == 0
    return pl.pallas_call(
        functools.partial(_moe_kernel, ctx_len),
        grid=(b, s // tm, ne, dff // TF),
        in_specs=[pl.BlockSpec((None, tm, d), lambda i, j, e, f: (i, j, 0)),
                  pl.BlockSpec((None, 2, 6, d), lambda i, j, e, f: (i, 0, 0, 0)),
                  pl.BlockSpec((1, d), lambda i, j, e, f: (0, 0)),
                  pl.BlockSpec((d, LANE), lambda i, j, e, f: (0, 0)),
                  pl.BlockSpec((tm, tm), lambda i, j, e, f: (0, 0), pipeline_mode=pl.Buffered(1)),
                  pl.BlockSpec((None, d, TF), lambda i, j, e, f: (e, 0, f)),
                  pl.BlockSpec((None, d, TF), lambda i, j, e, f: (e, 0, f)),
                  pl.BlockSpec((None, TF, d), lambda i, j, e, f: (e, f, 0))],
        out_specs=pl.BlockSpec((None, tm, d), lambda i, j, e, f: (i, j, 0)),
        out_shape=jax.ShapeDtypeStruct((b, s, d), F32),
        scratch_shapes=[pltpu.VMEM((tp, d), BF16),
                        pltpu.VMEM((tm, LANE), F32),
                        pltpu.VMEM((tm, LANE), F32),
                        pltpu.VMEM((LANE, tp), F32),
                        pltpu.VMEM((tm, 1), F32),
                        pltpu.VMEM((tm, 1), F32),
                        pltpu.VMEM((tr, d), BF16),
                        pltpu.VMEM((tr, d), F32),
                        pltpu.VMEM((tm, d), F32),
                        pltpu.SMEM((2,), jnp.int32)],
        input_output_aliases={0: 0},
        compiler_params=_cparams(("parallel", "parallel", "arbitrary", "arbitrary")),
        name="moe_top2",
    )(xs, modtab, gain, router_p, tri, wg, wu, wd)


def _pad_heads(w, n_heads, dh, axis=-1):
    axis = axis % w.ndim
    shp = w.shape[:axis] + (n_heads, dh) + w.shape[axis + 1:]
    pad = [(0, 0)] * (w.ndim + 1)
    pad[axis + 1] = (0, SLOT - dh)
    out = jnp.pad(w.reshape(shp), pad)
    return out.reshape(w.shape[:axis] + (n_heads * SLOT,) + w.shape[axis + 1:])


def _rope_tables(seq, ctx_len):
    t = jnp.arange(seq)
    rows = (t // GRID_W).astype(F32)
    cols = (t % GRID_W).astype(F32)

    def angles(rot_dim):
        nf = rot_dim // 4
        inv = ROPE_THETA ** (-jnp.arange(nf, dtype=F32) / nf)
        ar = rows[:, None] * inv
        ac = cols[:, None] * inv
        return jnp.concatenate([ar, ar, ac, ac], axis=-1)

    def slot_tables(rot_dim, lane0):
        ang = angles(rot_dim)
        quarter = rot_dim // 4
        first = (jnp.arange(rot_dim) % (2 * quarter)) < quarter
        cos = jnp.ones((seq, SLOT), F32).at[:, lane0:lane0 + rot_dim].set(jnp.cos(ang))
        sin = jnp.sin(ang)
        sin_m = jnp.zeros((seq, SLOT), F32).at[:, lane0:lane0 + rot_dim].set(jnp.where(first, -sin, 0.0))
        sin_p = jnp.zeros((seq, SLOT), F32).at[:, lane0:lane0 + rot_dim].set(jnp.where(first, 0.0, sin))
        ident = jnp.stack([jnp.ones((ctx_len, SLOT), F32), jnp.zeros((ctx_len, SLOT), F32),
                           jnp.zeros((ctx_len, SLOT), F32)])
        return jnp.concatenate([ident, jnp.stack([cos, sin_m, sin_p])], axis=1)

    return jnp.concatenate([slot_tables(A_DH, 0), slot_tables(L_ROPE, L_NOPE)], axis=0)


def _layer_params(w_in, m_conv, m_gate_b, m_norm, a_qnorm, a_knorm, l_cq_norm, l_ckv_norm, l_wuq, l_wukv,
                  l_qnorm, l_knorm, w_out):
    d = w_in.shape[0]
    offs = np.cumsum((0,) + IN_SIZES)
    seg = [w_in[:, offs[i]:offs[i + 1]] for i in range(len(IN_SIZES))]
    misc = jnp.zeros((d, SLOT), F32).at[:, 0:4 * M_HEADS].set(seg[4]).at[:, MISC_KR:MISC_KR + L_ROPE].set(seg[10])
    w_in_p = jnp.concatenate(
        [_pad_heads(seg[0], M_HEADS, M_DH), _pad_heads(seg[1], M_HEADS, M_DH),
         _pad_heads(seg[2], M_HEADS, M_DH), _pad_heads(seg[3], M_HEADS, M_DH), misc,
         _pad_heads(seg[5], A_HEADS, A_DH), _pad_heads(seg[6], A_KV, A_DH), _pad_heads(seg[7], A_KV, A_DH),
         seg[8], seg[9]], axis=1).astype(BF16)
    conv_w = jnp.concatenate([_pad_heads(m_conv[:, :M_HEADS * M_DH], M_HEADS, M_DH),
                              _pad_heads(m_conv[:, M_HEADS * M_DH:], M_HEADS, M_DH)], axis=1)
    conv_scale = jnp.concatenate([jnp.ones((1, M_HEADS * SLOT), F32),
                                  jnp.full((1, M_HEADS * SLOT), M_DH ** -0.5, F32)], axis=1)
    gate_b = jnp.zeros((1, SLOT), F32).at[0, 0:4 * M_HEADS].set(m_gate_b)
    pad1 = lambda g: jnp.pad(g, (0, SLOT - g.shape[0]))
    an = jnp.stack([pad1(a_qnorm), pad1(a_knorm)])
    ln = jnp.stack([pad1(l_qnorm), pad1(l_knorm)])
    wuq_p = _pad_heads(l_wuq, L_HEADS, L_QK).astype(BF16)
    kv = l_wukv.reshape(L_KVRANK, L_HEADS, L_NOPE + L_DV)
    wukv_p = jnp.concatenate(
        [_pad_heads(kv[:, :, :L_NOPE].reshape(L_KVRANK, -1), L_HEADS, L_NOPE),
         _pad_heads(kv[:, :, L_NOPE:].reshape(L_KVRANK, -1), L_HEADS, L_DV)], axis=1).astype(BF16)
    nm, na = M_HEADS * M_DH, A_HEADS * A_DH
    w_out_p = jnp.concatenate(
        [_pad_heads(w_out[:nm], M_HEADS, M_DH, axis=0), _pad_heads(w_out[nm:nm + na], A_HEADS, A_DH, axis=0),
         _pad_heads(w_out[nm + na:], L_HEADS, L_DV, axis=0)], axis=0).astype(BF16)
    m_norm_p = _pad_heads(m_norm[None, :], M_HEADS, M_DH)
    return dict(w_in_p=w_in_p, conv_w=conv_w, conv_scale=conv_scale, gate_b=gate_b, an=an, ln=ln,
                cqn=l_cq_norm[None, :], ckvn=l_ckv_norm[None, :], wuq_p=wuq_p, wukv_p=wukv_p,
                w_out_p=w_out_p, m_norm_p=m_norm_p)


def kernel(x, c, ctx, c_ctx, mod_w, mod_b, norm_mix, norm_ffn, w_in, m_conv, m_gate_b, m_norm, a_qnorm, a_knorm,
           l_cq_norm, l_ckv_norm, l_wuq, l_wukv, l_qnorm, l_knorm, w_out, ffn_wg, ffn_wu, ffn_wd, moe_router,
           moe_wg, moe_wu, moe_wd):
    b, seq, d = x.shape
    ctx_len = ctx.shape[1]
    depth = mod_w.shape[0]
    assert ctx_len % TB == 0 and ctx_len % TQ == 0 and ctx_len == ML and seq % TK == 0 and seq % GRID_W == 0
    xs = jnp.concatenate([ctx, x], axis=1)
    mod_rows = 16
    cc = jnp.zeros((mod_rows, d), F32).at[:b].set(c).at[b].set(c_ctx)
    mod_all = _mod_table(cc, mod_w, mod_b)
    rope_tab = _rope_tables(seq, ctx_len)
    for i in range(depth):
        need_ctx = i < depth - 1
        lat = mod_all[i, :b].reshape(b, 1, 6, d)
        cm = jnp.broadcast_to(mod_all[i, b].reshape(1, 1, 6, d), (b, 1, 6, d))
        modtab = jnp.concatenate([cm, lat], axis=1)
        p = _layer_params(w_in[i], m_conv[i], m_gate_b[i], m_norm[i], a_qnorm[i], a_knorm[i], l_cq_norm[i],
                          l_ckv_norm[i], l_wuq[i], l_wukv[i], l_qnorm[i], l_knorm[i], w_out[i])
        mqk, mo, misc, q_all, k_all, mv_t, v_all = _inproj(
            xs, modtab, norm_mix[i][None, :], p["w_in_p"], rope_tab, p["an"], p["ln"], p["cqn"], p["ckvn"],
            p["wuq_p"], p["wukv_p"], ctx_len)
        q_t, k_c = _conv(mqk, p["conv_w"], p["conv_scale"], ctx_len)
        hf, hb = _mlstm(q_t, k_c, mv_t, misc, p["gate_b"])
        gqa_kv = tuple(h // (A_HEADS // A_KV) for h in range(A_HEADS))
        o_gqa = _attention(q_all, k_all, v_all, ctx_len, need_ctx, 0, KV_GQA, A_KV, gqa_kv)
        o_mla = _attention(q_all, k_all, v_all, ctx_len, need_ctx, Q_MLA, 0, L_HEADS, tuple(range(L_HEADS)))
        xs = _mixout(xs, modtab, hf, hb, mo, o_gqa, o_mla, p["m_norm_p"], p["w_out_p"], ctx_len, not need_ctx)
        ffn_ctx = ctx_len if need_ctx else 0
        j = i // 2
        if i % 2 == 0:
            xs = _ffn(xs, modtab, norm_ffn[i][None, :], ffn_wg[j].astype(BF16), ffn_wu[j].astype(BF16),
                      ffn_wd[j].astype(BF16), ffn_ctx)
        else:
            router_p = jnp.pad(moe_router[j], ((0, 0), (0, LANE - N_EXPERTS)))
            xs = _moe(xs, modtab, norm_ffn[i][None, :], router_p, moe_wg[j].astype(BF16),
                      moe_wu[j].astype(BF16), moe_wd[j].astype(BF16), ffn_ctx)
    return xs
```

```python
import functools
import math

import numpy as np
import jax
import jax.numpy as jnp
from jax import lax
from jax.experimental import pallas as pl
from jax.experimental.pallas import tpu as pltpu

F32 = jnp.float32
BF16 = jnp.bfloat16

GRID_W = 64
EPS = 1e-6
ROPE_THETA = 10000.0
M_HEADS, M_DH = 4, 96
A_HEADS, A_KV, A_DH = 6, 2, 64
L_HEADS, L_NOPE, L_ROPE, L_DV = 4, 64, 32, 64
L_QK = L_NOPE + L_ROPE
L_QRANK, L_KVRANK = 256, 128
N_EXPERTS, TOP_K = 8, 2
IN_SIZES = (384, 384, 384, 384, 16, 384, 128, 128, 256, 128, 32)

LANE = 128
SLOT = LANE
N_KVHEADS = A_KV + L_HEADS
Q_MLA = 8
Q_SLOTS = Q_MLA + L_HEADS
KV_GQA = L_HEADS
VMEM_LIMIT = 56 * 1024 * 1024

O_MQK, O_MV, O_MO, O_MISC = 0, 1024, 1536, 2048
O_QA, O_KA, O_VA, O_CQ, O_CKV, IN_PAD = 2176, 2944, 3200, 3456, 3712, 3840
MISC_KR = 64
DEN_LANE = M_DH
SUM_LANE = A_DH
LOG2E = math.log2(math.e)
assert A_DH == L_DV and SUM_LANE < SLOT

TB = 256
TQ = 256
TK = 512
TM_FFN = 1088
TF = 512
ML = 256
RT = 144
GT = 2 * RT
ST = 3 * LANE
TF_MOE = 1792


def _cparams(sem):
    return pltpu.CompilerParams(dimension_semantics=sem, vmem_limit_bytes=VMEM_LIMIT)


def _sigmoid(x):
    return 1.0 / (1.0 + jnp.exp(-x))


def _silu(x):
    return x * _sigmoid(x)


def _lane_iota(shape):
    return lax.broadcasted_iota(jnp.int32, shape, len(shape) - 1)


def _row_iota(shape):
    return lax.broadcasted_iota(jnp.int32, shape, 0)


def _modulated(x, gain, mod_ref, t0, ctx_len, k_shift, k_scale):
    tm = x.shape[0]
    is_ctx = (t0 + _row_iota((tm, 1))) < ctx_len
    shift = jnp.where(is_ctx, mod_ref[0, k_shift:k_shift + 1, :], mod_ref[1, k_shift:k_shift + 1, :])
    scale = jnp.where(is_ctx, mod_ref[0, k_scale:k_scale + 1, :], mod_ref[1, k_scale:k_scale + 1, :])
    y = x * lax.rsqrt(jnp.mean(x * x, axis=-1, keepdims=True) + EPS) * gain
    return y * (1.0 + scale) + shift


def _gate_rows(mod_ref, t0, tm, ctx_len, k_gate):
    is_ctx = (t0 + _row_iota((tm, 1))) < ctx_len
    return jnp.where(is_ctx, mod_ref[0, k_gate:k_gate + 1, :], mod_ref[1, k_gate:k_gate + 1, :])


def _mod_kernel(c_ref, w_ref, b_ref, o_ref):
    s = _silu(c_ref[...]).astype(BF16)
    o_ref[...] = jnp.dot(s, w_ref[...].astype(BF16), preferred_element_type=F32) + b_ref[...]


def _mod_table(cc, mod_w, mod_b):
    depth, d, n = mod_w.shape
    rows = cc.shape[0]
    return pl.pallas_call(
        _mod_kernel,
        grid=(depth, n // d),
        in_specs=[pl.BlockSpec((rows, d), lambda l, j: (0, 0)),
                  pl.BlockSpec((None, d, d), lambda l, j: (l, 0, j)),
                  pl.BlockSpec((None, 1, d), lambda l, j: (l, 0, j))],
        out_specs=pl.BlockSpec((None, rows, d), lambda l, j: (l, 0, j)),
        out_shape=jax.ShapeDtypeStruct((depth, rows, n), F32),
        compiler_params=_cparams(("arbitrary", "arbitrary")),
        name="mod_table",
    )(cc, mod_w, mod_b.reshape(depth, 1, n))


def _slot_rms(x, lo, hi):
    lane = _lane_iota(x.shape)
    sq = jnp.where((lane >= lo) & (lane < hi), x * x, 0.0)
    return lax.rsqrt(jnp.sum(sq, axis=-1, keepdims=True) * (1.0 / (hi - lo)) + EPS)


def _rope(x, cos, sin_m, sin_p, quarter):
    return (x * cos + pltpu.roll(x, LANE - quarter, 1) * sin_m + pltpu.roll(x, quarter, 1) * sin_p)


def _inproj_kernel(ctx_len, x_ref, mod_ref, g_ref, w_ref, rope_ref, an_ref, ln_ref, cqn_ref, ckvn_ref,
                   wuq_ref, wukv_ref, mqk_ref, mo_ref, misc_ref, q_ref, k_ref, mvt_ref, vt_ref):
    tm = x_ref.shape[0]
    t0 = pl.program_id(1) * tm
    h = _modulated(x_ref[...], g_ref[...], mod_ref, t0, ctx_len, 0, 1).astype(BF16)

    def proj(a, b):
        return jnp.dot(h, w_ref[:, a:b], preferred_element_type=F32)

    mqk_ref[...] = proj(O_MQK, O_MV).astype(BF16)
    mv = proj(O_MV, O_MO)
    mvt_ref[...] = jnp.where(_lane_iota(mv.shape) % SLOT == DEN_LANE, 1.0, mv).T.astype(BF16)
    mo_ref[...] = proj(O_MO, O_MISC).astype(BF16)
    misc = proj(O_MISC, O_QA)
    misc_ref[...] = misc

    cos_a, sinm_a, sinp_a = rope_ref[0], rope_ref[1], rope_ref[2]
    cos_l, sinm_l, sinp_l = rope_ref[3], rope_ref[4], rope_ref[5]
    a_scale = A_DH ** -0.5 * LOG2E
    l_scale = L_QK ** -0.5 * LOG2E

    def with_ones_t(v):
        return jnp.where(_lane_iota(v.shape) % SLOT == SUM_LANE, 1.0, v).T.astype(BF16)

    pa = proj(O_QA, O_CQ)
    gq, gk = an_ref[0:1, :], an_ref[1:2, :]
    for i in range(A_HEADS):
        x = pa[:, i * SLOT:(i + 1) * SLOT]
        x = x * _slot_rms(x, 0, A_DH) * gq
        q_ref[:, i * SLOT:(i + 1) * SLOT] = (_rope(x, cos_a, sinm_a, sinp_a, A_DH // 4) * a_scale).astype(BF16)
    for i in range(A_KV):
        x = pa[:, (A_HEADS + i) * SLOT:(A_HEADS + i + 1) * SLOT]
        x = x * _slot_rms(x, 0, A_DH) * gk
        k_ref[:, (KV_GQA + i) * SLOT:(KV_GQA + i + 1) * SLOT] = (
            _rope(x, cos_a, sinm_a, sinp_a, A_DH // 4).astype(BF16))
    vt_ref[KV_GQA * SLOT:, :] = with_ones_t(pa[:, (A_HEADS + A_KV) * SLOT:(A_HEADS + 2 * A_KV) * SLOT])
    q_ref[:, A_HEADS * SLOT:Q_MLA * SLOT] = jnp.zeros((tm, (Q_MLA - A_HEADS) * SLOT), BF16)

    pc = proj(O_CQ, IN_PAD)
    cq = pc[:, 0:L_QRANK]
    cq = (cq * lax.rsqrt(jnp.mean(cq * cq, axis=-1, keepdims=True) + EPS) * cqn_ref[...]).astype(BF16)
    ckv = pc[:, L_QRANK:L_QRANK + L_KVRANK]
    ckv = (ckv * lax.rsqrt(jnp.mean(ckv * ckv, axis=-1, keepdims=True) + EPS) * ckvn_ref[...]).astype(BF16)
    ql = jnp.dot(cq, wuq_ref[...], preferred_element_type=F32)
    kvl = jnp.dot(ckv, wukv_ref[...], preferred_element_type=F32)
    gq_l, gk_l = ln_ref[0:1, :], ln_ref[1:2, :]
    lane = _lane_iota((tm, SLOT))
    kr = jnp.where((lane >= MISC_KR) & (lane < MISC_KR + L_ROPE), misc, 0.0)
    kr = kr * _slot_rms(kr, MISC_KR, MISC_KR + L_ROPE) * gk_l
    kr = _rope(kr, cos_l, sinm_l, sinp_l, L_ROPE // 4)
    for i in range(L_HEADS):
        x = ql[:, i * SLOT:(i + 1) * SLOT]
        inv = jnp.where(lane < L_NOPE, _slot_rms(x, 0, L_NOPE), _slot_rms(x, L_NOPE, L_QK))
        x = x * inv * gq_l
        q_ref[:, (Q_MLA + i) * SLOT:(Q_MLA + i + 1) * SLOT] = (
            _rope(x, cos_l, sinm_l, sinp_l, L_ROPE // 4) * l_scale).astype(BF16)
        kn = kvl[:, i * SLOT:(i + 1) * SLOT]
        kn = kn * _slot_rms(kn, 0, L_NOPE) * gk_l
        k_ref[:, i * SLOT:(i + 1) * SLOT] = (kn + kr).astype(BF16)
    vt_ref[0:KV_GQA * SLOT, :] = with_ones_t(kvl[:, L_HEADS * SLOT:])


def _inproj(xs, modtab, gain, w_in_p, rope_tab, an, ln, cqn, ckvn, wuq_p, wukv_p, ctx_len):
    b, s, d = xs.shape
    grid = (b, s // TB)
    tok = lambda n: pl.BlockSpec((None, TB, n), lambda i, j: (i, j, 0))
    full = lambda a: pl.BlockSpec(a.shape, lambda i, j: (0,) * a.ndim)
    out_widths = (1024, 512, SLOT, Q_SLOTS * SLOT, N_KVHEADS * SLOT)
    out_dtypes = (BF16, BF16, F32, BF16, BF16)
    out_specs = [tok(n) for n in out_widths]
    out_shape = [jax.ShapeDtypeStruct((b, s, n), dt) for n, dt in zip(out_widths, out_dtypes)]
    for n in (M_HEADS * SLOT, N_KVHEADS * SLOT):
        out_specs.append(pl.BlockSpec((None, n, TB), lambda i, j: (i, 0, j)))
        out_shape.append(jax.ShapeDtypeStruct((b, n, s), BF16))
    return pl.pallas_call(
        functools.partial(_inproj_kernel, ctx_len),
        grid=grid,
        in_specs=[tok(d),
                  pl.BlockSpec((None, 2, 6, d), lambda i, j: (i, 0, 0, 0)),
                  full(gain), full(w_in_p),
                  pl.BlockSpec((6, TB, SLOT), lambda i, j: (0, j, 0)),
                  full(an), full(ln), full(cqn), full(ckvn), full(wuq_p), full(wukv_p)],
        out_specs=out_specs,
        out_shape=out_shape,
        compiler_params=_cparams(("parallel", "parallel")),
        name="inproj",
    )(xs, modtab, gain, w_in_p, rope_tab, an, ln, cqn, ckvn, wuq_p, wukv_p)


def _conv_kernel(ctx_len, s_len, x_ref, prev_ref, next_ref, w_ref, sc_ref, qt_ref, k_ref):
    tm = x_ref.shape[0]
    t0 = pl.program_id(1) * tm
    x = x_ref[...].astype(F32)
    row = _row_iota((tm, 1))
    has_prev = jnp.logical_and(t0 != 0, t0 != ctx_len)
    has_next = jnp.logical_and(t0 + tm != ctx_len, t0 + tm != s_len)
    hp = jnp.where(has_prev, prev_ref[15:16, :].astype(F32), 0.0)
    hn = jnp.where(has_next, next_ref[0:1, :].astype(F32), 0.0)
    xp = jnp.where(row == 0, hp, pltpu.roll(x, 1, 0))
    xn = jnp.where(row == tm - 1, hn, pltpu.roll(x, tm - 1, 0))
    y = xp * w_ref[0:1, :] + x * w_ref[1:2, :] + xn * w_ref[2:3, :]
    y = _silu(y) * sc_ref[...]
    half = y.shape[1] // 2
    qt_ref[...] = y[:, :half].T.astype(qt_ref.dtype)
    k_ref[...] = y[:, half:].astype(k_ref.dtype)


def _conv(mqk, conv_w, conv_scale, ctx_len):
    b, s, n = mqk.shape
    hb = TB // 16
    last = s // 16 - 1
    return pl.pallas_call(
        functools.partial(_conv_kernel, ctx_len, s),
        grid=(b, s // TB),
        in_specs=[pl.BlockSpec((None, TB, n), lambda i, j: (i, j, 0)),
                  pl.BlockSpec((None, 16, n), lambda i, j: (i, jnp.maximum(j * hb - 1, 0), 0)),
                  pl.BlockSpec((None, 16, n), lambda i, j: (i, jnp.minimum((j + 1) * hb, last), 0)),
                  pl.BlockSpec((3, n), lambda i, j: (0, 0)),
                  pl.BlockSpec((1, n), lambda i, j: (0, 0))],
        out_specs=[pl.BlockSpec((None, n // 2, TB), lambda i, j: (i, 0, j)),
                   pl.BlockSpec((None, TB, n // 2), lambda i, j: (i, j, 0))],
        out_shape=[jax.ShapeDtypeStruct((b, n // 2, s), BF16), jax.ShapeDtypeStruct((b, s, n // 2), BF16)],
        compiler_params=_cparams(("parallel", "parallel")),
        name="mlstm_conv",
    )(mqk, mqk, mqk, conv_w, conv_scale)


def _log_sigmoid(x):
    return jnp.minimum(x, 0.0) - jnp.log(1.0 + jnp.exp(-jnp.abs(x)))


def _mlstm_kernel(n_chunks, qt_ref, k_ref, vt_ref, misc_ref, gb_ref, hf_ref, hb_ref, c_ref, m_ref, gt_sc, bt_sc):
    hp = pl.program_id(1)
    c_ref[...] = jnp.zeros_like(c_ref)
    m_ref[...] = jnp.zeros_like(m_ref)
    r = _row_iota((ML, ML))
    cidx = _lane_iota((ML, ML))
    tri_f32 = (r >= cidx).astype(F32)
    tri_b32 = (r <= cidx).astype(F32)
    lane = _lane_iota((ML, SLOT))
    lane_t = _lane_iota((1, ML))

    def chunk(t0, d, tri32, out_ref):
        g = misc_ref[pl.ds(t0, ML), :] + gb_ref[...]
        logf = _log_sigmoid(g)
        bcum = jnp.dot(tri32, logf, preferred_element_type=F32, precision=lax.Precision.HIGHEST)
        gt_sc[d] = g.T
        bt_sc[d] = bcum.T
        ib = g - pltpu.roll(bcum, SLOT - M_HEADS, 1)
        allowed = (r <= cidx) if d == 0 else (r >= cidx)
        for hh in range(2):
            li = d * 2 * M_HEADS + hp * 2 + hh
            ib_col = jnp.sum(jnp.where(lane == li, ib, 0.0), axis=-1, keepdims=True)
            i_row = gt_sc[d, pl.ds(li, 1), :]
            b_row = bt_sc[d, pl.ds(li + M_HEADS, 1), :]
            sidx = d * 2 + hh
            ct_st = c_ref[sidx]
            m_s = m_ref[sidx]
            k = k_ref[pl.ds(t0, ML), hh * SLOT:(hh + 1) * SLOT]
            qt = qt_ref[hh * SLOT:(hh + 1) * SLOT, pl.ds(t0, ML)]
            vt = vt_ref[hh * SLOT:(hh + 1) * SLOT, pl.ds(t0, ML)]
            dmat = jnp.where(allowed, b_row + ib_col, -jnp.inf)
            m_inter = b_row + m_s
            m_t = jnp.maximum(m_inter, jnp.max(dmat, axis=0, keepdims=True))
            kq = jnp.dot(k, qt, preferred_element_type=F32)
            w = (jnp.exp(dmat - m_t) * kq).astype(BF16)
            a_inter = jnp.exp(m_inter - m_t)
            num = a_inter * jnp.dot(ct_st.astype(BF16), qt, preferred_element_type=F32) + jnp.dot(
                vt, w, preferred_element_type=F32)
            den = num[DEN_LANE:DEN_LANE + 1, :]
            h_out = num / jnp.maximum(jnp.abs(den), jnp.exp(-m_t))
            out_ref[pl.ds(t0, ML), hh * SLOT:(hh + 1) * SLOT] = h_out.T.astype(out_ref.dtype)
            last = ML - 1 if d == 0 else 0
            total = jnp.sum(jnp.where(lane_t == last, b_row, 0.0), axis=-1, keepdims=True)
            gg = total - b_row + i_row
            m_new = jnp.maximum(total + m_s, jnp.max(gg, axis=-1, keepdims=True))
            decay = jnp.exp(total + m_s - m_new)[:, 0:1]
            wk = jnp.exp(gg - m_new)
            vw = (vt.astype(F32) * wk).astype(BF16)
            c_ref[sidx] = decay * ct_st + jnp.dot(vw, k, preferred_element_type=F32)
            m_ref[sidx] = m_new

    def step(n, carry):
        tf0 = pl.multiple_of(n * ML, ML)
        tb0 = pl.multiple_of(jnp.where(n == 0, 0, n_chunks - n) * ML, ML)
        chunk(tf0, 0, tri_f32, hf_ref)
        chunk(tb0, 1, tri_b32, hb_ref)
        return carry

    lax.fori_loop(0, n_chunks, step, 0)


def _mlstm(q_t, k_c, mv_t, misc, gate_b):
    b, s, _ = k_c.shape
    n_chunks = s // ML
    pair = 2 * SLOT
    spec = pl.BlockSpec((None, s, pair), lambda i, j: (i, 0, j))
    spec_t = pl.BlockSpec((None, pair, s), lambda i, j: (i, j, 0))
    return pl.pallas_call(
        functools.partial(_mlstm_kernel, n_chunks),
        grid=(b, M_HEADS // 2),
        in_specs=[spec_t, spec, spec_t,
                  pl.BlockSpec((None, s, SLOT), lambda i, j: (i, 0, 0)),
                  pl.BlockSpec((1, SLOT), lambda i, j: (0, 0))],
        out_specs=[spec, spec],
        out_shape=[jax.ShapeDtypeStruct((b, s, M_HEADS * SLOT), BF16)] * 2,
        scratch_shapes=[pltpu.VMEM((4, SLOT, SLOT), F32), pltpu.VMEM((4, 1, ML), F32),
                        pltpu.VMEM((2, SLOT, ML), F32), pltpu.VMEM((2, SLOT, ML), F32)],
        compiler_params=_cparams(("parallel", "parallel")),
        name="mlstm_scan",
    )(q_t, k_c, mv_t, misc, gate_b)


def _attn_kernel(ctx_len, q_off, kv_of, q_ref, k_ref, vt_ref, o_ref, st_sc, pt_sc, acc_sc):
    s_len = k_ref.shape[0]
    n_heads = len(kv_of)
    qs = [q_ref[:, g * SLOT:(g + 1) * SLOT] for g in range(n_heads)]

    def kv_cols(g):
        return slice(kv_of[g] * SLOT, (kv_of[g] + 1) * SLOT)

    def chunk(carry, rows):
        out = []
        for g in range(n_heads):
            m, acc = carry[g]
            k = k_ref[rows, kv_cols(g)]
            vt = vt_ref[kv_cols(g), rows]
            st = lax.dot_general(k, qs[g], (((1,), (1,)), ((), ())), preferred_element_type=F32)
            m_new = jnp.maximum(m, jnp.max(st, axis=0, keepdims=True))
            alpha = jnp.exp2(m - m_new)
            pt = jnp.exp2(st - m_new).astype(BF16)
            out.append((m_new, alpha * acc + jnp.dot(vt, pt, preferred_element_type=F32)))
        return tuple(out)

    def finish(carry):
        for g in range(n_heads):
            acc = carry[g][1]
            o = acc / acc[SUM_LANE:SUM_LANE + 1, :]
            o_ref[:, g * SLOT:(g + 1) * SLOT] = o.T.astype(o_ref.dtype)

    init = tuple((jnp.full((1, TQ), -jnp.inf, F32), jnp.zeros((SLOT, TQ), F32)) for _ in range(n_heads))
    is_ctx = (pl.program_id(2) + q_off) * TQ < ctx_len

    @pl.when(is_ctx)
    def _():
        finish(chunk(init, pl.ds(0, ctx_len)))

    n_lat = (s_len - ctx_len) // TK
    n0 = ctx_len + TK
    assert n_lat >= 4 and n_lat % 2 == 0 and st_sc.shape[1] == n0

    def rows_of(c):
        if isinstance(c, int):
            return (pl.ds(0, n0), n0) if c == 0 else (pl.ds(ctx_len + c * TK, TK), TK)
        return pl.ds(pl.multiple_of(ctx_len + c * TK, math.gcd(ctx_len, TK)), TK), TK

    def scores(c, par):
        rows, n = rows_of(c)
        for g in range(n_heads):
            st_sc[par * n_heads + g, 0:n, :] = lax.dot_general(
                k_ref[rows, kv_cols(g)], qs[g], (((1,), (1,)), ((), ())), preferred_element_type=F32)

    def softmax(par, m, n=TK):
        ms, alphas = [], []
        for g in range(n_heads):
            st = st_sc[par * n_heads + g, 0:n, :]
            m_new = jnp.maximum(m[g], jnp.max(st, axis=0, keepdims=True))
            pt_sc[par * n_heads + g, 0:n, :] = jnp.exp2(st - m_new).astype(BF16)
            ms.append(m_new)
            alphas.append(jnp.exp2(m[g] - m_new))
        return tuple(ms), tuple(alphas)

    def values(c, par, alpha):
        rows, n = rows_of(c)
        for g in range(n_heads):
            acc_sc[g] = alpha[g] * acc_sc[g] + jnp.dot(vt_ref[kv_cols(g), rows], pt_sc[par * n_heads + g, 0:n, :],
                                                       preferred_element_type=F32)

    def stage(c, par, m, alpha):
        scores(c + 1, 1 - par)
        m_new, alpha_new = softmax(par, m)
        values(c - 1, 1 - par, alpha)
        return m_new, alpha_new

    @pl.when(jnp.logical_not(is_ctx))
    def _():
        for g in range(n_heads):
            acc_sc[g] = init[g][1]
        scores(0, 0)
        m, alpha = softmax(0, tuple(c[0] for c in init), n0)
        scores(1, 1)
        m, alpha = stage(1, 1, m, alpha)

        def body(t, carry):
            c = 2 * t + 2
            return stage(c + 1, 1, *stage(c, 0, *carry))

        m, alpha = lax.fori_loop(0, (n_lat - 4) // 2, body, (m, alpha))
        m, alpha = stage(n_lat - 2, 0, m, alpha)
        values(n_lat - 2, 0, alpha)
        m, alpha = softmax(1, m)
        values(n_lat - 1, 1, alpha)
        finish(tuple((m[g], acc_sc[g]) for g in range(n_heads)))


def _attention(q_all, k_all, vt_all, ctx_len, need_ctx, q_slot0, kv_slot0, n_kv, kv_of):
    b, s, _ = q_all.shape
    q_off = 0 if need_ctx else ctx_len // TQ
    nq = s // TQ - q_off
    n_heads = len(kv_of)
    qw = n_heads * SLOT
    kw = n_kv * SLOT
    assert (q_slot0 * SLOT) % qw == 0 and (kv_slot0 * SLOT) % kw == 0
    qb0, kb0 = q_slot0 * SLOT // qw, kv_slot0 * SLOT // kw
    return pl.pallas_call(
        functools.partial(_attn_kernel, ctx_len, q_off, kv_of),
        grid=(b, 1, nq),
        in_specs=[pl.BlockSpec((None, TQ, qw), lambda i, g, j: (i, j + q_off, qb0)),
                  pl.BlockSpec((None, s, kw), lambda i, g, j: (i, 0, kb0)),
                  pl.BlockSpec((None, kw, s), lambda i, g, j: (i, kb0, 0))],
        out_specs=pl.BlockSpec((None, TQ, qw), lambda i, g, j: (i, j + q_off, 0)),
        out_shape=jax.ShapeDtypeStruct((b, s, qw), BF16),
        scratch_shapes=[pltpu.VMEM((2 * n_heads, ctx_len + TK, TQ), F32),
                        pltpu.VMEM((2 * n_heads, ctx_len + TK, TQ), BF16),
                        pltpu.VMEM((n_heads, SLOT, TQ), F32)],
        compiler_params=_cparams(("parallel", "parallel", "parallel")),
        name="attention",
    )(q_all, k_all, vt_all)


def _mixout_kernel(ctx_len, blk_off, x_ref, mod_ref, hf_ref, hb_ref, mo_ref, oa_ref, ol_ref, mn_ref, w_ref, o_ref):
    tm = x_ref.shape[0]
    t0 = (pl.program_id(1) + blk_off) * tm
    hm = hf_ref[...].astype(F32) + hb_ref[...].astype(F32)
    gate = _sigmoid(mo_ref[...].astype(F32))
    parts = []
    for i in range(M_HEADS):
        x = hm[:, i * SLOT:(i + 1) * SLOT]
        inv = lax.rsqrt(jnp.sum(x * x, axis=-1, keepdims=True) * (1.0 / M_DH) + EPS)
        parts.append(x * inv)
    hn = jnp.concatenate(parts, axis=-1) * mn_ref[...] * gate
    nm = M_HEADS * SLOT
    o = jnp.dot(hn.astype(BF16), w_ref[0:nm, :], preferred_element_type=F32)
    na = nm + A_HEADS * SLOT
    o = o + jnp.dot(oa_ref[...], w_ref[nm:na, :], preferred_element_type=F32)
    o = o + jnp.dot(ol_ref[...], w_ref[na:, :], preferred_element_type=F32)
    o_ref[...] = x_ref[...] + _gate_rows(mod_ref, t0, tm, ctx_len, 2) * o


def _mixout(xs, modtab, hf, hb, mo, o_gqa, o_mla, m_norm_p, w_out_p, ctx_len, lat_only):
    b, s, d = xs.shape
    blk_off = ctx_len // TB if lat_only else 0
    tok = lambda n: pl.BlockSpec((None, TB, n), lambda i, j: (i, j + blk_off, 0))
    full = lambda a: pl.BlockSpec(a.shape, lambda i, j: (0,) * a.ndim)
    return pl.pallas_call(
        functools.partial(_mixout_kernel, ctx_len, blk_off),
        grid=(b, s // TB - blk_off),
        in_specs=[tok(d),
                  pl.BlockSpec((None, 2, 6, d), lambda i, j: (i, 0, 0, 0)),
                  tok(M_HEADS * SLOT), tok(M_HEADS * SLOT), tok(M_HEADS * SLOT), tok(A_HEADS * SLOT),
                  tok(L_HEADS * SLOT), full(m_norm_p), full(w_out_p)],
        out_specs=pl.BlockSpec((None, TB, d), lambda i, j: (i, j, 0)),
        out_shape=jax.ShapeDtypeStruct((b, s - blk_off * TB, d), F32),
        input_output_aliases={} if lat_only else {0: 0},
        compiler_params=_cparams(("parallel", "parallel")),
        name="mix_out",
    )(xs, modtab, hf, hb, mo, o_gqa, o_mla, m_norm_p, w_out_p)


def _ffn_kernel(ctx_len, x_ref, mod_ref, g_ref, wg_ref, wu_ref, wd_ref, o_ref, h_sc, acc_sc):
    tm = x_ref.shape[0]
    t0 = pl.program_id(1) * tm
    f = pl.program_id(2)

    @pl.when(f == 0)
    def _():
        h_sc[...] = _modulated(x_ref[...], g_ref[...], mod_ref, t0, ctx_len, 3, 4).astype(BF16)
        acc_sc[...] = jnp.zeros_like(acc_sc)

    h = h_sc[...]
    a = jnp.dot(h, wg_ref[...], preferred_element_type=F32)
    u = jnp.dot(h, wu_ref[...], preferred_element_type=F32)
    acc_sc[...] += jnp.dot((_silu(a) * u).astype(BF16), wd_ref[...], preferred_element_type=F32)

    @pl.when(f == pl.num_programs(2) - 1)
    def _():
        o_ref[...] = x_ref[...] + _gate_rows(mod_ref, t0, tm, ctx_len, 5) * acc_sc[...]


def _ffn(xs, modtab, gain, wg, wu, wd, ctx_len):
    b, s, d = xs.shape
    tm = s // 4
    nf = wg.shape[1] // TF
    return pl.pallas_call(
        functools.partial(_ffn_kernel, ctx_len),
        grid=(b, s // tm, nf),
        in_specs=[pl.BlockSpec((None, tm, d), lambda i, j, f: (i, j, 0)),
                  pl.BlockSpec((None, 2, 6, d), lambda i, j, f: (i, 0, 0, 0)),
                  pl.BlockSpec((1, d), lambda i, j, f: (0, 0)),
                  pl.BlockSpec((d, TF), lambda i, j, f: (0, f)),
                  pl.BlockSpec((d, TF), lambda i, j, f: (0, f)),
                  pl.BlockSpec((TF, d), lambda i, j, f: (f, 0))],
        out_specs=pl.BlockSpec((None, tm, d), lambda i, j, f: (i, j, 0)),
        out_shape=jax.ShapeDtypeStruct((b, s, d), F32),
        scratch_shapes=[pltpu.VMEM((tm, d), BF16), pltpu.VMEM((tm, d), F32)],
        input_output_aliases={0: 0},
        compiler_params=_cparams(("parallel", "parallel", "arbitrary")),
        name="ffn_dense",
    )(xs, modtab, gain, wg, wu, wd)


def _top2_combine(logits):
    lane = _lane_iota(logits.shape)
    lane_f = lane.astype(F32)
    lg = jnp.where(lane < N_EXPERTS, logits, -jnp.inf)
    v1 = jnp.max(lg, axis=-1, keepdims=True)
    i1 = jnp.min(jnp.where(lg == v1, lane_f, float(LANE)), axis=-1, keepdims=True)
    rest = jnp.where(lane_f == i1, -jnp.inf, lg)
    v2 = jnp.max(rest, axis=-1, keepdims=True)
    i2 = jnp.min(jnp.where(rest == v2, lane_f, float(LANE)), axis=-1, keepdims=True)
    e2 = jnp.exp(v2 - v1)
    w1 = 1.0 / (1.0 + e2)
    w2 = e2 / (1.0 + e2)
    sel = jnp.where((lane_f == i1) | (lane_f == i2), 1.0, 0.0)
    return jnp.where(lane_f == i1, w1, 0.0) + jnp.where(lane_f == i2, w2, 0.0), sel


def _moe_kernel(ctx_len, x_ref, mod_ref, g_ref, r_ref, tri_ref, wg_ref, wu_ref, wd_ref, o_ref,
                h_sc, comb_sc, rank_sc, rankt_sc, rankc_sc, wc_sc, xg_sc, acc_sc, nt_sc):
    tm = x_ref.shape[0]
    t0 = pl.program_id(1) * tm
    e = pl.program_id(2)
    f = pl.program_id(3)

    @pl.when(jnp.logical_and(e == 0, f == 0))
    def _():
        h = _modulated(x_ref[...], g_ref[...], mod_ref, t0, ctx_len, 3, 4)
        logits = jnp.dot(h, r_ref[...], preferred_element_type=F32, precision=lax.Precision.HIGHEST)
        comb, sel = _top2_combine(logits)
        rank =jnp.dot(tri_ref[...], sel.astype(BF16), preferred_element_type=F32)
        rank = jnp.where(sel > 0.0, rank, -1.0)
        comb_sc[...] = comb
        rank_sc[...] = rank
        pad = rankt_sc.shape[1] - tm
        h_sc[0:tm, :] = h.astype(BF16)
        if pad:
            rankt_sc[...] = jnp.concatenate([rank, jnp.full((pad, LANE), -1.0, F32)], axis=0).T
            h_sc[tm:, :] = jnp.zeros((pad, h_sc.shape[1]), BF16)
        else:
            rankt_sc[...] = rank.T
        o_ref[...] = jnp.zeros_like(o_ref)

    @pl.when(f == 0)
    def _():
        lane = _lane_iota((tm, LANE))
        rank_c = jnp.sum(jnp.where(lane == e, rank_sc[...], 0.0), axis=-1, keepdims=True)
        rankc_sc[...] = rank_c
        wc_sc[...] = jnp.sum(jnp.where(lane == e, comb_sc[...], 0.0), axis=-1, keepdims=True)
        n_rows = jnp.sum(jnp.where(rank_c >= 0.0, 1.0, 0.0)).astype(jnp.int32)
        n_rt = lax.div(n_rows + (RT - 1), RT)
        n_gt = lax.div(n_rt * RT + (GT - 1), GT)
        n_st = lax.div(n_rt * RT + (ST - 1), ST)
        nt_sc[0] = n_rt
        nt_sc[1] = n_st
        rank_r = rankt_sc[pl.ds(e, 1), :]

        def gather(i, c):
            r0 = pl.multiple_of(i * GT, 16)
            tgt = (r0 + _row_iota((GT, 1))).astype(F32)
            sel_t = jnp.where(rank_r == tgt, 1.0, 0.0).astype(BF16)
            xg_sc[pl.ds(r0, GT), :] = jnp.dot(sel_t, h_sc[...], preferred_element_type=F32).astype(BF16)
            return c

        def clear(i, c):
            acc_sc[pl.ds(pl.multiple_of(i * ST, ST), ST), :] = jnp.zeros((ST, acc_sc.shape[1]), F32)
            return c

        lax.fori_loop(0, n_gt, gather, 0)
        lax.fori_loop(0, n_st, clear, 0)

    def expert(i, c):
        r0 = pl.multiple_of(i * RT, 16)
        rows = xg_sc[pl.ds(r0, RT), :]
        a = jnp.dot(rows, wg_ref[...], preferred_element_type=F32)
        u = jnp.dot(rows, wu_ref[...], preferred_element_type=F32)
        acc_sc[pl.ds(r0, RT), :] += jnp.dot((_silu(a) * u).astype(BF16), wd_ref[...],
                                            preferred_element_type=F32)
        return c

    lax.fori_loop(0, nt_sc[0], expert, 0)

    @pl.when(f == pl.num_programs(3) - 1)
    def _():
        def scatter(i, c):
            r0 = pl.multiple_of(i * ST, ST)
            tgt = (r0 + _lane_iota((1, ST))).astype(F32)
            w_t = jnp.where(rankc_sc[...] == tgt, wc_sc[...], 0.0).astype(BF16)
            o_ref[...] += jnp.dot(w_t, acc_sc[pl.ds(r0, ST), :].astype(BF16), preferred_element_type=F32)
            return c

        lax.fori_loop(0, nt_sc[1], scatter, 0)

    @pl.when(jnp.logical_and(e == pl.num_programs(2) - 1, f == pl.num_programs(3) - 1))
    def _():
        o_ref[...] = x_ref[...] + _gate_rows(mod_ref, t0, tm, ctx_len, 5) * o_ref[...]


def _moe(xs, modtab, gain, router_p, wg, wu, wd, ctx_len):
    b, s, d = xs.shape
    tm = s // 4
    tp = -(-tm // LANE) * LANE
    r_max = -(-tm // RT) * RT
    tr = max(-(-r_max // GT) * GT, -(-r_max // ST) * ST)
    ne, _, dff = wg.shape
    tri = jnp.tril(jnp.ones((tm, tm), BF16), -1)
    TF = TF_MOE
    assert dff % TF == 0
    return pl.pallas_call(
        functools.partial(_moe_kernel, ctx_len),
        grid=(b, s // tm, ne, dff // TF),
        in_specs=[pl.BlockSpec((None, tm, d), lambda i, j, e, f: (i, j, 0), pipeline_mode=pl.Buffered(1)),
                  pl.BlockSpec((None, 2, 6, d), lambda i, j, e, f: (i, 0, 0, 0)),
                  pl.BlockSpec((1, d), lambda i, j, e, f: (0, 0)),
                  pl.BlockSpec((d, LANE), lambda i, j, e, f: (0, 0)),
                  pl.BlockSpec((tm, tm), lambda i, j, e, f: (0, 0), pipeline_mode=pl.Buffered(1)),
                  pl.BlockSpec((None, d, TF), lambda i, j, e, f: (e, 0, f)),
                  pl.BlockSpec((None, d, TF), lambda i, j, e, f: (e, 0, f)),
                  pl.BlockSpec((None, TF, d), lambda i, j, e, f: (e, f, 0))],
        out_specs=pl.BlockSpec((None, tm, d), lambda i, j, e, f: (i, j, 0)),
        out_shape=jax.ShapeDtypeStruct((b, s, d), F32),
        scratch_shapes=[pltpu.VMEM((tp, d), BF16),
                        pltpu.VMEM((tm, LANE), F32),
                        pltpu.VMEM((tm, LANE), F32),
                        pltpu.VMEM((LANE, tp), F32),
                        pltpu.VMEM((tm, 1), F32),
                        pltpu.VMEM((tm, 1), F32),
                        pltpu.VMEM((tr, d), BF16),
                        pltpu.VMEM((tr, d), F32),
                        pltpu.SMEM((2,), jnp.int32)],
        input_output_aliases={0: 0},
        compiler_params=_cparams(("parallel", "parallel", "arbitrary", "arbitrary")),
        name="moe_top2",
    )(xs, modtab, gain, router_p, tri, wg, wu, wd)


def _pad_heads(w, n_heads, dh, axis=-1):
    axis = axis % w.ndim
    shp = w.shape[:axis] + (n_heads, dh) + w.shape[axis + 1:]
    pad = [(0, 0)] * (w.ndim + 1)
    pad[axis + 1] = (0, SLOT - dh)
    out = jnp.pad(w.reshape(shp), pad)
    return out.reshape(w.shape[:axis] + (n_heads * SLOT,) + w.shape[axis + 1:])


def _rope_tables(seq, ctx_len):
    t = jnp.arange(seq)
    rows = (t // GRID_W).astype(F32)
    cols = (t % GRID_W).astype(F32)

    def angles(rot_dim):
        nf = rot_dim // 4
        inv = ROPE_THETA ** (-jnp.arange(nf, dtype=F32) / nf)
        ar = rows[:, None] * inv
        ac = cols[:, None] * inv
        return jnp.concatenate([ar, ar, ac, ac], axis=-1)

    def slot_tables(rot_dim, lane0):
        ang = angles(rot_dim)
        quarter = rot_dim // 4
        first = (jnp.arange(rot_dim) % (2 * quarter)) < quarter
        cos = jnp.ones((seq, SLOT), F32).at[:, lane0:lane0 + rot_dim].set(jnp.cos(ang))
        sin = jnp.sin(ang)
        sin_m = jnp.zeros((seq, SLOT), F32).at[:, lane0:lane0 + rot_dim].set(jnp.where(first, -sin, 0.0))
        sin_p = jnp.zeros((seq, SLOT), F32).at[:, lane0:lane0 + rot_dim].set(jnp.where(first, 0.0, sin))
        ident = jnp.stack([jnp.ones((ctx_len, SLOT), F32), jnp.zeros((ctx_len, SLOT), F32),
                           jnp.zeros((ctx_len, SLOT), F32)])
        return jnp.concatenate([ident, jnp.stack([cos, sin_m, sin_p])], axis=1)

    return jnp.concatenate([slot_tables(A_DH, 0), slot_tables(L_ROPE, L_NOPE)], axis=0)


def _layer_params(w_in, m_conv, m_gate_b, m_norm, a_qnorm, a_knorm, l_cq_norm, l_ckv_norm, l_wuq, l_wukv,
                  l_qnorm, l_knorm, w_out):
    d = w_in.shape[0]
    offs = np.cumsum((0,) + IN_SIZES)
    seg = [w_in[:, offs[i]:offs[i + 1]] for i in range(len(IN_SIZES))]
    misc = jnp.zeros((d, SLOT), F32).at[:, 0:4 * M_HEADS].set(seg[4]).at[:, MISC_KR:MISC_KR + L_ROPE].set(seg[10])
    w_in_p = jnp.concatenate(
        [_pad_heads(seg[0], M_HEADS, M_DH), _pad_heads(seg[1], M_HEADS, M_DH),
         _pad_heads(seg[2], M_HEADS, M_DH), _pad_heads(seg[3], M_HEADS, M_DH), misc,
         _pad_heads(seg[5], A_HEADS, A_DH), _pad_heads(seg[6], A_KV, A_DH), _pad_heads(seg[7], A_KV, A_DH),
         seg[8], seg[9]], axis=1).astype(BF16)
    conv_w = jnp.concatenate([_pad_heads(m_conv[:, :M_HEADS * M_DH], M_HEADS, M_DH),
                              _pad_heads(m_conv[:, M_HEADS * M_DH:], M_HEADS, M_DH)], axis=1)
    conv_scale = jnp.concatenate([jnp.ones((1, M_HEADS * SLOT), F32),
                                  jnp.full((1, M_HEADS * SLOT), M_DH ** -0.5, F32)], axis=1)
    gate_b = jnp.zeros((1, SLOT), F32).at[0, 0:4 * M_HEADS].set(m_gate_b)
    pad1 = lambda g: jnp.pad(g, (0, SLOT - g.shape[0]))
    an = jnp.stack([pad1(a_qnorm), pad1(a_knorm)])
    ln = jnp.stack([pad1(l_qnorm), pad1(l_knorm)])
    wuq_p = _pad_heads(l_wuq, L_HEADS, L_QK).astype(BF16)
    kv = l_wukv.reshape(L_KVRANK, L_HEADS, L_NOPE + L_DV)
    wukv_p = jnp.concatenate(
        [_pad_heads(kv[:, :, :L_NOPE].reshape(L_KVRANK, -1), L_HEADS, L_NOPE),
         _pad_heads(kv[:, :, L_NOPE:].reshape(L_KVRANK, -1), L_HEADS, L_DV)], axis=1).astype(BF16)
    nm, na = M_HEADS * M_DH, A_HEADS * A_DH
    w_out_p = jnp.concatenate(
        [_pad_heads(w_out[:nm], M_HEADS, M_DH, axis=0), _pad_heads(w_out[nm:nm + na], A_HEADS, A_DH, axis=0),
         _pad_heads(w_out[nm + na:], L_HEADS, L_DV, axis=0)], axis=0).astype(BF16)
    m_norm_p = _pad_heads(m_norm[None, :], M_HEADS, M_DH)
    return dict(w_in_p=w_in_p, conv_w=conv_w, conv_scale=conv_scale, gate_b=gate_b, an=an, ln=ln,
                cqn=l_cq_norm[None, :], ckvn=l_ckv_norm[None, :], wuq_p=wuq_p, wukv_p=wukv_p,
                w_out_p=w_out_p, m_norm_p=m_norm_p)


def kernel(x, c, ctx, c_ctx, mod_w, mod_b, norm_mix, norm_ffn, w_in, m_conv, m_gate_b, m_norm, a_qnorm, a_knorm,
           l_cq_norm, l_ckv_norm, l_wuq, l_wukv, l_qnorm, l_knorm, w_out, ffn_wg, ffn_wu, ffn_wd, moe_router,
           moe_wg, moe_wu, moe_wd):
    b, seq, d = x.shape
    ctx_len = ctx.shape[1]
    depth = mod_w.shape[0]
    assert ctx_len % TB == 0 and ctx_len % TQ == 0 and ctx_len == ML and seq % TK == 0 and seq % GRID_W == 0
    xs = jnp.concatenate([ctx, x], axis=1)
    mod_rows = 16
    cc = jnp.zeros((mod_rows, d), F32).at[:b].set(c).at[b].set(c_ctx)
    mod_all = _mod_table(cc, mod_w, mod_b)
    rope_tab = _rope_tables(seq, ctx_len)
    for i in range(depth):
        need_ctx = i < depth - 1
        lat = mod_all[i, :b].reshape(b, 1, 6, d)
        cm = jnp.broadcast_to(mod_all[i, b].reshape(1, 1, 6, d), (b, 1, 6, d))
        modtab = jnp.concatenate([cm, lat], axis=1)
        p = _layer_params(w_in[i], m_conv[i], m_gate_b[i], m_norm[i], a_qnorm[i], a_knorm[i], l_cq_norm[i],
                          l_ckv_norm[i], l_wuq[i], l_wukv[i], l_qnorm[i], l_knorm[i], w_out[i])
        mqk, mo, misc, q_all, k_all, mv_t, v_all = _inproj(
            xs, modtab, norm_mix[i][None, :], p["w_in_p"], rope_tab, p["an"], p["ln"], p["cqn"], p["ckvn"],
            p["wuq_p"], p["wukv_p"], ctx_len)
        q_t, k_c = _conv(mqk, p["conv_w"], p["conv_scale"], ctx_len)
        hf, hb = _mlstm(q_t, k_c, mv_t, misc, p["gate_b"])
        gqa_kv = tuple(h // (A_HEADS // A_KV) for h in range(A_HEADS))
        o_gqa = _attention(q_all, k_all, v_all, ctx_len, need_ctx, 0, KV_GQA, A_KV, gqa_kv)
        o_mla = _attention(q_all, k_all, v_all, ctx_len, need_ctx, Q_MLA, 0, L_HEADS, tuple(range(L_HEADS)))
        xs = _mixout(xs, modtab, hf, hb, mo, o_gqa, o_mla, p["m_norm_p"], p["w_out_p"], ctx_len, not need_ctx)
        ffn_ctx = ctx_len if need_ctx else 0
        j = i // 2
        if i % 2 == 0:
            xs = _ffn(xs, modtab, norm_ffn[i][None, :], ffn_wg[j].astype(BF16), ffn_wu[j].astype(BF16),
                      ffn_wd[j].astype(BF16), ffn_ctx)
        else:
            router_p = jnp.pad(moe_router[j], ((0, 0), (0, LANE - N_EXPERTS)))
            xs = _moe(xs, modtab, norm_ffn[i][None, :], router_p, moe_wg[j].astype(BF16),
                      moe_wu[j].astype(BF16), moe_wd[j].astype(BF16), ffn_ctx)
    return xs
```

```python
import functools
import math

import numpy as np
import jax
import jax.numpy as jnp
from jax import lax
from jax.experimental import pallas as pl
from jax.experimental.pallas import tpu as pltpu

F32 = jnp.float32
BF16 = jnp.bfloat16

GRID_W = 64
EPS = 1e-6
ROPE_THETA = 10000.0
M_HEADS, M_DH = 4, 96
A_HEADS, A_KV, A_DH = 6, 2, 64
L_HEADS, L_NOPE, L_ROPE, L_DV = 4, 64, 32, 64
L_QK = L_NOPE + L_ROPE
L_QRANK, L_KVRANK = 256, 128
N_EXPERTS, TOP_K = 8, 2
IN_SIZES = (384, 384, 384, 384, 16, 384, 128, 128, 256, 128, 32)

LANE = 128
SLOT = LANE
N_KVHEADS = A_KV + L_HEADS
Q_MLA = 8
Q_SLOTS = Q_MLA + L_HEADS
KV_GQA = L_HEADS
VMEM_LIMIT = 56 * 1024 * 1024

O_MQK, O_MV, O_MO, O_MISC = 0, 1024, 1536, 2048
O_QA, O_KA, O_VA, O_CQ, O_CKV, IN_PAD = 2176, 2944, 3200, 3456, 3712, 3840
MISC_KR = 64
DEN_LANE = M_DH
SUM_LANE = A_DH
LOG2E = math.log2(math.e)
assert A_DH == L_DV and SUM_LANE < SLOT

TB = 256
TQ = 256
TK = 512
TM_FFN = 1088
TF = 896
ML = 256
RT = 144
ST = 2 * LANE
TF_MOE = 1792


def _cparams(sem):
    return pltpu.CompilerParams(dimension_semantics=sem, vmem_limit_bytes=VMEM_LIMIT)


def _sigmoid(x):
    return 1.0 / (1.0 + jnp.exp(-x))


def _silu(x):
    return x * _sigmoid(x)


def _lane_iota(shape):
    return lax.broadcasted_iota(jnp.int32, shape, len(shape) - 1)


def _row_iota(shape):
    return lax.broadcasted_iota(jnp.int32, shape, 0)


def _modulated(x, gain, mod_ref, t0, ctx_len, k_shift, k_scale):
    tm = x.shape[0]
    is_ctx = (t0 + _row_iota((tm, 1))) < ctx_len
    shift = jnp.where(is_ctx, mod_ref[0, k_shift:k_shift + 1, :], mod_ref[1, k_shift:k_shift + 1, :])
    scale = jnp.where(is_ctx, mod_ref[0, k_scale:k_scale + 1, :], mod_ref[1, k_scale:k_scale + 1, :])
    y = x * lax.rsqrt(jnp.mean(x * x, axis=-1, keepdims=True) + EPS) * gain
    return y * (1.0 + scale) + shift


def _gate_rows(mod_ref, t0, tm, ctx_len, k_gate):
    is_ctx = (t0 + _row_iota((tm, 1))) < ctx_len
    return jnp.where(is_ctx, mod_ref[0, k_gate:k_gate + 1, :], mod_ref[1, k_gate:k_gate + 1, :])


def _mod_kernel(c_ref, w_ref, b_ref, o_ref):
    s = _silu(c_ref[...]).astype(BF16)
    o_ref[...] = jnp.dot(s, w_ref[...].astype(BF16), preferred_element_type=F32) + b_ref[...]


def _mod_table(cc, mod_w, mod_b):
    depth, d, n = mod_w.shape
    rows = cc.shape[0]
    return pl.pallas_call(
        _mod_kernel,
        grid=(depth, n // d),
        in_specs=[pl.BlockSpec((rows, d), lambda l, j: (0, 0)),
                  pl.BlockSpec((None, d, d), lambda l, j: (l, 0, j)),
                  pl.BlockSpec((None, 1, d), lambda l, j: (l, 0, j))],
        out_specs=pl.BlockSpec((None, rows, d), lambda l, j: (l, 0, j)),
        out_shape=jax.ShapeDtypeStruct((depth, rows, n), F32),
        compiler_params=_cparams(("arbitrary", "arbitrary")),
        name="mod_table",
    )(cc, mod_w, mod_b.reshape(depth, 1, n))


def _slot_rms(x, lo, hi):
    lane = _lane_iota(x.shape)
    sq = jnp.where((lane >= lo) & (lane < hi), x * x, 0.0)
    return lax.rsqrt(jnp.sum(sq, axis=-1, keepdims=True) * (1.0 / (hi - lo)) + EPS)


def _rope(x, cos, sin_m, sin_p, quarter):
    return (x * cos + pltpu.roll(x, LANE - quarter, 1) * sin_m + pltpu.roll(x, quarter, 1) * sin_p)


def _inproj_kernel(ctx_len, x_ref, mod_ref, g_ref, w_ref, rope_ref, an_ref, ln_ref, cqn_ref, ckvn_ref,
                   wuq_ref, wukv_ref, mqk_ref, mo_ref, misc_ref, q_ref, k_ref, mvt_ref, vt_ref):
    tm = x_ref.shape[0]
    t0 = pl.program_id(1) * tm
    h = _modulated(x_ref[...], g_ref[...], mod_ref, t0, ctx_len, 0, 1).astype(BF16)

    def proj(a, b):
        return jnp.dot(h, w_ref[:, a:b], preferred_element_type=F32)

    mqk_ref[...] = proj(O_MQK, O_MV).astype(BF16)
    mv = proj(O_MV, O_MO)
    mvt_ref[...] = jnp.where(_lane_iota(mv.shape) % SLOT == DEN_LANE, 1.0, mv).T.astype(BF16)
    mo_ref[...] = proj(O_MO, O_MISC).astype(BF16)
    misc = proj(O_MISC, O_QA)
    misc_ref[...] = misc

    cos_a, sinm_a, sinp_a = rope_ref[0], rope_ref[1], rope_ref[2]
    cos_l, sinm_l, sinp_l = rope_ref[3], rope_ref[4], rope_ref[5]
    a_scale = A_DH ** -0.5 * LOG2E
    l_scale = L_QK ** -0.5 * LOG2E

    def with_ones_t(v):
        return jnp.where(_lane_iota(v.shape) % SLOT == SUM_LANE, 1.0, v).T.astype(BF16)

    pa = proj(O_QA, O_CQ)
    gq, gk = an_ref[0:1, :], an_ref[1:2, :]
    for i in range(A_HEADS):
        x = pa[:, i * SLOT:(i + 1) * SLOT]
        x = x * _slot_rms(x, 0, A_DH) * gq
        q_ref[:, i * SLOT:(i + 1) * SLOT] = (_rope(x, cos_a, sinm_a, sinp_a, A_DH // 4) * a_scale).astype(BF16)
    for i in range(A_KV):
        x = pa[:, (A_HEADS + i) * SLOT:(A_HEADS + i + 1) * SLOT]
        x = x * _slot_rms(x, 0, A_DH) * gk
        k_ref[:, (KV_GQA + i) * SLOT:(KV_GQA + i + 1) * SLOT] = (
            _rope(x, cos_a, sinm_a, sinp_a, A_DH // 4).astype(BF16))
    vt_ref[KV_GQA * SLOT:, :] = with_ones_t(pa[:, (A_HEADS + A_KV) * SLOT:(A_HEADS + 2 * A_KV) * SLOT])
    q_ref[:, A_HEADS * SLOT:Q_MLA * SLOT] = jnp.zeros((tm, (Q_MLA - A_HEADS) * SLOT), BF16)

    pc = proj(O_CQ, IN_PAD)
    cq = pc[:, 0:L_QRANK]
    cq = (cq * lax.rsqrt(jnp.mean(cq * cq, axis=-1, keepdims=True) + EPS) * cqn_ref[...]).astype(BF16)
    ckv = pc[:, L_QRANK:L_QRANK + L_KVRANK]
    ckv = (ckv * lax.rsqrt(jnp.mean(ckv * ckv, axis=-1, keepdims=True) + EPS) * ckvn_ref[...]).astype(BF16)
    ql = jnp.dot(cq, wuq_ref[...], preferred_element_type=F32)
    kvl = jnp.dot(ckv, wukv_ref[...], preferred_element_type=F32)
    gq_l, gk_l = ln_ref[0:1, :], ln_ref[1:2, :]
    lane = _lane_iota((tm, SLOT))
    kr = jnp.where((lane >= MISC_KR) & (lane < MISC_KR + L_ROPE), misc, 0.0)
    kr = kr * _slot_rms(kr, MISC_KR, MISC_KR + L_ROPE) * gk_l
    kr = _rope(kr, cos_l, sinm_l, sinp_l, L_ROPE // 4)
    for i in range(L_HEADS):
        x = ql[:, i * SLOT:(i + 1) * SLOT]
        inv = jnp.where(lane < L_NOPE, _slot_rms(x, 0, L_NOPE), _slot_rms(x, L_NOPE, L_QK))
        x = x * inv * gq_l
        q_ref[:, (Q_MLA + i) * SLOT:(Q_MLA + i + 1) * SLOT] = (
            _rope(x, cos_l, sinm_l, sinp_l, L_ROPE // 4) * l_scale).astype(BF16)
        kn = kvl[:, i * SLOT:(i + 1) * SLOT]
        kn = kn * _slot_rms(kn, 0, L_NOPE) * gk_l
        k_ref[:, i * SLOT:(i + 1) * SLOT] = (kn + kr).astype(BF16)
    vt_ref[0:KV_GQA * SLOT, :] = with_ones_t(kvl[:, L_HEADS * SLOT:])


def _inproj(xs, modtab, gain, w_in_p, rope_tab, an, ln, cqn, ckvn, wuq_p, wukv_p, ctx_len):
    b, s, d = xs.shape
    grid = (b, s // TB)
    tok = lambda n: pl.BlockSpec((None, TB, n), lambda i, j: (i, j, 0))
    full = lambda a: pl.BlockSpec(a.shape, lambda i, j: (0,) * a.ndim)
    out_widths = (1024, 512, SLOT, Q_SLOTS * SLOT, N_KVHEADS * SLOT)
    out_dtypes = (BF16, BF16, F32, BF16, BF16)
    out_specs = [tok(n) for n in out_widths]
    out_shape = [jax.ShapeDtypeStruct((b, s, n), dt) for n, dt in zip(out_widths, out_dtypes)]
    for n in (M_HEADS * SLOT, N_KVHEADS * SLOT):
        out_specs.append(pl.BlockSpec((None, n, TB), lambda i, j: (i, 0, j)))
        out_shape.append(jax.ShapeDtypeStruct((b, n, s), BF16))
    return pl.pallas_call(
        functools.partial(_inproj_kernel, ctx_len),
        grid=grid,
        in_specs=[tok(d),
                  pl.BlockSpec((None, 2, 6, d), lambda i, j: (i, 0, 0, 0)),
                  full(gain), full(w_in_p),
                  pl.BlockSpec((6, TB, SLOT), lambda i, j: (0, j, 0)),
                  full(an), full(ln), full(cqn), full(ckvn), full(wuq_p), full(wukv_p)],
        out_specs=out_specs,
        out_shape=out_shape,
        compiler_params=_cparams(("parallel", "parallel")),
        name="inproj",
    )(xs, modtab, gain, w_in_p, rope_tab, an, ln, cqn, ckvn, wuq_p, wukv_p)


def _conv_kernel(ctx_len, s_len, x_ref, prev_ref, next_ref, w_ref, sc_ref, qt_ref, k_ref):
    tm = x_ref.shape[0]
    t0 = pl.program_id(1) * tm
    x = x_ref[...].astype(F32)
    row = _row_iota((tm, 1))
    has_prev = jnp.logical_and(t0 != 0, t0 != ctx_len)
    has_next = jnp.logical_and(t0 + tm != ctx_len, t0 + tm != s_len)
    hp = jnp.where(has_prev, prev_ref[15:16, :].astype(F32), 0.0)
    hn = jnp.where(has_next, next_ref[0:1, :].astype(F32), 0.0)
    xp = jnp.where(row == 0, hp, pltpu.roll(x, 1, 0))
    xn = jnp.where(row == tm - 1, hn, pltpu.roll(x, tm - 1, 0))
    y = xp * w_ref[0:1, :] + x * w_ref[1:2, :] + xn * w_ref[2:3, :]
    y = _silu(y) * sc_ref[...]
    half = y.shape[1] // 2
    qt_ref[...] = y[:, :half].T.astype(qt_ref.dtype)
    k_ref[...] = y[:, half:].astype(k_ref.dtype)


def _conv(mqk, conv_w, conv_scale, ctx_len):
    b, s, n = mqk.shape
    hb = TB // 16
    last = s // 16 - 1
    return pl.pallas_call(
        functools.partial(_conv_kernel, ctx_len, s),
        grid=(b, s // TB),
        in_specs=[pl.BlockSpec((None, TB, n), lambda i, j: (i, j, 0)),
                  pl.BlockSpec((None, 16, n), lambda i, j: (i, jnp.maximum(j * hb - 1, 0), 0)),
                  pl.BlockSpec((None, 16, n), lambda i, j: (i, jnp.minimum((j + 1) * hb, last), 0)),
                  pl.BlockSpec((3, n), lambda i, j: (0, 0)),
                  pl.BlockSpec((1, n), lambda i, j: (0, 0))],
        out_specs=[pl.BlockSpec((None, n // 2, TB), lambda i, j: (i, 0, j)),
                   pl.BlockSpec((None, TB, n // 2), lambda i, j: (i, j, 0))],
        out_shape=[jax.ShapeDtypeStruct((b, n // 2, s), BF16), jax.ShapeDtypeStruct((b, s, n // 2), BF16)],
        compiler_params=_cparams(("parallel", "parallel")),
        name="mlstm_conv",
    )(mqk, mqk, mqk, conv_w, conv_scale)


def _log_sigmoid(x):
    return jnp.minimum(x, 0.0) - jnp.log(1.0 + jnp.exp(-jnp.abs(x)))


def _mlstm_kernel(n_chunks, qt_ref, k_ref, vt_ref, misc_ref, gb_ref, hf_ref, hb_ref, c_ref, m_ref, gt_sc, bt_sc):
    hp = pl.program_id(1)
    c_ref[...] = jnp.zeros_like(c_ref)
    m_ref[...] = jnp.zeros_like(m_ref)
    r = _row_iota((ML, ML))
    cidx = _lane_iota((ML, ML))
    tri_f32 = (r >= cidx).astype(F32)
    tri_b32 = (r <= cidx).astype(F32)
    lane = _lane_iota((ML, SLOT))
    lane_t = _lane_iota((1, ML))

    def chunk(t0, d, tri32, out_ref):
        g = misc_ref[pl.ds(t0, ML), :] + gb_ref[...]
        logf = _log_sigmoid(g)
        bcum = jnp.dot(tri32, logf, preferred_element_type=F32, precision=lax.Precision.HIGHEST)
        gt_sc[d] = g.T
        bt_sc[d] = bcum.T
        ib = g - pltpu.roll(bcum, SLOT - M_HEADS, 1)
        allowed = (r <= cidx) if d == 0 else (r >= cidx)
        for hh in range(2):
            li = d * 2 * M_HEADS + hp * 2 + hh
            ib_col = jnp.sum(jnp.where(lane == li, ib, 0.0), axis=-1, keepdims=True)
            i_row = gt_sc[d, pl.ds(li, 1), :]
            b_row = bt_sc[d, pl.ds(li + M_HEADS, 1), :]
            sidx = d * 2 + hh
            ct_st = c_ref[sidx]
            m_s = m_ref[sidx]
            k = k_ref[pl.ds(t0, ML), hh * SLOT:(hh + 1) * SLOT]
            qt = qt_ref[hh * SLOT:(hh + 1) * SLOT, pl.ds(t0, ML)]
            vt = vt_ref[hh * SLOT:(hh + 1) * SLOT, pl.ds(t0, ML)]
            dmat = jnp.where(allowed, b_row + ib_col, -jnp.inf)
            m_inter = b_row + m_s
            m_t = jnp.maximum(m_inter, jnp.max(dmat, axis=0, keepdims=True))
            kq = jnp.dot(k, qt, preferred_element_type=F32)
            w = (jnp.exp(dmat - m_t) * kq).astype(BF16)
            a_inter = jnp.exp(m_inter - m_t)
            num = a_inter * jnp.dot(ct_st.astype(BF16), qt, preferred_element_type=F32) + jnp.dot(
                vt, w, preferred_element_type=F32)
            den = num[DEN_LANE:DEN_LANE + 1, :]
            h_out = num / jnp.maximum(jnp.abs(den), jnp.exp(-m_t))
            out_ref[pl.ds(t0, ML), hh * SLOT:(hh + 1) * SLOT] = h_out.T.astype(out_ref.dtype)
            last = ML - 1 if d == 0 else 0
            total = jnp.sum(jnp.where(lane_t == last, b_row, 0.0), axis=-1, keepdims=True)
            gg = total - b_row + i_row
            m_new = jnp.maximum(total + m_s, jnp.max(gg, axis=-1, keepdims=True))
            decay = jnp.exp(total + m_s - m_new)[:, 0:1]
            wk = jnp.exp(gg - m_new)
            vw = (vt.astype(F32) * wk).astype(BF16)
            c_ref[sidx] = decay * ct_st + jnp.dot(vw, k, preferred_element_type=F32)
            m_ref[sidx] = m_new

    def step(n, carry):
        tf0 = pl.multiple_of(n * ML, ML)
        tb0 = pl.multiple_of(jnp.where(n == 0, 0, n_chunks - n) * ML, ML)
        chunk(tf0, 0, tri_f32, hf_ref)
        chunk(tb0, 1, tri_b32, hb_ref)
        return carry

    lax.fori_loop(0, n_chunks, step, 0)


def _mlstm(q_t, k_c, mv_t, misc, gate_b):
    b, s, _ = k_c.shape
    n_chunks = s // ML
    pair = 2 * SLOT
    spec = pl.BlockSpec((None, s, pair), lambda i, j: (i, 0, j))
    spec_t = pl.BlockSpec((None, pair, s), lambda i, j: (i, j, 0))
    return pl.pallas_call(
        functools.partial(_mlstm_kernel, n_chunks),
        grid=(b, M_HEADS // 2),
        in_specs=[spec_t, spec, spec_t,
                  pl.BlockSpec((None, s, SLOT), lambda i, j: (i, 0, 0)),
                  pl.BlockSpec((1, SLOT), lambda i, j: (0, 0))],
        out_specs=[spec, spec],
        out_shape=[jax.ShapeDtypeStruct((b, s, M_HEADS * SLOT), BF16)] * 2,
        scratch_shapes=[pltpu.VMEM((4, SLOT, SLOT), F32), pltpu.VMEM((4, 1, ML), F32),
                        pltpu.VMEM((2, SLOT, ML), F32), pltpu.VMEM((2, SLOT, ML), F32)],
        compiler_params=_cparams(("parallel", "parallel")),
        name="mlstm_scan",
    )(q_t, k_c, mv_t, misc, gate_b)


def _attn_kernel(ctx_len, q_off, kv_of, q_ref, k_ref, vt_ref, o_ref, st_sc, pt_sc, acc_sc):
    s_len = k_ref.shape[0]
    n_heads = len(kv_of)
    qs = [q_ref[:, g * SLOT:(g + 1) * SLOT] for g in range(n_heads)]

    def kv_cols(g):
        return slice(kv_of[g] * SLOT, (kv_of[g] + 1) * SLOT)

    def chunk(carry, rows):
        out = []
        for g in range(n_heads):
            m, acc = carry[g]
            k = k_ref[rows, kv_cols(g)]
            vt = vt_ref[kv_cols(g), rows]
            st = lax.dot_general(k, qs[g], (((1,), (1,)), ((), ())), preferred_element_type=F32)
            m_new = jnp.maximum(m, jnp.max(st, axis=0, keepdims=True))
            alpha = jnp.exp2(m - m_new)
            pt = jnp.exp2(st - m_new).astype(BF16)
            out.append((m_new, alpha * acc + jnp.dot(vt, pt, preferred_element_type=F32)))
        return tuple(out)

    def finish(carry):
        for g in range(n_heads):
            acc = carry[g][1]
            o = acc / acc[SUM_LANE:SUM_LANE + 1, :]
            o_ref[:, g * SLOT:(g + 1) * SLOT] = o.T.astype(o_ref.dtype)

    init = tuple((jnp.full((1, TQ), -jnp.inf, F32), jnp.zeros((SLOT, TQ), F32)) for _ in range(n_heads))
    is_ctx = (pl.program_id(2) + q_off) * TQ < ctx_len

    @pl.when(is_ctx)
    def _():
        finish(chunk(init, pl.ds(0, ctx_len)))

    n_lat = (s_len - ctx_len) // TK
    n0 = ctx_len + TK
    assert n_lat >= 4 and n_lat % 2 == 0 and st_sc.shape[1] == n0

    def rows_of(c):
        if isinstance(c, int):
            return (pl.ds(0, n0), n0) if c == 0 else (pl.ds(ctx_len + c * TK, TK), TK)
        return pl.ds(pl.multiple_of(ctx_len + c * TK, math.gcd(ctx_len, TK)), TK), TK

    def scores(c, par):
        rows, n = rows_of(c)
        for g in range(n_heads):
            st_sc[par * n_heads + g, 0:n, :] = lax.dot_general(
                k_ref[rows, kv_cols(g)], qs[g], (((1,), (1,)), ((), ())), preferred_element_type=F32)

    def softmax(par, m, n=TK):
        ms, alphas = [], []
        for g in range(n_heads):
            st = st_sc[par * n_heads + g, 0:n, :]
            m_new = jnp.maximum(m[g], jnp.max(st, axis=0, keepdims=True))
            pt_sc[par * n_heads + g, 0:n, :] = jnp.exp2(st - m_new).astype(BF16)
            ms.append(m_new)
            alphas.append(jnp.exp2(m[g] - m_new))
        return tuple(ms), tuple(alphas)

    def values(c, par, alpha):
        rows, n = rows_of(c)
        for g in range(n_heads):
            acc_sc[g] = alpha[g] * acc_sc[g] + jnp.dot(vt_ref[kv_cols(g), rows], pt_sc[par * n_heads + g, 0:n, :],
                                                       preferred_element_type=F32)

    def stage(c, par, m, alpha):
        scores(c + 1, 1 - par)
        m_new, alpha_new = softmax(par, m)
        values(c - 1, 1 - par, alpha)
        return m_new, alpha_new

    @pl.when(jnp.logical_not(is_ctx))
    def _():
        for g in range(n_heads):
            acc_sc[g] = init[g][1]
        scores(0, 0)
        m, alpha = softmax(0, tuple(c[0] for c in init), n0)
        scores(1, 1)
        m, alpha = stage(1, 1, m, alpha)

        def body(t, carry):
            c = 2 * t + 2
            return stage(c + 1, 1, *stage(c, 0, *carry))

        m, alpha = lax.fori_loop(0, (n_lat - 4) // 2, body, (m, alpha))
        m, alpha = stage(n_lat - 2, 0, m, alpha)
        values(n_lat - 2, 0, alpha)
        m, alpha = softmax(1, m)
        values(n_lat - 1, 1, alpha)
        finish(tuple((m[g], acc_sc[g]) for g in range(n_heads)))


def _attention(q_all, k_all, vt_all, ctx_len, need_ctx, q_slot0, kv_slot0, n_kv, kv_of):
    b, s, _ = q_all.shape
    q_off = 0 if need_ctx else ctx_len // TQ
    nq = s // TQ - q_off
    n_heads = len(kv_of)
    qw = n_heads * SLOT
    kw = n_kv * SLOT
    assert (q_slot0 * SLOT) % qw == 0 and (kv_slot0 * SLOT) % kw == 0
    qb0, kb0 = q_slot0 * SLOT // qw, kv_slot0 * SLOT // kw
    return pl.pallas_call(
        functools.partial(_attn_kernel, ctx_len, q_off, kv_of),
        grid=(b, 1, nq),
        in_specs=[pl.BlockSpec((None, TQ, qw), lambda i, g, j: (i, j + q_off, qb0)),
                  pl.BlockSpec((None, s, kw), lambda i, g, j: (i, 0, kb0)),
                  pl.BlockSpec((None, kw, s), lambda i, g, j: (i, kb0, 0))],
        out_specs=pl.BlockSpec((None, TQ, qw), lambda i, g, j: (i, j + q_off, 0)),
        out_shape=jax.ShapeDtypeStruct((b, s, qw), BF16),
        scratch_shapes=[pltpu.VMEM((2 * n_heads, ctx_len + TK, TQ), F32),
                        pltpu.VMEM((2 * n_heads, ctx_len + TK, TQ), BF16),
                        pltpu.VMEM((n_heads, SLOT, TQ), F32)],
        compiler_params=_cparams(("parallel", "parallel", "parallel")),
        name="attention",
    )(q_all, k_all, vt_all)


def _mixout_kernel(ctx_len, blk_off, x_ref, mod_ref, hf_ref, hb_ref, mo_ref, oa_ref, ol_ref, mn_ref, w_ref, o_ref):
    tm = x_ref.shape[0]
    t0 = (pl.program_id(1) + blk_off) * tm
    hm = hf_ref[...].astype(F32) + hb_ref[...].astype(F32)
    gate = _sigmoid(mo_ref[...].astype(F32))
    parts = []
    for i in range(M_HEADS):
        x = hm[:, i * SLOT:(i + 1) * SLOT]
        inv = lax.rsqrt(jnp.sum(x * x, axis=-1, keepdims=True) * (1.0 / M_DH) + EPS)
        parts.append(x * inv)
    hn = jnp.concatenate(parts, axis=-1) * mn_ref[...] * gate
    nm = M_HEADS * SLOT
    o = jnp.dot(hn.astype(BF16), w_ref[0:nm, :], preferred_element_type=F32)
    na = nm + A_HEADS * SLOT
    o = o + jnp.dot(oa_ref[...], w_ref[nm:na, :], preferred_element_type=F32)
    o = o + jnp.dot(ol_ref[...], w_ref[na:, :], preferred_element_type=F32)
    o_ref[...] = x_ref[...] + _gate_rows(mod_ref, t0, tm, ctx_len, 2) * o


def _mixout(xs, modtab, hf, hb, mo, o_gqa, o_mla, m_norm_p, w_out_p, ctx_len, lat_only):
    b, s, d = xs.shape
    blk_off = ctx_len // TB if lat_only else 0
    tok = lambda n: pl.BlockSpec((None, TB, n), lambda i, j: (i, j + blk_off, 0))
    full = lambda a: pl.BlockSpec(a.shape, lambda i, j: (0,) * a.ndim)
    return pl.pallas_call(
        functools.partial(_mixout_kernel, ctx_len, blk_off),
        grid=(b, s // TB - blk_off),
        in_specs=[tok(d),
                  pl.BlockSpec((None, 2, 6, d), lambda i, j: (i, 0, 0, 0)),
                  tok(M_HEADS * SLOT), tok(M_HEADS * SLOT), tok(M_HEADS * SLOT), tok(A_HEADS * SLOT),
                  tok(L_HEADS * SLOT), full(m_norm_p), full(w_out_p)],
        out_specs=pl.BlockSpec((None, TB, d), lambda i, j: (i, j, 0)),
        out_shape=jax.ShapeDtypeStruct((b, s - blk_off * TB, d), F32),
        input_output_aliases={} if lat_only else {0: 0},
        compiler_params=_cparams(("parallel", "parallel")),
        name="mix_out",
    )(xs, modtab, hf, hb, mo, o_gqa, o_mla, m_norm_p, w_out_p)


def _ffn_kernel(ctx_len, x_ref, mod_ref, g_ref, wg_ref, wu_ref, wd_ref, o_ref, h_sc, acc_sc):
    tm = x_ref.shape[0]
    t0 = pl.program_id(1) * tm
    f = pl.program_id(2)

    @pl.when(f == 0)
    def _():
        h_sc[...] = _modulated(x_ref[...], g_ref[...], mod_ref, t0, ctx_len, 3, 4).astype(BF16)
        acc_sc[...] = jnp.zeros_like(acc_sc)

    h = h_sc[...]
    a = jnp.dot(h, wg_ref[...], preferred_element_type=F32)
    u = jnp.dot(h, wu_ref[...], preferred_element_type=F32)
    acc_sc[...] += jnp.dot((_silu(a) * u).astype(BF16), wd_ref[...], preferred_element_type=F32)

    @pl.when(f == pl.num_programs(2) - 1)
    def _():
        o_ref[...] = x_ref[...] + _gate_rows(mod_ref, t0, tm, ctx_len, 5) * acc_sc[...]


def _ffn(xs, modtab, gain, wg, wu, wd, ctx_len):
    b, s, d = xs.shape
    tm = s // 4
    nf = wg.shape[1] // TF
    return pl.pallas_call(
        functools.partial(_ffn_kernel, ctx_len),
        grid=(b, s // tm, nf),
        in_specs=[pl.BlockSpec((None, tm, d), lambda i, j, f: (i, j, 0)),
                  pl.BlockSpec((None, 2, 6, d), lambda i, j, f: (i, 0, 0, 0)),
                  pl.BlockSpec((1, d), lambda i, j, f: (0, 0)),
                  pl.BlockSpec((d, TF), lambda i, j, f: (0, f)),
                  pl.BlockSpec((d, TF), lambda i, j, f: (0, f)),
                  pl.BlockSpec((TF, d), lambda i, j, f: (f, 0))],
        out_specs=pl.BlockSpec((None, tm, d), lambda i, j, f: (i, j, 0)),
        out_shape=jax.ShapeDtypeStruct((b, s, d), F32),
        scratch_shapes=[pltpu.VMEM((tm, d), BF16), pltpu.VMEM((tm, d), F32)],
        input_output_aliases={0: 0},
        compiler_params=_cparams(("parallel", "parallel", "arbitrary")),
        name="ffn_dense",
    )(xs, modtab, gain, wg, wu, wd)


def _top2_combine(logits):
    lane = _lane_iota(logits.shape)
    lane_f = lane.astype(F32)
    lg = jnp.where(lane < N_EXPERTS, logits, -jnp.inf)
    v1 = jnp.max(lg, axis=-1, keepdims=True)
    i1 = jnp.min(jnp.where(lg == v1, lane_f, float(LANE)), axis=-1, keepdims=True)
    rest = jnp.where(lane_f == i1, -jnp.inf, lg)
    v2 = jnp.max(rest, axis=-1, keepdims=True)
    i2 = jnp.min(jnp.where(rest == v2, lane_f, float(LANE)), axis=-1, keepdims=True)
    e2 = jnp.exp(v2 - v1)
    w1 = 1.0 / (1.0 + e2)
    w2 = e2 / (1.0 + e2)
    sel = jnp.where((lane_f == i1) | (lane_f == i2), 1.0, 0.0)
    return jnp.where(lane_f == i1, w1, 0.0) + jnp.where(lane_f == i2, w2, 0.0), sel


def _moe_kernel(ctx_len, n_grp, x_ref, mod_ref, g_ref, r_ref, tri_ref, wg_ref, wu_ref, wd_ref, o_ref,
                h_sc, comb_sc, rank_sc, rankt_sc, rankc_sc, wc_sc, xg_sc, acc_sc, nt_sc):
    tm = x_ref.shape[0]
    grp = tm // n_grp
    t0 = pl.program_id(1) * tm
    e = pl.program_id(2)
    f = pl.program_id(3)

    @pl.when(jnp.logical_and(e == 0, f == 0))
    def _():
        h = _modulated(x_ref[...], g_ref[...], mod_ref, t0, ctx_len, 3, 4)
        logits = jnp.dot(h, r_ref[...], preferred_element_type=F32, precision=lax.Precision.HIGHEST)
        comb, sel = _top2_combine(logits)
        sel16 = sel.astype(BF16)
        rank = jnp.concatenate(
            [jnp.dot(tri_ref[...], sel16[g * grp:(g + 1) * grp], preferred_element_type=F32)
             for g in range(n_grp)], axis=0)
        rank = jnp.where(sel > 0.0, rank, -1.0)
        comb_sc[...] = comb
        rank_sc[...] = rank
        rankt_sc[...] = rank.T
        h_sc[...] = h.astype(BF16)
        o_ref[...] = jnp.zeros_like(o_ref)

    @pl.when(f == 0)
    def _():
        lane = _lane_iota((tm, LANE))
        rank_c = jnp.sum(jnp.where(lane == e, rank_sc[...], 0.0), axis=-1, keepdims=True)
        rankc_sc[...] = rank_c
        wc_sc[...] = jnp.sum(jnp.where(lane == e, comb_sc[...], 0.0), axis=-1, keepdims=True)
        base = jnp.int32(0)
        for g in range(n_grp):
            n_rows = jnp.sum(jnp.where(rank_c[g * grp:(g + 1) * grp] >= 0.0, 1.0, 0.0)).astype(jnp.int32)
            n_rt = lax.div(n_rows + (RT - 1), RT)
            nt_sc[1 + 2 * g] = base
            nt_sc[2 + 2 * g] = lax.div(n_rt * RT + (ST - 1), ST)
            rank_r = rankt_sc[pl.ds(e, 1), g * grp:(g + 1) * grp]
            h_g = h_sc[g * grp:(g + 1) * grp, :]

            def gather(i, c, base=base, rank_r=rank_r, h_g=h_g):
                r0 = pl.multiple_of((base + i) * RT, 16)
                tgt = (i * RT + _row_iota((RT, 1))).astype(F32)
                sel_t = jnp.where(rank_r == tgt, 1.0, 0.0).astype(BF16)
                xg_sc[pl.ds(r0, RT), :] = jnp.dot(sel_t, h_g, preferred_element_type=F32).astype(BF16)
                return c

            lax.fori_loop(0, n_rt, gather, 0)
            base = base + n_rt
        nt_sc[0] = base

        def clear(i, c):
            acc_sc[pl.ds(pl.multiple_of(i * ST, ST), ST), :] = jnp.zeros((ST, acc_sc.shape[1]), F32)
            return c

        lax.fori_loop(0, lax.div(base * RT + (2 * ST - 1), ST), clear, 0)

    def expert(i, c):
        r0 = pl.multiple_of(i * RT, 16)
        rows = xg_sc[pl.ds(r0, RT), :]
        a = jnp.dot(rows, wg_ref[...], preferred_element_type=F32)
        u = jnp.dot(rows, wu_ref[...], preferred_element_type=F32)
        acc_sc[pl.ds(r0, RT), :] += jnp.dot((_silu(a) * u).astype(BF16), wd_ref[...],
                                            preferred_element_type=F32)
        return c

    lax.fori_loop(0, nt_sc[0], expert, 0)

    @pl.when(f == pl.num_programs(3) - 1)
    def _():
        for g in range(n_grp):
            row0 = nt_sc[1 + 2 * g] * RT
            rank_g = rankc_sc[g * grp:(g + 1) * grp, :]
            w_g = wc_sc[g * grp:(g + 1) * grp, :]

            def scatter(i, c, g=g, row0=row0, rank_g=rank_g, w_g=w_g):
                r0 = pl.multiple_of(row0 + i * ST, 16)
                tgt = (i * ST + _lane_iota((1, ST))).astype(F32)
                w_t = jnp.where(rank_g == tgt, w_g, 0.0).astype(BF16)
                o_ref[g * grp:(g + 1) * grp, :] += jnp.dot(w_t, acc_sc[pl.ds(r0, ST), :].astype(BF16),
                                                           preferred_element_type=F32)
                return c

            lax.fori_loop(0, nt_sc[2 + 2 * g], scatter, 0)

    @pl.when(jnp.logical_and(e == pl.num_programs(2) - 1, f == pl.num_programs(3) - 1))
    def _():
        o_ref[...] = x_ref[...] + _gate_rows(mod_ref, t0, tm, ctx_len, 5) * o_ref[...]


def _moe(xs, modtab, gain, router_p, wg, wu, wd, ctx_len):
    b, s, d = xs.shape
    tm = s // 4
    assert tm % LANE == 0
    n_grp = 2 if tm % (2 * LANE) == 0 else 1
    grp = tm // n_grp
    tr = -(-(n_grp * -(-grp // RT) * RT + ST) // ST) * ST
    ne, _, dff = wg.shape
    tri = jnp.tril(jnp.ones((grp, grp), BF16), -1)
    TF = TF_MOE
    assert dff % TF == 0
    return pl.pallas_call(
        functools.partial(_moe_kernel, ctx_len, n_grp),
        grid=(b, s // tm, ne, dff // TF),
        in_specs=[pl.BlockSpec((None, tm, d), lambda i, j, e, f: (i, j, 0), pipeline_mode=pl.Buffered(1)),
                  pl.BlockSpec((None, 2, 6, d), lambda i, j, e, f: (i, 0, 0, 0)),
                  pl.BlockSpec((1, d), lambda i, j, e, f: (0, 0)),
                  pl.BlockSpec((d, LANE), lambda i, j, e, f: (0, 0)),
                  pl.BlockSpec((grp, grp), lambda i, j, e, f: (0, 0), pipeline_mode=pl.Buffered(1)),
                  pl.BlockSpec((None, d, TF), lambda i, j, e, f: (e, 0, f)),
                  pl.BlockSpec((None, d, TF), lambda i, j, e, f: (e, 0, f)),
                  pl.BlockSpec((None, TF, d), lambda i, j, e, f: (e, f, 0))],
        out_specs=pl.BlockSpec((None, tm, d), lambda i, j, e, f: (i, j, 0)),
        out_shape=jax.ShapeDtypeStruct((b, s, d), F32),
        scratch_shapes=[pltpu.VMEM((tm, d), BF16),
                        pltpu.VMEM((tm, LANE), F32),
                        pltpu.VMEM((tm, LANE), F32),
                        pltpu.VMEM((LANE, tm), F32),
                        pltpu.VMEM((tm, 1), F32),
                        pltpu.VMEM((tm, 1), F32),
                        pltpu.VMEM((tr, d), BF16),
                        pltpu.VMEM((tr, d), F32),
                        pltpu.SMEM((1 + 2 * n_grp,), jnp.int32)],
        input_output_aliases={0: 0},
        compiler_params=_cparams(("parallel", "parallel", "arbitrary", "arbitrary")),
        name="moe_top2",
    )(xs, modtab, gain, router_p, tri, wg, wu, wd)


def _pad_heads(w, n_heads, dh, axis=-1):
    axis = axis % w.ndim
    shp = w.shape[:axis] + (n_heads, dh) + w.shape[axis + 1:]
    pad = [(0, 0)] * (w.ndim + 1)
    pad[axis + 1] = (0, SLOT - dh)
    out = jnp.pad(w.reshape(shp), pad)
    return out.reshape(w.shape[:axis] + (n_heads * SLOT,) + w.shape[axis + 1:])


def _rope_tables(seq, ctx_len):
    t = jnp.arange(seq)
    rows = (t // GRID_W).astype(F32)
    cols = (t % GRID_W).astype(F32)

    def angles(rot_dim):
        nf = rot_dim // 4
        inv = ROPE_THETA ** (-jnp.arange(nf, dtype=F32) / nf)
        ar = rows[:, None] * inv
        ac = cols[:, None] * inv
        return jnp.concatenate([ar, ar, ac, ac], axis=-1)

    def slot_tables(rot_dim, lane0):
        ang = angles(rot_dim)
        quarter = rot_dim // 4
        first = (jnp.arange(rot_dim) % (2 * quarter)) < quarter
        cos = jnp.ones((seq, SLOT), F32).at[:, lane0:lane0 + rot_dim].set(jnp.cos(ang))
        sin = jnp.sin(ang)
        sin_m = jnp.zeros((seq, SLOT), F32).at[:, lane0:lane0 + rot_dim].set(jnp.where(first, -sin, 0.0))
        sin_p = jnp.zeros((seq, SLOT), F32).at[:, lane0:lane0 + rot_dim].set(jnp.where(first, 0.0, sin))
        ident = jnp.stack([jnp.ones((ctx_len, SLOT), F32), jnp.zeros((ctx_len, SLOT), F32),
                           jnp.zeros((ctx_len, SLOT), F32)])
        return jnp.concatenate([ident, jnp.stack([cos, sin_m, sin_p])], axis=1)

    return jnp.concatenate([slot_tables(A_DH, 0), slot_tables(L_ROPE, L_NOPE)], axis=0)


def _layer_params(w_in, m_conv, m_gate_b, m_norm, a_qnorm, a_knorm, l_cq_norm, l_ckv_norm, l_wuq, l_wukv,
                  l_qnorm, l_knorm, w_out):
    d = w_in.shape[0]
    offs = np.cumsum((0,) + IN_SIZES)
    seg = [w_in[:, offs[i]:offs[i + 1]] for i in range(len(IN_SIZES))]
    misc = jnp.zeros((d, SLOT), F32).at[:, 0:4 * M_HEADS].set(seg[4]).at[:, MISC_KR:MISC_KR + L_ROPE].set(seg[10])
    w_in_p = jnp.concatenate(
        [_pad_heads(seg[0], M_HEADS, M_DH), _pad_heads(seg[1], M_HEADS, M_DH),
         _pad_heads(seg[2], M_HEADS, M_DH), _pad_heads(seg[3], M_HEADS, M_DH), misc,
         _pad_heads(seg[5], A_HEADS, A_DH), _pad_heads(seg[6], A_KV, A_DH), _pad_heads(seg[7], A_KV, A_DH),
         seg[8], seg[9]], axis=1).astype(BF16)
    conv_w = jnp.concatenate([_pad_heads(m_conv[:, :M_HEADS * M_DH], M_HEADS, M_DH),
                              _pad_heads(m_conv[:, M_HEADS * M_DH:], M_HEADS, M_DH)], axis=1)
    conv_scale = jnp.concatenate([jnp.ones((1, M_HEADS * SLOT), F32),
                                  jnp.full((1, M_HEADS * SLOT), M_DH ** -0.5, F32)], axis=1)
    gate_b = jnp.zeros((1, SLOT), F32).at[0, 0:4 * M_HEADS].set(m_gate_b)
    pad1 = lambda g: jnp.pad(g, (0, SLOT - g.shape[0]))
    an = jnp.stack([pad1(a_qnorm), pad1(a_knorm)])
    ln = jnp.stack([pad1(l_qnorm), pad1(l_knorm)])
    wuq_p = _pad_heads(l_wuq, L_HEADS, L_QK).astype(BF16)
    kv = l_wukv.reshape(L_KVRANK, L_HEADS, L_NOPE + L_DV)
    wukv_p = jnp.concatenate(
        [_pad_heads(kv[:, :, :L_NOPE].reshape(L_KVRANK, -1), L_HEADS, L_NOPE),
         _pad_heads(kv[:, :, L_NOPE:].reshape(L_KVRANK, -1), L_HEADS, L_DV)], axis=1).astype(BF16)
    nm, na = M_HEADS * M_DH, A_HEADS * A_DH
    w_out_p = jnp.concatenate(
        [_pad_heads(w_out[:nm], M_HEADS, M_DH, axis=0), _pad_heads(w_out[nm:nm + na], A_HEADS, A_DH, axis=0),
         _pad_heads(w_out[nm + na:], L_HEADS, L_DV, axis=0)], axis=0).astype(BF16)
    m_norm_p = _pad_heads(m_norm[None, :], M_HEADS, M_DH)
    return dict(w_in_p=w_in_p, conv_w=conv_w, conv_scale=conv_scale, gate_b=gate_b, an=an, ln=ln,
                cqn=l_cq_norm[None, :], ckvn=l_ckv_norm[None, :], wuq_p=wuq_p, wukv_p=wukv_p,
                w_out_p=w_out_p, m_norm_p=m_norm_p)


def kernel(x, c, ctx, c_ctx, mod_w, mod_b, norm_mix, norm_ffn, w_in, m_conv, m_gate_b, m_norm, a_qnorm, a_knorm,
           l_cq_norm, l_ckv_norm, l_wuq, l_wukv, l_qnorm, l_knorm, w_out, ffn_wg, ffn_wu, ffn_wd, moe_router,
           moe_wg, moe_wu, moe_wd):
    b, seq, d = x.shape
    ctx_len = ctx.shape[1]
    depth = mod_w.shape[0]
    assert ctx_len % TB == 0 and ctx_len % TQ == 0 and ctx_len == ML and seq % TK == 0 and seq % GRID_W == 0
    xs = jnp.concatenate([ctx, x], axis=1)
    mod_rows = 16
    cc = jnp.zeros((mod_rows, d), F32).at[:b].set(c).at[b].set(c_ctx)
    mod_all = _mod_table(cc, mod_w, mod_b)
    rope_tab = _rope_tables(seq, ctx_len)
    for i in range(depth):
        need_ctx = i < depth - 1
        lat = mod_all[i, :b].reshape(b, 1, 6, d)
        cm = jnp.broadcast_to(mod_all[i, b].reshape(1, 1, 6, d), (b, 1, 6, d))
        modtab = jnp.concatenate([cm, lat], axis=1)
        p = _layer_params(w_in[i], m_conv[i], m_gate_b[i], m_norm[i], a_qnorm[i], a_knorm[i], l_cq_norm[i],
                          l_ckv_norm[i], l_wuq[i], l_wukv[i], l_qnorm[i], l_knorm[i], w_out[i])
        mqk, mo, misc, q_all, k_all, mv_t, v_all = _inproj(
            xs, modtab, norm_mix[i][None, :], p["w_in_p"], rope_tab, p["an"], p["ln"], p["cqn"], p["ckvn"],
            p["wuq_p"], p["wukv_p"], ctx_len)
        q_t, k_c = _conv(mqk, p["conv_w"], p["conv_scale"], ctx_len)
        hf, hb = _mlstm(q_t, k_c, mv_t, misc, p["gate_b"])
        gqa_kv = tuple(h // (A_HEADS // A_KV) for h in range(A_HEADS))
        o_gqa = _attention(q_all, k_all, v_all, ctx_len, need_ctx, 0, KV_GQA, A_KV, gqa_kv)
        o_mla = _attention(q_all, k_all, v_all, ctx_len, need_ctx, Q_MLA, 0, L_HEADS, tuple(range(L_HEADS)))
        xs = _mixout(xs, modtab, hf, hb, mo, o_gqa, o_mla, p["m_norm_p"], p["w_out_p"], ctx_len, not need_ctx)
        ffn_ctx = ctx_len if need_ctx else 0
        j = i // 2
        if i % 2 == 0:
            xs = _ffn(xs, modtab, norm_ffn[i][None, :], ffn_wg[j].astype(BF16), ffn_wu[j].astype(BF16),
                      ffn_wd[j].astype(BF16), ffn_ctx)
        else:
            router_p = jnp.pad(moe_router[j], ((0, 0), (0, LANE - N_EXPERTS)))
            xs = _moe(xs, modtab, norm_ffn[i][None, :], router_p, moe_wg[j].astype(BF16),
                      moe_wu[j].astype(BF16), moe_wd[j].astype(BF16), ffn_ctx)
    return xs
```

```python
import functools
import math

import numpy as np
import jax
import jax.numpy as jnp
from jax import lax
from jax.experimental import pallas as pl
from jax.experimental.pallas import tpu as pltpu

F32 = jnp.float32
BF16 = jnp.bfloat16

GRID_W = 64
EPS = 1e-6
ROPE_THETA = 10000.0
M_HEADS, M_DH = 4, 96
A_HEADS, A_KV, A_DH = 6, 2, 64
L_HEADS, L_NOPE, L_ROPE, L_DV = 4, 64, 32, 64
L_QK = L_NOPE + L_ROPE
L_QRANK, L_KVRANK = 256, 128
N_EXPERTS, TOP_K = 8, 2
IN_SIZES = (384, 384, 384, 384, 16, 384, 128, 128, 256, 128, 32)

LANE = 128
SLOT = LANE
N_KVHEADS = A_KV + L_HEADS
Q_MLA = 8
Q_SLOTS = Q_MLA + L_HEADS
KV_GQA = L_HEADS
VMEM_LIMIT = 56 * 1024 * 1024

O_MQK, O_MV, O_MO, O_MISC = 0, 1024, 1536, 2048
O_QA, O_KA, O_VA, O_CQ, O_CKV, IN_PAD = 2176, 2944, 3200, 3456, 3712, 3840
MISC_KR = 64
DEN_LANE = M_DH
SUM_LANE = A_DH
LOG2E = math.log2(math.e)
assert A_DH == L_DV and SUM_LANE < SLOT

TB = 256
TQ = 256
TK = 512
TM_FFN = 1088
TF = 512
ML = 256
RT = 144
GT = 2 * RT
ST = 3 * LANE
TF_MOE = 1792


def _cparams(sem):
    return pltpu.CompilerParams(dimension_semantics=sem, vmem_limit_bytes=VMEM_LIMIT)


def _sigmoid(x):
    return 1.0 / (1.0 + jnp.exp(-x))


def _silu(x):
    return x * _sigmoid(x)


def _lane_iota(shape):
    return lax.broadcasted_iota(jnp.int32, shape, len(shape) - 1)


def _row_iota(shape):
    return lax.broadcasted_iota(jnp.int32, shape, 0)


def _modulated(x, gain, mod_ref, t0, ctx_len, k_shift, k_scale):
    tm = x.shape[0]
    is_ctx = (t0 + _row_iota((tm, 1))) < ctx_len
    shift = jnp.where(is_ctx, mod_ref[0, k_shift:k_shift + 1, :], mod_ref[1, k_shift:k_shift + 1, :])
    scale = jnp.where(is_ctx, mod_ref[0, k_scale:k_scale + 1, :], mod_ref[1, k_scale:k_scale + 1, :])
    y = x * lax.rsqrt(jnp.mean(x * x, axis=-1, keepdims=True) + EPS) * gain
    return y * (1.0 + scale) + shift


def _gate_rows(mod_ref, t0, tm, ctx_len, k_gate):
    is_ctx = (t0 + _row_iota((tm, 1))) < ctx_len
    return jnp.where(is_ctx, mod_ref[0, k_gate:k_gate + 1, :], mod_ref[1, k_gate:k_gate + 1, :])


def _mod_kernel(c_ref, w_ref, b_ref, o_ref):
    s = _silu(c_ref[...]).astype(BF16)
    o_ref[...] = jnp.dot(s, w_ref[...].astype(BF16), preferred_element_type=F32) + b_ref[...]


def _mod_table(cc, mod_w, mod_b):
    depth, d, n = mod_w.shape
    rows = cc.shape[0]
    return pl.pallas_call(
        _mod_kernel,
        grid=(depth, n // d),
        in_specs=[pl.BlockSpec((rows, d), lambda l, j: (0, 0)),
                  pl.BlockSpec((None, d, d), lambda l, j: (l, 0, j)),
                  pl.BlockSpec((None, 1, d), lambda l, j: (l, 0, j))],
        out_specs=pl.BlockSpec((None, rows, d), lambda l, j: (l, 0, j)),
        out_shape=jax.ShapeDtypeStruct((depth, rows, n), F32),
        compiler_params=_cparams(("arbitrary", "arbitrary")),
        name="mod_table",
    )(cc, mod_w, mod_b.reshape(depth, 1, n))


def _slot_rms(x, lo, hi):
    lane = _lane_iota(x.shape)
    sq = jnp.where((lane >= lo) & (lane < hi), x * x, 0.0)
    return lax.rsqrt(jnp.sum(sq, axis=-1, keepdims=True) * (1.0 / (hi - lo)) + EPS)


def _rope(x, cos, sin_m, sin_p, quarter):
    return (x * cos + pltpu.roll(x, LANE - quarter, 1) * sin_m + pltpu.roll(x, quarter, 1) * sin_p)


def _inproj_kernel(ctx_len, x_ref, mod_ref, g_ref, w_ref, rope_ref, an_ref, ln_ref, cqn_ref, ckvn_ref,
                   wuq_ref, wukv_ref, mqk_ref, mo_ref, misc_ref, q_ref, k_ref, mvt_ref, vt_ref):
    tm = x_ref.shape[0]
    t0 = pl.program_id(1) * tm
    h = _modulated(x_ref[...], g_ref[...], mod_ref, t0, ctx_len, 0, 1).astype(BF16)

    def proj(a, b):
        return jnp.dot(h, w_ref[:, a:b], preferred_element_type=F32)

    mqk_ref[...] = proj(O_MQK, O_MV).astype(BF16)
    mv = proj(O_MV, O_MO)
    mvt_ref[...] = jnp.where(_lane_iota(mv.shape) % SLOT == DEN_LANE, 1.0, mv).T.astype(BF16)
    mo_ref[...] = proj(O_MO, O_MISC).astype(BF16)
    misc = proj(O_MISC, O_QA)
    misc_ref[...] = misc

    cos_a, sinm_a, sinp_a = rope_ref[0], rope_ref[1], rope_ref[2]
    cos_l, sinm_l, sinp_l = rope_ref[3], rope_ref[4], rope_ref[5]
    a_scale = A_DH ** -0.5 * LOG2E
    l_scale = L_QK ** -0.5 * LOG2E

    def with_ones_t(v):
        return jnp.where(_lane_iota(v.shape) % SLOT == SUM_LANE, 1.0, v).T.astype(BF16)

    pa = proj(O_QA, O_CQ)
    gq, gk = an_ref[0:1, :], an_ref[1:2, :]
    for i in range(A_HEADS):
        x = pa[:, i * SLOT:(i + 1) * SLOT]
        x = x * _slot_rms(x, 0, A_DH) * gq
        q_ref[:, i * SLOT:(i + 1) * SLOT] = (_rope(x, cos_a, sinm_a, sinp_a, A_DH // 4) * a_scale).astype(BF16)
    for i in range(A_KV):
        x = pa[:, (A_HEADS + i) * SLOT:(A_HEADS + i + 1) * SLOT]
        x = x * _slot_rms(x, 0, A_DH) * gk
        k_ref[:, (KV_GQA + i) * SLOT:(KV_GQA + i + 1) * SLOT] = (
            _rope(x, cos_a, sinm_a, sinp_a, A_DH // 4).astype(BF16))
    vt_ref[KV_GQA * SLOT:, :] = with_ones_t(pa[:, (A_HEADS + A_KV) * SLOT:(A_HEADS + 2 * A_KV) * SLOT])
    q_ref[:, A_HEADS * SLOT:Q_MLA * SLOT] = jnp.zeros((tm, (Q_MLA - A_HEADS) * SLOT), BF16)

    pc = proj(O_CQ, IN_PAD)
    cq = pc[:, 0:L_QRANK]
    cq = (cq * lax.rsqrt(jnp.mean(cq * cq, axis=-1, keepdims=True) + EPS) * cqn_ref[...]).astype(BF16)
    ckv = pc[:, L_QRANK:L_QRANK + L_KVRANK]
    ckv = (ckv * lax.rsqrt(jnp.mean(ckv * ckv, axis=-1, keepdims=True) + EPS) * ckvn_ref[...]).astype(BF16)
    ql = jnp.dot(cq, wuq_ref[...], preferred_element_type=F32)
    kvl = jnp.dot(ckv, wukv_ref[...], preferred_element_type=F32)
    gq_l, gk_l = ln_ref[0:1, :], ln_ref[1:2, :]
    lane = _lane_iota((tm, SLOT))
    kr = jnp.where((lane >= MISC_KR) & (lane < MISC_KR + L_ROPE), misc, 0.0)
    kr = kr * _slot_rms(kr, MISC_KR, MISC_KR + L_ROPE) * gk_l
    kr = _rope(kr, cos_l, sinm_l, sinp_l, L_ROPE // 4)
    for i in range(L_HEADS):
        x = ql[:, i * SLOT:(i + 1) * SLOT]
        inv = jnp.where(lane < L_NOPE, _slot_rms(x, 0, L_NOPE), _slot_rms(x, L_NOPE, L_QK))
        x = x * inv * gq_l
        q_ref[:, (Q_MLA + i) * SLOT:(Q_MLA + i + 1) * SLOT] = (
            _rope(x, cos_l, sinm_l, sinp_l, L_ROPE // 4) * l_scale).astype(BF16)
        kn = kvl[:, i * SLOT:(i + 1) * SLOT]
        kn = kn * _slot_rms(kn, 0, L_NOPE) * gk_l
        k_ref[:, i * SLOT:(i + 1) * SLOT] = (kn + kr).astype(BF16)
    vt_ref[0:KV_GQA * SLOT, :] = with_ones_t(kvl[:, L_HEADS * SLOT:])


def _inproj(xs, modtab, gain, w_in_p, rope_tab, an, ln, cqn, ckvn, wuq_p, wukv_p, ctx_len):
    b, s, d = xs.shape
    grid = (b, s // TB)
    tok = lambda n: pl.BlockSpec((None, TB, n), lambda i, j: (i, j, 0))
    full = lambda a: pl.BlockSpec(a.shape, lambda i, j: (0,) * a.ndim)
    out_widths = (1024, 512, SLOT, Q_SLOTS * SLOT, N_KVHEADS * SLOT)
    out_dtypes = (BF16, BF16, F32, BF16, BF16)
    out_specs = [tok(n) for n in out_widths]
    out_shape = [jax.ShapeDtypeStruct((b, s, n), dt) for n, dt in zip(out_widths, out_dtypes)]
    for n in (M_HEADS * SLOT, N_KVHEADS * SLOT):
        out_specs.append(pl.BlockSpec((None, n, TB), lambda i, j: (i, 0, j)))
        out_shape.append(jax.ShapeDtypeStruct((b, n, s), BF16))
    return pl.pallas_call(
        functools.partial(_inproj_kernel, ctx_len),
        grid=grid,
        in_specs=[tok(d),
                  pl.BlockSpec((None, 2, 6, d), lambda i, j: (i, 0, 0, 0)),
                  full(gain), full(w_in_p),
                  pl.BlockSpec((6, TB, SLOT), lambda i, j: (0, j, 0)),
                  full(an), full(ln), full(cqn), full(ckvn), full(wuq_p), full(wukv_p)],
        out_specs=out_specs,
        out_shape=out_shape,
        compiler_params=_cparams(("parallel", "parallel")),
        name="inproj",
    )(xs, modtab, gain, w_in_p, rope_tab, an, ln, cqn, ckvn, wuq_p, wukv_p)


def _conv_kernel(ctx_len, s_len, x_ref, prev_ref, next_ref, w_ref, sc_ref, qt_ref, k_ref):
    tm = x_ref.shape[0]
    t0 = pl.program_id(1) * tm
    x = x_ref[...].astype(F32)
    row = _row_iota((tm, 1))
    has_prev = jnp.logical_and(t0 != 0, t0 != ctx_len)
    has_next = jnp.logical_and(t0 + tm != ctx_len, t0 + tm != s_len)
    hp = jnp.where(has_prev, prev_ref[15:16, :].astype(F32), 0.0)
    hn = jnp.where(has_next, next_ref[0:1, :].astype(F32), 0.0)
    xp = jnp.where(row == 0, hp, pltpu.roll(x, 1, 0))
    xn = jnp.where(row == tm - 1, hn, pltpu.roll(x, tm - 1, 0))
    y = xp * w_ref[0:1, :] + x * w_ref[1:2, :] + xn * w_ref[2:3, :]
    y = _silu(y) * sc_ref[...]
    half = y.shape[1] // 2
    qt_ref[...] = y[:, :half].T.astype(qt_ref.dtype)
    k_ref[...] = y[:, half:].astype(k_ref.dtype)


def _conv(mqk, conv_w, conv_scale, ctx_len):
    b, s, n = mqk.shape
    hb = TB // 16
    last = s // 16 - 1
    return pl.pallas_call(
        functools.partial(_conv_kernel, ctx_len, s),
        grid=(b, s // TB),
        in_specs=[pl.BlockSpec((None, TB, n), lambda i, j: (i, j, 0)),
                  pl.BlockSpec((None, 16, n), lambda i, j: (i, jnp.maximum(j * hb - 1, 0), 0)),
                  pl.BlockSpec((None, 16, n), lambda i, j: (i, jnp.minimum((j + 1) * hb, last), 0)),
                  pl.BlockSpec((3, n), lambda i, j: (0, 0)),
                  pl.BlockSpec((1, n), lambda i, j: (0, 0))],
        out_specs=[pl.BlockSpec((None, n // 2, TB), lambda i, j: (i, 0, j)),
                   pl.BlockSpec((None, TB, n // 2), lambda i, j: (i, j, 0))],
        out_shape=[jax.ShapeDtypeStruct((b, n // 2, s), BF16), jax.ShapeDtypeStruct((b, s, n // 2), BF16)],
        compiler_params=_cparams(("parallel", "parallel")),
        name="mlstm_conv",
    )(mqk, mqk, mqk, conv_w, conv_scale)


def _log_sigmoid(x):
    return jnp.minimum(x, 0.0) - jnp.log(1.0 + jnp.exp(-jnp.abs(x)))


def _mlstm_kernel(n_chunks, qt_ref, k_ref, vt_ref, misc_ref, gb_ref, hf_ref, hb_ref, c_ref, m_ref, gt_sc, bt_sc):
    hp = pl.program_id(1)
    c_ref[...] = jnp.zeros_like(c_ref)
    m_ref[...] = jnp.zeros_like(m_ref)
    r = _row_iota((ML, ML))
    cidx = _lane_iota((ML, ML))
    tri_f32 = (r >= cidx).astype(F32)
    tri_b32 = (r <= cidx).astype(F32)
    lane = _lane_iota((ML, SLOT))
    lane_t = _lane_iota((1, ML))

    def chunk(t0, d, tri32, out_ref):
        g = misc_ref[pl.ds(t0, ML), :] + gb_ref[...]
        logf = _log_sigmoid(g)
        bcum = jnp.dot(tri32, logf, preferred_element_type=F32, precision=lax.Precision.HIGHEST)
        gt_sc[d] = g.T
        bt_sc[d] = bcum.T
        ib = g - pltpu.roll(bcum, SLOT - M_HEADS, 1)
        allowed = (r <= cidx) if d == 0 else (r >= cidx)
        for hh in range(2):
            li = d * 2 * M_HEADS + hp * 2 + hh
            ib_col = jnp.sum(jnp.where(lane == li, ib, 0.0), axis=-1, keepdims=True)
            i_row = gt_sc[d, pl.ds(li, 1), :]
            b_row = bt_sc[d, pl.ds(li + M_HEADS, 1), :]
            sidx = d * 2 + hh
            ct_st = c_ref[sidx]
            m_s = m_ref[sidx]
            k = k_ref[pl.ds(t0, ML), hh * SLOT:(hh + 1) * SLOT]
            qt = qt_ref[hh * SLOT:(hh + 1) * SLOT, pl.ds(t0, ML)]
            vt = vt_ref[hh * SLOT:(hh + 1) * SLOT, pl.ds(t0, ML)]
            dmat = jnp.where(allowed, b_row + ib_col, -jnp.inf)
            m_inter = b_row + m_s
            m_t = jnp.maximum(m_inter, jnp.max(dmat, axis=0, keepdims=True))
            kq = jnp.dot(k, qt, preferred_element_type=F32)
            w = (jnp.exp(dmat - m_t) * kq).astype(BF16)
            a_inter = jnp.exp(m_inter - m_t)
            num = a_inter * jnp.dot(ct_st.astype(BF16), qt, preferred_element_type=F32) + jnp.dot(
                vt, w, preferred_element_type=F32)
            den = num[DEN_LANE:DEN_LANE + 1, :]
            h_out = num / jnp.maximum(jnp.abs(den), jnp.exp(-m_t))
            out_ref[pl.ds(t0, ML), hh * SLOT:(hh + 1) * SLOT] = h_out.T.astype(out_ref.dtype)
            last = ML - 1 if d == 0 else 0
            total = jnp.sum(jnp.where(lane_t == last, b_row, 0.0), axis=-1, keepdims=True)
            gg = total - b_row + i_row
            m_new = jnp.maximum(total + m_s, jnp.max(gg, axis=-1, keepdims=True))
            decay = jnp.exp(total + m_s - m_new)[:, 0:1]
            wk = jnp.exp(gg - m_new)
            vw = (vt.astype(F32) * wk).astype(BF16)
            c_ref[sidx] = decay * ct_st + jnp.dot(vw, k, preferred_element_type=F32)
            m_ref[sidx] = m_new

    def step(n, carry):
        tf0 = pl.multiple_of(n * ML, ML)
        tb0 = pl.multiple_of(jnp.where(n == 0, 0, n_chunks - n) * ML, ML)
        chunk(tf0, 0, tri_f32, hf_ref)
        chunk(tb0, 1, tri_b32, hb_ref)
        return carry

    lax.fori_loop(0, n_chunks, step, 0)


def _mlstm(q_t, k_c, mv_t, misc, gate_b):
    b, s, _ = k_c.shape
    n_chunks = s // ML
    pair = 2 * SLOT
    spec = pl.BlockSpec((None, s, pair), lambda i, j: (i, 0, j))
    spec_t = pl.BlockSpec((None, pair, s), lambda i, j: (i, j, 0))
    return pl.pallas_call(
        functools.partial(_mlstm_kernel, n_chunks),
        grid=(b, M_HEADS // 2),
        in_specs=[spec_t, spec, spec_t,
                  pl.BlockSpec((None, s, SLOT), lambda i, j: (i, 0, 0)),
                  pl.BlockSpec((1, SLOT), lambda i, j: (0, 0))],
        out_specs=[spec, spec],
        out_shape=[jax.ShapeDtypeStruct((b, s, M_HEADS * SLOT), BF16)] * 2,
        scratch_shapes=[pltpu.VMEM((4, SLOT, SLOT), F32), pltpu.VMEM((4, 1, ML), F32),
                        pltpu.VMEM((2, SLOT, ML), F32), pltpu.VMEM((2, SLOT, ML), F32)],
        compiler_params=_cparams(("parallel", "parallel")),
        name="mlstm_scan",
    )(q_t, k_c, mv_t, misc, gate_b)


def _attn_kernel(ctx_len, q_off, kv_of, q_ref, k_ref, vt_ref, o_ref, st_sc, pt_sc, acc_sc):
    s_len = k_ref.shape[0]
    n_heads = len(kv_of)
    qs = [q_ref[:, g * SLOT:(g + 1) * SLOT] for g in range(n_heads)]

    def kv_cols(g):
        return slice(kv_of[g] * SLOT, (kv_of[g] + 1) * SLOT)

    def chunk(carry, rows):
        out = []
        for g in range(n_heads):
            m, acc = carry[g]
            k = k_ref[rows, kv_cols(g)]
            vt = vt_ref[kv_cols(g), rows]
            st = lax.dot_general(k, qs[g], (((1,), (1,)), ((), ())), preferred_element_type=F32)
            m_new = jnp.maximum(m, jnp.max(st, axis=0, keepdims=True))
            alpha = jnp.exp2(m - m_new)
            pt = jnp.exp2(st - m_new).astype(BF16)
            out.append((m_new, alpha * acc + jnp.dot(vt, pt, preferred_element_type=F32)))
        return tuple(out)

    def finish(carry):
        for g in range(0, n_heads, 2):
            pair = []
            for acc in (carry[g][1], carry[g + 1][1]):
                pair.append(acc[0:SUM_LANE, :] / acc[SUM_LANE:SUM_LANE + 1, :])
            o_ref[:, (g // 2) * SLOT:(g // 2 + 1) * SLOT] = jnp.concatenate(pair, axis=0).T.astype(o_ref.dtype)

    init = tuple((jnp.full((1, TQ), -jnp.inf, F32), jnp.zeros((SLOT, TQ), F32)) for _ in range(n_heads))
    is_ctx = (pl.program_id(2) + q_off) * TQ < ctx_len

    @pl.when(is_ctx)
    def _():
        finish(chunk(init, pl.ds(0, ctx_len)))

    n_lat = (s_len - ctx_len) // TK
    n0 = ctx_len + TK
    assert n_lat >= 4 and n_lat % 2 == 0 and st_sc.shape[1] == n0

    def rows_of(c):
        if isinstance(c, int):
            return (pl.ds(0, n0), n0) if c == 0 else (pl.ds(ctx_len + c * TK, TK), TK)
        return pl.ds(pl.multiple_of(ctx_len + c * TK, math.gcd(ctx_len, TK)), TK), TK

    def scores(c, par):
        rows, n = rows_of(c)
        for g in range(n_heads):
            st_sc[par * n_heads + g, 0:n, :] = lax.dot_general(
                k_ref[rows, kv_cols(g)], qs[g], (((1,), (1,)), ((), ())), preferred_element_type=F32)

    def softmax(par, m, n=TK):
        ms, alphas = [], []
        for g in range(n_heads):
            st = st_sc[par * n_heads + g, 0:n, :]
            m_new = jnp.maximum(m[g], jnp.max(st, axis=0, keepdims=True))
            pt_sc[par * n_heads + g, 0:n, :] = jnp.exp2(st - m_new).astype(BF16)
            ms.append(m_new)
            alphas.append(jnp.exp2(m[g] - m_new))
        return tuple(ms), tuple(alphas)

    def values(c, par, alpha):
        rows, n = rows_of(c)
        for g in range(n_heads):
            acc_sc[g] = alpha[g] * acc_sc[g] + jnp.dot(vt_ref[kv_cols(g), rows], pt_sc[par * n_heads + g, 0:n, :],
                                                       preferred_element_type=F32)

    def stage(c, par, m, alpha):
        scores(c + 1, 1 - par)
        m_new, alpha_new = softmax(par, m)
        values(c - 1, 1 - par, alpha)
        return m_new, alpha_new

    @pl.when(jnp.logical_not(is_ctx))
    def _():
        for g in range(n_heads):
            acc_sc[g] = init[g][1]
        scores(0, 0)
        m, alpha = softmax(0, tuple(c[0] for c in init), n0)
        scores(1, 1)
        m, alpha = stage(1, 1, m, alpha)

        def body(t, carry):
            c = 2 * t + 2
            return stage(c + 1, 1, *stage(c, 0, *carry))

        m, alpha = lax.fori_loop(0, (n_lat - 4) // 2, body, (m, alpha))
        m, alpha = stage(n_lat - 2, 0, m, alpha)
        values(n_lat - 2, 0, alpha)
        m, alpha = softmax(1, m)
        values(n_lat - 1, 1, alpha)
        finish(tuple((m[g], acc_sc[g]) for g in range(n_heads)))


def _attention(q_all, k_all, vt_all, ctx_len, need_ctx, q_slot0, kv_slot0, n_kv, kv_of):
    b, s, _ = q_all.shape
    q_off = 0 if need_ctx else ctx_len // TQ
    nq = s // TQ - q_off
    n_heads = len(kv_of)
    qw = n_heads * SLOT
    kw = n_kv * SLOT
    assert (q_slot0 * SLOT) % qw == 0 and (kv_slot0 * SLOT) % kw == 0
    qb0, kb0 = q_slot0 * SLOT // qw, kv_slot0 * SLOT // kw
    return pl.pallas_call(
        functools.partial(_attn_kernel, ctx_len, q_off, kv_of),
        grid=(b, 1, nq),
        in_specs=[pl.BlockSpec((None, TQ, qw), lambda i, g, j: (i, j + q_off, qb0)),
                  pl.BlockSpec((None, s, kw), lambda i, g, j: (i, 0, kb0)),
                  pl.BlockSpec((None, kw, s), lambda i, g, j: (i, kb0, 0))],
        out_specs=pl.BlockSpec((None, TQ, qw // 2), lambda i, g, j: (i, j + q_off, 0)),
        out_shape=jax.ShapeDtypeStruct((b, s, qw // 2), BF16),
        scratch_shapes=[pltpu.VMEM((2 * n_heads, ctx_len + TK, TQ), F32),
                        pltpu.VMEM((2 * n_heads, ctx_len + TK, TQ), BF16),
                        pltpu.VMEM((n_heads, SLOT, TQ), F32)],
        compiler_params=_cparams(("parallel", "parallel", "parallel")),
        name="attention",
    )(q_all, k_all, vt_all)


def _mixout_kernel(ctx_len, blk_off, x_ref, mod_ref, hf_ref, hb_ref, mo_ref, oa_ref, ol_ref, mn_ref, w_ref, o_ref):
    tm = x_ref.shape[0]
    t0 = (pl.program_id(1) + blk_off) * tm
    hm = hf_ref[...].astype(F32) + hb_ref[...].astype(F32)
    gate = _sigmoid(mo_ref[...].astype(F32))
    parts = []
    for i in range(M_HEADS):
        x = hm[:, i * SLOT:(i + 1) * SLOT]
        inv = lax.rsqrt(jnp.sum(x * x, axis=-1, keepdims=True) * (1.0 / M_DH) + EPS)
        parts.append(x * inv)
    hn = jnp.concatenate(parts, axis=-1) * mn_ref[...] * gate
    nm = M_HEADS * SLOT
    o = jnp.dot(hn.astype(BF16), w_ref[0:nm, :], preferred_element_type=F32)
    na = nm + A_HEADS * A_DH
    o = o + jnp.dot(oa_ref[...], w_ref[nm:na, :], preferred_element_type=F32)
    o = o + jnp.dot(ol_ref[...], w_ref[na:, :], preferred_element_type=F32)
    o_ref[...] = x_ref[...] + _gate_rows(mod_ref, t0, tm, ctx_len, 2) * o


def _mixout(xs, modtab, hf, hb, mo, o_gqa, o_mla, m_norm_p, w_out_p, ctx_len, lat_only):
    b, s, d = xs.shape
    blk_off = ctx_len // TB if lat_only else 0
    tok = lambda n: pl.BlockSpec((None, TB, n), lambda i, j: (i, j + blk_off, 0))
    full = lambda a: pl.BlockSpec(a.shape, lambda i, j: (0,) * a.ndim)
    return pl.pallas_call(
        functools.partial(_mixout_kernel, ctx_len, blk_off),
        grid=(b, s // TB - blk_off),
        in_specs=[tok(d),
                  pl.BlockSpec((None, 2, 6, d), lambda i, j: (i, 0, 0, 0)),
                  tok(M_HEADS * SLOT), tok(M_HEADS * SLOT), tok(M_HEADS * SLOT), tok(A_HEADS * A_DH),
                  tok(L_HEADS * L_DV), full(m_norm_p), full(w_out_p)],
        out_specs=pl.BlockSpec((None, TB, d), lambda i, j: (i, j, 0)),
        out_shape=jax.ShapeDtypeStruct((b, s - blk_off * TB, d), F32),
        input_output_aliases={} if lat_only else {0: 0},
        compiler_params=_cparams(("parallel", "parallel")),
        name="mix_out",
    )(xs, modtab, hf, hb, mo, o_gqa, o_mla, m_norm_p, w_out_p)


def _ffn_kernel(ctx_len, x_ref, mod_ref, g_ref, wg_ref, wu_ref, wd_ref, o_ref, h_sc, acc_sc):
    tm = x_ref.shape[0]
    t0 = pl.program_id(1) * tm
    f = pl.program_id(2)

    @pl.when(f == 0)
    def _():
        h_sc[...] = _modulated(x_ref[...], g_ref[...], mod_ref, t0, ctx_len, 3, 4).astype(BF16)
        acc_sc[...] = jnp.zeros_like(acc_sc)

    h = h_sc[...]
    a = jnp.dot(h, wg_ref[...], preferred_element_type=F32)
    u = jnp.dot(h, wu_ref[...], preferred_element_type=F32)
    acc_sc[...] += jnp.dot((_silu(a) * u).astype(BF16), wd_ref[...], preferred_element_type=F32)

    @pl.when(f == pl.num_programs(2) - 1)
    def _():
        o_ref[...] = x_ref[...] + _gate_rows(mod_ref, t0, tm, ctx_len, 5) * acc_sc[...]


def _ffn(xs, modtab, gain, wg, wu, wd, ctx_len):
    b, s, d = xs.shape
    tm = s // 4
    nf = wg.shape[1] // TF
    return pl.pallas_call(
        functools.partial(_ffn_kernel, ctx_len),
        grid=(b, s // tm, nf),
        in_specs=[pl.BlockSpec((None, tm, d), lambda i, j, f: (i, j, 0)),
                  pl.BlockSpec((None, 2, 6, d), lambda i, j, f: (i, 0, 0, 0)),
                  pl.BlockSpec((1, d), lambda i, j, f: (0, 0)),
                  pl.BlockSpec((d, TF), lambda i, j, f: (0, f)),
                  pl.BlockSpec((d, TF), lambda i, j, f: (0, f)),
                  pl.BlockSpec((TF, d), lambda i, j, f: (f, 0))],
        out_specs=pl.BlockSpec((None, tm, d), lambda i, j, f: (i, j, 0)),
        out_shape=jax.ShapeDtypeStruct((b, s, d), F32),
        scratch_shapes=[pltpu.VMEM((tm, d), BF16), pltpu.VMEM((tm, d), F32)],
        input_output_aliases={0: 0},
        compiler_params=_cparams(("parallel", "parallel", "arbitrary")),
        name="ffn_dense",
    )(xs, modtab, gain, wg, wu, wd)


def _top2_combine(logits):
    lane = _lane_iota(logits.shape)
    lane_f = lane.astype(F32)
    lg = jnp.where(lane < N_EXPERTS, logits, -jnp.inf)
    v1 = jnp.max(lg, axis=-1, keepdims=True)
    i1 = jnp.min(jnp.where(lg == v1, lane_f, float(LANE)), axis=-1, keepdims=True)
    rest = jnp.where(lane_f == i1, -jnp.inf, lg)
    v2 = jnp.max(rest, axis=-1, keepdims=True)
    i2 = jnp.min(jnp.where(rest == v2, lane_f, float(LANE)), axis=-1, keepdims=True)
    e2 = jnp.exp(v2 - v1)
    w1 = 1.0 / (1.0 + e2)
    w2 = e2 / (1.0 + e2)
    sel = jnp.where((lane_f == i1) | (lane_f == i2), 1.0, 0.0)
    return jnp.where(lane_f == i1, w1, 0.0) + jnp.where(lane_f == i2, w2, 0.0), sel


def _moe_kernel(ctx_len, x_ref, mod_ref, g_ref, r_ref, tri_ref, wg_ref, wu_ref, wd_ref, o_ref,
                h_sc, comb_sc, rank_sc, rankt_sc, rankc_sc, wc_sc, xg_sc, acc_sc, nt_sc):
    tm = x_ref.shape[0]
    t0 = pl.program_id(1) * tm
    e = pl.program_id(2)
    f = pl.program_id(3)

    @pl.when(jnp.logical_and(e == 0, f == 0))
    def _():
        h = _modulated(x_ref[...], g_ref[...], mod_ref, t0, ctx_len, 3, 4)
        logits = jnp.dot(h, r_ref[...], preferred_element_type=F32, precision=lax.Precision.HIGHEST)
        comb, sel = _top2_combine(logits)
        rank =jnp.dot(tri_ref[...], sel.astype(BF16), preferred_element_type=F32)
        rank = jnp.where(sel > 0.0, rank, -1.0)
        comb_sc[...] = comb
        rank_sc[...] = rank
        pad = rankt_sc.shape[1] - tm
        h_sc[0:tm, :] = h.astype(BF16)
        if pad:
            rankt_sc[...] = jnp.concatenate([rank, jnp.full((pad, LANE), -1.0, F32)], axis=0).T
            h_sc[tm:, :] = jnp.zeros((pad, h_sc.shape[1]), BF16)
        else:
            rankt_sc[...] = rank.T
        o_ref[...] = jnp.zeros_like(o_ref)

    @pl.when(f == 0)
    def _():
        lane = _lane_iota((tm, LANE))
        rank_c = jnp.sum(jnp.where(lane == e, rank_sc[...], 0.0), axis=-1, keepdims=True)
        rankc_sc[...] = rank_c
        wc_sc[...] = jnp.sum(jnp.where(lane == e, comb_sc[...], 0.0), axis=-1, keepdims=True)
        n_rows = jnp.sum(jnp.where(rank_c >= 0.0, 1.0, 0.0)).astype(jnp.int32)
        n_rt = lax.div(n_rows + (RT - 1), RT)
        n_gt = lax.div(n_rt * RT + (GT - 1), GT)
        n_st = lax.div(n_rt * RT + (ST - 1), ST)
        nt_sc[0] = n_rt
        nt_sc[1] = n_st
        rank_r = rankt_sc[pl.ds(e, 1), :]

        def gather(i, c):
            r0 = pl.multiple_of(i * GT, 16)
            tgt = (r0 + _row_iota((GT, 1))).astype(F32)
            sel_t = jnp.where(rank_r == tgt, 1.0, 0.0).astype(BF16)
            xg_sc[pl.ds(r0, GT), :] = jnp.dot(sel_t, h_sc[...], preferred_element_type=F32).astype(BF16)
            return c

        def clear(i, c):
            acc_sc[pl.ds(pl.multiple_of(i * ST, ST), ST), :] = jnp.zeros((ST, acc_sc.shape[1]), F32)
            return c

        lax.fori_loop(0, n_gt, gather, 0)
        lax.fori_loop(0, n_st, clear, 0)

    def expert(i, c):
        r0 = pl.multiple_of(i * RT, 16)
        rows = xg_sc[pl.ds(r0, RT), :]
        a = jnp.dot(rows, wg_ref[...], preferred_element_type=F32)
        u = jnp.dot(rows, wu_ref[...], preferred_element_type=F32)
        acc_sc[pl.ds(r0, RT), :] += jnp.dot((_silu(a) * u).astype(BF16), wd_ref[...],
                                            preferred_element_type=F32)
        return c

    lax.fori_loop(0, nt_sc[0], expert, 0)

    @pl.when(f == pl.num_programs(3) - 1)
    def _():
        def scatter(i, c):
            r0 = pl.multiple_of(i * ST, ST)
            tgt = (r0 + _lane_iota((1, ST))).astype(F32)
            w_t = jnp.where(rankc_sc[...] == tgt, wc_sc[...], 0.0).astype(BF16)
            o_ref[...] += jnp.dot(w_t, acc_sc[pl.ds(r0, ST), :].astype(BF16), preferred_element_type=F32)
            return c

        lax.fori_loop(0, nt_sc[1], scatter, 0)

    @pl.when(jnp.logical_and(e == pl.num_programs(2) - 1, f == pl.num_programs(3) - 1))
    def _():
        o_ref[...] = x_ref[...] + _gate_rows(mod_ref, t0, tm, ctx_len, 5) * o_ref[...]


def _moe(xs, modtab, gain, router_p, wg, wu, wd, ctx_len):
    b, s, d = xs.shape
    tm = s // 4
    tp = -(-tm // LANE) * LANE
    r_max = -(-tm // RT) * RT
    tr = max(-(-r_max // GT) * GT, -(-r_max // ST) * ST)
    ne, _, dff = wg.shape
    tri = jnp.tril(jnp.ones((tm, tm), BF16), -1)
    TF = TF_MOE
    assert dff % TF == 0
    return pl.pallas_call(
        functools.partial(_moe_kernel, ctx_len),
        grid=(b, s // tm, ne, dff // TF),
        in_specs=[pl.BlockSpec((None, tm, d), lambda i, j, e, f: (i, j, 0), pipeline_mode=pl.Buffered(1)),
                  pl.BlockSpec((None, 2, 6, d), lambda i, j, e, f: (i, 0, 0, 0)),
                  pl.BlockSpec((1, d), lambda i, j, e, f: (0, 0)),
                  pl.BlockSpec((d, LANE), lambda i, j, e, f: (0, 0)),
                  pl.BlockSpec((tm, tm), lambda i, j, e, f: (0, 0), pipeline_mode=pl.Buffered(1)),
                  pl.BlockSpec((None, d, TF), lambda i, j, e, f: (e, 0, f)),
                  pl.BlockSpec((None, d, TF), lambda i, j, e, f: (e, 0, f)),
                  pl.BlockSpec((None, TF, d), lambda i, j, e, f: (e, f, 0))],
        out_specs=pl.BlockSpec((None, tm, d), lambda i, j, e, f: (i, j, 0)),
        out_shape=jax.ShapeDtypeStruct((b, s, d), F32),
        scratch_shapes=[pltpu.VMEM((tp, d), BF16),
                        pltpu.VMEM((tm, LANE), F32),
                        pltpu.VMEM((tm, LANE), F32),
                        pltpu.VMEM((LANE, tp), F32),
                        pltpu.VMEM((tm, 1), F32),
                        pltpu.VMEM((tm, 1), F32),
                        pltpu.VMEM((tr, d), BF16),
                        pltpu.VMEM((tr, d), F32),
                        pltpu.SMEM((2,), jnp.int32)],
        input_output_aliases={0: 0},
        compiler_params=_cparams(("parallel", "parallel", "arbitrary", "arbitrary")),
        name="moe_top2",
    )(xs, modtab, gain, router_p, tri, wg, wu, wd)


def _pad_heads(w, n_heads, dh, axis=-1):
    axis = axis % w.ndim
    shp = w.shape[:axis] + (n_heads, dh) + w.shape[axis + 1:]
    pad = [(0, 0)] * (w.ndim + 1)
    pad[axis + 1] = (0, SLOT - dh)
    out = jnp.pad(w.reshape(shp), pad)
    return out.reshape(w.shape[:axis] + (n_heads * SLOT,) + w.shape[axis + 1:])


def _rope_tables(seq, ctx_len):
    t = jnp.arange(seq)
    rows = (t // GRID_W).astype(F32)
    cols = (t % GRID_W).astype(F32)

    def angles(rot_dim):
        nf = rot_dim // 4
        inv = ROPE_THETA ** (-jnp.arange(nf, dtype=F32) / nf)
        ar = rows[:, None] * inv
        ac = cols[:, None] * inv
        return jnp.concatenate([ar, ar, ac, ac], axis=-1)

    def slot_tables(rot_dim, lane0):
        ang = angles(rot_dim)
        quarter = rot_dim // 4
        first = (jnp.arange(rot_dim) % (2 * quarter)) < quarter
        cos = jnp.ones((seq, SLOT), F32).at[:, lane0:lane0 + rot_dim].set(jnp.cos(ang))
        sin = jnp.sin(ang)
        sin_m = jnp.zeros((seq, SLOT), F32).at[:, lane0:lane0 + rot_dim].set(jnp.where(first, -sin, 0.0))
        sin_p = jnp.zeros((seq, SLOT), F32).at[:, lane0:lane0 + rot_dim].set(jnp.where(first, 0.0, sin))
        ident = jnp.stack([jnp.ones((ctx_len, SLOT), F32), jnp.zeros((ctx_len, SLOT), F32),
                           jnp.zeros((ctx_len, SLOT), F32)])
        return jnp.concatenate([ident, jnp.stack([cos, sin_m, sin_p])], axis=1)

    return jnp.concatenate([slot_tables(A_DH, 0), slot_tables(L_ROPE, L_NOPE)], axis=0)


def _layer_params(w_in, m_conv, m_gate_b, m_norm, a_qnorm, a_knorm, l_cq_norm, l_ckv_norm, l_wuq, l_wukv,
                  l_qnorm, l_knorm, w_out):
    d = w_in.shape[0]
    offs = np.cumsum((0,) + IN_SIZES)
    seg = [w_in[:, offs[i]:offs[i + 1]] for i in range(len(IN_SIZES))]
    misc = jnp.zeros((d, SLOT), F32).at[:, 0:4 * M_HEADS].set(seg[4]).at[:, MISC_KR:MISC_KR + L_ROPE].set(seg[10])
    w_in_p = jnp.concatenate(
        [_pad_heads(seg[0], M_HEADS, M_DH), _pad_heads(seg[1], M_HEADS, M_DH),
         _pad_heads(seg[2], M_HEADS, M_DH), _pad_heads(seg[3], M_HEADS, M_DH), misc,
         _pad_heads(seg[5], A_HEADS, A_DH), _pad_heads(seg[6], A_KV, A_DH), _pad_heads(seg[7], A_KV, A_DH),
         seg[8], seg[9]], axis=1).astype(BF16)
    conv_w = jnp.concatenate([_pad_heads(m_conv[:, :M_HEADS * M_DH], M_HEADS, M_DH),
                              _pad_heads(m_conv[:, M_HEADS * M_DH:], M_HEADS, M_DH)], axis=1)
    conv_scale = jnp.concatenate([jnp.ones((1, M_HEADS * SLOT), F32),
                                  jnp.full((1, M_HEADS * SLOT), M_DH ** -0.5, F32)], axis=1)
    gate_b = jnp.zeros((1, SLOT), F32).at[0, 0:4 * M_HEADS].set(m_gate_b)
    pad1 = lambda g: jnp.pad(g, (0, SLOT - g.shape[0]))
    an = jnp.stack([pad1(a_qnorm), pad1(a_knorm)])
    ln = jnp.stack([pad1(l_qnorm), pad1(l_knorm)])
    wuq_p = _pad_heads(l_wuq, L_HEADS, L_QK).astype(BF16)
    kv = l_wukv.reshape(L_KVRANK, L_HEADS, L_NOPE + L_DV)
    wukv_p = jnp.concatenate(
        [_pad_heads(kv[:, :, :L_NOPE].reshape(L_KVRANK, -1), L_HEADS, L_NOPE),
         _pad_heads(kv[:, :, L_NOPE:].reshape(L_KVRANK, -1), L_HEADS, L_DV)], axis=1).astype(BF16)
    nm = M_HEADS * M_DH
    w_out_p = jnp.concatenate([_pad_heads(w_out[:nm], M_HEADS, M_DH, axis=0), w_out[nm:]], axis=0).astype(BF16)
    m_norm_p = _pad_heads(m_norm[None, :], M_HEADS, M_DH)
    return dict(w_in_p=w_in_p, conv_w=conv_w, conv_scale=conv_scale, gate_b=gate_b, an=an, ln=ln,
                cqn=l_cq_norm[None, :], ckvn=l_ckv_norm[None, :], wuq_p=wuq_p, wukv_p=wukv_p,
                w_out_p=w_out_p, m_norm_p=m_norm_p)


def kernel(x, c, ctx, c_ctx, mod_w, mod_b, norm_mix, norm_ffn, w_in, m_conv, m_gate_b, m_norm, a_qnorm, a_knorm,
           l_cq_norm, l_ckv_norm, l_wuq, l_wukv, l_qnorm, l_knorm, w_out, ffn_wg, ffn_wu, ffn_wd, moe_router,
           moe_wg, moe_wu, moe_wd):
    b, seq, d = x.shape
    ctx_len = ctx.shape[1]
    depth = mod_w.shape[0]
    assert ctx_len % TB == 0 and ctx_len % TQ == 0 and ctx_len == ML and seq % TK == 0 and seq % GRID_W == 0
    xs = jnp.concatenate([ctx, x], axis=1)
    mod_rows = 16
    cc = jnp.zeros((mod_rows, d), F32).at[:b].set(c).at[b].set(c_ctx)
    mod_all = _mod_table(cc, mod_w, mod_b)
    rope_tab = _rope_tables(seq, ctx_len)
    for i in range(depth):
        need_ctx = i < depth - 1
        lat = mod_all[i, :b].reshape(b, 1, 6, d)
        cm = jnp.broadcast_to(mod_all[i, b].reshape(1, 1, 6, d), (b, 1, 6, d))
        modtab = jnp.concatenate([cm, lat], axis=1)
        p = _layer_params(w_in[i], m_conv[i], m_gate_b[i], m_norm[i], a_qnorm[i], a_knorm[i], l_cq_norm[i],
                          l_ckv_norm[i], l_wuq[i], l_wukv[i], l_qnorm[i], l_knorm[i], w_out[i])
        mqk, mo, misc, q_all, k_all, mv_t, v_all = _inproj(
            xs, modtab, norm_mix[i][None, :], p["w_in_p"], rope_tab, p["an"], p["ln"], p["cqn"], p["ckvn"],
            p["wuq_p"], p["wukv_p"], ctx_len)
        q_t, k_c = _conv(mqk, p["conv_w"], p["conv_scale"], ctx_len)
        hf, hb = _mlstm(q_t, k_c, mv_t, misc, p["gate_b"])
        gqa_kv = tuple(h // (A_HEADS // A_KV) for h in range(A_HEADS))
        o_gqa = _attention(q_all, k_all, v_all, ctx_len, need_ctx, 0, KV_GQA, A_KV, gqa_kv)
        o_mla = _attention(q_all, k_all, v_all, ctx_len, need_ctx, Q_MLA, 0, L_HEADS, tuple(range(L_HEADS)))
        xs = _mixout(xs, modtab, hf, hb, mo, o_gqa, o_mla, p["m_norm_p"], p["w_out_p"], ctx_len, not need_ctx)
        ffn_ctx = ctx_len if need_ctx else 0
        j = i // 2
        if i % 2 == 0:
            xs = _ffn(xs, modtab, norm_ffn[i][None, :], ffn_wg[j].astype(BF16), ffn_wu[j].astype(BF16),
                      ffn_wd[j].astype(BF16), ffn_ctx)
        else:
            router_p = jnp.pad(moe_router[j], ((0, 0), (0, LANE - N_EXPERTS)))
            xs = _moe(xs, modtab, norm_ffn[i][None, :], router_p, moe_wg[j].astype(BF16),
                      moe_wu[j].astype(BF16), moe_wd[j].astype(BF16), ffn_ctx)
    return xs
```

```python
import functools
import math

import numpy as np
import jax
import jax.numpy as jnp
from jax import lax
from jax.experimental import pallas as pl
from jax.experimental.pallas import tpu as pltpu

F32 = jnp.float32
BF16 = jnp.bfloat16

GRID_W = 64
EPS = 1e-6
ROPE_THETA = 10000.0
M_HEADS, M_DH = 4, 96
A_HEADS, A_KV, A_DH = 6, 2, 64
L_HEADS, L_NOPE, L_ROPE, L_DV = 4, 64, 32, 64
L_QK = L_NOPE + L_ROPE
L_QRANK, L_KVRANK = 256, 128
N_EXPERTS, TOP_K = 8, 2
IN_SIZES = (384, 384, 384, 384, 16, 384, 128, 128, 256, 128, 32)

LANE = 128
SLOT = LANE
N_KVHEADS = A_KV + L_HEADS
Q_MLA = 8
Q_SLOTS = Q_MLA + L_HEADS
KV_GQA = L_HEADS
VMEM_LIMIT = 56 * 1024 * 1024

O_MQK, O_MV, O_MO, O_MISC = 0, 1024, 1536, 2048
O_QA, O_KA, O_VA, O_CQ, O_CKV, IN_PAD = 2176, 2944, 3200, 3456, 3712, 3840
MISC_KR = 64
DEN_LANE = M_DH
SUM_LANE = A_DH
LOG2E = math.log2(math.e)
assert A_DH == L_DV and SUM_LANE < SLOT

TB = 256
TQ = 256
TK = 512
TM_FFN = 1088
TF = 512
ML = 256
RT = 144
GT = 2 * RT
ST = 3 * LANE
TF_MOE = 1792


def _cparams(sem):
    return pltpu.CompilerParams(dimension_semantics=sem, vmem_limit_bytes=VMEM_LIMIT)


def _sigmoid(x):
    return 1.0 / (1.0 + jnp.exp(-x))


def _silu(x):
    return x * _sigmoid(x)


def _lane_iota(shape):
    return lax.broadcasted_iota(jnp.int32, shape, len(shape) - 1)


def _row_iota(shape):
    return lax.broadcasted_iota(jnp.int32, shape, 0)


def _modulated(x, gain, mod_ref, t0, ctx_len, k_shift, k_scale):
    tm = x.shape[0]
    is_ctx = (t0 + _row_iota((tm, 1))) < ctx_len
    shift = jnp.where(is_ctx, mod_ref[0, k_shift:k_shift + 1, :], mod_ref[1, k_shift:k_shift + 1, :])
    scale = jnp.where(is_ctx, mod_ref[0, k_scale:k_scale + 1, :], mod_ref[1, k_scale:k_scale + 1, :])
    y = x * lax.rsqrt(jnp.mean(x * x, axis=-1, keepdims=True) + EPS) * gain
    return y * (1.0 + scale) + shift


def _gate_rows(mod_ref, t0, tm, ctx_len, k_gate):
    is_ctx = (t0 + _row_iota((tm, 1))) < ctx_len
    return jnp.where(is_ctx, mod_ref[0, k_gate:k_gate + 1, :], mod_ref[1, k_gate:k_gate + 1, :])


def _mod_kernel(c_ref, w_ref, b_ref, o_ref):
    s = _silu(c_ref[...]).astype(BF16)
    o_ref[...] = jnp.dot(s, w_ref[...].astype(BF16), preferred_element_type=F32) + b_ref[...]


def _mod_table(cc, mod_w, mod_b):
    depth, d, n = mod_w.shape
    rows = cc.shape[0]
    return pl.pallas_call(
        _mod_kernel,
        grid=(depth, n // d),
        in_specs=[pl.BlockSpec((rows, d), lambda l, j: (0, 0)),
                  pl.BlockSpec((None, d, d), lambda l, j: (l, 0, j)),
                  pl.BlockSpec((None, 1, d), lambda l, j: (l, 0, j))],
        out_specs=pl.BlockSpec((None, rows, d), lambda l, j: (l, 0, j)),
        out_shape=jax.ShapeDtypeStruct((depth, rows, n), F32),
        compiler_params=_cparams(("arbitrary", "arbitrary")),
        name="mod_table",
    )(cc, mod_w, mod_b.reshape(depth, 1, n))


def _slot_rms(x, lo, hi):
    lane = _lane_iota(x.shape)
    sq = jnp.where((lane >= lo) & (lane < hi), x * x, 0.0)
    return lax.rsqrt(jnp.sum(sq, axis=-1, keepdims=True) * (1.0 / (hi - lo)) + EPS)


def _rope(x, cos, sin_m, sin_p, quarter):
    return (x * cos + pltpu.roll(x, LANE - quarter, 1) * sin_m + pltpu.roll(x, quarter, 1) * sin_p)


def _inproj_kernel(ctx_len, x_ref, mod_ref, g_ref, w_ref, rope_ref, an_ref, ln_ref, cqn_ref, ckvn_ref,
                   wuq_ref, wukv_ref, mqk_ref, mo_ref, misc_ref, q_ref, k_ref, mvt_ref, vt_ref):
    tm = x_ref.shape[0]
    t0 = pl.program_id(1) * tm
    h = _modulated(x_ref[...], g_ref[...], mod_ref, t0, ctx_len, 0, 1).astype(BF16)

    def proj(a, b):
        return jnp.dot(h, w_ref[:, a:b], preferred_element_type=F32)

    misc = proj(O_MISC, O_QA)
    misc_ref[...] = misc

    cos_a, sinm_a, sinp_a = rope_ref[0], rope_ref[1], rope_ref[2]
    cos_l, sinm_l, sinp_l = rope_ref[3], rope_ref[4], rope_ref[5]
    a_scale = A_DH ** -0.5 * LOG2E
    l_scale = L_QK ** -0.5 * LOG2E

    def with_ones_t(v):
        return jnp.where(_lane_iota(v.shape) % SLOT == SUM_LANE, 1.0, v).T.astype(BF16)

    pc = proj(O_CQ, IN_PAD)
    cq = pc[:, 0:L_QRANK]
    cq = (cq * lax.rsqrt(jnp.mean(cq * cq, axis=-1, keepdims=True) + EPS) * cqn_ref[...]).astype(BF16)
    ckv = pc[:, L_QRANK:L_QRANK + L_KVRANK]
    ckv = (ckv * lax.rsqrt(jnp.mean(ckv * ckv, axis=-1, keepdims=True) + EPS) * ckvn_ref[...]).astype(BF16)
    ql = jnp.dot(cq, wuq_ref[...], preferred_element_type=F32)
    kvl = jnp.dot(ckv, wukv_ref[...], preferred_element_type=F32)
    gq_l, gk_l = ln_ref[0:1, :], ln_ref[1:2, :]
    lane = _lane_iota((tm, SLOT))
    kr = jnp.where((lane >= MISC_KR) & (lane < MISC_KR + L_ROPE), misc, 0.0)
    kr = kr * _slot_rms(kr, MISC_KR, MISC_KR + L_ROPE) * gk_l
    kr = _rope(kr, cos_l, sinm_l, sinp_l, L_ROPE // 4)
    for i in range(L_HEADS):
        x = ql[:, i * SLOT:(i + 1) * SLOT]
        inv = jnp.where(lane < L_NOPE, _slot_rms(x, 0, L_NOPE), _slot_rms(x, L_NOPE, L_QK))
        x = x * inv * gq_l
        q_ref[:, (Q_MLA + i) * SLOT:(Q_MLA + i + 1) * SLOT] = (
            _rope(x, cos_l, sinm_l, sinp_l, L_ROPE // 4) * l_scale).astype(BF16)
        kn = kvl[:, i * SLOT:(i + 1) * SLOT]
        kn = kn * _slot_rms(kn, 0, L_NOPE) * gk_l
        k_ref[:, i * SLOT:(i + 1) * SLOT] = (kn + kr).astype(BF16)
    vt_ref[0:KV_GQA * SLOT, :] = with_ones_t(kvl[:, L_HEADS * SLOT:])

    pa = proj(O_QA, O_CQ)
    gq, gk = an_ref[0:1, :], an_ref[1:2, :]
    for i in range(A_HEADS):
        x = pa[:, i * SLOT:(i + 1) * SLOT]
        x = x * _slot_rms(x, 0, A_DH) * gq
        q_ref[:, i * SLOT:(i + 1) * SLOT] = (_rope(x, cos_a, sinm_a, sinp_a, A_DH // 4) * a_scale).astype(BF16)
    for i in range(A_KV):
        x = pa[:, (A_HEADS + i) * SLOT:(A_HEADS + i + 1) * SLOT]
        x = x * _slot_rms(x, 0, A_DH) * gk
        k_ref[:, (KV_GQA + i) * SLOT:(KV_GQA + i + 1) * SLOT] = (
            _rope(x, cos_a, sinm_a, sinp_a, A_DH // 4).astype(BF16))
    vt_ref[KV_GQA * SLOT:, :] = with_ones_t(pa[:, (A_HEADS + A_KV) * SLOT:(A_HEADS + 2 * A_KV) * SLOT])
    q_ref[:, A_HEADS * SLOT:Q_MLA * SLOT] = jnp.zeros((tm, (Q_MLA - A_HEADS) * SLOT), BF16)

    mv = proj(O_MV, O_MO)
    mvt_ref[...] = jnp.where(_lane_iota(mv.shape) % SLOT == DEN_LANE, 1.0, mv).T.astype(BF16)
    mqk_ref[...] = proj(O_MQK, O_MV).astype(BF16)
    mo_ref[...] = proj(O_MO, O_MISC).astype(BF16)


def _inproj(xs, modtab, gain, w_in_p, rope_tab, an, ln, cqn, ckvn, wuq_p, wukv_p, ctx_len):
    b, s, d = xs.shape
    grid = (b, s // TB)
    tok = lambda n: pl.BlockSpec((None, TB, n), lambda i, j: (i, j, 0))
    full = lambda a: pl.BlockSpec(a.shape, lambda i, j: (0,) * a.ndim)
    out_widths = (1024, 512, SLOT, Q_SLOTS * SLOT, N_KVHEADS * SLOT)
    out_dtypes = (BF16, BF16, F32, BF16, BF16)
    out_specs = [tok(n) for n in out_widths]
    out_shape = [jax.ShapeDtypeStruct((b, s, n), dt) for n, dt in zip(out_widths, out_dtypes)]
    for n in (M_HEADS * SLOT, N_KVHEADS * SLOT):
        out_specs.append(pl.BlockSpec((None, n, TB), lambda i, j: (i, 0, j)))
        out_shape.append(jax.ShapeDtypeStruct((b, n, s), BF16))
    return pl.pallas_call(
        functools.partial(_inproj_kernel, ctx_len),
        grid=grid,
        in_specs=[tok(d),
                  pl.BlockSpec((None, 2, 6, d), lambda i, j: (i, 0, 0, 0)),
                  full(gain), full(w_in_p),
                  pl.BlockSpec((6, TB, SLOT), lambda i, j: (0, j, 0)),
                  full(an), full(ln), full(cqn), full(ckvn), full(wuq_p), full(wukv_p)],
        out_specs=out_specs,
        out_shape=out_shape,
        compiler_params=_cparams(("parallel", "parallel")),
        name="inproj",
    )(xs, modtab, gain, w_in_p, rope_tab, an, ln, cqn, ckvn, wuq_p, wukv_p)


def _conv_kernel(ctx_len, s_len, x_ref, prev_ref, next_ref, w_ref, sc_ref, qt_ref, k_ref):
    tm = x_ref.shape[0]
    t0 = pl.program_id(1) * tm
    x = x_ref[...].astype(F32)
    row = _row_iota((tm, 1))
    has_prev = jnp.logical_and(t0 != 0, t0 != ctx_len)
    has_next = jnp.logical_and(t0 + tm != ctx_len, t0 + tm != s_len)
    hp = jnp.where(has_prev, prev_ref[15:16, :].astype(F32), 0.0)
    hn = jnp.where(has_next, next_ref[0:1, :].astype(F32), 0.0)
    xp = jnp.where(row == 0, hp, pltpu.roll(x, 1, 0))
    xn = jnp.where(row == tm - 1, hn, pltpu.roll(x, tm - 1, 0))
    y = xp * w_ref[0:1, :] + x * w_ref[1:2, :] + xn * w_ref[2:3, :]
    y = _silu(y) * sc_ref[...]
    half = y.shape[1] // 2
    qt_ref[...] = y[:, :half].T.astype(qt_ref.dtype)
    k_ref[...] = y[:, half:].astype(k_ref.dtype)


def _conv(mqk, conv_w, conv_scale, ctx_len):
    b, s, n = mqk.shape
    hb = TB // 16
    last = s // 16 - 1
    return pl.pallas_call(
        functools.partial(_conv_kernel, ctx_len, s),
        grid=(b, s // TB),
        in_specs=[pl.BlockSpec((None, TB, n), lambda i, j: (i, j, 0)),
                  pl.BlockSpec((None, 16, n), lambda i, j: (i, jnp.maximum(j * hb - 1, 0), 0)),
                  pl.BlockSpec((None, 16, n), lambda i, j: (i, jnp.minimum((j + 1) * hb, last), 0)),
                  pl.BlockSpec((3, n), lambda i, j: (0, 0)),
                  pl.BlockSpec((1, n), lambda i, j: (0, 0))],
        out_specs=[pl.BlockSpec((None, n // 2, TB), lambda i, j: (i, 0, j)),
                   pl.BlockSpec((None, TB, n // 2), lambda i, j: (i, j, 0))],
        out_shape=[jax.ShapeDtypeStruct((b, n // 2, s), BF16), jax.ShapeDtypeStruct((b, s, n // 2), BF16)],
        compiler_params=_cparams(("parallel", "parallel")),
        name="mlstm_conv",
    )(mqk, mqk, mqk, conv_w, conv_scale)


def _log_sigmoid(x):
    return jnp.minimum(x, 0.0) - jnp.log(1.0 + jnp.exp(-jnp.abs(x)))


def _mlstm_kernel(n_chunks, qt_ref, k_ref, vt_ref, misc_ref, gb_ref, hf_ref, hb_ref, c_ref, m_ref, gt_sc, bt_sc):
    hp = pl.program_id(1)
    c_ref[...] = jnp.zeros_like(c_ref)
    m_ref[...] = jnp.zeros_like(m_ref)
    r = _row_iota((ML, ML))
    cidx = _lane_iota((ML, ML))
    tri_f32 = (r >= cidx).astype(F32)
    tri_b32 = (r <= cidx).astype(F32)
    lane = _lane_iota((ML, SLOT))
    lane_t = _lane_iota((1, ML))

    def chunk(t0, d, tri32, out_ref):
        g = misc_ref[pl.ds(t0, ML), :] + gb_ref[...]
        logf = _log_sigmoid(g)
        bcum = jnp.dot(tri32, logf, preferred_element_type=F32, precision=lax.Precision.HIGHEST)
        gt_sc[d] = g.T
        bt_sc[d] = bcum.T
        ib = g - pltpu.roll(bcum, SLOT - M_HEADS, 1)
        allowed = (r <= cidx) if d == 0 else (r >= cidx)
        for hh in range(2):
            li = d * 2 * M_HEADS + hp * 2 + hh
            ib_col = jnp.sum(jnp.where(lane == li, ib, 0.0), axis=-1, keepdims=True)
            i_row = gt_sc[d, pl.ds(li, 1), :]
            b_row = bt_sc[d, pl.ds(li + M_HEADS, 1), :]
            sidx = d * 2 + hh
            ct_st = c_ref[sidx]
            m_s = m_ref[sidx]
            k = k_ref[pl.ds(t0, ML), hh * SLOT:(hh + 1) * SLOT]
            qt = qt_ref[hh * SLOT:(hh + 1) * SLOT, pl.ds(t0, ML)]
            vt = vt_ref[hh * SLOT:(hh + 1) * SLOT, pl.ds(t0, ML)]
            dmat = jnp.where(allowed, b_row + ib_col, -jnp.inf)
            m_inter = b_row + m_s
            m_t = jnp.maximum(m_inter, jnp.max(dmat, axis=0, keepdims=True))
            kq = jnp.dot(k, qt, preferred_element_type=F32)
            w = (jnp.exp(dmat - m_t) * kq).astype(BF16)
            a_inter = jnp.exp(m_inter - m_t)
            num = a_inter * jnp.dot(ct_st.astype(BF16), qt, preferred_element_type=F32) + jnp.dot(
                vt, w, preferred_element_type=F32)
            den = num[DEN_LANE:DEN_LANE + 1, :]
            h_out = num / jnp.maximum(jnp.abs(den), jnp.exp(-m_t))
            out_ref[pl.ds(t0, ML), hh * SLOT:(hh + 1) * SLOT] = h_out.T.astype(out_ref.dtype)
            last = ML - 1 if d == 0 else 0
            total = jnp.sum(jnp.where(lane_t == last, b_row, 0.0), axis=-1, keepdims=True)
            gg = total - b_row + i_row
            m_new = jnp.maximum(total + m_s, jnp.max(gg, axis=-1, keepdims=True))
            decay = jnp.exp(total + m_s - m_new)[:, 0:1]
            wk = jnp.exp(gg - m_new)
            vw = (vt.astype(F32) * wk).astype(BF16)
            c_ref[sidx] = decay * ct_st + jnp.dot(vw, k, preferred_element_type=F32)
            m_ref[sidx] = m_new

    def step(n, carry):
        tf0 = pl.multiple_of(n * ML, ML)
        tb0 = pl.multiple_of(jnp.where(n == 0, 0, n_chunks - n) * ML, ML)
        chunk(tf0, 0, tri_f32, hf_ref)
        chunk(tb0, 1, tri_b32, hb_ref)
        return carry

    lax.fori_loop(0, n_chunks, step, 0)


def _mlstm(q_t, k_c, mv_t, misc, gate_b):
    b, s, _ = k_c.shape
    n_chunks = s // ML
    pair = 2 * SLOT
    spec = pl.BlockSpec((None, s, pair), lambda i, j: (i, 0, j))
    spec_t = pl.BlockSpec((None, pair, s), lambda i, j: (i, j, 0))
    return pl.pallas_call(
        functools.partial(_mlstm_kernel, n_chunks),
        grid=(b, M_HEADS // 2),
        in_specs=[spec_t, spec, spec_t,
                  pl.BlockSpec((None, s, SLOT), lambda i, j: (i, 0, 0)),
                  pl.BlockSpec((1, SLOT), lambda i, j: (0, 0))],
        out_specs=[spec, spec],
        out_shape=[jax.ShapeDtypeStruct((b, s, M_HEADS * SLOT), BF16)] * 2,
        scratch_shapes=[pltpu.VMEM((4, SLOT, SLOT), F32), pltpu.VMEM((4, 1, ML), F32),
                        pltpu.VMEM((2, SLOT, ML), F32), pltpu.VMEM((2, SLOT, ML), F32)],
        compiler_params=_cparams(("parallel", "parallel")),
        name="mlstm_scan",
    )(q_t, k_c, mv_t, misc, gate_b)


def _attn_kernel(ctx_len, q_off, kv_of, q_ref, k_ref, vt_ref, o_ref, st_sc, pt_sc, acc_sc):
    s_len = k_ref.shape[0]
    n_heads = len(kv_of)
    qs = [q_ref[:, g * SLOT:(g + 1) * SLOT] for g in range(n_heads)]

    def kv_cols(g):
        return slice(kv_of[g] * SLOT, (kv_of[g] + 1) * SLOT)

    def chunk(carry, rows):
        out = []
        for g in range(n_heads):
            m, acc = carry[g]
            k = k_ref[rows, kv_cols(g)]
            vt = vt_ref[kv_cols(g), rows]
            st = lax.dot_general(k, qs[g], (((1,), (1,)), ((), ())), preferred_element_type=F32)
            m_new = jnp.maximum(m, jnp.max(st, axis=0, keepdims=True))
            alpha = jnp.exp2(m - m_new)
            pt = jnp.exp2(st - m_new).astype(BF16)
            out.append((m_new, alpha * acc + jnp.dot(vt, pt, preferred_element_type=F32)))
        return tuple(out)

    def finish(carry):
        for g in range(n_heads):
            acc = carry[g][1]
            o = acc / acc[SUM_LANE:SUM_LANE + 1, :]
            o_ref[:, g * SLOT:(g + 1) * SLOT] = o.T.astype(o_ref.dtype)

    init = tuple((jnp.full((1, TQ), -jnp.inf, F32), jnp.zeros((SLOT, TQ), F32)) for _ in range(n_heads))
    is_ctx = (pl.program_id(2) + q_off) * TQ < ctx_len

    @pl.when(is_ctx)
    def _():
        finish(chunk(init, pl.ds(0, ctx_len)))

    n_lat = (s_len - ctx_len) // TK
    n0 = ctx_len + TK
    assert n_lat >= 4 and n_lat % 2 == 0 and st_sc.shape[1] == n0

    def rows_of(c):
        if isinstance(c, int):
            return (pl.ds(0, n0), n0) if c == 0 else (pl.ds(ctx_len + c * TK, TK), TK)
        return pl.ds(pl.multiple_of(ctx_len + c * TK, math.gcd(ctx_len, TK)), TK), TK

    def scores(c, par):
        rows, n = rows_of(c)
        for g in range(n_heads):
            st_sc[par * n_heads + g, 0:n, :] = lax.dot_general(
                k_ref[rows, kv_cols(g)], qs[g], (((1,), (1,)), ((), ())), preferred_element_type=F32)

    def softmax(par, m, n=TK):
        ms, alphas = [], []
        for g in range(n_heads):
            st = st_sc[par * n_heads + g, 0:n, :]
            m_new = jnp.maximum(m[g], jnp.max(st, axis=0, keepdims=True))
            pt_sc[par * n_heads + g, 0:n, :] = jnp.exp2(st - m_new).astype(BF16)
            ms.append(m_new)
            alphas.append(jnp.exp2(m[g] - m_new))
        return tuple(ms), tuple(alphas)

    def values(c, par, alpha):
        rows, n = rows_of(c)
        for g in range(n_heads):
            acc_sc[g] = alpha[g] * acc_sc[g] + jnp.dot(vt_ref[kv_cols(g), rows], pt_sc[par * n_heads + g, 0:n, :],
                                                       preferred_element_type=F32)

    def stage(c, par, m, alpha):
        scores(c + 1, 1 - par)
        m_new, alpha_new = softmax(par, m)
        values(c - 1, 1 - par, alpha)
        return m_new, alpha_new

    @pl.when(jnp.logical_not(is_ctx))
    def _():
        for g in range(n_heads):
            acc_sc[g] = init[g][1]
        scores(0, 0)
        m, alpha = softmax(0, tuple(c[0] for c in init), n0)
        scores(1, 1)
        m, alpha = stage(1, 1, m, alpha)

        def body(t, carry):
            c = 2 * t + 2
            return stage(c + 1, 1, *stage(c, 0, *carry))

        m, alpha = lax.fori_loop(0, (n_lat - 4) // 2, body, (m, alpha))
        m, alpha = stage(n_lat - 2, 0, m, alpha)
        values(n_lat - 2, 0, alpha)
        m, alpha = softmax(1, m)
        values(n_lat - 1, 1, alpha)
        finish(tuple((m[g], acc_sc[g]) for g in range(n_heads)))


def _attention(q_all, k_all, vt_all, ctx_len, need_ctx, q_slot0, kv_slot0, n_kv, kv_of):
    b, s, _ = q_all.shape
    q_off = 0 if need_ctx else ctx_len // TQ
    nq = s // TQ - q_off
    n_heads = len(kv_of)
    qw = n_heads * SLOT
    kw = n_kv * SLOT
    assert (q_slot0 * SLOT) % qw == 0 and (kv_slot0 * SLOT) % kw == 0
    qb0, kb0 = q_slot0 * SLOT // qw, kv_slot0 * SLOT // kw
    return pl.pallas_call(
        functools.partial(_attn_kernel, ctx_len, q_off, kv_of),
        grid=(b, 1, nq),
        in_specs=[pl.BlockSpec((None, TQ, qw), lambda i, g, j: (i, j + q_off, qb0)),
                  pl.BlockSpec((None, s, kw), lambda i, g, j: (i, 0, kb0)),
                  pl.BlockSpec((None, kw, s), lambda i, g, j: (i, kb0, 0))],
        out_specs=pl.BlockSpec((None, TQ, qw), lambda i, g, j: (i, j + q_off, 0)),
        out_shape=jax.ShapeDtypeStruct((b, s, qw), BF16),
        scratch_shapes=[pltpu.VMEM((2 * n_heads, ctx_len + TK, TQ), F32),
                        pltpu.VMEM((2 * n_heads, ctx_len + TK, TQ), BF16),
                        pltpu.VMEM((n_heads, SLOT, TQ), F32)],
        compiler_params=_cparams(("parallel", "parallel", "parallel")),
        name="attention",
    )(q_all, k_all, vt_all)


def _mixout_kernel(ctx_len, blk_off, x_ref, mod_ref, hf_ref, hb_ref, mo_ref, oa_ref, ol_ref, mn_ref, w_ref, o_ref):
    tm = x_ref.shape[0]
    t0 = (pl.program_id(1) + blk_off) * tm
    hm = hf_ref[...].astype(F32) + hb_ref[...].astype(F32)
    gate = _sigmoid(mo_ref[...].astype(F32))
    parts = []
    for i in range(M_HEADS):
        x = hm[:, i * SLOT:(i + 1) * SLOT]
        inv = lax.rsqrt(jnp.sum(x * x, axis=-1, keepdims=True) * (1.0 / M_DH) + EPS)
        parts.append(x * inv)
    hn = jnp.concatenate(parts, axis=-1) * mn_ref[...] * gate
    nm = M_HEADS * SLOT
    o = jnp.dot(hn.astype(BF16), w_ref[0:nm, :], preferred_element_type=F32)
    na = nm + A_HEADS * SLOT
    o = o + jnp.dot(oa_ref[...], w_ref[nm:na, :], preferred_element_type=F32)
    o = o + jnp.dot(ol_ref[...], w_ref[na:, :], preferred_element_type=F32)
    o_ref[...] = x_ref[...] + _gate_rows(mod_ref, t0, tm, ctx_len, 2) * o


def _mixout(xs, modtab, hf, hb, mo, o_gqa, o_mla, m_norm_p, w_out_p, ctx_len, lat_only):
    b, s, d = xs.shape
    blk_off = ctx_len // TB if lat_only else 0
    tok = lambda n: pl.BlockSpec((None, TB, n), lambda i, j: (i, j + blk_off, 0))
    full = lambda a: pl.BlockSpec(a.shape, lambda i, j: (0,) * a.ndim)
    return pl.pallas_call(
        functools.partial(_mixout_kernel, ctx_len, blk_off),
        grid=(b, s // TB - blk_off),
        in_specs=[tok(d),
                  pl.BlockSpec((None, 2, 6, d), lambda i, j: (i, 0, 0, 0)),
                  tok(M_HEADS * SLOT), tok(M_HEADS * SLOT), tok(M_HEADS * SLOT), tok(A_HEADS * SLOT),
                  tok(L_HEADS * SLOT), full(m_norm_p), full(w_out_p)],
        out_specs=pl.BlockSpec((None, TB, d), lambda i, j: (i, j, 0)),
        out_shape=jax.ShapeDtypeStruct((b, s - blk_off * TB, d), F32),
        input_output_aliases={} if lat_only else {0: 0},
        compiler_params=_cparams(("parallel", "parallel")),
        name="mix_out",
    )(xs, modtab, hf, hb, mo, o_gqa, o_mla, m_norm_p, w_out_p)


def _ffn_kernel(ctx_len, x_ref, mod_ref, g_ref, wg_ref, wu_ref, wd_ref, o_ref, h_sc, acc_sc):
    tm = x_ref.shape[0]
    t0 = pl.program_id(1) * tm
    f = pl.program_id(2)

    @pl.when(f == 0)
    def _():
        h_sc[...] = _modulated(x_ref[...], g_ref[...], mod_ref, t0, ctx_len, 3, 4).astype(BF16)
        acc_sc[...] = jnp.zeros_like(acc_sc)

    h = h_sc[...]
    a = jnp.dot(h, wg_ref[...], preferred_element_type=F32)
    u = jnp.dot(h, wu_ref[...], preferred_element_type=F32)
    acc_sc[...] += jnp.dot((_silu(a) * u).astype(BF16), wd_ref[...], preferred_element_type=F32)

    @pl.when(f == pl.num_programs(2) - 1)
    def _():
        o_ref[...] = x_ref[...] + _gate_rows(mod_ref, t0, tm, ctx_len, 5) * acc_sc[...]


def _ffn(xs, modtab, gain, wg, wu, wd, ctx_len):
    b, s, d = xs.shape
    tm = s // 4
    nf = wg.shape[1] // TF
    return pl.pallas_call(
        functools.partial(_ffn_kernel, ctx_len),
        grid=(b, s // tm, nf),
        in_specs=[pl.BlockSpec((None, tm, d), lambda i, j, f: (i, j, 0)),
                  pl.BlockSpec((None, 2, 6, d), lambda i, j, f: (i, 0, 0, 0)),
                  pl.BlockSpec((1, d), lambda i, j, f: (0, 0)),
                  pl.BlockSpec((d, TF), lambda i, j, f: (0, f)),
                  pl.BlockSpec((d, TF), lambda i, j, f: (0, f)),
                  pl.BlockSpec((TF, d), lambda i, j, f: (f, 0))],
        out_specs=pl.BlockSpec((None, tm, d), lambda i, j, f: (i, j, 0)),
        out_shape=jax.ShapeDtypeStruct((b, s, d), F32),
        scratch_shapes=[pltpu.VMEM((tm, d), BF16), pltpu.VMEM((tm, d), F32)],
        input_output_aliases={0: 0},
        compiler_params=_cparams(("parallel", "parallel", "arbitrary")),
        name="ffn_dense",
    )(xs, modtab, gain, wg, wu, wd)


def _top2_combine(logits):
    lane = _lane_iota(logits.shape)
    lane_f = lane.astype(F32)
    lg = jnp.where(lane < N_EXPERTS, logits, -jnp.inf)
    v1 = jnp.max(lg, axis=-1, keepdims=True)
    i1 = jnp.min(jnp.where(lg == v1, lane_f, float(LANE)), axis=-1, keepdims=True)
    rest = jnp.where(lane_f == i1, -jnp.inf, lg)
    v2 = jnp.max(rest, axis=-1, keepdims=True)
    i2 = jnp.min(jnp.where(rest == v2, lane_f, float(LANE)), axis=-1, keepdims=True)
    e2 = jnp.exp(v2 - v1)
    w1 = 1.0 / (1.0 + e2)
    w2 = e2 / (1.0 + e2)
    sel = jnp.where((lane_f == i1) | (lane_f == i2), 1.0, 0.0)
    return jnp.where(lane_f == i1, w1, 0.0) + jnp.where(lane_f == i2, w2, 0.0), sel


def _moe_kernel(ctx_len, x_ref, mod_ref, g_ref, r_ref, tri_ref, wg_ref, wu_ref, wd_ref, o_ref,
                h_sc, comb_sc, rank_sc, rankt_sc, rankc_sc, wc_sc, xg_sc, acc_sc, nt_sc):
    tm = x_ref.shape[0]
    t0 = pl.program_id(1) * tm
    e = pl.program_id(2)
    f = pl.program_id(3)

    @pl.when(jnp.logical_and(e == 0, f == 0))
    def _():
        h = _modulated(x_ref[...], g_ref[...], mod_ref, t0, ctx_len, 3, 4)
        logits = jnp.dot(h, r_ref[...], preferred_element_type=F32, precision=lax.Precision.HIGHEST)
        comb, sel = _top2_combine(logits)
        rank =jnp.dot(tri_ref[...], sel.astype(BF16), preferred_element_type=F32)
        rank = jnp.where(sel > 0.0, rank, -1.0)
        comb_sc[...] = comb
        rank_sc[...] = rank
        pad = rankt_sc.shape[1] - tm
        h_sc[0:tm, :] = h.astype(BF16)
        if pad:
            rankt_sc[...] = jnp.concatenate([rank, jnp.full((pad, LANE), -1.0, F32)], axis=0).T
            h_sc[tm:, :] = jnp.zeros((pad, h_sc.shape[1]), BF16)
        else:
            rankt_sc[...] = rank.T
        o_ref[...] = jnp.zeros_like(o_ref)

    @pl.when(f == 0)
    def _():
        lane = _lane_iota((tm, LANE))
        rank_c = jnp.sum(jnp.where(lane == e, rank_sc[...], 0.0), axis=-1, keepdims=True)
        rankc_sc[...] = rank_c
        wc_sc[...] = jnp.sum(jnp.where(lane == e, comb_sc[...], 0.0), axis=-1, keepdims=True)
        n_rows = jnp.sum(jnp.where(rank_c >= 0.0, 1.0, 0.0)).astype(jnp.int32)
        n_rt = lax.div(n_rows + (RT - 1), RT)
        n_gt = lax.div(n_rt * RT + (GT - 1), GT)
        n_st = lax.div(n_rt * RT + (ST - 1), ST)
        nt_sc[0] = n_rt
        nt_sc[1] = n_st
        rank_r = rankt_sc[pl.ds(e, 1), :]

        def gather(i, c):
            r0 = pl.multiple_of(i * GT, 16)
            tgt = (r0 + _row_iota((GT, 1))).astype(F32)
            sel_t = jnp.where(rank_r == tgt, 1.0, 0.0).astype(BF16)
            xg_sc[pl.ds(r0, GT), :] = jnp.dot(sel_t, h_sc[...], preferred_element_type=F32).astype(BF16)
            return c

        def clear(i, c):
            acc_sc[pl.ds(pl.multiple_of(i * ST, ST), ST), :] = jnp.zeros((ST, acc_sc.shape[1]), F32)
            return c

        lax.fori_loop(0, n_gt, gather, 0)
        lax.fori_loop(0, n_st, clear, 0)

    def expert(i, c):
        r0 = pl.multiple_of(i * RT, 16)
        rows = xg_sc[pl.ds(r0, RT), :]
        a = jnp.dot(rows, wg_ref[...], preferred_element_type=F32)
        u = jnp.dot(rows, wu_ref[...], preferred_element_type=F32)
        acc_sc[pl.ds(r0, RT), :] += jnp.dot((_silu(a) * u).astype(BF16), wd_ref[...],
                                            preferred_element_type=F32)
        return c

    lax.fori_loop(0, nt_sc[0], expert, 0)

    @pl.when(f == pl.num_programs(3) - 1)
    def _():
        def scatter(i, c):
            r0 = pl.multiple_of(i * ST, ST)
            tgt = (r0 + _lane_iota((1, ST))).astype(F32)
            w_t = jnp.where(rankc_sc[...] == tgt, wc_sc[...], 0.0).astype(BF16)
            o_ref[...] += jnp.dot(w_t, acc_sc[pl.ds(r0, ST), :].astype(BF16), preferred_element_type=F32)
            return c

        lax.fori_loop(0, nt_sc[1], scatter, 0)

    @pl.when(jnp.logical_and(e == pl.num_programs(2) - 1, f == pl.num_programs(3) - 1))
    def _():
        o_ref[...] = x_ref[...] + _gate_rows(mod_ref, t0, tm, ctx_len, 5) * o_ref[...]


def _moe(xs, modtab, gain, router_p, wg, wu, wd, ctx_len):
    b, s, d = xs.shape
    tm = s // 4
    tp = -(-tm // LANE) * LANE
    r_max = -(-tm // RT) * RT
    tr = max(-(-r_max // GT) * GT, -(-r_max // ST) * ST)
    ne, _, dff = wg.shape
    tri = jnp.tril(jnp.ones((tm, tm), BF16), -1)
    TF = TF_MOE
    assert dff % TF == 0
    return pl.pallas_call(
        functools.partial(_moe_kernel, ctx_len),
        grid=(b, s // tm, ne, dff // TF),
        in_specs=[pl.BlockSpec((None, tm, d), lambda i, j, e, f: (i, j, 0), pipeline_mode=pl.Buffered(1)),
                  pl.BlockSpec((None, 2, 6, d), lambda i, j, e, f: (i, 0, 0, 0)),
                  pl.BlockSpec((1, d), lambda i, j, e, f: (0, 0)),
                  pl.BlockSpec((d, LANE), lambda i, j, e, f: (0, 0)),
                  pl.BlockSpec((tm, tm), lambda i, j, e, f: (0, 0), pipeline_mode=pl.Buffered(1)),
                  pl.BlockSpec((None, d, TF), lambda i, j, e, f: (e, 0, f)),
                  pl.BlockSpec((None, d, TF), lambda i, j, e, f: (e, 0, f)),
                  pl.BlockSpec((None, TF, d), lambda i, j, e, f: (e, f, 0))],
        out_specs=pl.BlockSpec((None, tm, d), lambda i, j, e, f: (i, j, 0)),
        out_shape=jax.ShapeDtypeStruct((b, s, d), F32),
        scratch_shapes=[pltpu.VMEM((tp, d), BF16),
                        pltpu.VMEM((tm, LANE), F32),
                        pltpu.VMEM((tm, LANE), F32),
                        pltpu.VMEM((LANE, tp), F32),
                        pltpu.VMEM((tm, 1), F32),
                        pltpu.VMEM((tm, 1), F32),
                        pltpu.VMEM((tr, d), BF16),
                        pltpu.VMEM((tr, d), F32),
                        pltpu.SMEM((2,), jnp.int32)],
        input_output_aliases={0: 0},
        compiler_params=_cparams(("parallel", "parallel", "arbitrary", "arbitrary")),
        name="moe_top2",
    )(xs, modtab, gain, router_p, tri, wg, wu, wd)


def _pad_heads(w, n_heads, dh, axis=-1):
    axis = axis % w.ndim
    shp = w.shape[:axis] + (n_heads, dh) + w.shape[axis + 1:]
    pad = [(0, 0)] * (w.ndim + 1)
    pad[axis + 1] = (0, SLOT - dh)
    out = jnp.pad(w.reshape(shp), pad)
    return out.reshape(w.shape[:axis] + (n_heads * SLOT,) + w.shape[axis + 1:])


def _rope_tables(seq, ctx_len):
    t = jnp.arange(seq)
    rows = (t // GRID_W).astype(F32)
    cols = (t % GRID_W).astype(F32)

    def angles(rot_dim):
        nf = rot_dim // 4
        inv = ROPE_THETA ** (-jnp.arange(nf, dtype=F32) / nf)
        ar = rows[:, None] * inv
        ac = cols[:, None] * inv
        return jnp.concatenate([ar, ar, ac, ac], axis=-1)

    def slot_tables(rot_dim, lane0):
        ang = angles(rot_dim)
        quarter = rot_dim // 4
        first = (jnp.arange(rot_dim) % (2 * quarter)) < quarter
        cos = jnp.ones((seq, SLOT), F32).at[:, lane0:lane0 + rot_dim].set(jnp.cos(ang))
        sin = jnp.sin(ang)
        sin_m = jnp.zeros((seq, SLOT), F32).at[:, lane0:lane0 + rot_dim].set(jnp.where(first, -sin, 0.0))
        sin_p = jnp.zeros((seq, SLOT), F32).at[:, lane0:lane0 + rot_dim].set(jnp.where(first, 0.0, sin))
        ident = jnp.stack([jnp.ones((ctx_len, SLOT), F32), jnp.zeros((ctx_len, SLOT), F32),
                           jnp.zeros((ctx_len, SLOT), F32)])
        return jnp.concatenate([ident, jnp.stack([cos, sin_m, sin_p])], axis=1)

    return jnp.concatenate([slot_tables(A_DH, 0), slot_tables(L_ROPE, L_NOPE)], axis=0)


def _layer_params(w_in, m_conv, m_gate_b, m_norm, a_qnorm, a_knorm, l_cq_norm, l_ckv_norm, l_wuq, l_wukv,
                  l_qnorm, l_knorm, w_out):
    d = w_in.shape[0]
    offs = np.cumsum((0,) + IN_SIZES)
    seg = [w_in[:, offs[i]:offs[i + 1]] for i in range(len(IN_SIZES))]
    misc = jnp.zeros((d, SLOT), F32).at[:, 0:4 * M_HEADS].set(seg[4]).at[:, MISC_KR:MISC_KR + L_ROPE].set(seg[10])
    w_in_p = jnp.concatenate(
        [_pad_heads(seg[0], M_HEADS, M_DH), _pad_heads(seg[1], M_HEADS, M_DH),
         _pad_heads(seg[2], M_HEADS, M_DH), _pad_heads(seg[3], M_HEADS, M_DH), misc,
         _pad_heads(seg[5], A_HEADS, A_DH), _pad_heads(seg[6], A_KV, A_DH), _pad_heads(seg[7], A_KV, A_DH),
         seg[8], seg[9]], axis=1).astype(BF16)
    conv_w = jnp.concatenate([_pad_heads(m_conv[:, :M_HEADS * M_DH], M_HEADS, M_DH),
                              _pad_heads(m_conv[:, M_HEADS * M_DH:], M_HEADS, M_DH)], axis=1)
    conv_scale = jnp.concatenate([jnp.ones((1, M_HEADS * SLOT), F32),
                                  jnp.full((1, M_HEADS * SLOT), M_DH ** -0.5, F32)], axis=1)
    gate_b = jnp.zeros((1, SLOT), F32).at[0, 0:4 * M_HEADS].set(m_gate_b)
    pad1 = lambda g: jnp.pad(g, (0, SLOT - g.shape[0]))
    an = jnp.stack([pad1(a_qnorm), pad1(a_knorm)])
    ln = jnp.stack([pad1(l_qnorm), pad1(l_knorm)])
    wuq_p = _pad_heads(l_wuq, L_HEADS, L_QK).astype(BF16)
    kv = l_wukv.reshape(L_KVRANK, L_HEADS, L_NOPE + L_DV)
    wukv_p = jnp.concatenate(
        [_pad_heads(kv[:, :, :L_NOPE].reshape(L_KVRANK, -1), L_HEADS, L_NOPE),
         _pad_heads(kv[:, :, L_NOPE:].reshape(L_KVRANK, -1), L_HEADS, L_DV)], axis=1).astype(BF16)
    nm, na = M_HEADS * M_DH, A_HEADS * A_DH
    w_out_p = jnp.concatenate(
        [_pad_heads(w_out[:nm], M_HEADS, M_DH, axis=0), _pad_heads(w_out[nm:nm + na], A_HEADS, A_DH, axis=0),
         _pad_heads(w_out[nm + na:], L_HEADS, L_DV, axis=0)], axis=0).astype(BF16)
    m_norm_p = _pad_heads(m_norm[None, :], M_HEADS, M_DH)
    return dict(w_in_p=w_in_p, conv_w=conv_w, conv_scale=conv_scale, gate_b=gate_b, an=an, ln=ln,
                cqn=l_cq_norm[None, :], ckvn=l_ckv_norm[None, :], wuq_p=wuq_p, wukv_p=wukv_p,
                w_out_p=w_out_p, m_norm_p=m_norm_p)


def kernel(x, c, ctx, c_ctx, mod_w, mod_b, norm_mix, norm_ffn, w_in, m_conv, m_gate_b, m_norm, a_qnorm, a_knorm,
           l_cq_norm, l_ckv_norm, l_wuq, l_wukv, l_qnorm, l_knorm, w_out, ffn_wg, ffn_wu, ffn_wd, moe_router,
           moe_wg, moe_wu, moe_wd):
    b, seq, d = x.shape
    ctx_len = ctx.shape[1]
    depth = mod_w.shape[0]
    assert ctx_len % TB == 0 and ctx_len % TQ == 0 and ctx_len == ML and seq % TK == 0 and seq % GRID_W == 0
    xs = jnp.concatenate([ctx, x], axis=1)
    mod_rows = 16
    cc = jnp.zeros((mod_rows, d), F32).at[:b].set(c).at[b].set(c_ctx)
    mod_all = _mod_table(cc, mod_w, mod_b)
    rope_tab = _rope_tables(seq, ctx_len)
    for i in range(depth):
        need_ctx = i < depth - 1
        lat = mod_all[i, :b].reshape(b, 1, 6, d)
        cm = jnp.broadcast_to(mod_all[i, b].reshape(1, 1, 6, d), (b, 1, 6, d))
        modtab = jnp.concatenate([cm, lat], axis=1)
        p = _layer_params(w_in[i], m_conv[i], m_gate_b[i], m_norm[i], a_qnorm[i], a_knorm[i], l_cq_norm[i],
                          l_ckv_norm[i], l_wuq[i], l_wukv[i], l_qnorm[i], l_knorm[i], w_out[i])
        mqk, mo, misc, q_all, k_all, mv_t, v_all = _inproj(
            xs, modtab, norm_mix[i][None, :], p["w_in_p"], rope_tab, p["an"], p["ln"], p["cqn"], p["ckvn"],
            p["wuq_p"], p["wukv_p"], ctx_len)
        q_t, k_c = _conv(mqk, p["conv_w"], p["conv_scale"], ctx_len)
        hf, hb = _mlstm(q_t, k_c, mv_t, misc, p["gate_b"])
        gqa_kv = tuple(h // (A_HEADS // A_KV) for h in range(A_HEADS))
        o_gqa = _attention(q_all, k_all, v_all, ctx_len, need_ctx, 0, KV_GQA, A_KV, gqa_kv)
        o_mla = _attention(q_all, k_all, v_all, ctx_len, need_ctx, Q_MLA, 0, L_HEADS, tuple(range(L_HEADS)))
        xs = _mixout(xs, modtab, hf, hb, mo, o_gqa, o_mla, p["m_norm_p"], p["w_out_p"], ctx_len, not need_ctx)
        ffn_ctx = ctx_len if need_ctx else 0
        j = i // 2
        if i % 2 == 0:
            xs = _ffn(xs, modtab, norm_ffn[i][None, :], ffn_wg[j].astype(BF16), ffn_wu[j].astype(BF16),
                      ffn_wd[j].astype(BF16), ffn_ctx)
        else:
            router_p = jnp.pad(moe_router[j], ((0, 0), (0, LANE - N_EXPERTS)))
            xs = _moe(xs, modtab, norm_ffn[i][None, :], router_p, moe_wg[j].astype(BF16),
                      moe_wu[j].astype(BF16), moe_wd[j].astype(BF16), ffn_ctx)
    return xs
```

```python
import functools
import math

import numpy as np
import jax
import jax.numpy as jnp
from jax import lax
from jax.experimental import pallas as pl
from jax.experimental.pallas import tpu as pltpu

F32 = jnp.float32
BF16 = jnp.bfloat16

GRID_W = 64
EPS = 1e-6
ROPE_THETA = 10000.0
M_HEADS, M_DH = 4, 96
A_HEADS, A_KV, A_DH = 6, 2, 64
L_HEADS, L_NOPE, L_ROPE, L_DV = 4, 64, 32, 64
L_QK = L_NOPE + L_ROPE
L_QRANK, L_KVRANK = 256, 128
N_EXPERTS, TOP_K = 8, 2
IN_SIZES = (384, 384, 384, 384, 16, 384, 128, 128, 256, 128, 32)

LANE = 128
SLOT = LANE
N_KVHEADS = A_KV + L_HEADS
Q_MLA = 8
Q_SLOTS = Q_MLA + L_HEADS
KV_GQA = L_HEADS
VMEM_LIMIT = 56 * 1024 * 1024

O_MQK, O_MV, O_MO, O_MISC = 0, 1024, 1536, 2048
O_QA, O_KA, O_VA, O_CQ, O_CKV, IN_PAD = 2176, 2944, 3200, 3456, 3712, 3840
MISC_KR = 64
DEN_LANE = M_DH
SUM_LANE = A_DH
LOG2E = math.log2(math.e)
assert A_DH == L_DV and SUM_LANE < SLOT

TB = 256
TQ = 256
TK = 512
TM_FFN = 1088
TF = 512
ML = 256
RT = 144
GT = 2 * RT
ST = 3 * LANE
TF_MOE = 1792


def _cparams(sem):
    return pltpu.CompilerParams(dimension_semantics=sem, vmem_limit_bytes=VMEM_LIMIT)


def _sigmoid(x):
    return 1.0 / (1.0 + jnp.exp(-x))


def _silu(x):
    return x * _sigmoid(x)


def _lane_iota(shape):
    return lax.broadcasted_iota(jnp.int32, shape, len(shape) - 1)


def _row_iota(shape):
    return lax.broadcasted_iota(jnp.int32, shape, 0)


def _modulated(x, gain, mod_ref, t0, ctx_len, k_shift, k_scale):
    tm = x.shape[0]
    is_ctx = (t0 + _row_iota((tm, 1))) < ctx_len
    shift = jnp.where(is_ctx, mod_ref[0, k_shift:k_shift + 1, :], mod_ref[1, k_shift:k_shift + 1, :])
    scale = jnp.where(is_ctx, mod_ref[0, k_scale:k_scale + 1, :], mod_ref[1, k_scale:k_scale + 1, :])
    y = x * lax.rsqrt(jnp.mean(x * x, axis=-1, keepdims=True) + EPS) * gain
    return y * (1.0 + scale) + shift


def _gate_rows(mod_ref, t0, tm, ctx_len, k_gate):
    is_ctx = (t0 + _row_iota((tm, 1))) < ctx_len
    return jnp.where(is_ctx, mod_ref[0, k_gate:k_gate + 1, :], mod_ref[1, k_gate:k_gate + 1, :])


def _mod_kernel(c_ref, w_ref, b_ref, o_ref):
    s = _silu(c_ref[...]).astype(BF16)
    o_ref[...] = jnp.dot(s, w_ref[...].astype(BF16), preferred_element_type=F32) + b_ref[...]


def _mod_table(cc, mod_w, mod_b):
    depth, d, n = mod_w.shape
    rows = cc.shape[0]
    return pl.pallas_call(
        _mod_kernel,
        grid=(depth, n // d),
        in_specs=[pl.BlockSpec((rows, d), lambda l, j: (0, 0)),
                  pl.BlockSpec((None, d, d), lambda l, j: (l, 0, j)),
                  pl.BlockSpec((None, 1, d), lambda l, j: (l, 0, j))],
        out_specs=pl.BlockSpec((None, rows, d), lambda l, j: (l, 0, j)),
        out_shape=jax.ShapeDtypeStruct((depth, rows, n), F32),
        compiler_params=_cparams(("arbitrary", "arbitrary")),
        name="mod_table",
    )(cc, mod_w, mod_b.reshape(depth, 1, n))


def _slot_rms(x, lo, hi):
    lane = _lane_iota(x.shape)
    sq = jnp.where((lane >= lo) & (lane < hi), x * x, 0.0)
    return lax.rsqrt(jnp.sum(sq, axis=-1, keepdims=True) * (1.0 / (hi - lo)) + EPS)


def _rope(x, cos, sin_m, sin_p, quarter):
    return (x * cos + pltpu.roll(x, LANE - quarter, 1) * sin_m + pltpu.roll(x, quarter, 1) * sin_p)


def _inproj_kernel(ctx_len, x_ref, mod_ref, g_ref, w_ref, rope_ref, an_ref, ln_ref, cqn_ref, ckvn_ref,
                   wuq_ref, wukv_ref, mqk_ref, mo_ref, misc_ref, q_ref, k_ref, mvt_ref, vt_ref):
    tm = x_ref.shape[0]
    t0 = pl.program_id(1) * tm
    h = _modulated(x_ref[...], g_ref[...], mod_ref, t0, ctx_len, 0, 1).astype(BF16)

    def proj(a, b):
        return jnp.dot(h, w_ref[:, a:b], preferred_element_type=F32)

    misc = proj(O_MISC, O_QA)
    misc_ref[...] = misc

    cos_a, sinm_a, sinp_a = rope_ref[0], rope_ref[1], rope_ref[2]
    cos_l, sinm_l, sinp_l = rope_ref[3], rope_ref[4], rope_ref[5]
    a_scale = A_DH ** -0.5 * LOG2E
    l_scale = L_QK ** -0.5 * LOG2E

    def with_ones_t(v):
        return jnp.where(_lane_iota(v.shape) % SLOT == SUM_LANE, 1.0, v).T.astype(BF16)

    pc = proj(O_CQ, IN_PAD)
    cq = pc[:, 0:L_QRANK]
    cq = (cq * lax.rsqrt(jnp.mean(cq * cq, axis=-1, keepdims=True) + EPS) * cqn_ref[...]).astype(BF16)
    ckv = pc[:, L_QRANK:L_QRANK + L_KVRANK]
    ckv = (ckv * lax.rsqrt(jnp.mean(ckv * ckv, axis=-1, keepdims=True) + EPS) * ckvn_ref[...]).astype(BF16)
    ql = jnp.dot(cq, wuq_ref[...], preferred_element_type=F32)
    kvl = jnp.dot(ckv, wukv_ref[...], preferred_element_type=F32)
    gq_l, gk_l = ln_ref[0:1, :], ln_ref[1:2, :]
    lane = _lane_iota((tm, SLOT))
    kr = jnp.where((lane >= MISC_KR) & (lane < MISC_KR + L_ROPE), misc, 0.0)
    kr = kr * _slot_rms(kr, MISC_KR, MISC_KR + L_ROPE) * gk_l
    kr = _rope(kr, cos_l, sinm_l, sinp_l, L_ROPE // 4)
    for i in range(L_HEADS):
        x = ql[:, i * SLOT:(i + 1) * SLOT]
        inv = jnp.where(lane < L_NOPE, _slot_rms(x, 0, L_NOPE), _slot_rms(x, L_NOPE, L_QK))
        x = x * inv * gq_l
        q_ref[:, (Q_MLA + i) * SLOT:(Q_MLA + i + 1) * SLOT] = (
            _rope(x, cos_l, sinm_l, sinp_l, L_ROPE // 4) * l_scale).astype(BF16)
        kn = kvl[:, i * SLOT:(i + 1) * SLOT]
        kn = kn * _slot_rms(kn, 0, L_NOPE) * gk_l
        k_ref[:, i * SLOT:(i + 1) * SLOT] = (kn + kr).astype(BF16)
    vt_ref[0:KV_GQA * SLOT, :] = with_ones_t(kvl[:, L_HEADS * SLOT:])

    pa = proj(O_QA, O_CQ)
    gq, gk = an_ref[0:1, :], an_ref[1:2, :]
    for i in range(A_HEADS):
        x = pa[:, i * SLOT:(i + 1) * SLOT]
        x = x * _slot_rms(x, 0, A_DH) * gq
        q_ref[:, i * SLOT:(i + 1) * SLOT] = (_rope(x, cos_a, sinm_a, sinp_a, A_DH // 4) * a_scale).astype(BF16)
    for i in range(A_KV):
        x = pa[:, (A_HEADS + i) * SLOT:(A_HEADS + i + 1) * SLOT]
        x = x * _slot_rms(x, 0, A_DH) * gk
        k_ref[:, (KV_GQA + i) * SLOT:(KV_GQA + i + 1) * SLOT] = (
            _rope(x, cos_a, sinm_a, sinp_a, A_DH // 4).astype(BF16))
    vt_ref[KV_GQA * SLOT:, :] = with_ones_t(pa[:, (A_HEADS + A_KV) * SLOT:(A_HEADS + 2 * A_KV) * SLOT])
    q_ref[:, A_HEADS * SLOT:Q_MLA * SLOT] = jnp.zeros((tm, (Q_MLA - A_HEADS) * SLOT), BF16)

    mv = proj(O_MV, O_MO)
    mvt_ref[...] = jnp.where(_lane_iota(mv.shape) % SLOT == DEN_LANE, 1.0, mv).T.astype(BF16)
    mqk_ref[...] = proj(O_MQK, O_MV).astype(BF16)
    mo_ref[...] = proj(O_MO, O_MISC).astype(BF16)


def _inproj(xs, modtab, gain, w_in_p, rope_tab, an, ln, cqn, ckvn, wuq_p, wukv_p, ctx_len):
    b, s, d = xs.shape
    grid = (b, s // TB)
    tok = lambda n: pl.BlockSpec((None, TB, n), lambda i, j: (i, j, 0))
    full = lambda a: pl.BlockSpec(a.shape, lambda i, j: (0,) * a.ndim)
    out_widths = (1024, 512, SLOT, Q_SLOTS * SLOT, N_KVHEADS * SLOT)
    out_dtypes = (BF16, BF16, F32, BF16, BF16)
    out_specs = [tok(n) for n in out_widths]
    out_shape = [jax.ShapeDtypeStruct((b, s, n), dt) for n, dt in zip(out_widths, out_dtypes)]
    for n in (M_HEADS * SLOT, N_KVHEADS * SLOT):
        out_specs.append(pl.BlockSpec((None, n, TB), lambda i, j: (i, 0, j)))
        out_shape.append(jax.ShapeDtypeStruct((b, n, s), BF16))
    return pl.pallas_call(
        functools.partial(_inproj_kernel, ctx_len),
        grid=grid,
        in_specs=[tok(d),
                  pl.BlockSpec((None, 2, 6, d), lambda i, j: (i, 0, 0, 0)),
                  full(gain), full(w_in_p),
                  pl.BlockSpec((6, TB, SLOT), lambda i, j: (0, j, 0)),
                  full(an), full(ln), full(cqn), full(ckvn), full(wuq_p), full(wukv_p)],
        out_specs=out_specs,
        out_shape=out_shape,
        compiler_params=_cparams(("parallel", "parallel")),
        name="inproj",
    )(xs, modtab, gain, w_in_p, rope_tab, an, ln, cqn, ckvn, wuq_p, wukv_p)


def _conv_kernel(ctx_len, s_len, x_ref, prev_ref, next_ref, w_ref, sc_ref, qt_ref, k_ref):
    tm = x_ref.shape[0]
    t0 = pl.program_id(1) * tm
    x = x_ref[...].astype(F32)
    row = _row_iota((tm, 1))
    has_prev = jnp.logical_and(t0 != 0, t0 != ctx_len)
    has_next = jnp.logical_and(t0 + tm != ctx_len, t0 + tm != s_len)
    hp = jnp.where(has_prev, prev_ref[15:16, :].astype(F32), 0.0)
    hn = jnp.where(has_next, next_ref[0:1, :].astype(F32), 0.0)
    xp = jnp.where(row == 0, hp, pltpu.roll(x, 1, 0))
    xn = jnp.where(row == tm - 1, hn, pltpu.roll(x, tm - 1, 0))
    y = xp * w_ref[0:1, :] + x * w_ref[1:2, :] + xn * w_ref[2:3, :]
    y = _silu(y) * sc_ref[...]
    half = y.shape[1] // 2
    qt_ref[...] = y[:, :half].T.astype(qt_ref.dtype)
    k_ref[...] = y[:, half:].astype(k_ref.dtype)


def _conv(mqk, conv_w, conv_scale, ctx_len):
    b, s, n = mqk.shape
    hb = TB // 16
    last = s // 16 - 1
    return pl.pallas_call(
        functools.partial(_conv_kernel, ctx_len, s),
        grid=(b, s // TB),
        in_specs=[pl.BlockSpec((None, TB, n), lambda i, j: (i, j, 0)),
                  pl.BlockSpec((None, 16, n), lambda i, j: (i, jnp.maximum(j * hb - 1, 0), 0)),
                  pl.BlockSpec((None, 16, n), lambda i, j: (i, jnp.minimum((j + 1) * hb, last), 0)),
                  pl.BlockSpec((3, n), lambda i, j: (0, 0)),
                  pl.BlockSpec((1, n), lambda i, j: (0, 0))],
        out_specs=[pl.BlockSpec((None, n // 2, TB), lambda i, j: (i, 0, j)),
                   pl.BlockSpec((None, TB, n // 2), lambda i, j: (i, j, 0))],
        out_shape=[jax.ShapeDtypeStruct((b, n // 2, s), BF16), jax.ShapeDtypeStruct((b, s, n // 2), BF16)],
        compiler_params=_cparams(("parallel", "parallel")),
        name="mlstm_conv",
    )(mqk, mqk, mqk, conv_w, conv_scale)


def _log_sigmoid(x):
    return jnp.minimum(x, 0.0) - jnp.log(1.0 + jnp.exp(-jnp.abs(x)))


def _mlstm_kernel(n_chunks, qt_ref, k_ref, vt_ref, misc_ref, gb_ref, hf_ref, hb_ref, c_ref, m_ref, gt_sc, bt_sc):
    hp = pl.program_id(1)
    c_ref[...] = jnp.zeros_like(c_ref)
    m_ref[...] = jnp.zeros_like(m_ref)
    r = _row_iota((ML, ML))
    cidx = _lane_iota((ML, ML))
    tri_f32 = (r >= cidx).astype(F32)
    tri_b32 = (r <= cidx).astype(F32)
    lane = _lane_iota((ML, SLOT))
    lane_t = _lane_iota((1, ML))

    def products(t0, d, hh):
        sidx = d * 2 + hh
        ct_st = c_ref[sidx]
        k = k_ref[pl.ds(t0, ML), hh * SLOT:(hh + 1) * SLOT]
        qt = qt_ref[hh * SLOT:(hh + 1) * SLOT, pl.ds(t0, ML)]
        vt = vt_ref[hh * SLOT:(hh + 1) * SLOT, pl.ds(t0, ML)]
        kq = jnp.dot(k, qt, preferred_element_type=F32)
        cq = jnp.dot(ct_st.astype(BF16), qt, preferred_element_type=F32)
        return ct_st, k, vt, kq, cq

    def gates(t0, d, tri32):
        g = misc_ref[pl.ds(t0, ML), :] + gb_ref[...]
        logf = _log_sigmoid(g)
        bcum = jnp.dot(tri32, logf, preferred_element_type=F32, precision=lax.Precision.HIGHEST)
        gt_sc[d] = g.T
        bt_sc[d] = bcum.T
        return g - pltpu.roll(bcum, SLOT - M_HEADS, 1)

    def chain(t0, d, hh, ib, prod, out_ref):
        ct_st, k, vt, kq, cq = prod
        allowed = (r <= cidx) if d == 0 else (r >= cidx)
        li = d * 2 * M_HEADS + hp * 2 + hh
        ib_col = jnp.sum(jnp.where(lane == li, ib, 0.0), axis=-1, keepdims=True)
        i_row = gt_sc[d, pl.ds(li, 1), :]
        b_row = bt_sc[d, pl.ds(li + M_HEADS, 1), :]
        sidx = d * 2 + hh
        m_s = m_ref[sidx]
        dmat = jnp.where(allowed, b_row + ib_col, -jnp.inf)
        m_inter = b_row + m_s
        m_t = jnp.maximum(m_inter, jnp.max(dmat, axis=0, keepdims=True))
        w = (jnp.exp(dmat - m_t) * kq).astype(BF16)
        a_inter = jnp.exp(m_inter - m_t)
        num = a_inter * cq + jnp.dot(vt, w, preferred_element_type=F32)
        den = num[DEN_LANE:DEN_LANE + 1, :]
        h_out = num / jnp.maximum(jnp.abs(den), jnp.exp(-m_t))
        out_ref[pl.ds(t0, ML), hh * SLOT:(hh + 1) * SLOT] = h_out.T.astype(out_ref.dtype)
        last = ML - 1 if d == 0 else 0
        total = jnp.sum(jnp.where(lane_t == last, b_row, 0.0), axis=-1, keepdims=True)
        gg = total - b_row + i_row
        m_new = jnp.maximum(total + m_s, jnp.max(gg, axis=-1, keepdims=True))
        decay = jnp.exp(total + m_s - m_new)[:, 0:1]
        wk = jnp.exp(gg - m_new)
        vw = (vt.astype(F32) * wk).astype(BF16)
        c_ref[sidx] = decay * ct_st + jnp.dot(vw, k, preferred_element_type=F32)
        m_ref[sidx] = m_new

    def step(n, carry):
        tf0 = pl.multiple_of(n * ML, ML)
        tb0 = pl.multiple_of(jnp.where(n == 0, 0, n_chunks - n) * ML, ML)
        chains = [(tf0, 0, 0, hf_ref), (tf0, 0, 1, hf_ref), (tb0, 1, 0, hb_ref), (tb0, 1, 1, hb_ref)]
        prods = [products(t0, d, hh) for t0, d, hh, _ in chains]
        ibs = (gates(tf0, 0, tri_f32), gates(tb0, 1, tri_b32))
        for (t0, d, hh, out_ref), prod in zip(chains, prods):
            chain(t0, d, hh, ibs[d], prod, out_ref)
        return carry

    lax.fori_loop(0, n_chunks, step, 0)


def _mlstm(q_t, k_c, mv_t, misc, gate_b):
    b, s, _ = k_c.shape
    n_chunks = s // ML
    pair = 2 * SLOT
    spec = pl.BlockSpec((None, s, pair), lambda i, j: (i, 0, j))
    spec_t = pl.BlockSpec((None, pair, s), lambda i, j: (i, j, 0))
    return pl.pallas_call(
        functools.partial(_mlstm_kernel, n_chunks),
        grid=(b, M_HEADS // 2),
        in_specs=[spec_t, spec, spec_t,
                  pl.BlockSpec((None, s, SLOT), lambda i, j: (i, 0, 0)),
                  pl.BlockSpec((1, SLOT), lambda i, j: (0, 0))],
        out_specs=[spec, spec],
        out_shape=[jax.ShapeDtypeStruct((b, s, M_HEADS * SLOT), BF16)] * 2,
        scratch_shapes=[pltpu.VMEM((4, SLOT, SLOT), F32), pltpu.VMEM((4, 1, ML), F32),
                        pltpu.VMEM((2, SLOT, ML), F32), pltpu.VMEM((2, SLOT, ML), F32)],
        compiler_params=_cparams(("parallel", "parallel")),
        name="mlstm_scan",
    )(q_t, k_c, mv_t, misc, gate_b)


def _attn_kernel(ctx_len, q_off, kv_of, q_ref, k_ref, vt_ref, o_ref, st_sc, pt_sc, acc_sc):
    s_len = k_ref.shape[0]
    n_heads = len(kv_of)
    qs = [q_ref[:, g * SLOT:(g + 1) * SLOT] for g in range(n_heads)]

    def kv_cols(g):
        return slice(kv_of[g] * SLOT, (kv_of[g] + 1) * SLOT)

    def chunk(carry, rows):
        out = []
        for g in range(n_heads):
            m, acc = carry[g]
            k = k_ref[rows, kv_cols(g)]
            vt = vt_ref[kv_cols(g), rows]
            st = lax.dot_general(k, qs[g], (((1,), (1,)), ((), ())), preferred_element_type=F32)
            m_new = jnp.maximum(m, jnp.max(st, axis=0, keepdims=True))
            alpha = jnp.exp2(m - m_new)
            pt = jnp.exp2(st - m_new).astype(BF16)
            out.append((m_new, alpha * acc + jnp.dot(vt, pt, preferred_element_type=F32)))
        return tuple(out)

    def finish(carry):
        for g in range(n_heads):
            acc = carry[g][1]
            o = acc / acc[SUM_LANE:SUM_LANE + 1, :]
            o_ref[:, g * SLOT:(g + 1) * SLOT] = o.T.astype(o_ref.dtype)

    init = tuple((jnp.full((1, TQ), -jnp.inf, F32), jnp.zeros((SLOT, TQ), F32)) for _ in range(n_heads))
    is_ctx = (pl.program_id(2) + q_off) * TQ < ctx_len

    @pl.when(is_ctx)
    def _():
        finish(chunk(init, pl.ds(0, ctx_len)))

    n_lat = (s_len - ctx_len) // TK
    n0 = ctx_len + TK
    assert n_lat >= 4 and n_lat % 2 == 0 and st_sc.shape[1] == n0

    def rows_of(c):
        if isinstance(c, int):
            return (pl.ds(0, n0), n0) if c == 0 else (pl.ds(ctx_len + c * TK, TK), TK)
        return pl.ds(pl.multiple_of(ctx_len + c * TK, math.gcd(ctx_len, TK)), TK), TK

    def scores(c, par):
        rows, n = rows_of(c)
        for g in range(n_heads):
            st_sc[par * n_heads + g, 0:n, :] = lax.dot_general(
                k_ref[rows, kv_cols(g)], qs[g], (((1,), (1,)), ((), ())), preferred_element_type=F32)

    def softmax(par, m, n=TK):
        ms, alphas = [], []
        for g in range(n_heads):
            st = st_sc[par * n_heads + g, 0:n, :]
            m_new = jnp.maximum(m[g], jnp.max(st, axis=0, keepdims=True))
            pt_sc[par * n_heads + g, 0:n, :] = jnp.exp2(st - m_new).astype(BF16)
            ms.append(m_new)
            alphas.append(jnp.exp2(m[g] - m_new))
        return tuple(ms), tuple(alphas)

    def values(c, par, alpha):
        rows, n = rows_of(c)
        for g in range(n_heads):
            acc_sc[g] = alpha[g] * acc_sc[g] + jnp.dot(vt_ref[kv_cols(g), rows], pt_sc[par * n_heads + g, 0:n, :],
                                                       preferred_element_type=F32)

    def stage(c, par, m, alpha):
        scores(c + 1, 1 - par)
        m_new, alpha_new = softmax(par, m)
        values(c - 1, 1 - par, alpha)
        return m_new, alpha_new

    @pl.when(jnp.logical_not(is_ctx))
    def _():
        for g in range(n_heads):
            acc_sc[g] = init[g][1]
        scores(0, 0)
        m, alpha = softmax(0, tuple(c[0] for c in init), n0)
        scores(1, 1)
        m, alpha = stage(1, 1, m, alpha)

        def body(t, carry):
            c = 2 * t + 2
            return stage(c + 1, 1, *stage(c, 0, *carry))

        m, alpha = lax.fori_loop(0, (n_lat - 4) // 2, body, (m, alpha))
        m, alpha = stage(n_lat - 2, 0, m, alpha)
        values(n_lat - 2, 0, alpha)
        m, alpha = softmax(1, m)
        values(n_lat - 1, 1, alpha)
        finish(tuple((m[g], acc_sc[g]) for g in range(n_heads)))


def _attention(q_all, k_all, vt_all, ctx_len, need_ctx, q_slot0, kv_slot0, n_kv, kv_of):
    b, s, _ = q_all.shape
    q_off = 0 if need_ctx else ctx_len // TQ
    nq = s // TQ - q_off
    n_heads = len(kv_of)
    qw = n_heads * SLOT
    kw = n_kv * SLOT
    assert (q_slot0 * SLOT) % qw == 0 and (kv_slot0 * SLOT) % kw == 0
    qb0, kb0 = q_slot0 * SLOT // qw, kv_slot0 * SLOT // kw
    return pl.pallas_call(
        functools.partial(_attn_kernel, ctx_len, q_off, kv_of),
        grid=(b, 1, nq),
        in_specs=[pl.BlockSpec((None, TQ, qw), lambda i, g, j: (i, j + q_off, qb0)),
                  pl.BlockSpec((None, s, kw), lambda i, g, j: (i, 0, kb0)),
                  pl.BlockSpec((None, kw, s), lambda i, g, j: (i, kb0, 0))],
        out_specs=pl.BlockSpec((None, TQ, qw), lambda i, g, j: (i, j + q_off, 0)),
        out_shape=jax.ShapeDtypeStruct((b, s, qw), BF16),
        scratch_shapes=[pltpu.VMEM((2 * n_heads, ctx_len + TK, TQ), F32),
                        pltpu.VMEM((2 * n_heads, ctx_len + TK, TQ), BF16),
                        pltpu.VMEM((n_heads, SLOT, TQ), F32)],
        compiler_params=_cparams(("parallel", "parallel", "parallel")),
        name="attention",
    )(q_all, k_all, vt_all)


def _mixout_kernel(ctx_len, blk_off, x_ref, mod_ref, hf_ref, hb_ref, mo_ref, oa_ref, ol_ref, mn_ref, w_ref, o_ref):
    tm = x_ref.shape[0]
    t0 = (pl.program_id(1) + blk_off) * tm
    hm = hf_ref[...].astype(F32) + hb_ref[...].astype(F32)
    gate = _sigmoid(mo_ref[...].astype(F32))
    parts = []
    for i in range(M_HEADS):
        x = hm[:, i * SLOT:(i + 1) * SLOT]
        inv = lax.rsqrt(jnp.sum(x * x, axis=-1, keepdims=True) * (1.0 / M_DH) + EPS)
        parts.append(x * inv)
    hn = jnp.concatenate(parts, axis=-1) * mn_ref[...] * gate
    nm = M_HEADS * SLOT
    o = jnp.dot(hn.astype(BF16), w_ref[0:nm, :], preferred_element_type=F32)
    na = nm + A_HEADS * SLOT
    o = o + jnp.dot(oa_ref[...], w_ref[nm:na, :], preferred_element_type=F32)
    o = o + jnp.dot(ol_ref[...], w_ref[na:, :], preferred_element_type=F32)
    o_ref[...] = x_ref[...] + _gate_rows(mod_ref, t0, tm, ctx_len, 2) * o


def _mixout(xs, modtab, hf, hb, mo, o_gqa, o_mla, m_norm_p, w_out_p, ctx_len, lat_only):
    b, s, d = xs.shape
    blk_off = ctx_len // TB if lat_only else 0
    tok = lambda n: pl.BlockSpec((None, TB, n), lambda i, j: (i, j + blk_off, 0))
    full = lambda a: pl.BlockSpec(a.shape, lambda i, j: (0,) * a.ndim)
    return pl.pallas_call(
        functools.partial(_mixout_kernel, ctx_len, blk_off),
        grid=(b, s // TB - blk_off),
        in_specs=[tok(d),
                  pl.BlockSpec((None, 2, 6, d), lambda i, j: (i, 0, 0, 0)),
                  tok(M_HEADS * SLOT), tok(M_HEADS * SLOT), tok(M_HEADS * SLOT), tok(A_HEADS * SLOT),
                  tok(L_HEADS * SLOT), full(m_norm_p), full(w_out_p)],
        out_specs=pl.BlockSpec((None, TB, d), lambda i, j: (i, j, 0)),
        out_shape=jax.ShapeDtypeStruct((b, s - blk_off * TB, d), F32),
        input_output_aliases={} if lat_only else {0: 0},
        compiler_params=_cparams(("parallel", "parallel")),
        name="mix_out",
    )(xs, modtab, hf, hb, mo, o_gqa, o_mla, m_norm_p, w_out_p)


def _ffn_kernel(ctx_len, x_ref, mod_ref, g_ref, wg_ref, wu_ref, wd_ref, o_ref, h_sc, acc_sc):
    tm = x_ref.shape[0]
    t0 = pl.program_id(1) * tm
    f = pl.program_id(2)

    @pl.when(f == 0)
    def _():
        h_sc[...] = _modulated(x_ref[...], g_ref[...], mod_ref, t0, ctx_len, 3, 4).astype(BF16)
        acc_sc[...] = jnp.zeros_like(acc_sc)

    h = h_sc[...]
    a = jnp.dot(h, wg_ref[...], preferred_element_type=F32)
    u = jnp.dot(h, wu_ref[...], preferred_element_type=F32)
    acc_sc[...] += jnp.dot((_silu(a) * u).astype(BF16), wd_ref[...], preferred_element_type=F32)

    @pl.when(f == pl.num_programs(2) - 1)
    def _():
        o_ref[...] = x_ref[...] + _gate_rows(mod_ref, t0, tm, ctx_len, 5) * acc_sc[...]


def _ffn(xs, modtab, gain, wg, wu, wd, ctx_len):
    b, s, d = xs.shape
    tm = s // 4
    nf = wg.shape[1] // TF
    return pl.pallas_call(
        functools.partial(_ffn_kernel, ctx_len),
        grid=(b, s // tm, nf),
        in_specs=[pl.BlockSpec((None, tm, d), lambda i, j, f: (i, j, 0)),
                  pl.BlockSpec((None, 2, 6, d), lambda i, j, f: (i, 0, 0, 0)),
                  pl.BlockSpec((1, d), lambda i, j, f: (0, 0)),
                  pl.BlockSpec((d, TF), lambda i, j, f: (0, f)),
                  pl.BlockSpec((d, TF), lambda i, j, f: (0, f)),
                  pl.BlockSpec((TF, d), lambda i, j, f: (f, 0))],
        out_specs=pl.BlockSpec((None, tm, d), lambda i, j, f: (i, j, 0)),
        out_shape=jax.ShapeDtypeStruct((b, s, d), F32),
        scratch_shapes=[pltpu.VMEM((tm, d), BF16), pltpu.VMEM((tm, d), F32)],
        input_output_aliases={0: 0},
        compiler_params=_cparams(("parallel", "parallel", "arbitrary")),
        name="ffn_dense",
    )(xs, modtab, gain, wg, wu, wd)


def _top2_combine(logits):
    lane = _lane_iota(logits.shape)
    lane_f = lane.astype(F32)
    lg = jnp.where(lane < N_EXPERTS, logits, -jnp.inf)
    v1 = jnp.max(lg, axis=-1, keepdims=True)
    i1 = jnp.min(jnp.where(lg == v1, lane_f, float(LANE)), axis=-1, keepdims=True)
    rest = jnp.where(lane_f == i1, -jnp.inf, lg)
    v2 = jnp.max(rest, axis=-1, keepdims=True)
    i2 = jnp.min(jnp.where(rest == v2, lane_f, float(LANE)), axis=-1, keepdims=True)
    e2 = jnp.exp(v2 - v1)
    w1 = 1.0 / (1.0 + e2)
    w2 = e2 / (1.0 + e2)
    sel = jnp.where((lane_f == i1) | (lane_f == i2), 1.0, 0.0)
    return jnp.where(lane_f == i1, w1, 0.0) + jnp.where(lane_f == i2, w2, 0.0), sel


def _moe_kernel(ctx_len, x_ref, mod_ref, g_ref, r_ref, tri_ref, wg_ref, wu_ref, wd_ref, o_ref,
                h_sc, comb_sc, rank_sc, rankt_sc, rankc_sc, wc_sc, xg_sc, acc_sc, nt_sc):
    tm = x_ref.shape[0]
    t0 = pl.program_id(1) * tm
    e = pl.program_id(2)
    f = pl.program_id(3)

    @pl.when(jnp.logical_and(e == 0, f == 0))
    def _():
        h = _modulated(x_ref[...], g_ref[...], mod_ref, t0, ctx_len, 3, 4)
        logits = jnp.dot(h, r_ref[...], preferred_element_type=F32, precision=lax.Precision.HIGHEST)
        comb, sel = _top2_combine(logits)
        rank =jnp.dot(tri_ref[...], sel.astype(BF16), preferred_element_type=F32)
        rank = jnp.where(sel > 0.0, rank, -1.0)
        comb_sc[...] = comb
        rank_sc[...] = rank
        pad = rankt_sc.shape[1] - tm
        h_sc[0:tm, :] = h.astype(BF16)
        if pad:
            rankt_sc[...] = jnp.concatenate([rank, jnp.full((pad, LANE), -1.0, F32)], axis=0).T
            h_sc[tm:, :] = jnp.zeros((pad, h_sc.shape[1]), BF16)
        else:
            rankt_sc[...] = rank.T
        o_ref[...] = jnp.zeros_like(o_ref)

    @pl.when(f == 0)
    def _():
        lane = _lane_iota((tm, LANE))
        rank_c = jnp.sum(jnp.where(lane == e, rank_sc[...], 0.0), axis=-1, keepdims=True)
        rankc_sc[...] = rank_c
        wc_sc[...] = jnp.sum(jnp.where(lane == e, comb_sc[...], 0.0), axis=-1, keepdims=True)
        n_rows = jnp.sum(jnp.where(rank_c >= 0.0, 1.0, 0.0)).astype(jnp.int32)
        n_rt = lax.div(n_rows + (RT - 1), RT)
        n_gt = lax.div(n_rt * RT + (GT - 1), GT)
        n_st = lax.div(n_rt * RT + (ST - 1), ST)
        nt_sc[0] = n_rt
        nt_sc[1] = n_st
        rank_r = rankt_sc[pl.ds(e, 1), :]

        def gather(i, c):
            r0 = pl.multiple_of(i * GT, 16)
            tgt = (r0 + _row_iota((GT, 1))).astype(F32)
            sel_t = jnp.where(rank_r == tgt, 1.0, 0.0).astype(BF16)
            xg_sc[pl.ds(r0, GT), :] = jnp.dot(sel_t, h_sc[...], preferred_element_type=F32).astype(BF16)
            return c

        def clear(i, c):
            acc_sc[pl.ds(pl.multiple_of(i * ST, ST), ST), :] = jnp.zeros((ST, acc_sc.shape[1]), F32)
            return c

        lax.fori_loop(0, n_gt, gather, 0)
        lax.fori_loop(0, n_st, clear, 0)

    def expert(i, c):
        r0 = pl.multiple_of(i * RT, 16)
        rows = xg_sc[pl.ds(r0, RT), :]
        a = jnp.dot(rows, wg_ref[...], preferred_element_type=F32)
        u = jnp.dot(rows, wu_ref[...], preferred_element_type=F32)
        acc_sc[pl.ds(r0, RT), :] += jnp.dot((_silu(a) * u).astype(BF16), wd_ref[...],
                                            preferred_element_type=F32)
        return c

    lax.fori_loop(0, nt_sc[0], expert, 0)

    @pl.when(f == pl.num_programs(3) - 1)
    def _():
        def scatter(i, c):
            r0 = pl.multiple_of(i * ST, ST)
            tgt = (r0 + _lane_iota((1, ST))).astype(F32)
            w_t = jnp.where(rankc_sc[...] == tgt, wc_sc[...], 0.0).astype(BF16)
            o_ref[...] += jnp.dot(w_t, acc_sc[pl.ds(r0, ST), :].astype(BF16), preferred_element_type=F32)
            return c

        lax.fori_loop(0, nt_sc[1], scatter, 0)

    @pl.when(jnp.logical_and(e == pl.num_programs(2) - 1, f == pl.num_programs(3) - 1))
    def _():
        o_ref[...] = x_ref[...] + _gate_rows(mod_ref, t0, tm, ctx_len, 5) * o_ref[...]


def _moe(xs, modtab, gain, router_p, wg, wu, wd, ctx_len):
    b, s, d = xs.shape
    tm = s // 4
    tp = -(-tm // LANE) * LANE
    r_max = -(-tm // RT) * RT
    tr = max(-(-r_max // GT) * GT, -(-r_max // ST) * ST)
    ne, _, dff = wg.shape
    tri = jnp.tril(jnp.ones((tm, tm), BF16), -1)
    TF = TF_MOE
    assert dff % TF == 0
    return pl.pallas_call(
        functools.partial(_moe_kernel, ctx_len),
        grid=(b, s // tm, ne, dff // TF),
        in_specs=[pl.BlockSpec((None, tm, d), lambda i, j, e, f: (i, j, 0), pipeline_mode=pl.Buffered(1)),
                  pl.BlockSpec((None, 2, 6, d), lambda i, j, e, f: (i, 0, 0, 0)),
                  pl.BlockSpec((1, d), lambda i, j, e, f: (0, 0)),
                  pl.BlockSpec((d, LANE), lambda i, j, e, f: (0, 0)),
                  pl.BlockSpec((tm, tm), lambda i, j, e, f: (0, 0), pipeline_mode=pl.Buffered(1)),
                  pl.BlockSpec((None, d, TF), lambda i, j, e, f: (e, 0, f)),
                  pl.BlockSpec((None, d, TF), lambda i, j, e, f: (e, 0, f)),
                  pl.BlockSpec((None, TF, d), lambda i, j, e, f: (e, f, 0))],
        out_specs=pl.BlockSpec((None, tm, d), lambda i, j, e, f: (i, j, 0)),
        out_shape=jax.ShapeDtypeStruct((b, s, d), F32),
        scratch_shapes=[pltpu.VMEM((tp, d), BF16),
                        pltpu.VMEM((tm, LANE), F32),
                        pltpu.VMEM((tm, LANE), F32),
                        pltpu.VMEM((LANE, tp), F32),
                        pltpu.VMEM((tm, 1), F32),
                        pltpu.VMEM((tm, 1), F32),
                        pltpu.VMEM((tr, d), BF16),
                        pltpu.VMEM((tr, d), F32),
                        pltpu.SMEM((2,), jnp.int32)],
        input_output_aliases={0: 0},
        compiler_params=_cparams(("parallel", "parallel", "arbitrary", "arbitrary")),
        name="moe_top2",
    )(xs, modtab, gain, router_p, tri, wg, wu, wd)


def _pad_heads(w, n_heads, dh, axis=-1):
    axis = axis % w.ndim
    shp = w.shape[:axis] + (n_heads, dh) + w.shape[axis + 1:]
    pad = [(0, 0)] * (w.ndim + 1)
    pad[axis + 1] = (0, SLOT - dh)
    out = jnp.pad(w.reshape(shp), pad)
    return out.reshape(w.shape[:axis] + (n_heads * SLOT,) + w.shape[axis + 1:])


def _rope_tables(seq, ctx_len):
    t = jnp.arange(seq)
    rows = (t // GRID_W).astype(F32)
    cols = (t % GRID_W).astype(F32)

    def angles(rot_dim):
        nf = rot_dim // 4
        inv = ROPE_THETA ** (-jnp.arange(nf, dtype=F32) / nf)
        ar = rows[:, None] * inv
        ac = cols[:, None] * inv
        return jnp.concatenate([ar, ar, ac, ac], axis=-1)

    def slot_tables(rot_dim, lane0):
        ang = angles(rot_dim)
        quarter = rot_dim // 4
        first = (jnp.arange(rot_dim) % (2 * quarter)) < quarter
        cos = jnp.ones((seq, SLOT), F32).at[:, lane0:lane0 + rot_dim].set(jnp.cos(ang))
        sin = jnp.sin(ang)
        sin_m = jnp.zeros((seq, SLOT), F32).at[:, lane0:lane0 + rot_dim].set(jnp.where(first, -sin, 0.0))
        sin_p = jnp.zeros((seq, SLOT), F32).at[:, lane0:lane0 + rot_dim].set(jnp.where(first, 0.0, sin))
        ident = jnp.stack([jnp.ones((ctx_len, SLOT), F32), jnp.zeros((ctx_len, SLOT), F32),
                           jnp.zeros((ctx_len, SLOT), F32)])
        return jnp.concatenate([ident, jnp.stack([cos, sin_m, sin_p])], axis=1)

    return jnp.concatenate([slot_tables(A_DH, 0), slot_tables(L_ROPE, L_NOPE)], axis=0)


def _layer_params(w_in, m_conv, m_gate_b, m_norm, a_qnorm, a_knorm, l_cq_norm, l_ckv_norm, l_wuq, l_wukv,
                  l_qnorm, l_knorm, w_out):
    d = w_in.shape[0]
    offs = np.cumsum((0,) + IN_SIZES)
    seg = [w_in[:, offs[i]:offs[i + 1]] for i in range(len(IN_SIZES))]
    misc = jnp.zeros((d, SLOT), F32).at[:, 0:4 * M_HEADS].set(seg[4]).at[:, MISC_KR:MISC_KR + L_ROPE].set(seg[10])
    w_in_p = jnp.concatenate(
        [_pad_heads(seg[0], M_HEADS, M_DH), _pad_heads(seg[1], M_HEADS, M_DH),
         _pad_heads(seg[2], M_HEADS, M_DH), _pad_heads(seg[3], M_HEADS, M_DH), misc,
         _pad_heads(seg[5], A_HEADS, A_DH), _pad_heads(seg[6], A_KV, A_DH), _pad_heads(seg[7], A_KV, A_DH),
         seg[8], seg[9]], axis=1).astype(BF16)
    conv_w = jnp.concatenate([_pad_heads(m_conv[:, :M_HEADS * M_DH], M_HEADS, M_DH),
                              _pad_heads(m_conv[:, M_HEADS * M_DH:], M_HEADS, M_DH)], axis=1)
    conv_scale = jnp.concatenate([jnp.ones((1, M_HEADS * SLOT), F32),
                                  jnp.full((1, M_HEADS * SLOT), M_DH ** -0.5, F32)], axis=1)
    gate_b = jnp.zeros((1, SLOT), F32).at[0, 0:4 * M_HEADS].set(m_gate_b)
    pad1 = lambda g: jnp.pad(g, (0, SLOT - g.shape[0]))
    an = jnp.stack([pad1(a_qnorm), pad1(a_knorm)])
    ln = jnp.stack([pad1(l_qnorm), pad1(l_knorm)])
    wuq_p = _pad_heads(l_wuq, L_HEADS, L_QK).astype(BF16)
    kv = l_wukv.reshape(L_KVRANK, L_HEADS, L_NOPE + L_DV)
    wukv_p = jnp.concatenate(
        [_pad_heads(kv[:, :, :L_NOPE].reshape(L_KVRANK, -1), L_HEADS, L_NOPE),
         _pad_heads(kv[:, :, L_NOPE:].reshape(L_KVRANK, -1), L_HEADS, L_DV)], axis=1).astype(BF16)
    nm, na = M_HEADS * M_DH, A_HEADS * A_DH
    w_out_p = jnp.concatenate(
        [_pad_heads(w_out[:nm], M_HEADS, M_DH, axis=0), _pad_heads(w_out[nm:nm + na], A_HEADS, A_DH, axis=0),
         _pad_heads(w_out[nm + na:], L_HEADS, L_DV, axis=0)], axis=0).astype(BF16)
    m_norm_p = _pad_heads(m_norm[None, :], M_HEADS, M_DH)
    return dict(w_in_p=w_in_p, conv_w=conv_w, conv_scale=conv_scale, gate_b=gate_b, an=an, ln=ln,
                cqn=l_cq_norm[None, :], ckvn=l_ckv_norm[None, :], wuq_p=wuq_p, wukv_p=wukv_p,
                w_out_p=w_out_p, m_norm_p=m_norm_p)


def kernel(x, c, ctx, c_ctx, mod_w, mod_b, norm_mix, norm_ffn, w_in, m_conv, m_gate_b, m_norm, a_qnorm, a_knorm,
           l_cq_norm, l_ckv_norm, l_wuq, l_wukv, l_qnorm, l_knorm, w_out, ffn_wg, ffn_wu, ffn_wd, moe_router,
           moe_wg, moe_wu, moe_wd):
    b, seq, d = x.shape
    ctx_len = ctx.shape[1]
    depth = mod_w.shape[0]
    assert ctx_len % TB == 0 and ctx_len % TQ == 0 and ctx_len == ML and seq % TK == 0 and seq % GRID_W == 0
    xs = jnp.concatenate([ctx, x], axis=1)
    mod_rows = 16
    cc = jnp.zeros((mod_rows, d), F32).at[:b].set(c).at[b].set(c_ctx)
    mod_all = _mod_table(cc, mod_w, mod_b)
    rope_tab = _rope_tables(seq, ctx_len)
    for i in range(depth):
        need_ctx = i < depth - 1
        lat = mod_all[i, :b].reshape(b, 1, 6, d)
        cm = jnp.broadcast_to(mod_all[i, b].reshape(1, 1, 6, d), (b, 1, 6, d))
        modtab = jnp.concatenate([cm, lat], axis=1)
        p = _layer_params(w_in[i], m_conv[i], m_gate_b[i], m_norm[i], a_qnorm[i], a_knorm[i], l_cq_norm[i],
                          l_ckv_norm[i], l_wuq[i], l_wukv[i], l_qnorm[i], l_knorm[i], w_out[i])
        mqk, mo, misc, q_all, k_all, mv_t, v_all = _inproj(
            xs, modtab, norm_mix[i][None, :], p["w_in_p"], rope_tab, p["an"], p["ln"], p["cqn"], p["ckvn"],
            p["wuq_p"], p["wukv_p"], ctx_len)
        q_t, k_c = _conv(mqk, p["conv_w"], p["conv_scale"], ctx_len)
        hf, hb = _mlstm(q_t, k_c, mv_t, misc, p["gate_b"])
        gqa_kv = tuple(h // (A_HEADS // A_KV) for h in range(A_HEADS))
        o_gqa = _attention(q_all, k_all, v_all, ctx_len, need_ctx, 0, KV_GQA, A_KV, gqa_kv)
        o_mla = _attention(q_all, k_all, v_all, ctx_len, need_ctx, Q_MLA, 0, L_HEADS, tuple(range(L_HEADS)))
        xs = _mixout(xs, modtab, hf, hb, mo, o_gqa, o_mla, p["m_norm_p"], p["w_out_p"], ctx_len, not need_ctx)
        ffn_ctx = ctx_len if need_ctx else 0
        j = i // 2
        if i % 2 == 0:
            xs = _ffn(xs, modtab, norm_ffn[i][None, :], ffn_wg[j].astype(BF16), ffn_wu[j].astype(BF16),
                      ffn_wd[j].astype(BF16), ffn_ctx)
        else:
            router_p = jnp.pad(moe_router[j], ((0, 0), (0, LANE - N_EXPERTS)))
            xs = _moe(xs, modtab, norm_ffn[i][None, :], router_p, moe_wg[j].astype(BF16),
                      moe_wu[j].astype(BF16), moe_wd[j].astype(BF16), ffn_ctx)
    return xs
```

```python
import functools
import math

import numpy as np
import jax
import jax.numpy as jnp
from jax import lax
from jax.experimental import pallas as pl
from jax.experimental.pallas import tpu as pltpu

F32 = jnp.float32
BF16 = jnp.bfloat16

GRID_W = 64
EPS = 1e-6
ROPE_THETA = 10000.0
M_HEADS, M_DH = 4, 96
A_HEADS, A_KV, A_DH = 6, 2, 64
L_HEADS, L_NOPE, L_ROPE, L_DV = 4, 64, 32, 64
L_QK = L_NOPE + L_ROPE
L_QRANK, L_KVRANK = 256, 128
N_EXPERTS, TOP_K = 8, 2
IN_SIZES = (384, 384, 384, 384, 16, 384, 128, 128, 256, 128, 32)

LANE = 128
SLOT = LANE
N_KVHEADS = A_KV + L_HEADS
Q_MLA = 8
Q_SLOTS = Q_MLA + L_HEADS
KV_GQA = L_HEADS
VMEM_LIMIT = 56 * 1024 * 1024

O_MQK, O_MV, O_MO, O_MISC = 0, 1024, 1536, 2048
O_QA, O_KA, O_VA, O_CQ, O_CKV, IN_PAD = 2176, 2944, 3200, 3456, 3712, 3840
MISC_KR = 64
DEN_LANE = M_DH
SUM_LANE = A_DH
LOG2E = math.log2(math.e)
assert A_DH == L_DV and SUM_LANE < SLOT

TB = 256
TQ = 256
TK = 512
TM_FFN = 1088
TF = 512
ML = 256
RT = 144
GT = 2 * RT
ST = 3 * LANE
TF_MOE = 1792


def _cparams(sem):
    return pltpu.CompilerParams(dimension_semantics=sem, vmem_limit_bytes=VMEM_LIMIT)


def _sigmoid(x):
    return 1.0 / (1.0 + jnp.exp(-x))


def _silu(x):
    return x * _sigmoid(x)


def _lane_iota(shape):
    return lax.broadcasted_iota(jnp.int32, shape, len(shape) - 1)


def _row_iota(shape):
    return lax.broadcasted_iota(jnp.int32, shape, 0)


def _modulated(x, gain, mod_ref, t0, ctx_len, k_shift, k_scale):
    tm = x.shape[0]
    is_ctx = (t0 + _row_iota((tm, 1))) < ctx_len
    shift = jnp.where(is_ctx, mod_ref[0, k_shift:k_shift + 1, :], mod_ref[1, k_shift:k_shift + 1, :])
    scale = jnp.where(is_ctx, mod_ref[0, k_scale:k_scale + 1, :], mod_ref[1, k_scale:k_scale + 1, :])
    y = x * lax.rsqrt(jnp.mean(x * x, axis=-1, keepdims=True) + EPS) * gain
    return y * (1.0 + scale) + shift


def _gate_rows(mod_ref, t0, tm, ctx_len, k_gate):
    is_ctx = (t0 + _row_iota((tm, 1))) < ctx_len
    return jnp.where(is_ctx, mod_ref[0, k_gate:k_gate + 1, :], mod_ref[1, k_gate:k_gate + 1, :])


def _mod_kernel(c_ref, w_ref, b_ref, o_ref):
    s = _silu(c_ref[...]).astype(BF16)
    o_ref[...] = jnp.dot(s, w_ref[...].astype(BF16), preferred_element_type=F32) + b_ref[...]


def _mod_table(cc, mod_w, mod_b):
    depth, d, n = mod_w.shape
    rows = cc.shape[0]
    return pl.pallas_call(
        _mod_kernel,
        grid=(depth, n // d),
        in_specs=[pl.BlockSpec((rows, d), lambda l, j: (0, 0)),
                  pl.BlockSpec((None, d, d), lambda l, j: (l, 0, j)),
                  pl.BlockSpec((None, 1, d), lambda l, j: (l, 0, j))],
        out_specs=pl.BlockSpec((None, rows, d), lambda l, j: (l, 0, j)),
        out_shape=jax.ShapeDtypeStruct((depth, rows, n), F32),
        compiler_params=_cparams(("arbitrary", "arbitrary")),
        name="mod_table",
    )(cc, mod_w, mod_b.reshape(depth, 1, n))


def _slot_rms(x, lo, hi):
    lane = _lane_iota(x.shape)
    sq = jnp.where((lane >= lo) & (lane < hi), x * x, 0.0)
    return lax.rsqrt(jnp.sum(sq, axis=-1, keepdims=True) * (1.0 / (hi - lo)) + EPS)


def _rope(x, cos, sin_m, sin_p, quarter):
    return (x * cos + pltpu.roll(x, LANE - quarter, 1) * sin_m + pltpu.roll(x, quarter, 1) * sin_p)


def _inproj_kernel(ctx_len, x_ref, mod_ref, g_ref, w_ref, rope_ref, an_ref, ln_ref, cqn_ref, ckvn_ref,
                   wuq_ref, wukv_ref, mqk_ref, mo_ref, misc_ref, q_ref, k_ref, mvt_ref, vt_ref):
    tm = x_ref.shape[0]
    t0 = pl.program_id(1) * tm
    h = _modulated(x_ref[...], g_ref[...], mod_ref, t0, ctx_len, 0, 1).astype(BF16)

    def proj(a, b):
        return jnp.dot(h, w_ref[:, a:b], preferred_element_type=F32)

    misc = proj(O_MISC, O_QA)
    misc_ref[...] = misc

    cos_a, sinm_a, sinp_a = rope_ref[0], rope_ref[1], rope_ref[2]
    cos_l, sinm_l, sinp_l = rope_ref[3], rope_ref[4], rope_ref[5]
    a_scale = A_DH ** -0.5 * LOG2E
    l_scale = L_QK ** -0.5 * LOG2E

    def with_ones_t(v):
        return jnp.where(_lane_iota(v.shape) % SLOT == SUM_LANE, 1.0, v).T.astype(BF16)

    pc = proj(O_CQ, IN_PAD)
    cq = pc[:, 0:L_QRANK]
    cq = (cq * lax.rsqrt(jnp.mean(cq * cq, axis=-1, keepdims=True) + EPS) * cqn_ref[...]).astype(BF16)
    ckv = pc[:, L_QRANK:L_QRANK + L_KVRANK]
    ckv = (ckv * lax.rsqrt(jnp.mean(ckv * ckv, axis=-1, keepdims=True) + EPS) * ckvn_ref[...]).astype(BF16)
    ql = jnp.dot(cq, wuq_ref[...], preferred_element_type=F32)
    kvl = jnp.dot(ckv, wukv_ref[...], preferred_element_type=F32)
    gq_l, gk_l = ln_ref[0:1, :], ln_ref[1:2, :]
    lane = _lane_iota((tm, SLOT))
    kr = jnp.where((lane >= MISC_KR) & (lane < MISC_KR + L_ROPE), misc, 0.0)
    kr = kr * _slot_rms(kr, MISC_KR, MISC_KR + L_ROPE) * gk_l
    kr = _rope(kr, cos_l, sinm_l, sinp_l, L_ROPE // 4)
    for i in range(L_HEADS):
        x = ql[:, i * SLOT:(i + 1) * SLOT]
        inv = jnp.where(lane < L_NOPE, _slot_rms(x, 0, L_NOPE), _slot_rms(x, L_NOPE, L_QK))
        x = x * inv * gq_l
        q_ref[:, (Q_MLA + i) * SLOT:(Q_MLA + i + 1) * SLOT] = (
            _rope(x, cos_l, sinm_l, sinp_l, L_ROPE // 4) * l_scale).astype(BF16)
        kn = kvl[:, i * SLOT:(i + 1) * SLOT]
        kn = kn * _slot_rms(kn, 0, L_NOPE) * gk_l
        k_ref[:, i * SLOT:(i + 1) * SLOT] = (kn + kr).astype(BF16)
    vt_ref[0:KV_GQA * SLOT, :] = with_ones_t(kvl[:, L_HEADS * SLOT:])

    mqk_ref[...] = proj(O_MQK, O_MV).astype(BF16)
    pa = proj(O_QA, O_CQ)
    gq, gk = an_ref[0:1, :], an_ref[1:2, :]
    for i in range(A_HEADS):
        x = pa[:, i * SLOT:(i + 1) * SLOT]
        x = x * _slot_rms(x, 0, A_DH) * gq
        q_ref[:, i * SLOT:(i + 1) * SLOT] = (_rope(x, cos_a, sinm_a, sinp_a, A_DH // 4) * a_scale).astype(BF16)
    for i in range(A_KV):
        x = pa[:, (A_HEADS + i) * SLOT:(A_HEADS + i + 1) * SLOT]
        x = x * _slot_rms(x, 0, A_DH) * gk
        k_ref[:, (KV_GQA + i) * SLOT:(KV_GQA + i + 1) * SLOT] = (
            _rope(x, cos_a, sinm_a, sinp_a, A_DH // 4).astype(BF16))
    vt_ref[KV_GQA * SLOT:, :] = with_ones_t(pa[:, (A_HEADS + A_KV) * SLOT:(A_HEADS + 2 * A_KV) * SLOT])
    q_ref[:, A_HEADS * SLOT:Q_MLA * SLOT] = jnp.zeros((tm, (Q_MLA - A_HEADS) * SLOT), BF16)

    mv = proj(O_MV, O_MO)
    mvt_ref[...] = jnp.where(_lane_iota(mv.shape) % SLOT == DEN_LANE, 1.0, mv).T.astype(BF16)
    mo_ref[...] = proj(O_MO, O_MISC).astype(BF16)


def _inproj(xs, modtab, gain, w_in_p, rope_tab, an, ln, cqn, ckvn, wuq_p, wukv_p, ctx_len):
    b, s, d = xs.shape
    grid = (b, s // TB)
    tok = lambda n: pl.BlockSpec((None, TB, n), lambda i, j: (i, j, 0))
    full = lambda a: pl.BlockSpec(a.shape, lambda i, j: (0,) * a.ndim)
    out_widths = (1024, 512, SLOT, Q_SLOTS * SLOT, N_KVHEADS * SLOT)
    out_dtypes = (BF16, BF16, F32, BF16, BF16)
    out_specs = [tok(n) for n in out_widths]
    out_shape = [jax.ShapeDtypeStruct((b, s, n), dt) for n, dt in zip(out_widths, out_dtypes)]
    for n in (M_HEADS * SLOT, N_KVHEADS * SLOT):
        out_specs.append(pl.BlockSpec((None, n, TB), lambda i, j: (i, 0, j)))
        out_shape.append(jax.ShapeDtypeStruct((b, n, s), BF16))
    return pl.pallas_call(
        functools.partial(_inproj_kernel, ctx_len),
        grid=grid,
        in_specs=[tok(d),
                  pl.BlockSpec((None, 2, 6, d), lambda i, j: (i, 0, 0, 0)),
                  full(gain), full(w_in_p),
                  pl.BlockSpec((6, TB, SLOT), lambda i, j: (0, j, 0)),
                  full(an), full(ln), full(cqn), full(ckvn), full(wuq_p), full(wukv_p)],
        out_specs=out_specs,
        out_shape=out_shape,
        compiler_params=_cparams(("parallel", "parallel")),
        name="inproj",
    )(xs, modtab, gain, w_in_p, rope_tab, an, ln, cqn, ckvn, wuq_p, wukv_p)


def _conv_kernel(ctx_len, s_len, x_ref, prev_ref, next_ref, w_ref, sc_ref, qt_ref, k_ref):
    tm = x_ref.shape[0]
    t0 = pl.program_id(1) * tm
    x = x_ref[...].astype(F32)
    row = _row_iota((tm, 1))
    has_prev = jnp.logical_and(t0 != 0, t0 != ctx_len)
    has_next = jnp.logical_and(t0 + tm != ctx_len, t0 + tm != s_len)
    hp = jnp.where(has_prev, prev_ref[15:16, :].astype(F32), 0.0)
    hn = jnp.where(has_next, next_ref[0:1, :].astype(F32), 0.0)
    xp = jnp.where(row == 0, hp, pltpu.roll(x, 1, 0))
    xn = jnp.where(row == tm - 1, hn, pltpu.roll(x, tm - 1, 0))
    y = xp * w_ref[0:1, :] + x * w_ref[1:2, :] + xn * w_ref[2:3, :]
    y = _silu(y) * sc_ref[...]
    half = y.shape[1] // 2
    qt_ref[...] = y[:, :half].T.astype(qt_ref.dtype)
    k_ref[...] = y[:, half:].astype(k_ref.dtype)


def _conv(mqk, conv_w, conv_scale, ctx_len):
    b, s, n = mqk.shape
    hb = TB // 16
    last = s // 16 - 1
    return pl.pallas_call(
        functools.partial(_conv_kernel, ctx_len, s),
        grid=(b, s // TB),
        in_specs=[pl.BlockSpec((None, TB, n), lambda i, j: (i, j, 0)),
                  pl.BlockSpec((None, 16, n), lambda i, j: (i, jnp.maximum(j * hb - 1, 0), 0)),
                  pl.BlockSpec((None, 16, n), lambda i, j: (i, jnp.minimum((j + 1) * hb, last), 0)),
                  pl.BlockSpec((3, n), lambda i, j: (0, 0)),
                  pl.BlockSpec((1, n), lambda i, j: (0, 0))],
        out_specs=[pl.BlockSpec((None, n // 2, TB), lambda i, j: (i, 0, j)),
                   pl.BlockSpec((None, TB, n // 2), lambda i, j: (i, j, 0))],
        out_shape=[jax.ShapeDtypeStruct((b, n // 2, s), BF16), jax.ShapeDtypeStruct((b, s, n // 2), BF16)],
        compiler_params=_cparams(("parallel", "parallel")),
        name="mlstm_conv",
    )(mqk, mqk, mqk, conv_w, conv_scale)


def _log_sigmoid(x):
    return jnp.minimum(x, 0.0) - jnp.log(1.0 + jnp.exp(-jnp.abs(x)))


def _mlstm_kernel(n_chunks, qt_ref, k_ref, vt_ref, misc_ref, gb_ref, hf_ref, hb_ref, c_ref, m_ref, gt_sc, bt_sc):
    hp = pl.program_id(1)
    c_ref[...] = jnp.zeros_like(c_ref)
    m_ref[...] = jnp.zeros_like(m_ref)
    r = _row_iota((ML, ML))
    cidx = _lane_iota((ML, ML))
    tri_f32 = (r >= cidx).astype(F32)
    tri_b32 = (r <= cidx).astype(F32)
    lane = _lane_iota((ML, SLOT))
    lane_t = _lane_iota((1, ML))

    def products(t0, d, hh):
        sidx = d * 2 + hh
        ct_st = c_ref[sidx]
        k = k_ref[pl.ds(t0, ML), hh * SLOT:(hh + 1) * SLOT]
        qt = qt_ref[hh * SLOT:(hh + 1) * SLOT, pl.ds(t0, ML)]
        vt = vt_ref[hh * SLOT:(hh + 1) * SLOT, pl.ds(t0, ML)]
        kq = jnp.dot(k, qt, preferred_element_type=F32)
        cq = jnp.dot(ct_st.astype(BF16), qt, preferred_element_type=F32)
        return ct_st, k, vt, kq, cq

    def gates(t0, d, tri32):
        g = misc_ref[pl.ds(t0, ML), :] + gb_ref[...]
        logf = _log_sigmoid(g)
        bcum = jnp.dot(tri32, logf, preferred_element_type=F32, precision=lax.Precision.HIGHEST)
        gt_sc[d] = g.T
        bt_sc[d] = bcum.T
        return g - pltpu.roll(bcum, SLOT - M_HEADS, 1)

    def chain(t0, d, hh, ib, prod, out_ref):
        ct_st, k, vt, kq, cq = prod
        allowed = (r <= cidx) if d == 0 else (r >= cidx)
        li = d * 2 * M_HEADS + hp * 2 + hh
        ib_col = jnp.sum(jnp.where(lane == li, ib, 0.0), axis=-1, keepdims=True)
        i_row = gt_sc[d, pl.ds(li, 1), :]
        b_row = bt_sc[d, pl.ds(li + M_HEADS, 1), :]
        sidx = d * 2 + hh
        m_s = m_ref[sidx]
        dmat = jnp.where(allowed, b_row + ib_col, -jnp.inf)
        m_inter = b_row + m_s
        m_t = jnp.maximum(m_inter, jnp.max(dmat, axis=0, keepdims=True))
        w = (jnp.exp(dmat - m_t) * kq).astype(BF16)
        a_inter = jnp.exp(m_inter - m_t)
        num = a_inter * cq + jnp.dot(vt, w, preferred_element_type=F32)
        den = num[DEN_LANE:DEN_LANE + 1, :]
        h_out = num / jnp.maximum(jnp.abs(den), jnp.exp(-m_t))
        out_ref[pl.ds(t0, ML), hh * SLOT:(hh + 1) * SLOT] = h_out.T.astype(out_ref.dtype)
        last = ML - 1 if d == 0 else 0
        total = jnp.sum(jnp.where(lane_t == last, b_row, 0.0), axis=-1, keepdims=True)
        gg = total - b_row + i_row
        m_new = jnp.maximum(total + m_s, jnp.max(gg, axis=-1, keepdims=True))
        decay = jnp.exp(total + m_s - m_new)[:, 0:1]
        wk = jnp.exp(gg - m_new)
        vw = (vt.astype(F32) * wk).astype(BF16)
        c_ref[sidx] = decay * ct_st + jnp.dot(vw, k, preferred_element_type=F32)
        m_ref[sidx] = m_new

    def step(n, carry):
        tf0 = pl.multiple_of(n * ML, ML)
        tb0 = pl.multiple_of(jnp.where(n == 0, 0, n_chunks - n) * ML, ML)
        chains = [(tf0, 0, 0, hf_ref), (tf0, 0, 1, hf_ref), (tb0, 1, 0, hb_ref), (tb0, 1, 1, hb_ref)]
        ibs = (gates(tf0, 0, tri_f32), gates(tb0, 1, tri_b32))
        prods = [products(t0, d, hh) for t0, d, hh, _ in chains]
        for (t0, d, hh, out_ref), prod in zip(chains, prods):
            chain(t0, d, hh, ibs[d], prod, out_ref)
        return carry

    lax.fori_loop(0, n_chunks, step, 0)


def _mlstm(q_t, k_c, mv_t, misc, gate_b):
    b, s, _ = k_c.shape
    n_chunks = s // ML
    pair = 2 * SLOT
    spec = pl.BlockSpec((None, s, pair), lambda i, j: (i, 0, j))
    spec_t = pl.BlockSpec((None, pair, s), lambda i, j: (i, j, 0))
    return pl.pallas_call(
        functools.partial(_mlstm_kernel, n_chunks),
        grid=(b, M_HEADS // 2),
        in_specs=[spec_t, spec, spec_t,
                  pl.BlockSpec((None, s, SLOT), lambda i, j: (i, 0, 0)),
                  pl.BlockSpec((1, SLOT), lambda i, j: (0, 0))],
        out_specs=[spec, spec],
        out_shape=[jax.ShapeDtypeStruct((b, s, M_HEADS * SLOT), BF16)] * 2,
        scratch_shapes=[pltpu.VMEM((4, SLOT, SLOT), F32), pltpu.VMEM((4, 1, ML), F32),
                        pltpu.VMEM((2, SLOT, ML), F32), pltpu.VMEM((2, SLOT, ML), F32)],
        compiler_params=_cparams(("parallel", "parallel")),
        name="mlstm_scan",
    )(q_t, k_c, mv_t, misc, gate_b)


def _attn_kernel(ctx_len, q_off, kv_of, q_ref, k_ref, vt_ref, o_ref, st_sc, pt_sc, acc_sc):
    s_len = k_ref.shape[0]
    n_heads = len(kv_of)
    qs = [q_ref[:, g * SLOT:(g + 1) * SLOT] for g in range(n_heads)]

    def kv_cols(g):
        return slice(kv_of[g] * SLOT, (kv_of[g] + 1) * SLOT)

    def chunk(carry, rows):
        out = []
        for g in range(n_heads):
            m, acc = carry[g]
            k = k_ref[rows, kv_cols(g)]
            vt = vt_ref[kv_cols(g), rows]
            st = lax.dot_general(k, qs[g], (((1,), (1,)), ((), ())), preferred_element_type=F32)
            m_new = jnp.maximum(m, jnp.max(st, axis=0, keepdims=True))
            alpha = jnp.exp2(m - m_new)
            pt = jnp.exp2(st - m_new).astype(BF16)
            out.append((m_new, alpha * acc + jnp.dot(vt, pt, preferred_element_type=F32)))
        return tuple(out)

    def finish(carry):
        for g in range(n_heads):
            acc = carry[g][1]
            o = acc / acc[SUM_LANE:SUM_LANE + 1, :]
            o_ref[:, g * SLOT:(g + 1) * SLOT] = o.T.astype(o_ref.dtype)

    init = tuple((jnp.full((1, TQ), -jnp.inf, F32), jnp.zeros((SLOT, TQ), F32)) for _ in range(n_heads))
    is_ctx = (pl.program_id(2) + q_off) * TQ < ctx_len

    @pl.when(is_ctx)
    def _():
        finish(chunk(init, pl.ds(0, ctx_len)))

    n_lat = (s_len - ctx_len) // TK
    n0 = ctx_len + TK
    assert n_lat >= 4 and n_lat % 2 == 0 and st_sc.shape[1] == n0

    def rows_of(c):
        if isinstance(c, int):
            return (pl.ds(0, n0), n0) if c == 0 else (pl.ds(ctx_len + c * TK, TK), TK)
        return pl.ds(pl.multiple_of(ctx_len + c * TK, math.gcd(ctx_len, TK)), TK), TK

    def scores(c, par):
        rows, n = rows_of(c)
        for g in range(n_heads):
            st_sc[par * n_heads + g, 0:n, :] = lax.dot_general(
                k_ref[rows, kv_cols(g)], qs[g], (((1,), (1,)), ((), ())), preferred_element_type=F32)

    def softmax(par, m, n=TK):
        ms, alphas = [], []
        for g in range(n_heads):
            st = st_sc[par * n_heads + g, 0:n, :]
            m_new = jnp.maximum(m[g], jnp.max(st, axis=0, keepdims=True))
            pt_sc[par * n_heads + g, 0:n, :] = jnp.exp2(st - m_new).astype(BF16)
            ms.append(m_new)
            alphas.append(jnp.exp2(m[g] - m_new))
        return tuple(ms), tuple(alphas)

    def values(c, par, alpha):
        rows, n = rows_of(c)
        for g in range(n_heads):
            acc_sc[g] = alpha[g] * acc_sc[g] + jnp.dot(vt_ref[kv_cols(g), rows], pt_sc[par * n_heads + g, 0:n, :],
                                                       preferred_element_type=F32)

    def stage(c, par, m, alpha):
        scores(c + 1, 1 - par)
        m_new, alpha_new = softmax(par, m)
        values(c - 1, 1 - par, alpha)
        return m_new, alpha_new

    @pl.when(jnp.logical_not(is_ctx))
    def _():
        for g in range(n_heads):
            acc_sc[g] = init[g][1]
        scores(0, 0)
        m, alpha = softmax(0, tuple(c[0] for c in init), n0)
        scores(1, 1)
        m, alpha = stage(1, 1, m, alpha)

        def body(t, carry):
            c = 2 * t + 2
            return stage(c + 1, 1, *stage(c, 0, *carry))

        m, alpha = lax.fori_loop(0, (n_lat - 4) // 2, body, (m, alpha))
        m, alpha = stage(n_lat - 2, 0, m, alpha)
        values(n_lat - 2, 0, alpha)
        m, alpha = softmax(1, m)
        values(n_lat - 1, 1, alpha)
        finish(tuple((m[g], acc_sc[g]) for g in range(n_heads)))


def _attention(q_all, k_all, vt_all, ctx_len, need_ctx, q_slot0, kv_slot0, n_kv, kv_of):
    b, s, _ = q_all.shape
    q_off = 0 if need_ctx else ctx_len // TQ
    nq = s // TQ - q_off
    n_heads = len(kv_of)
    qw = n_heads * SLOT
    kw = n_kv * SLOT
    assert (q_slot0 * SLOT) % qw == 0 and (kv_slot0 * SLOT) % kw == 0
    qb0, kb0 = q_slot0 * SLOT // qw, kv_slot0 * SLOT // kw
    return pl.pallas_call(
        functools.partial(_attn_kernel, ctx_len, q_off, kv_of),
        grid=(b, 1, nq),
        in_specs=[pl.BlockSpec((None, TQ, qw), lambda i, g, j: (i, j + q_off, qb0)),
                  pl.BlockSpec((None, s, kw), lambda i, g, j: (i, 0, kb0)),
                  pl.BlockSpec((None, kw, s), lambda i, g, j: (i, kb0, 0))],
        out_specs=pl.BlockSpec((None, TQ, qw), lambda i, g, j: (i, j + q_off, 0)),
        out_shape=jax.ShapeDtypeStruct((b, s, qw), BF16),
        scratch_shapes=[pltpu.VMEM((2 * n_heads, ctx_len + TK, TQ), F32),
                        pltpu.VMEM((2 * n_heads, ctx_len + TK, TQ), BF16),
                        pltpu.VMEM((n_heads, SLOT, TQ), F32)],
        compiler_params=_cparams(("parallel", "parallel", "parallel")),
        name="attention",
    )(q_all, k_all, vt_all)


def _mixout_kernel(ctx_len, blk_off, x_ref, mod_ref, hf_ref, hb_ref, mo_ref, oa_ref, ol_ref, mn_ref, w_ref, o_ref):
    tm = x_ref.shape[0]
    t0 = (pl.program_id(1) + blk_off) * tm
    hm = hf_ref[...].astype(F32) + hb_ref[...].astype(F32)
    gate = _sigmoid(mo_ref[...].astype(F32))
    parts = []
    for i in range(M_HEADS):
        x = hm[:, i * SLOT:(i + 1) * SLOT]
        inv = lax.rsqrt(jnp.sum(x * x, axis=-1, keepdims=True) * (1.0 / M_DH) + EPS)
        parts.append(x * inv)
    hn = jnp.concatenate(parts, axis=-1) * mn_ref[...] * gate
    nm = M_HEADS * SLOT
    o = jnp.dot(hn.astype(BF16), w_ref[0:nm, :], preferred_element_type=F32)
    na = nm + A_HEADS * SLOT
    o = o + jnp.dot(oa_ref[...], w_ref[nm:na, :], preferred_element_type=F32)
    o = o + jnp.dot(ol_ref[...], w_ref[na:, :], preferred_element_type=F32)
    o_ref[...] = x_ref[...] + _gate_rows(mod_ref, t0, tm, ctx_len, 2) * o


def _mixout(xs, modtab, hf, hb, mo, o_gqa, o_mla, m_norm_p, w_out_p, ctx_len, lat_only):
    b, s, d = xs.shape
    blk_off = ctx_len // TB if lat_only else 0
    tok = lambda n: pl.BlockSpec((None, TB, n), lambda i, j: (i, j + blk_off, 0))
    full = lambda a: pl.BlockSpec(a.shape, lambda i, j: (0,) * a.ndim)
    return pl.pallas_call(
        functools.partial(_mixout_kernel, ctx_len, blk_off),
        grid=(b, s // TB - blk_off),
        in_specs=[tok(d),
                  pl.BlockSpec((None, 2, 6, d), lambda i, j: (i, 0, 0, 0)),
                  tok(M_HEADS * SLOT), tok(M_HEADS * SLOT), tok(M_HEADS * SLOT), tok(A_HEADS * SLOT),
                  tok(L_HEADS * SLOT), full(m_norm_p), full(w_out_p)],
        out_specs=pl.BlockSpec((None, TB, d), lambda i, j: (i, j, 0)),
        out_shape=jax.ShapeDtypeStruct((b, s - blk_off * TB, d), F32),
        input_output_aliases={} if lat_only else {0: 0},
        compiler_params=_cparams(("parallel", "parallel")),
        name="mix_out",
    )(xs, modtab, hf, hb, mo, o_gqa, o_mla, m_norm_p, w_out_p)


def _ffn_kernel(ctx_len, x_ref, mod_ref, g_ref, wg_ref, wu_ref, wd_ref, o_ref, h_sc, acc_sc):
    tm = x_ref.shape[0]
    t0 = pl.program_id(1) * tm
    f = pl.program_id(2)

    @pl.when(f == 0)
    def _():
        h_sc[...] = _modulated(x_ref[...], g_ref[...], mod_ref, t0, ctx_len, 3, 4).astype(BF16)
        acc_sc[...] = jnp.zeros_like(acc_sc)

    h = h_sc[...]
    a = jnp.dot(h, wg_ref[...], preferred_element_type=F32)
    u = jnp.dot(h, wu_ref[...], preferred_element_type=F32)
    acc_sc[...] += jnp.dot((_silu(a) * u).astype(BF16), wd_ref[...], preferred_element_type=F32)

    @pl.when(f == pl.num_programs(2) - 1)
    def _():
        o_ref[...] = x_ref[...] + _gate_rows(mod_ref, t0, tm, ctx_len, 5) * acc_sc[...]


def _ffn(xs, modtab, gain, wg, wu, wd, ctx_len):
    b, s, d = xs.shape
    tm = s // 4
    nf = wg.shape[1] // TF
    return pl.pallas_call(
        functools.partial(_ffn_kernel, ctx_len),
        grid=(b, s // tm, nf),
        in_specs=[pl.BlockSpec((None, tm, d), lambda i, j, f: (i, j, 0)),
                  pl.BlockSpec((None, 2, 6, d), lambda i, j, f: (i, 0, 0, 0)),
                  pl.BlockSpec((1, d), lambda i, j, f: (0, 0)),
                  pl.BlockSpec((d, TF), lambda i, j, f: (0, f)),
                  pl.BlockSpec((d, TF), lambda i, j, f: (0, f)),
                  pl.BlockSpec((TF, d), lambda i, j, f: (f, 0))],
        out_specs=pl.BlockSpec((None, tm, d), lambda i, j, f: (i, j, 0)),
        out_shape=jax.ShapeDtypeStruct((b, s, d), F32),
        scratch_shapes=[pltpu.VMEM((tm, d), BF16), pltpu.VMEM((tm, d), F32)],
        input_output_aliases={0: 0},
        compiler_params=_cparams(("parallel", "parallel", "arbitrary")),
        name="ffn_dense",
    )(xs, modtab, gain, wg, wu, wd)


def _top2_combine(logits):
    lane = _lane_iota(logits.shape)
    lane_f = lane.astype(F32)
    lg = jnp.where(lane < N_EXPERTS, logits, -jnp.inf)
    v1 = jnp.max(lg, axis=-1, keepdims=True)
    i1 = jnp.min(jnp.where(lg == v1, lane_f, float(LANE)), axis=-1, keepdims=True)
    rest = jnp.where(lane_f == i1, -jnp.inf, lg)
    v2 = jnp.max(rest, axis=-1, keepdims=True)
    i2 = jnp.min(jnp.where(rest == v2, lane_f, float(LANE)), axis=-1, keepdims=True)
    e2 = jnp.exp(v2 - v1)
    w1 = 1.0 / (1.0 + e2)
    w2 = e2 / (1.0 + e2)
    sel = jnp.where((lane_f == i1) | (lane_f == i2), 1.0, 0.0)
    return jnp.where(lane_f == i1, w1, 0.0) + jnp.where(lane_f == i2, w2, 0.0), sel


def _moe_kernel(ctx_len, x_ref, mod_ref, g_ref, r_ref, tri_ref, wg_ref, wu_ref, wd_ref, o_ref,
                h_sc, comb_sc, rank_sc, rankt_sc, rankc_sc, wc_sc, xg_sc, acc_sc, nt_sc):
    tm = x_ref.shape[0]
    t0 = pl.program_id(1) * tm
    e = pl.program_id(2)
    f = pl.program_id(3)

    @pl.when(jnp.logical_and(e == 0, f == 0))
    def _():
        h = _modulated(x_ref[...], g_ref[...], mod_ref, t0, ctx_len, 3, 4)
        logits = jnp.dot(h, r_ref[...], preferred_element_type=F32, precision=lax.Precision.HIGHEST)
        comb, sel = _top2_combine(logits)
        rank =jnp.dot(tri_ref[...], sel.astype(BF16), preferred_element_type=F32)
        rank = jnp.where(sel > 0.0, rank, -1.0)
        comb_sc[...] = comb
        rank_sc[...] = rank
        pad = rankt_sc.shape[1] - tm
        h_sc[0:tm, :] = h.astype(BF16)
        if pad:
            rankt_sc[...] = jnp.concatenate([rank, jnp.full((pad, LANE), -1.0, F32)], axis=0).T
            h_sc[tm:, :] = jnp.zeros((pad, h_sc.shape[1]), BF16)
        else:
            rankt_sc[...] = rank.T
        o_ref[...] = jnp.zeros_like(o_ref)

    @pl.when(f == 0)
    def _():
        lane = _lane_iota((tm, LANE))
        rank_c = jnp.sum(jnp.where(lane == e, rank_sc[...], 0.0), axis=-1, keepdims=True)
        rankc_sc[...] = rank_c
        wc_sc[...] = jnp.sum(jnp.where(lane == e, comb_sc[...], 0.0), axis=-1, keepdims=True)
        n_rows = jnp.sum(jnp.where(rank_c >= 0.0, 1.0, 0.0)).astype(jnp.int32)
        n_rt = lax.div(n_rows + (RT - 1), RT)
        n_gt = lax.div(n_rt * RT + (GT - 1), GT)
        n_st = lax.div(n_rt * RT + (ST - 1), ST)
        nt_sc[0] = n_rt
        nt_sc[1] = n_st
        rank_r = rankt_sc[pl.ds(e, 1), :]

        def gather(i, c):
            r0 = pl.multiple_of(i * GT, 16)
            tgt = (r0 + _row_iota((GT, 1))).astype(F32)
            sel_t = jnp.where(rank_r == tgt, 1.0, 0.0).astype(BF16)
            xg_sc[pl.ds(r0, GT), :] = jnp.dot(sel_t, h_sc[...], preferred_element_type=F32).astype(BF16)
            return c

        def clear(i, c):
            acc_sc[pl.ds(pl.multiple_of(i * ST, ST), ST), :] = jnp.zeros((ST, acc_sc.shape[1]), F32)
            return c

        lax.fori_loop(0, n_gt, gather, 0)
        lax.fori_loop(0, n_st, clear, 0)

    def expert(i, c):
        r0 = pl.multiple_of(i * RT, 16)
        rows = xg_sc[pl.ds(r0, RT), :]
        a = jnp.dot(rows, wg_ref[...], preferred_element_type=F32)
        u = jnp.dot(rows, wu_ref[...], preferred_element_type=F32)
        acc_sc[pl.ds(r0, RT), :] += jnp.dot((_silu(a) * u).astype(BF16), wd_ref[...],
                                            preferred_element_type=F32)
        return c

    lax.fori_loop(0, nt_sc[0], expert, 0)

    @pl.when(f == pl.num_programs(3) - 1)
    def _():
        def scatter(i, c):
            r0 = pl.multiple_of(i * ST, ST)
            tgt = (r0 + _lane_iota((1, ST))).astype(F32)
            w_t = jnp.where(rankc_sc[...] == tgt, wc_sc[...], 0.0).astype(BF16)
            o_ref[...] += jnp.dot(w_t, acc_sc[pl.ds(r0, ST), :].astype(BF16), preferred_element_type=F32)
            return c

        lax.fori_loop(0, nt_sc[1], scatter, 0)

    @pl.when(jnp.logical_and(e == pl.num_programs(2) - 1, f == pl.num_programs(3) - 1))
    def _():
        o_ref[...] = x_ref[...] + _gate_rows(mod_ref, t0, tm, ctx_len, 5) * o_ref[...]


def _moe(xs, modtab, gain, router_p, wg, wu, wd, ctx_len):
    b, s, d = xs.shape
    tm = s // 4
    tp = -(-tm // LANE) * LANE
    r_max = -(-tm // RT) * RT
    tr = max(-(-r_max // GT) * GT, -(-r_max // ST) * ST)
    ne, _, dff = wg.shape
    tri = jnp.tril(jnp.ones((tm, tm), BF16), -1)
    TF = TF_MOE
    assert dff % TF == 0
    return pl.pallas_call(
        functools.partial(_moe_kernel, ctx_len),
        grid=(b, s // tm, ne, dff // TF),
        in_specs=[pl.BlockSpec((None, tm, d), lambda i, j, e, f: (i, j, 0), pipeline_mode=pl.Buffered(1)),
                  pl.BlockSpec((None, 2, 6, d), lambda i, j, e, f: (i, 0, 0, 0)),
                  pl.BlockSpec((1, d), lambda i, j, e, f: (0, 0)),
                  pl.BlockSpec((d, LANE), lambda i, j, e, f: (0, 0)),
                  pl.BlockSpec((tm, tm), lambda i, j, e, f: (0, 0), pipeline_mode=pl.Buffered(1)),
                  pl.BlockSpec((None, d, TF), lambda i, j, e, f: (e, 0, f)),
                  pl.BlockSpec((None, d, TF), lambda i, j, e, f: (e, 0, f)),
                  pl.BlockSpec((None, TF, d), lambda i, j, e, f: (e, f, 0))],
        out_specs=pl.BlockSpec((None, tm, d), lambda i, j, e, f: (i, j, 0)),
        out_shape=jax.ShapeDtypeStruct((b, s, d), F32),
        scratch_shapes=[pltpu.VMEM((tp, d), BF16),
                        pltpu.VMEM((tm, LANE), F32),
                        pltpu.VMEM((tm, LANE), F32),
                        pltpu.VMEM((LANE, tp), F32),
                        pltpu.VMEM((tm, 1), F32),
                        pltpu.VMEM((tm, 1), F32),
                        pltpu.VMEM((tr, d), BF16),
                        pltpu.VMEM((tr, d), F32),
                        pltpu.SMEM((2,), jnp.int32)],
        input_output_aliases={0: 0},
        compiler_params=_cparams(("parallel", "parallel", "arbitrary", "arbitrary")),
        name="moe_top2",
    )(xs, modtab, gain, router_p, tri, wg, wu, wd)


def _pad_heads(w, n_heads, dh, axis=-1):
    axis = axis % w.ndim
    shp = w.shape[:axis] + (n_heads, dh) + w.shape[axis + 1:]
    pad = [(0, 0)] * (w.ndim + 1)
    pad[axis + 1] = (0, SLOT - dh)
    out = jnp.pad(w.reshape(shp), pad)
    return out.reshape(w.shape[:axis] + (n_heads * SLOT,) + w.shape[axis + 1:])


def _rope_tables(seq, ctx_len):
    t = jnp.arange(seq)
    rows = (t // GRID_W).astype(F32)
    cols = (t % GRID_W).astype(F32)

    def angles(rot_dim):
        nf = rot_dim // 4
        inv = ROPE_THETA ** (-jnp.arange(nf, dtype=F32) / nf)
        ar = rows[:, None] * inv
        ac = cols[:, None] * inv
        return jnp.concatenate([ar, ar, ac, ac], axis=-1)

    def slot_tables(rot_dim, lane0):
        ang = angles(rot_dim)
        quarter = rot_dim // 4
        first = (jnp.arange(rot_dim) % (2 * quarter)) < quarter
        cos = jnp.ones((seq, SLOT), F32).at[:, lane0:lane0 + rot_dim].set(jnp.cos(ang))
        sin = jnp.sin(ang)
        sin_m = jnp.zeros((seq, SLOT), F32).at[:, lane0:lane0 + rot_dim].set(jnp.where(first, -sin, 0.0))
        sin_p = jnp.zeros((seq, SLOT), F32).at[:, lane0:lane0 + rot_dim].set(jnp.where(first, 0.0, sin))
        ident = jnp.stack([jnp.ones((ctx_len, SLOT), F32), jnp.zeros((ctx_len, SLOT), F32),
                           jnp.zeros((ctx_len, SLOT), F32)])
        return jnp.concatenate([ident, jnp.stack([cos, sin_m, sin_p])], axis=1)

    return jnp.concatenate([slot_tables(A_DH, 0), slot_tables(L_ROPE, L_NOPE)], axis=0)


def _layer_params(w_in, m_conv, m_gate_b, m_norm, a_qnorm, a_knorm, l_cq_norm, l_ckv_norm, l_wuq, l_wukv,
                  l_qnorm, l_knorm, w_out):
    d = w_in.shape[0]
    offs = np.cumsum((0,) + IN_SIZES)
    seg = [w_in[:, offs[i]:offs[i + 1]] for i in range(len(IN_SIZES))]
    misc = jnp.zeros((d, SLOT), F32).at[:, 0:4 * M_HEADS].set(seg[4]).at[:, MISC_KR:MISC_KR + L_ROPE].set(seg[10])
    w_in_p = jnp.concatenate(
        [_pad_heads(seg[0], M_HEADS, M_DH), _pad_heads(seg[1], M_HEADS, M_DH),
         _pad_heads(seg[2], M_HEADS, M_DH), _pad_heads(seg[3], M_HEADS, M_DH), misc,
         _pad_heads(seg[5], A_HEADS, A_DH), _pad_heads(seg[6], A_KV, A_DH), _pad_heads(seg[7], A_KV, A_DH),
         seg[8], seg[9]], axis=1).astype(BF16)
    conv_w = jnp.concatenate([_pad_heads(m_conv[:, :M_HEADS * M_DH], M_HEADS, M_DH),
                              _pad_heads(m_conv[:, M_HEADS * M_DH:], M_HEADS, M_DH)], axis=1)
    conv_scale = jnp.concatenate([jnp.ones((1, M_HEADS * SLOT), F32),
                                  jnp.full((1, M_HEADS * SLOT), M_DH ** -0.5, F32)], axis=1)
    gate_b = jnp.zeros((1, SLOT), F32).at[0, 0:4 * M_HEADS].set(m_gate_b)
    pad1 = lambda g: jnp.pad(g, (0, SLOT - g.shape[0]))
    an = jnp.stack([pad1(a_qnorm), pad1(a_knorm)])
    ln = jnp.stack([pad1(l_qnorm), pad1(l_knorm)])
    wuq_p = _pad_heads(l_wuq, L_HEADS, L_QK).astype(BF16)
    kv = l_wukv.reshape(L_KVRANK, L_HEADS, L_NOPE + L_DV)
    wukv_p = jnp.concatenate(
        [_pad_heads(kv[:, :, :L_NOPE].reshape(L_KVRANK, -1), L_HEADS, L_NOPE),
         _pad_heads(kv[:, :, L_NOPE:].reshape(L_KVRANK, -1), L_HEADS, L_DV)], axis=1).astype(BF16)
    nm, na = M_HEADS * M_DH, A_HEADS * A_DH
    w_out_p = jnp.concatenate(
        [_pad_heads(w_out[:nm], M_HEADS, M_DH, axis=0), _pad_heads(w_out[nm:nm + na], A_HEADS, A_DH, axis=0),
         _pad_heads(w_out[nm + na:], L_HEADS, L_DV, axis=0)], axis=0).astype(BF16)
    m_norm_p = _pad_heads(m_norm[None, :], M_HEADS, M_DH)
    return dict(w_in_p=w_in_p, conv_w=conv_w, conv_scale=conv_scale, gate_b=gate_b, an=an, ln=ln,
                cqn=l_cq_norm[None, :], ckvn=l_ckv_norm[None, :], wuq_p=wuq_p, wukv_p=wukv_p,
                w_out_p=w_out_p, m_norm_p=m_norm_p)


def kernel(x, c, ctx, c_ctx, mod_w, mod_b, norm_mix, norm_ffn, w_in, m_conv, m_gate_b, m_norm, a_qnorm, a_knorm,
           l_cq_norm, l_ckv_norm, l_wuq, l_wukv, l_qnorm, l_knorm, w_out, ffn_wg, ffn_wu, ffn_wd, moe_router,
           moe_wg, moe_wu, moe_wd):
    b, seq, d = x.shape
    ctx_len = ctx.shape[1]
    depth = mod_w.shape[0]
    assert ctx_len % TB == 0 and ctx_len % TQ == 0 and ctx_len == ML and seq % TK == 0 and seq % GRID_W == 0
    xs = jnp.concatenate([ctx, x], axis=1)
    mod_rows = 16
    cc = jnp.zeros((mod_rows, d), F32).at[:b].set(c).at[b].set(c_ctx)
    mod_all = _mod_table(cc, mod_w, mod_b)
    rope_tab = _rope_tables(seq, ctx_len)
    for i in range(depth):
        need_ctx = i < depth - 1
        lat = mod_all[i, :b].reshape(b, 1, 6, d)
        cm = jnp.broadcast_to(mod_all[i, b].reshape(1, 1, 6, d), (b, 1, 6, d))
        modtab = jnp.concatenate([cm, lat], axis=1)
        p = _layer_params(w_in[i], m_conv[i], m_gate_b[i], m_norm[i], a_qnorm[i], a_knorm[i], l_cq_norm[i],
                          l_ckv_norm[i], l_wuq[i], l_wukv[i], l_qnorm[i], l_knorm[i], w_out[i])
        mqk, mo, misc, q_all, k_all, mv_t, v_all = _inproj(
            xs, modtab, norm_mix[i][None, :], p["w_in_p"], rope_tab, p["an"], p["ln"], p["cqn"], p["ckvn"],
            p["wuq_p"], p["wukv_p"], ctx_len)
        q_t, k_c = _conv(mqk, p["conv_w"], p["conv_scale"], ctx_len)
        hf, hb = _mlstm(q_t, k_c, mv_t, misc, p["gate_b"])
        gqa_kv = tuple(h // (A_HEADS // A_KV) for h in range(A_HEADS))
        o_gqa = _attention(q_all, k_all, v_all, ctx_len, need_ctx, 0, KV_GQA, A_KV, gqa_kv)
        o_mla = _attention(q_all, k_all, v_all, ctx_len, need_ctx, Q_MLA, 0, L_HEADS, tuple(range(L_HEADS)))
        xs = _mixout(xs, modtab, hf, hb, mo, o_gqa, o_mla, p["m_norm_p"], p["w_out_p"], ctx_len, not need_ctx)
        ffn_ctx = ctx_len if need_ctx else 0
        j = i // 2
        if i % 2 == 0:
            xs = _ffn(xs, modtab, norm_ffn[i][None, :], ffn_wg[j].astype(BF16), ffn_wu[j].astype(BF16),
                      ffn_wd[j].astype(BF16), ffn_ctx)
        else:
            router_p = jnp.pad(moe_router[j], ((0, 0), (0, LANE - N_EXPERTS)))
            xs = _moe(xs, modtab, norm_ffn[i][None, :], router_p, moe_wg[j].astype(BF16),
                      moe_wu[j].astype(BF16), moe_wd[j].astype(BF16), ffn_ctx)
    return xs
```

```python
import functools
import math

import numpy as np
import jax
import jax.numpy as jnp
from jax import lax
from jax.experimental import pallas as pl
from jax.experimental.pallas import tpu as pltpu

F32 = jnp.float32
BF16 = jnp.bfloat16

GRID_W = 64
EPS = 1e-6
ROPE_THETA = 10000.0
M_HEADS, M_DH = 4, 96
A_HEADS, A_KV, A_DH = 6, 2, 64
L_HEADS, L_NOPE, L_ROPE, L_DV = 4, 64, 32, 64
L_QK = L_NOPE + L_ROPE
L_QRANK, L_KVRANK = 256, 128
N_EXPERTS, TOP_K = 8, 2
IN_SIZES = (384, 384, 384, 384, 16, 384, 128, 128, 256, 128, 32)

LANE = 128
SLOT = LANE
N_KVHEADS = A_KV + L_HEADS
Q_MLA = 8
Q_SLOTS = Q_MLA + L_HEADS
KV_GQA = L_HEADS
VMEM_LIMIT = 56 * 1024 * 1024

O_MQK, O_MV, O_MO, O_MISC = 0, 1024, 1536, 2048
O_QA, O_KA, O_VA, O_CQ, O_CKV, IN_PAD = 2176, 2944, 3200, 3456, 3712, 3840
MISC_KR = 64
DEN_LANE = M_DH
SUM_LANE = A_DH
LOG2E = math.log2(math.e)
assert A_DH == L_DV and SUM_LANE < SLOT

TB = 256
TQ = 256
TK = 512
TM_FFN = 1088
TF = 512
ML = 256
RT = 144
GT = 2 * RT
ST = 3 * LANE
TF_MOE = 1792


def _cparams(sem):
    return pltpu.CompilerParams(dimension_semantics=sem, vmem_limit_bytes=VMEM_LIMIT)


def _sigmoid(x):
    return 1.0 / (1.0 + jnp.exp(-x))


def _silu(x):
    return x * _sigmoid(x)


def _lane_iota(shape):
    return lax.broadcasted_iota(jnp.int32, shape, len(shape) - 1)


def _row_iota(shape):
    return lax.broadcasted_iota(jnp.int32, shape, 0)


def _modulated(x, gain, mod_ref, t0, ctx_len, k_shift, k_scale):
    tm = x.shape[0]
    is_ctx = (t0 + _row_iota((tm, 1))) < ctx_len
    shift = jnp.where(is_ctx, mod_ref[0, k_shift:k_shift + 1, :], mod_ref[1, k_shift:k_shift + 1, :])
    scale = jnp.where(is_ctx, mod_ref[0, k_scale:k_scale + 1, :], mod_ref[1, k_scale:k_scale + 1, :])
    y = x * lax.rsqrt(jnp.mean(x * x, axis=-1, keepdims=True) + EPS) * gain
    return y * (1.0 + scale) + shift


def _gate_rows(mod_ref, t0, tm, ctx_len, k_gate):
    is_ctx = (t0 + _row_iota((tm, 1))) < ctx_len
    return jnp.where(is_ctx, mod_ref[0, k_gate:k_gate + 1, :], mod_ref[1, k_gate:k_gate + 1, :])


def _mod_kernel(c_ref, w_ref, b_ref, o_ref):
    s = _silu(c_ref[...]).astype(BF16)
    o_ref[...] = jnp.dot(s, w_ref[...].astype(BF16), preferred_element_type=F32) + b_ref[...]


def _mod_table(cc, mod_w, mod_b):
    depth, d, n = mod_w.shape
    rows = cc.shape[0]
    return pl.pallas_call(
        _mod_kernel,
        grid=(depth, n // d),
        in_specs=[pl.BlockSpec((rows, d), lambda l, j: (0, 0)),
                  pl.BlockSpec((None, d, d), lambda l, j: (l, 0, j)),
                  pl.BlockSpec((None, 1, d), lambda l, j: (l, 0, j))],
        out_specs=pl.BlockSpec((None, rows, d), lambda l, j: (l, 0, j)),
        out_shape=jax.ShapeDtypeStruct((depth, rows, n), F32),
        compiler_params=_cparams(("arbitrary", "arbitrary")),
        name="mod_table",
    )(cc, mod_w, mod_b.reshape(depth, 1, n))


def _slot_rms(x, lo, hi):
    lane = _lane_iota(x.shape)
    sq = jnp.where((lane >= lo) & (lane < hi), x * x, 0.0)
    return lax.rsqrt(jnp.sum(sq, axis=-1, keepdims=True) * (1.0 / (hi - lo)) + EPS)


def _rope(x, cos, sin_m, sin_p, quarter):
    return (x * cos + pltpu.roll(x, LANE - quarter, 1) * sin_m + pltpu.roll(x, quarter, 1) * sin_p)


def _inproj_kernel(ctx_len, x_ref, mod_ref, g_ref, w_ref, rope_ref, an_ref, ln_ref, cqn_ref, ckvn_ref,
                   wuq_ref, wukv_ref, mqk_ref, mo_ref, misc_ref, q_ref, k_ref, mvt_ref, vt_ref):
    tm = x_ref.shape[0]
    t0 = pl.program_id(1) * tm
    h = _modulated(x_ref[...], g_ref[...], mod_ref, t0, ctx_len, 0, 1).astype(BF16)

    def proj(a, b):
        return jnp.dot(h, w_ref[:, a:b], preferred_element_type=F32)

    misc = proj(O_MISC, O_QA)
    misc_ref[...] = misc

    cos_a, sinm_a, sinp_a = rope_ref[0], rope_ref[1], rope_ref[2]
    cos_l, sinm_l, sinp_l = rope_ref[3], rope_ref[4], rope_ref[5]
    a_scale = A_DH ** -0.5 * LOG2E
    l_scale = L_QK ** -0.5 * LOG2E

    def with_ones_t(v):
        return jnp.where(_lane_iota(v.shape) % SLOT == SUM_LANE, 1.0, v).T.astype(BF16)

    pc = proj(O_CQ, IN_PAD)
    cq = pc[:, 0:L_QRANK]
    cq = (cq * lax.rsqrt(jnp.mean(cq * cq, axis=-1, keepdims=True) + EPS) * cqn_ref[...]).astype(BF16)
    ckv = pc[:, L_QRANK:L_QRANK + L_KVRANK]
    ckv = (ckv * lax.rsqrt(jnp.mean(ckv * ckv, axis=-1, keepdims=True) + EPS) * ckvn_ref[...]).astype(BF16)
    ql = jnp.dot(cq, wuq_ref[...], preferred_element_type=F32)
    kvl = jnp.dot(ckv, wukv_ref[...], preferred_element_type=F32)
    gq_l, gk_l = ln_ref[0:1, :], ln_ref[1:2, :]
    lane = _lane_iota((tm, SLOT))
    kr = jnp.where((lane >= MISC_KR) & (lane < MISC_KR + L_ROPE), misc, 0.0)
    kr = kr * _slot_rms(kr, MISC_KR, MISC_KR + L_ROPE) * gk_l
    kr = _rope(kr, cos_l, sinm_l, sinp_l, L_ROPE // 4)
    for i in range(L_HEADS):
        x = ql[:, i * SLOT:(i + 1) * SLOT]
        inv = jnp.where(lane < L_NOPE, _slot_rms(x, 0, L_NOPE), _slot_rms(x, L_NOPE, L_QK))
        x = x * inv * gq_l
        q_ref[:, (Q_MLA + i) * SLOT:(Q_MLA + i + 1) * SLOT] = (
            _rope(x, cos_l, sinm_l, sinp_l, L_ROPE // 4) * l_scale).astype(BF16)
        kn = kvl[:, i * SLOT:(i + 1) * SLOT]
        kn = kn * _slot_rms(kn, 0, L_NOPE) * gk_l
        k_ref[:, i * SLOT:(i + 1) * SLOT] = (kn + kr).astype(BF16)
    vt_ref[0:KV_GQA * SLOT, :] = with_ones_t(kvl[:, L_HEADS * SLOT:])

    mqk_ref[...] = proj(O_MQK, O_MV).astype(BF16)
    pa = proj(O_QA, O_CQ)
    gq, gk = an_ref[0:1, :], an_ref[1:2, :]
    for i in range(A_HEADS):
        x = pa[:, i * SLOT:(i + 1) * SLOT]
        x = x * _slot_rms(x, 0, A_DH) * gq
        q_ref[:, i * SLOT:(i + 1) * SLOT] = (_rope(x, cos_a, sinm_a, sinp_a, A_DH // 4) * a_scale).astype(BF16)
    for i in range(A_KV):
        x = pa[:, (A_HEADS + i) * SLOT:(A_HEADS + i + 1) * SLOT]
        x = x * _slot_rms(x, 0, A_DH) * gk
        k_ref[:, (KV_GQA + i) * SLOT:(KV_GQA + i + 1) * SLOT] = (
            _rope(x, cos_a, sinm_a, sinp_a, A_DH // 4).astype(BF16))
    vt_ref[KV_GQA * SLOT:, :] = with_ones_t(pa[:, (A_HEADS + A_KV) * SLOT:(A_HEADS + 2 * A_KV) * SLOT])
    q_ref[:, A_HEADS * SLOT:Q_MLA * SLOT] = jnp.zeros((tm, (Q_MLA - A_HEADS) * SLOT), BF16)

    mv = proj(O_MV, O_MO)
    mvt_ref[...] = jnp.where(_lane_iota(mv.shape) % SLOT == DEN_LANE, 1.0, mv).T.astype(BF16)
    mo_ref[...] = proj(O_MO, O_MISC).astype(BF16)


def _inproj(xs, modtab, gain, w_in_p, rope_tab, an, ln, cqn, ckvn, wuq_p, wukv_p, ctx_len):
    b, s, d = xs.shape
    grid = (b, s // TB)
    tok = lambda n: pl.BlockSpec((None, TB, n), lambda i, j: (i, j, 0))
    full = lambda a: pl.BlockSpec(a.shape, lambda i, j: (0,) * a.ndim)
    out_widths = (1024, 512, SLOT, Q_SLOTS * SLOT, N_KVHEADS * SLOT)
    out_dtypes = (BF16, BF16, F32, BF16, BF16)
    out_specs = [tok(n) for n in out_widths]
    out_shape = [jax.ShapeDtypeStruct((b, s, n), dt) for n, dt in zip(out_widths, out_dtypes)]
    for n in (M_HEADS * SLOT, N_KVHEADS * SLOT):
        out_specs.append(pl.BlockSpec((None, n, TB), lambda i, j: (i, 0, j)))
        out_shape.append(jax.ShapeDtypeStruct((b, n, s), BF16))
    return pl.pallas_call(
        functools.partial(_inproj_kernel, ctx_len),
        grid=grid,
        in_specs=[tok(d),
                  pl.BlockSpec((None, 2, 6, d), lambda i, j: (i, 0, 0, 0)),
                  full(gain), full(w_in_p),
                  pl.BlockSpec((6, TB, SLOT), lambda i, j: (0, j, 0)),
                  full(an), full(ln), full(cqn), full(ckvn), full(wuq_p), full(wukv_p)],
        out_specs=out_specs,
        out_shape=out_shape,
        compiler_params=_cparams(("parallel", "parallel")),
        name="inproj",
    )(xs, modtab, gain, w_in_p, rope_tab, an, ln, cqn, ckvn, wuq_p, wukv_p)


def _conv_kernel(ctx_len, s_len, x_ref, prev_ref, next_ref, w_ref, sc_ref, qt_ref, k_ref):
    tm = x_ref.shape[0]
    t0 = pl.program_id(1) * tm
    x = x_ref[...].astype(F32)
    row = _row_iota((tm, 1))
    has_prev = jnp.logical_and(t0 != 0, t0 != ctx_len)
    has_next = jnp.logical_and(t0 + tm != ctx_len, t0 + tm != s_len)
    hp = jnp.where(has_prev, prev_ref[15:16, :].astype(F32), 0.0)
    hn = jnp.where(has_next, next_ref[0:1, :].astype(F32), 0.0)
    xp = jnp.where(row == 0, hp, pltpu.roll(x, 1, 0))
    xn = jnp.where(row == tm - 1, hn, pltpu.roll(x, tm - 1, 0))
    y = xp * w_ref[0:1, :] + x * w_ref[1:2, :] + xn * w_ref[2:3, :]
    y = _silu(y) * sc_ref[...]
    half = y.shape[1] // 2
    qt_ref[...] = y[:, :half].T.astype(qt_ref.dtype)
    k_ref[...] = y[:, half:].astype(k_ref.dtype)


def _conv(mqk, conv_w, conv_scale, ctx_len):
    b, s, n = mqk.shape
    hb = TB // 16
    last = s // 16 - 1
    return pl.pallas_call(
        functools.partial(_conv_kernel, ctx_len, s),
        grid=(b, s // TB),
        in_specs=[pl.BlockSpec((None, TB, n), lambda i, j: (i, j, 0)),
                  pl.BlockSpec((None, 16, n), lambda i, j: (i, jnp.maximum(j * hb - 1, 0), 0)),
                  pl.BlockSpec((None, 16, n), lambda i, j: (i, jnp.minimum((j + 1) * hb, last), 0)),
                  pl.BlockSpec((3, n), lambda i, j: (0, 0)),
                  pl.BlockSpec((1, n), lambda i, j: (0, 0))],
        out_specs=[pl.BlockSpec((None, n // 2, TB), lambda i, j: (i, 0, j)),
                   pl.BlockSpec((None, TB, n // 2), lambda i, j: (i, j, 0))],
        out_shape=[jax.ShapeDtypeStruct((b, n // 2, s), BF16), jax.ShapeDtypeStruct((b, s, n // 2), BF16)],
        compiler_params=_cparams(("parallel", "parallel")),
        name="mlstm_conv",
    )(mqk, mqk, mqk, conv_w, conv_scale)


def _log_sigmoid(x):
    return jnp.minimum(x, 0.0) - jnp.log(1.0 + jnp.exp(-jnp.abs(x)))


def _mlstm_kernel(n_chunks, qt_ref, k_ref, vt_ref, misc_ref, gb_ref, hf_ref, hb_ref, c_ref, m_ref, gt_sc, bt_sc):
    hp = pl.program_id(1)
    c_ref[...] = jnp.zeros_like(c_ref)
    m_ref[...] = jnp.zeros_like(m_ref)
    r = _row_iota((ML, ML))
    cidx = _lane_iota((ML, ML))
    tri_f32 = (r >= cidx).astype(F32)
    tri_b32 = (r <= cidx).astype(F32)
    lane = _lane_iota((ML, SLOT))
    lane_t = _lane_iota((1, ML))

    def products(t0, d, hh):
        sidx = d * 2 + hh
        ct_st = c_ref[sidx]
        k = k_ref[pl.ds(t0, ML), hh * SLOT:(hh + 1) * SLOT]
        qt = qt_ref[hh * SLOT:(hh + 1) * SLOT, pl.ds(t0, ML)]
        vt = vt_ref[hh * SLOT:(hh + 1) * SLOT, pl.ds(t0, ML)]
        kq = jnp.dot(k, qt, preferred_element_type=F32)
        cq = jnp.dot(ct_st.astype(BF16), qt, preferred_element_type=F32)
        return ct_st, k, vt, kq, cq

    def gates(t0, d, tri32):
        g = misc_ref[pl.ds(t0, ML), :] + gb_ref[...]
        logf = _log_sigmoid(g)
        bcum = jnp.dot(tri32, logf, preferred_element_type=F32, precision=lax.Precision.HIGHEST)
        gt_sc[d] = g.T
        bt_sc[d] = bcum.T
        return g - pltpu.roll(bcum, SLOT - M_HEADS, 1)

    def chain(t0, d, hh, ib, prod, out_ref):
        ct_st, k, vt, kq, cq = prod
        allowed = (r <= cidx) if d == 0 else (r >= cidx)
        li = d * 2 * M_HEADS + hp * 2 + hh
        ib_col = jnp.sum(jnp.where(lane == li, ib, 0.0), axis=-1, keepdims=True)
        i_row = gt_sc[d, pl.ds(li, 1), :]
        b_row = bt_sc[d, pl.ds(li + M_HEADS, 1), :]
        sidx = d * 2 + hh
        m_s = m_ref[sidx]
        dmat = jnp.where(allowed, b_row + ib_col, -jnp.inf)
        m_inter = b_row + m_s
        m_t = jnp.maximum(m_inter, jnp.max(dmat, axis=0, keepdims=True))
        w = (jnp.exp(dmat - m_t) * kq).astype(BF16)
        a_inter = jnp.exp(m_inter - m_t)
        num = a_inter * cq + jnp.dot(vt, w, preferred_element_type=F32)
        den = num[DEN_LANE:DEN_LANE + 1, :]
        h_out = num / jnp.maximum(jnp.abs(den), jnp.exp(-m_t))
        out_ref[pl.ds(t0, ML), hh * SLOT:(hh + 1) * SLOT] = h_out.T.astype(out_ref.dtype)
        last = ML - 1 if d == 0 else 0
        total = jnp.sum(jnp.where(lane_t == last, b_row, 0.0), axis=-1, keepdims=True)
        gg = total - b_row + i_row
        m_new = jnp.maximum(total + m_s, jnp.max(gg, axis=-1, keepdims=True))
        decay = jnp.exp(total + m_s - m_new)[:, 0:1]
        wk = jnp.exp(gg - m_new)
        vw = (vt.astype(F32) * wk).astype(BF16)
        c_ref[sidx] = decay * ct_st + jnp.dot(vw, k, preferred_element_type=F32)
        m_ref[sidx] = m_new

    def step(n, carry):
        tf0 = pl.multiple_of(n * ML, ML)
        tb0 = pl.multiple_of(jnp.where(n == 0, 0, n_chunks - n) * ML, ML)
        chains = [(tf0, 0, 0, hf_ref), (tf0, 0, 1, hf_ref), (tb0, 1, 0, hb_ref), (tb0, 1, 1, hb_ref)]
        ibs = (gates(tf0, 0, tri_f32), gates(tb0, 1, tri_b32))
        prods = [products(t0, d, hh) for t0, d, hh, _ in chains]
        for (t0, d, hh, out_ref), prod in zip(chains, prods):
            chain(t0, d, hh, ibs[d], prod, out_ref)
        return carry

    lax.fori_loop(0, n_chunks, step, 0)


def _mlstm(q_t, k_c, mv_t, misc, gate_b):
    b, s, _ = k_c.shape
    n_chunks = s // ML
    pair = 2 * SLOT
    spec = pl.BlockSpec((None, s, pair), lambda i, j: (i, 0, j))
    spec_t = pl.BlockSpec((None, pair, s), lambda i, j: (i, j, 0))
    return pl.pallas_call(
        functools.partial(_mlstm_kernel, n_chunks),
        grid=(b, M_HEADS // 2),
        in_specs=[spec_t, spec, spec_t,
                  pl.BlockSpec((None, s, SLOT), lambda i, j: (i, 0, 0)),
                  pl.BlockSpec((1, SLOT), lambda i, j: (0, 0))],
        out_specs=[spec, spec],
        out_shape=[jax.ShapeDtypeStruct((b, s, M_HEADS * SLOT), BF16)] * 2,
        scratch_shapes=[pltpu.VMEM((4, SLOT, SLOT), F32), pltpu.VMEM((4, 1, ML), F32),
                        pltpu.VMEM((2, SLOT, ML), F32), pltpu.VMEM((2, SLOT, ML), F32)],
        compiler_params=_cparams(("parallel", "parallel")),
        name="mlstm_scan",
    )(q_t, k_c, mv_t, misc, gate_b)


def _attn_kernel(ctx_len, q_off, kv_of, q_of, q_ref, k_ref, vt_ref, o_ref, st_sc, pt_sc, acc_sc):
    s_len = k_ref.shape[0]
    n_heads = len(kv_of)
    qs = [q_ref[:, q_of[g] * SLOT:(q_of[g] + 1) * SLOT] for g in range(n_heads)]

    def kv_cols(g):
        return slice(kv_of[g] * SLOT, (kv_of[g] + 1) * SLOT)

    def chunk(carry, rows):
        out = []
        for g in range(n_heads):
            m, acc = carry[g]
            k = k_ref[rows, kv_cols(g)]
            vt = vt_ref[kv_cols(g), rows]
            st = lax.dot_general(k, qs[g], (((1,), (1,)), ((), ())), preferred_element_type=F32)
            m_new = jnp.maximum(m, jnp.max(st, axis=0, keepdims=True))
            alpha = jnp.exp2(m - m_new)
            pt = jnp.exp2(st - m_new).astype(BF16)
            out.append((m_new, alpha * acc + jnp.dot(vt, pt, preferred_element_type=F32)))
        return tuple(out)

    def finish(carry):
        for g in range(n_heads):
            acc = carry[g][1]
            o = acc / acc[SUM_LANE:SUM_LANE + 1, :]
            o_ref[:, q_of[g] * SLOT:(q_of[g] + 1) * SLOT] = o.T.astype(o_ref.dtype)
        for slot in range(o_ref.shape[1] // SLOT):
            if slot not in q_of:
                o_ref[:, slot * SLOT:(slot + 1) * SLOT] = jnp.zeros((TQ, SLOT), o_ref.dtype)

    init = tuple((jnp.full((1, TQ), -jnp.inf, F32), jnp.zeros((SLOT, TQ), F32)) for _ in range(n_heads))
    is_ctx = (pl.program_id(2) + q_off) * TQ < ctx_len

    @pl.when(is_ctx)
    def _():
        finish(chunk(init, pl.ds(0, ctx_len)))

    n_lat = (s_len - ctx_len) // TK
    n0 = ctx_len + TK
    assert n_lat >= 4 and n_lat % 2 == 0 and st_sc.shape[1] == n0

    def rows_of(c):
        if isinstance(c, int):
            return (pl.ds(0, n0), n0) if c == 0 else (pl.ds(ctx_len + c * TK, TK), TK)
        return pl.ds(pl.multiple_of(ctx_len + c * TK, math.gcd(ctx_len, TK)), TK), TK

    def scores(c, par):
        rows, n = rows_of(c)
        for g in range(n_heads):
            st_sc[par * n_heads + g, 0:n, :] = lax.dot_general(
                k_ref[rows, kv_cols(g)], qs[g], (((1,), (1,)), ((), ())), preferred_element_type=F32)

    def softmax(par, m, n=TK):
        ms, alphas = [], []
        for g in range(n_heads):
            st = st_sc[par * n_heads + g, 0:n, :]
            m_new = jnp.maximum(m[g], jnp.max(st, axis=0, keepdims=True))
            pt_sc[par * n_heads + g, 0:n, :] = jnp.exp2(st - m_new).astype(BF16)
            ms.append(m_new)
            alphas.append(jnp.exp2(m[g] - m_new))
        return tuple(ms), tuple(alphas)

    def values(c, par, alpha):
        rows, n = rows_of(c)
        for g in range(n_heads):
            acc_sc[g] = alpha[g] * acc_sc[g] + jnp.dot(vt_ref[kv_cols(g), rows], pt_sc[par * n_heads + g, 0:n, :],
                                                       preferred_element_type=F32)

    def stage(c, par, m, alpha):
        scores(c + 1, 1 - par)
        m_new, alpha_new = softmax(par, m)
        values(c - 1, 1 - par, alpha)
        return m_new, alpha_new

    @pl.when(jnp.logical_not(is_ctx))
    def _():
        for g in range(n_heads):
            acc_sc[g] = init[g][1]
        scores(0, 0)
        m, alpha = softmax(0, tuple(c[0] for c in init), n0)
        scores(1, 1)
        m, alpha = stage(1, 1, m, alpha)

        def body(t, carry):
            c = 2 * t + 2
            return stage(c + 1, 1, *stage(c, 0, *carry))

        m, alpha = lax.fori_loop(0, (n_lat - 4) // 2, body, (m, alpha))
        m, alpha = stage(n_lat - 2, 0, m, alpha)
        values(n_lat - 2, 0, alpha)
        m, alpha = softmax(1, m)
        values(n_lat - 1, 1, alpha)
        finish(tuple((m[g], acc_sc[g]) for g in range(n_heads)))


def _attention(q_all, k_all, vt_all, ctx_len, need_ctx, kv_of, q_of):
    b, s, qw = q_all.shape
    q_off = 0 if need_ctx else ctx_len // TQ
    nq = s // TQ - q_off
    n_heads = len(kv_of)
    kw = k_all.shape[2]
    qb0, kb0 = 0, 0
    return pl.pallas_call(
        functools.partial(_attn_kernel, ctx_len, q_off, kv_of, q_of),
        grid=(b, 1, nq),
        in_specs=[pl.BlockSpec((None, TQ, qw), lambda i, g, j: (i, j + q_off, qb0)),
                  pl.BlockSpec((None, s, kw), lambda i, g, j: (i, 0, kb0), pipeline_mode=pl.Buffered(1)),
                  pl.BlockSpec((None, kw, s), lambda i, g, j: (i, kb0, 0), pipeline_mode=pl.Buffered(1))],
        out_specs=pl.BlockSpec((None, TQ, qw), lambda i, g, j: (i, j + q_off, 0)),
        out_shape=jax.ShapeDtypeStruct((b, s, qw), BF16),
        scratch_shapes=[pltpu.VMEM((2 * n_heads, ctx_len + TK, TQ), F32),
                        pltpu.VMEM((2 * n_heads, ctx_len + TK, TQ), BF16),
                        pltpu.VMEM((n_heads, SLOT, TQ), F32)],
        compiler_params=_cparams(("parallel", "parallel", "parallel")),
        name="attention",
    )(q_all, k_all, vt_all)


def _mixout_kernel(ctx_len, blk_off, x_ref, mod_ref, hf_ref, hb_ref, mo_ref, oa_ref, ol_ref, mn_ref, w_ref, o_ref):
    tm = x_ref.shape[0]
    t0 = (pl.program_id(1) + blk_off) * tm
    hm = hf_ref[...].astype(F32) + hb_ref[...].astype(F32)
    gate = _sigmoid(mo_ref[...].astype(F32))
    parts = []
    for i in range(M_HEADS):
        x = hm[:, i * SLOT:(i + 1) * SLOT]
        inv = lax.rsqrt(jnp.sum(x * x, axis=-1, keepdims=True) * (1.0 / M_DH) + EPS)
        parts.append(x * inv)
    hn = jnp.concatenate(parts, axis=-1) * mn_ref[...] * gate
    nm = M_HEADS * SLOT
    o = jnp.dot(hn.astype(BF16), w_ref[0:nm, :], preferred_element_type=F32)
    na = nm + A_HEADS * SLOT
    o = o + jnp.dot(oa_ref[...], w_ref[nm:na, :], preferred_element_type=F32)
    o = o + jnp.dot(ol_ref[...], w_ref[na:, :], preferred_element_type=F32)
    o_ref[...] = x_ref[...] + _gate_rows(mod_ref, t0, tm, ctx_len, 2) * o


def _mixout(xs, modtab, hf, hb, mo, o_gqa, o_mla, m_norm_p, w_out_p, ctx_len, lat_only):
    b, s, d = xs.shape
    blk_off = ctx_len // TB if lat_only else 0
    tok = lambda n: pl.BlockSpec((None, TB, n), lambda i, j: (i, j + blk_off, 0))
    full = lambda a: pl.BlockSpec(a.shape, lambda i, j: (0,) * a.ndim)
    return pl.pallas_call(
        functools.partial(_mixout_kernel, ctx_len, blk_off),
        grid=(b, s // TB - blk_off),
        in_specs=[tok(d),
                  pl.BlockSpec((None, 2, 6, d), lambda i, j: (i, 0, 0, 0)),
                  tok(M_HEADS * SLOT), tok(M_HEADS * SLOT), tok(M_HEADS * SLOT), tok(A_HEADS * SLOT),
                  pl.BlockSpec((None, TB, L_HEADS * SLOT),
                               lambda i, j: (i, j + blk_off, Q_MLA // L_HEADS)),
                  full(m_norm_p), full(w_out_p)],
        out_specs=pl.BlockSpec((None, TB, d), lambda i, j: (i, j, 0)),
        out_shape=jax.ShapeDtypeStruct((b, s - blk_off * TB, d), F32),
        input_output_aliases={} if lat_only else {0: 0},
        compiler_params=_cparams(("parallel", "parallel")),
        name="mix_out",
    )(xs, modtab, hf, hb, mo, o_gqa, o_mla, m_norm_p, w_out_p)


def _ffn_kernel(ctx_len, x_ref, mod_ref, g_ref, wg_ref, wu_ref, wd_ref, o_ref, h_sc, acc_sc):
    tm = x_ref.shape[0]
    t0 = pl.program_id(1) * tm
    f = pl.program_id(2)

    @pl.when(f == 0)
    def _():
        h_sc[...] = _modulated(x_ref[...], g_ref[...], mod_ref, t0, ctx_len, 3, 4).astype(BF16)
        acc_sc[...] = jnp.zeros_like(acc_sc)

    h = h_sc[...]
    a = jnp.dot(h, wg_ref[...], preferred_element_type=F32)
    u = jnp.dot(h, wu_ref[...], preferred_element_type=F32)
    acc_sc[...] += jnp.dot((_silu(a) * u).astype(BF16), wd_ref[...], preferred_element_type=F32)

    @pl.when(f == pl.num_programs(2) - 1)
    def _():
        o_ref[...] = x_ref[...] + _gate_rows(mod_ref, t0, tm, ctx_len, 5) * acc_sc[...]


def _ffn(xs, modtab, gain, wg, wu, wd, ctx_len):
    b, s, d = xs.shape
    tm = s // 4
    nf = wg.shape[1] // TF
    return pl.pallas_call(
        functools.partial(_ffn_kernel, ctx_len),
        grid=(b, s // tm, nf),
        in_specs=[pl.BlockSpec((None, tm, d), lambda i, j, f: (i, j, 0)),
                  pl.BlockSpec((None, 2, 6, d), lambda i, j, f: (i, 0, 0, 0)),
                  pl.BlockSpec((1, d), lambda i, j, f: (0, 0)),
                  pl.BlockSpec((d, TF), lambda i, j, f: (0, f)),
                  pl.BlockSpec((d, TF), lambda i, j, f: (0, f)),
                  pl.BlockSpec((TF, d), lambda i, j, f: (f, 0))],
        out_specs=pl.BlockSpec((None, tm, d), lambda i, j, f: (i, j, 0)),
        out_shape=jax.ShapeDtypeStruct((b, s, d), F32),
        scratch_shapes=[pltpu.VMEM((tm, d), BF16), pltpu.VMEM((tm, d), F32)],
        input_output_aliases={0: 0},
        compiler_params=_cparams(("parallel", "parallel", "arbitrary")),
        name="ffn_dense",
    )(xs, modtab, gain, wg, wu, wd)


def _top2_combine(logits):
    lane = _lane_iota(logits.shape)
    lane_f = lane.astype(F32)
    lg = jnp.where(lane < N_EXPERTS, logits, -jnp.inf)
    v1 = jnp.max(lg, axis=-1, keepdims=True)
    i1 = jnp.min(jnp.where(lg == v1, lane_f, float(LANE)), axis=-1, keepdims=True)
    rest = jnp.where(lane_f == i1, -jnp.inf, lg)
    v2 = jnp.max(rest, axis=-1, keepdims=True)
    i2 = jnp.min(jnp.where(rest == v2, lane_f, float(LANE)), axis=-1, keepdims=True)
    e2 = jnp.exp(v2 - v1)
    w1 = 1.0 / (1.0 + e2)
    w2 = e2 / (1.0 + e2)
    sel = jnp.where((lane_f == i1) | (lane_f == i2), 1.0, 0.0)
    return jnp.where(lane_f == i1, w1, 0.0) + jnp.where(lane_f == i2, w2, 0.0), sel


def _moe_kernel(ctx_len, x_ref, mod_ref, g_ref, r_ref, tri_ref, wg_ref, wu_ref, wd_ref, o_ref,
                h_sc, comb_sc, rank_sc, rankt_sc, rankc_sc, wc_sc, xg_sc, acc_sc, nt_sc):
    tm = x_ref.shape[0]
    t0 = pl.program_id(1) * tm
    e = pl.program_id(2)
    f = pl.program_id(3)

    @pl.when(jnp.logical_and(e == 0, f == 0))
    def _():
        h = _modulated(x_ref[...], g_ref[...], mod_ref, t0, ctx_len, 3, 4)
        logits = jnp.dot(h, r_ref[...], preferred_element_type=F32, precision=lax.Precision.HIGHEST)
        comb, sel = _top2_combine(logits)
        rank =jnp.dot(tri_ref[...], sel.astype(BF16), preferred_element_type=F32)
        rank = jnp.where(sel > 0.0, rank, -1.0)
        comb_sc[...] = comb
        rank_sc[...] = rank
        pad = rankt_sc.shape[1] - tm
        h_sc[0:tm, :] = h.astype(BF16)
        if pad:
            rankt_sc[...] = jnp.concatenate([rank, jnp.full((pad, LANE), -1.0, F32)], axis=0).T
            h_sc[tm:, :] = jnp.zeros((pad, h_sc.shape[1]), BF16)
        else:
            rankt_sc[...] = rank.T
        o_ref[...] = jnp.zeros_like(o_ref)

    @pl.when(f == 0)
    def _():
        lane = _lane_iota((tm, LANE))
        rank_c = jnp.sum(jnp.where(lane == e, rank_sc[...], 0.0), axis=-1, keepdims=True)
        rankc_sc[...] = rank_c
        wc_sc[...] = jnp.sum(jnp.where(lane == e, comb_sc[...], 0.0), axis=-1, keepdims=True)
        n_rows = jnp.sum(jnp.where(rank_c >= 0.0, 1.0, 0.0)).astype(jnp.int32)
        n_rt = lax.div(n_rows + (RT - 1), RT)
        n_gt = lax.div(n_rt * RT + (GT - 1), GT)
        n_st = lax.div(n_rt * RT + (ST - 1), ST)
        nt_sc[0] = n_rt
        nt_sc[1] = n_st
        rank_r = rankt_sc[pl.ds(e, 1), :]

        def gather(i, c):
            r0 = pl.multiple_of(i * GT, 16)
            tgt = (r0 + _row_iota((GT, 1))).astype(F32)
            sel_t = jnp.where(rank_r == tgt, 1.0, 0.0).astype(BF16)
            xg_sc[pl.ds(r0, GT), :] = jnp.dot(sel_t, h_sc[...], preferred_element_type=F32).astype(BF16)
            return c

        def clear(i, c):
            acc_sc[pl.ds(pl.multiple_of(i * ST, ST), ST), :] = jnp.zeros((ST, acc_sc.shape[1]), F32)
            return c

        lax.fori_loop(0, n_gt, gather, 0)
        lax.fori_loop(0, n_st, clear, 0)

    def expert(i, c):
        r0 = pl.multiple_of(i * RT, 16)
        rows = xg_sc[pl.ds(r0, RT), :]
        a = jnp.dot(rows, wg_ref[...], preferred_element_type=F32)
        u = jnp.dot(rows, wu_ref[...], preferred_element_type=F32)
        acc_sc[pl.ds(r0, RT), :] += jnp.dot((_silu(a) * u).astype(BF16), wd_ref[...],
                                            preferred_element_type=F32)
        return c

    lax.fori_loop(0, nt_sc[0], expert, 0)

    @pl.when(f == pl.num_programs(3) - 1)
    def _():
        def scatter(i, c):
            r0 = pl.multiple_of(i * ST, ST)
            tgt = (r0 + _lane_iota((1, ST))).astype(F32)
            w_t = jnp.where(rankc_sc[...] == tgt, wc_sc[...], 0.0).astype(BF16)
            o_ref[...] += jnp.dot(w_t, acc_sc[pl.ds(r0, ST), :].astype(BF16), preferred_element_type=F32)
            return c

        lax.fori_loop(0, nt_sc[1], scatter, 0)

    @pl.when(jnp.logical_and(e == pl.num_programs(2) - 1, f == pl.num_programs(3) - 1))
    def _():
        o_ref[...] = x_ref[...] + _gate_rows(mod_ref, t0, tm, ctx_len, 5) * o_ref[...]


def _moe(xs, modtab, gain, router_p, wg, wu, wd, ctx_len):
    b, s, d = xs.shape
    tm = s // 4
    tp = -(-tm // LANE) * LANE
    r_max = -(-tm // RT) * RT
    tr = max(-(-r_max // GT) * GT, -(-r_max // ST) * ST)
    ne, _, dff = wg.shape
    tri = jnp.tril(jnp.ones((tm, tm), BF16), -1)
    TF = TF_MOE
    assert dff % TF == 0
    return pl.pallas_call(
        functools.partial(_moe_kernel, ctx_len),
        grid=(b, s // tm, ne, dff // TF),
        in_specs=[pl.BlockSpec((None, tm, d), lambda i, j, e, f: (i, j, 0), pipeline_mode=pl.Buffered(1)),
                  pl.BlockSpec((None, 2, 6, d), lambda i, j, e, f: (i, 0, 0, 0)),
                  pl.BlockSpec((1, d), lambda i, j, e, f: (0, 0)),
                  pl.BlockSpec((d, LANE), lambda i, j, e, f: (0, 0)),
                  pl.BlockSpec((tm, tm), lambda i, j, e, f: (0, 0), pipeline_mode=pl.Buffered(1)),
                  pl.BlockSpec((None, d, TF), lambda i, j, e, f: (e, 0, f)),
                  pl.BlockSpec((None, d, TF), lambda i, j, e, f: (e, 0, f)),
                  pl.BlockSpec((None, TF, d), lambda i, j, e, f: (e, f, 0))],
        out_specs=pl.BlockSpec((None, tm, d), lambda i, j, e, f: (i, j, 0)),
        out_shape=jax.ShapeDtypeStruct((b, s, d), F32),
        scratch_shapes=[pltpu.VMEM((tp, d), BF16),
                        pltpu.VMEM((tm, LANE), F32),
                        pltpu.VMEM((tm, LANE), F32),
                        pltpu.VMEM((LANE, tp), F32),
                        pltpu.VMEM((tm, 1), F32),
                        pltpu.VMEM((tm, 1), F32),
                        pltpu.VMEM((tr, d), BF16),
                        pltpu.VMEM((tr, d), F32),
                        pltpu.SMEM((2,), jnp.int32)],
        input_output_aliases={0: 0},
        compiler_params=_cparams(("parallel", "parallel", "arbitrary", "arbitrary")),
        name="moe_top2",
    )(xs, modtab, gain, router_p, tri, wg, wu, wd)


def _pad_heads(w, n_heads, dh, axis=-1):
    axis = axis % w.ndim
    shp = w.shape[:axis] + (n_heads, dh) + w.shape[axis + 1:]
    pad = [(0, 0)] * (w.ndim + 1)
    pad[axis + 1] = (0, SLOT - dh)
    out = jnp.pad(w.reshape(shp), pad)
    return out.reshape(w.shape[:axis] + (n_heads * SLOT,) + w.shape[axis + 1:])


def _rope_tables(seq, ctx_len):
    t = jnp.arange(seq)
    rows = (t // GRID_W).astype(F32)
    cols = (t % GRID_W).astype(F32)

    def angles(rot_dim):
        nf = rot_dim // 4
        inv = ROPE_THETA ** (-jnp.arange(nf, dtype=F32) / nf)
        ar = rows[:, None] * inv
        ac = cols[:, None] * inv
        return jnp.concatenate([ar, ar, ac, ac], axis=-1)

    def slot_tables(rot_dim, lane0):
        ang = angles(rot_dim)
        quarter = rot_dim // 4
        first = (jnp.arange(rot_dim) % (2 * quarter)) < quarter
        cos = jnp.ones((seq, SLOT), F32).at[:, lane0:lane0 + rot_dim].set(jnp.cos(ang))
        sin = jnp.sin(ang)
        sin_m = jnp.zeros((seq, SLOT), F32).at[:, lane0:lane0 + rot_dim].set(jnp.where(first, -sin, 0.0))
        sin_p = jnp.zeros((seq, SLOT), F32).at[:, lane0:lane0 + rot_dim].set(jnp.where(first, 0.0, sin))
        ident = jnp.stack([jnp.ones((ctx_len, SLOT), F32), jnp.zeros((ctx_len, SLOT), F32),
                           jnp.zeros((ctx_len, SLOT), F32)])
        return jnp.concatenate([ident, jnp.stack([cos, sin_m, sin_p])], axis=1)

    return jnp.concatenate([slot_tables(A_DH, 0), slot_tables(L_ROPE, L_NOPE)], axis=0)


def _layer_params(w_in, m_conv, m_gate_b, m_norm, a_qnorm, a_knorm, l_cq_norm, l_ckv_norm, l_wuq, l_wukv,
                  l_qnorm, l_knorm, w_out):
    d = w_in.shape[0]
    offs = np.cumsum((0,) + IN_SIZES)
    seg = [w_in[:, offs[i]:offs[i + 1]] for i in range(len(IN_SIZES))]
    misc = jnp.zeros((d, SLOT), F32).at[:, 0:4 * M_HEADS].set(seg[4]).at[:, MISC_KR:MISC_KR + L_ROPE].set(seg[10])
    w_in_p = jnp.concatenate(
        [_pad_heads(seg[0], M_HEADS, M_DH), _pad_heads(seg[1], M_HEADS, M_DH),
         _pad_heads(seg[2], M_HEADS, M_DH), _pad_heads(seg[3], M_HEADS, M_DH), misc,
         _pad_heads(seg[5], A_HEADS, A_DH), _pad_heads(seg[6], A_KV, A_DH), _pad_heads(seg[7], A_KV, A_DH),
         seg[8], seg[9]], axis=1).astype(BF16)
    conv_w = jnp.concatenate([_pad_heads(m_conv[:, :M_HEADS * M_DH], M_HEADS, M_DH),
                              _pad_heads(m_conv[:, M_HEADS * M_DH:], M_HEADS, M_DH)], axis=1)
    conv_scale = jnp.concatenate([jnp.ones((1, M_HEADS * SLOT), F32),
                                  jnp.full((1, M_HEADS * SLOT), M_DH ** -0.5, F32)], axis=1)
    gate_b = jnp.zeros((1, SLOT), F32).at[0, 0:4 * M_HEADS].set(m_gate_b)
    pad1 = lambda g: jnp.pad(g, (0, SLOT - g.shape[0]))
    an = jnp.stack([pad1(a_qnorm), pad1(a_knorm)])
    ln = jnp.stack([pad1(l_qnorm), pad1(l_knorm)])
    wuq_p = _pad_heads(l_wuq, L_HEADS, L_QK).astype(BF16)
    kv = l_wukv.reshape(L_KVRANK, L_HEADS, L_NOPE + L_DV)
    wukv_p = jnp.concatenate(
        [_pad_heads(kv[:, :, :L_NOPE].reshape(L_KVRANK, -1), L_HEADS, L_NOPE),
         _pad_heads(kv[:, :, L_NOPE:].reshape(L_KVRANK, -1), L_HEADS, L_DV)], axis=1).astype(BF16)
    nm, na = M_HEADS * M_DH, A_HEADS * A_DH
    w_out_p = jnp.concatenate(
        [_pad_heads(w_out[:nm], M_HEADS, M_DH, axis=0), _pad_heads(w_out[nm:nm + na], A_HEADS, A_DH, axis=0),
         _pad_heads(w_out[nm + na:], L_HEADS, L_DV, axis=0)], axis=0).astype(BF16)
    m_norm_p = _pad_heads(m_norm[None, :], M_HEADS, M_DH)
    return dict(w_in_p=w_in_p, conv_w=conv_w, conv_scale=conv_scale, gate_b=gate_b, an=an, ln=ln,
                cqn=l_cq_norm[None, :], ckvn=l_ckv_norm[None, :], wuq_p=wuq_p, wukv_p=wukv_p,
                w_out_p=w_out_p, m_norm_p=m_norm_p)


def kernel(x, c, ctx, c_ctx, mod_w, mod_b, norm_mix, norm_ffn, w_in, m_conv, m_gate_b, m_norm, a_qnorm, a_knorm,
           l_cq_norm, l_ckv_norm, l_wuq, l_wukv, l_qnorm, l_knorm, w_out, ffn_wg, ffn_wu, ffn_wd, moe_router,
           moe_wg, moe_wu, moe_wd):
    b, seq, d = x.shape
    ctx_len = ctx.shape[1]
    depth = mod_w.shape[0]
    assert ctx_len % TB == 0 and ctx_len % TQ == 0 and ctx_len == ML and seq % TK == 0 and seq % GRID_W == 0
    xs = jnp.concatenate([ctx, x], axis=1)
    mod_rows = 16
    cc = jnp.zeros((mod_rows, d), F32).at[:b].set(c).at[b].set(c_ctx)
    mod_all = _mod_table(cc, mod_w, mod_b)
    rope_tab = _rope_tables(seq, ctx_len)
    for i in range(depth):
        need_ctx = i < depth - 1
        lat = mod_all[i, :b].reshape(b, 1, 6, d)
        cm = jnp.broadcast_to(mod_all[i, b].reshape(1, 1, 6, d), (b, 1, 6, d))
        modtab = jnp.concatenate([cm, lat], axis=1)
        p = _layer_params(w_in[i], m_conv[i], m_gate_b[i], m_norm[i], a_qnorm[i], a_knorm[i], l_cq_norm[i],
                          l_ckv_norm[i], l_wuq[i], l_wukv[i], l_qnorm[i], l_knorm[i], w_out[i])
        mqk, mo, misc, q_all, k_all, mv_t, v_all = _inproj(
            xs, modtab, norm_mix[i][None, :], p["w_in_p"], rope_tab, p["an"], p["ln"], p["cqn"], p["ckvn"],
            p["wuq_p"], p["wukv_p"], ctx_len)
        q_t, k_c = _conv(mqk, p["conv_w"], p["conv_scale"], ctx_len)
        hf, hb = _mlstm(q_t, k_c, mv_t, misc, p["gate_b"])
        gqa_kv = tuple(h // (A_HEADS // A_KV) for h in range(A_HEADS))
        kv_of = tuple(KV_GQA + kv for kv in gqa_kv) + tuple(range(L_HEADS))
        q_of = tuple(range(A_HEADS)) + tuple(Q_MLA + h for h in range(L_HEADS))
        o_gqa = o_mla = _attention(q_all, k_all, v_all, ctx_len, need_ctx, kv_of, q_of)
        xs = _mixout(xs, modtab, hf, hb, mo, o_gqa, o_mla, p["m_norm_p"], p["w_out_p"], ctx_len, not need_ctx)
        ffn_ctx = ctx_len if need_ctx else 0
        j = i // 2
        if i % 2 == 0:
            xs = _ffn(xs, modtab, norm_ffn[i][None, :], ffn_wg[j].astype(BF16), ffn_wu[j].astype(BF16),
                      ffn_wd[j].astype(BF16), ffn_ctx)
        else:
            router_p = jnp.pad(moe_router[j], ((0, 0), (0, LANE - N_EXPERTS)))
            xs = _moe(xs, modtab, norm_ffn[i][None, :], router_p, moe_wg[j].astype(BF16),
                      moe_wu[j].astype(BF16), moe_wd[j].astype(BF16), ffn_ctx)
    return xs
```
